```python
import math
import jax
import jax.numpy as jnp
from jax import lax
import numpy as np

D_MODEL = 2048
BATCH = 4
SEQ = 4096
DEPTH = 2

GRID_W = 64
CTX_LEN = 256

D_MIX = D_MODEL
N_MIXERS = 4
MIXER_W = D_MIX // N_MIXERS
HEAD_DIM = 64

POOL_WINDOWS = (2, 4, 8, 16)
POOL_GROUP = MIXER_W // len(POOL_WINDOWS)

N_HEADS = MIXER_W // HEAD_DIM
N_KV_HEADS = 2
KV_GROUP = N_HEADS // N_KV_HEADS
KV_W = N_KV_HEADS * HEAD_DIM
WINDOW = 128
ATTN_BLOCK = 128
ROPE_THETA = 10000.0

HYENA_ORDER = 2
HYENA_EMB = 33
HYENA_HIDDEN = 64
SHORT_CONV = 3
HYENA_TARGET = 1e-2
HYENA_FAST = 0.3
HYENA_SLOW = 1.5

GMLP_CHUNK = 128
GMLP_GROUPS = MIXER_W // HEAD_DIM
GMLP_GROUP_DIM = MIXER_W // GMLP_GROUPS

N_EXPERTS = 32
N_EXPERT_GROUPS = 8
TOP_K = 2
D_EXPERT = 768
MOE_BLOCK = 256

LN_EPS = 1e-6

IN_SPLITS = (MIXER_W, MIXER_W, KV_W, KV_W, (HYENA_ORDER + 1) * MIXER_W, 2 * MIXER_W)
IN_W = sum(IN_SPLITS)

kernel_name = 'hybrid_pool_swa_hyena_gmlp_moe_dit'


def _layer_norm(x):
    xf = x.astype(jnp.float32)
    mu = jnp.mean(xf, -1, keepdims=True)
    var = jnp.mean(jnp.square(xf - mu), -1, keepdims=True)
    return ((xf - mu) * lax.rsqrt(var + LN_EPS)).astype(x.dtype)


def _modulate(x, shift, scale):
    return _layer_norm(x) * (1 + scale) + shift


def _post_norm(x, y, alpha, g, b):
    return _layer_norm(alpha * x + y) * g + b


def _group_rms(y, g):
    yf = y.astype(jnp.float32).reshape(y.shape[:-1] + (N_MIXERS, MIXER_W))
    yf = yf * lax.rsqrt(jnp.mean(jnp.square(yf), -1, keepdims=True) + LN_EPS)
    return yf.reshape(y.shape).astype(y.dtype) * g


def _split_proj(z):
    cuts, acc = [], 0
    for w in IN_SPLITS[:-1]:
        acc += w
        cuts.append(acc)
    return jnp.split(z, cuts, axis=-1)


def _pool_mixer(p, w_pool, pool_scale):
    B, L, C = p.shape
    pf = p.astype(jnp.float32)
    cs = jnp.concatenate([jnp.zeros((B, 1, C), jnp.float32), jnp.cumsum(pf, axis=1)], axis=1)
    t = jnp.arange(L)
    outs = []
    for gi, w in enumerate(POOL_WINDOWS):
        sl = slice(gi * POOL_GROUP, (gi + 1) * POOL_GROUP)
        lo = jnp.clip(t - w // 2, 0, L)
        hi = jnp.clip(t + w // 2, 0, L)
        cg = cs[:, :, sl]
        mean = (cg[:, hi] - cg[:, lo]) / (hi - lo).astype(jnp.float32)[None, :, None]
        outs.append((mean - pf[:, :, sl]).astype(p.dtype) @ w_pool[gi])
    return jnp.concatenate(outs, -1) * pool_scale


def _axial_rope(x, row_idx, col_idx):
    half = HEAD_DIM // 2
    nf = half // 2
    inv = ROPE_THETA ** (-jnp.arange(nf, dtype=jnp.float32) / nf)
    xf = x.astype(jnp.float32)

    def rot(xa, pos):
        ang = pos.astype(jnp.float32)[:, None] * inv[None, :]
        cos = jnp.cos(ang)[None, :, None, :]
        sin = jnp.sin(ang)[None, :, None, :]
        a, b = xa[..., :nf], xa[..., nf:]
        return jnp.concatenate([a * cos - b * sin, b * cos + a * sin], -1)

    out = jnp.concatenate([rot(xf[..., :half], row_idx), rot(xf[..., half:], col_idx)], -1)
    return out.astype(x.dtype)


def _attn_latent(q, k, v, kc, vc, sink):
    B, L = q.shape[:2]
    nb = L // ATTN_BLOCK
    scale = HEAD_DIM ** -0.5
    qb = q.reshape(B, nb, ATTN_BLOCK, N_KV_HEADS, KV_GROUP, HEAD_DIM)

    def band(t):
        tp = jnp.pad(t, ((0, 0), (ATTN_BLOCK, ATTN_BLOCK), (0, 0), (0, 0)))
        tp = tp.reshape(B, nb + 2, ATTN_BLOCK, N_KV_HEADS, HEAD_DIM)
        return jnp.concatenate([tp[:, :-2], tp[:, 1:-1], tp[:, 2:]], axis=2)

    kb, vb = band(k), band(v)
    s_band = jnp.einsum('bnqkgd,bnskd->bnkgqs', qb, kb, preferred_element_type=jnp.float32) * scale
    s_ctx = jnp.einsum('bnqkgd,bckd->bnkgqc', qb, kc, preferred_element_type=jnp.float32) * scale
    qpos = jnp.arange(nb)[:, None] * ATTN_BLOCK + jnp.arange(ATTN_BLOCK)[None, :]
    kpos = (jnp.arange(nb)[:, None] - 1) * ATTN_BLOCK + jnp.arange(3 * ATTN_BLOCK)[None, :]
    valid = ((jnp.abs(qpos[:, :, None] - kpos[:, None, :]) <= WINDOW)
             & (kpos >= 0)[:, None, :] & (kpos < L)[:, None, :])
    s_band = jnp.where(valid[None, :, None, None], s_band, -jnp.inf)
    sink_b = jnp.broadcast_to(
        sink.astype(jnp.float32).reshape(N_KV_HEADS, KV_GROUP)[None, None, :, :, None, None],
        s_band.shape[:-1] + (1,))
    p = jax.nn.softmax(jnp.concatenate([sink_b, s_band, s_ctx], -1), axis=-1)
    nk = 3 * ATTN_BLOCK
    p_band = p[..., 1:1 + nk].astype(v.dtype)
    p_ctx = p[..., 1 + nk:].astype(v.dtype)
    out = (jnp.einsum('bnkgqs,bnskd->bnqkgd', p_band, vb)
           + jnp.einsum('bnkgqc,bckd->bnqkgd', p_ctx, vc))
    return out.reshape(B, L, N_HEADS * HEAD_DIM)


def _attn_context(qc, kc, vc, sink):
    B, C = qc.shape[:2]
    qg = qc.reshape(B, C, N_KV_HEADS, KV_GROUP, HEAD_DIM)
    s = jnp.einsum('bqkgd,bckd->bkgqc', qg, kc, preferred_element_type=jnp.float32) * HEAD_DIM ** -0.5
    sink_b = jnp.broadcast_to(
        sink.astype(jnp.float32).reshape(N_KV_HEADS, KV_GROUP)[None, :, :, None, None], s.shape[:-1] + (1,))
    p = jax.nn.softmax(jnp.concatenate([sink_b, s], -1), axis=-1)[..., 1:].astype(vc.dtype)
    return jnp.einsum('bkgqc,bckd->bqkgd', p, vc).reshape(B, C, N_HEADS * HEAD_DIM)


def _short_conv(z, w, b):
    L = z.shape[1]
    p = SHORT_CONV // 2
    zp = jnp.pad(z, ((0, 0), (p, p), (0, 0)))
    out = b
    for i in range(SHORT_CONV):
        out = out + zp[:, i:i + L] * w[i]
    return out


def _hyena_filters(L, w1, b1, f1, w2, b2, f2, w3):
    f32 = jnp.float32
    t = jnp.linspace(0.0, 1.0, L, dtype=f32)[:, None]
    bands = (HYENA_EMB - 1) // 2
    freqs = jnp.linspace(1e-4, bands - 1, bands, dtype=f32)[None, :]
    ang = (2.0 * math.pi / L) * jnp.arange(L, dtype=f32)[:, None] * freqs
    feat = jnp.concatenate([t, jnp.cos(ang), -jnp.sin(ang)], -1)
    hdn = jnp.sin(f1.astype(f32) * (feat @ w1.astype(f32) + b1.astype(f32)))
    hdn = jnp.sin(f2.astype(f32) * (hdn @ w2.astype(f32) + b2.astype(f32)))
    filt = hdn @ w3.astype(f32)
    deltas = jnp.abs(jnp.linspace(math.log(HYENA_TARGET) / HYENA_SLOW,
                                  math.log(HYENA_TARGET) / HYENA_FAST, MIXER_W, dtype=f32))
    decay = jnp.exp(-t * deltas[None, :])
    return filt.reshape(L, HYENA_ORDER, 2, MIXER_W) * decay[:, None, None, :]


def _two_sided_fftconv(u, hf, hb, skip):
    B, L, C = u.shape
    k2 = jnp.concatenate([hf, jnp.zeros((1, C), hf.dtype), hb[1:][::-1]], axis=0)
    uf = jnp.fft.rfft(u.astype(jnp.float32), n=2 * L, axis=1)
    kf = jnp.fft.rfft(k2, n=2 * L, axis=0)
    y = jnp.fft.irfft(uf * kf[None], n=2 * L, axis=1)[:, :L]
    return (y + u.astype(jnp.float32) * skip.astype(jnp.float32)).astype(u.dtype)


def _hyena_mixer(z, lp):
    L = z.shape[1]
    z = _short_conv(z, lp['hy_conv_w'], lp['hy_conv_b'])
    parts = jnp.split(z, HYENA_ORDER + 1, axis=-1)
    gates, y = parts[:-1], parts[-1]
    filt = _hyena_filters(L, lp['hy_w1'], lp['hy_b1'], lp['hy_f1'], lp['hy_w2'], lp['hy_b2'],
                          lp['hy_f2'], lp['hy_w3'])
    for o in range(HYENA_ORDER):
        y = gates[o] * _two_sided_fftconv(y, filt[:, o, 0], filt[:, o, 1], lp['hy_skip'][o])
    return y


def _gmlp_mixer(z, ln_g, ln_b, w_s, b_s):
    B, L = z.shape[:2]
    z = jax.nn.gelu(z)
    u, v = jnp.split(z, 2, axis=-1)
    v = _layer_norm(v) * ln_g + ln_b
    nc = L // GMLP_CHUNK
    vc = v.reshape(B, nc, GMLP_CHUNK, GMLP_GROUPS, GMLP_GROUP_DIM)
    s = jnp.einsum('gpq,bnqgc->bnpgc', w_s, vc) + b_s.T[None, None, :, :, None]
    return u * s.reshape(B, L, MIXER_W)


def _token_mixer(h_lat, h_ctx, row_idx, col_idx, lp, need_ctx):
    B, L = h_lat.shape[:2]
    C = h_ctx.shape[1]
    zp_l, q_l, k_l, v_l, zh_l, zg_l = _split_proj(h_lat @ lp['w_in'])
    zp_c, q_c, k_c, v_c, zh_c, zg_c = _split_proj(h_ctx @ lp['w_in'])
    q_l = _axial_rope(q_l.reshape(B, L, N_HEADS, HEAD_DIM), row_idx, col_idx)
    k_l = _axial_rope(k_l.reshape(B, L, N_KV_HEADS, HEAD_DIM), row_idx, col_idx)
    v_l = v_l.reshape(B, L, N_KV_HEADS, HEAD_DIM)
    k_c = k_c.reshape(B, C, N_KV_HEADS, HEAD_DIM)
    v_c = v_c.reshape(B, C, N_KV_HEADS, HEAD_DIM)

    def merge(parts):
        return _group_rms(jnp.concatenate(parts, -1), lp['mix_norm_g']) @ lp['w_out']

    y_lat = merge([
        _pool_mixer(zp_l, lp['pool_w'], lp['pool_scale']),
        _attn_latent(q_l, k_l, v_l, k_c, v_c, lp['attn_sink']),
        _hyena_mixer(zh_l, lp),
        _gmlp_mixer(zg_l, lp['gm_ln_g'], lp['gm_ln_b'], lp['gm_ws'], lp['gm_bs']),
    ])
    y_ctx = None
    if need_ctx:
        y_ctx = merge([
            _pool_mixer(zp_c, lp['pool_w'], lp['pool_scale']),
            _attn_context(q_c.reshape(B, C, N_HEADS, HEAD_DIM), k_c, v_c, lp['attn_sink']),
            _hyena_mixer(zh_c, lp),
            _gmlp_mixer(zg_c, lp['gm_ln_g'], lp['gm_ln_b'], lp['gm_ws'], lp['gm_bs']),
        ])
    return y_lat, y_ctx


def _route(h, w_router, router_bias):
    T = h.shape[0]
    per = N_EXPERTS // N_EXPERT_GROUPS
    s = jax.nn.sigmoid(jnp.dot(h, w_router, preferred_element_type=jnp.float32))
    sel = s + router_bias.astype(jnp.float32)
    grp = lax.top_k(sel.reshape(T, N_EXPERT_GROUPS, per), 2)[0].sum(-1)
    best = jnp.argmax(grp, -1)
    in_grp = (jnp.arange(N_EXPERTS) // per)[None, :] == best[:, None]
    _, idx = lax.top_k(jnp.where(in_grp, sel, -jnp.inf), TOP_K)
    w = jnp.take_along_axis(s, idx, -1)
    return idx, w / jnp.sum(w, -1, keepdims=True)


def _moe(h, w_router, router_bias, w_gate, w_up, w_down):
    T, D = h.shape
    idx, gate = _route(h, w_router, router_bias)
    A = T * TOP_K
    flat_e = idx.reshape(A)
    flat_tok = jnp.repeat(jnp.arange(T, dtype=jnp.int32), TOP_K)
    flat_w = gate.reshape(A)
    order = jnp.argsort(flat_e)
    se = flat_e[order]
    counts = jnp.bincount(flat_e, length=N_EXPERTS)
    padded = (counts + MOE_BLOCK - 1) // MOE_BLOCK * MOE_BLOCK
    pad_end = jnp.cumsum(padded)
    pad_start = pad_end - padded
    start = jnp.cumsum(counts) - counts
    dest = pad_start[se] + jnp.arange(A) - start[se]
    n_blocks = (A + N_EXPERTS * (MOE_BLOCK - 1) + MOE_BLOCK - 1) // MOE_BLOCK
    n_slots = n_blocks * MOE_BLOCK
    slot_tok = jnp.full((n_slots,), T, jnp.int32).at[dest].set(flat_tok[order])
    slot_w = jnp.zeros((n_slots,), jnp.float32).at[dest].set(flat_w[order])
    block_e = jnp.minimum(jnp.searchsorted(pad_end, jnp.arange(n_blocks) * MOE_BLOCK, side='right'),
                          N_EXPERTS - 1)
    h_pad = jnp.concatenate([h, jnp.zeros((1, D), h.dtype)], axis=0)

    def run_block(args):
        tok, e = args
        xb = h_pad[tok]
        return (jax.nn.silu(xb @ w_gate[e]) * (xb @ w_up[e])) @ w_down[e]

    y = lax.map(run_block, (slot_tok.reshape(n_blocks, MOE_BLOCK), block_e))
    y = y.reshape(n_slots, D) * slot_w[:, None].astype(y.dtype)
    return jnp.zeros((T + 1, D), y.dtype).at[slot_tok].add(y)[:T]


def setup_inputs(seed: int = 0) -> dict:
    key = jax.random.key(seed)
    ks = iter(jax.random.split(key, 40))

    def nrm(shape, s):
        return jax.random.normal(next(ks), shape, jnp.float32) * s

    beta = (8 * DEPTH) ** -0.25
    Lr, D = DEPTH, D_MODEL
    return {
        'x': nrm((BATCH, SEQ, D), 1.0),
        'c': nrm((BATCH, D), 1.0),
        'ctx': nrm((BATCH, CTX_LEN, D), 1.0),
        'c_ctx': nrm((D,), 1.0),
        'w_ada': nrm((Lr, D, 6 * D), D ** -0.5),
        'b_ada': nrm((Lr, 6 * D), 0.02),
        'w_in': nrm((Lr, D, IN_W), D ** -0.5),
        'w_out': nrm((Lr, D_MIX, D), D_MIX ** -0.5 * beta),
        'mix_norm_g': 1.0 + nrm((Lr, D_MIX), 0.02),
        'pool_w': nrm((Lr, len(POOL_WINDOWS), POOL_GROUP, POOL_GROUP), POOL_GROUP ** -0.5),
        'pool_scale': 1.0 + nrm((Lr, MIXER_W), 0.1),
        'attn_sink': nrm((Lr, N_HEADS), 1.0),
        'hy_conv_w': nrm((Lr, SHORT_CONV, (HYENA_ORDER + 1) * MIXER_W), SHORT_CONV ** -0.5),
        'hy_conv_b': nrm((Lr, (HYENA_ORDER + 1) * MIXER_W), 0.02),
        'hy_w1': nrm((Lr, HYENA_EMB, HYENA_HIDDEN), HYENA_EMB ** -0.5),
        'hy_b1': nrm((Lr, HYENA_HIDDEN), 0.1),
        'hy_f1': 1.0 + nrm((Lr, HYENA_HIDDEN), 0.1),
        'hy_w2': nrm((Lr, HYENA_HIDDEN, HYENA_HIDDEN), HYENA_HIDDEN ** -0.5),
        'hy_b2': nrm((Lr, HYENA_HIDDEN), 0.1),
        'hy_f2': 1.0 + nrm((Lr, HYENA_HIDDEN), 0.1),
        'hy_w3': nrm((Lr, HYENA_HIDDEN, HYENA_ORDER * 2 * MIXER_W), 0.1 * HYENA_HIDDEN ** -0.5),
        'hy_skip': nrm((Lr, HYENA_ORDER, MIXER_W), 1.0),
        'gm_ln_g': 1.0 + nrm((Lr, MIXER_W), 0.02),
        'gm_ln_b': nrm((Lr, MIXER_W), 0.02),
        'gm_ws': nrm((Lr, GMLP_GROUPS, GMLP_CHUNK, GMLP_CHUNK), 0.02),
        'gm_bs': 1.0 + nrm((Lr, GMLP_GROUPS, GMLP_CHUNK), 0.02),
        'ln1_g': 1.0 + nrm((Lr, D), 0.02),
        'ln1_b': nrm((Lr, D), 0.02),
        'ln2_g': 1.0 + nrm((Lr, D), 0.02),
        'ln2_b': nrm((Lr, D), 0.02),
        'w_router': nrm((D, N_EXPERTS), D ** -0.5),
        'router_bias': nrm((N_EXPERTS,), 0.01),
        'w_gate': nrm((Lr, N_EXPERTS, D, D_EXPERT), D ** -0.5),
        'w_up': nrm((Lr, N_EXPERTS, D, D_EXPERT), D ** -0.5),
        'w_down': nrm((Lr, N_EXPERTS, D_EXPERT, D), D_EXPERT ** -0.5 * beta),
    }


def reference(x, c, ctx, c_ctx, w_ada, b_ada, w_in, w_out, mix_norm_g, pool_w, pool_scale, attn_sink,
              hy_conv_w, hy_conv_b, hy_w1, hy_b1, hy_f1, hy_w2, hy_b2, hy_f2, hy_w3, hy_skip,
              gm_ln_g, gm_ln_b, gm_ws, gm_bs, ln1_g, ln1_b, ln2_g, ln2_b, w_router, router_bias,
              w_gate, w_up, w_down):
    alpha = (2 * DEPTH) ** 0.25
    B, L, D = x.shape
    C = ctx.shape[1]
    rows = L // GRID_W
    row_idx = jnp.repeat(jnp.arange(rows), GRID_W)
    col_idx = jnp.tile(jnp.arange(GRID_W), rows)
    for layer in range(DEPTH):
        last = layer == DEPTH - 1
        lp = {
            'w_in': w_in[layer], 'w_out': w_out[layer], 'mix_norm_g': mix_norm_g[layer],
            'pool_w': pool_w[layer], 'pool_scale': pool_scale[layer], 'attn_sink': attn_sink[layer],
            'hy_conv_w': hy_conv_w[layer], 'hy_conv_b': hy_conv_b[layer],
            'hy_w1': hy_w1[layer], 'hy_b1': hy_b1[layer], 'hy_f1': hy_f1[layer],
            'hy_w2': hy_w2[layer], 'hy_b2': hy_b2[layer], 'hy_f2': hy_f2[layer],
            'hy_w3': hy_w3[layer], 'hy_skip': hy_skip[layer],
            'gm_ln_g': gm_ln_g[layer], 'gm_ln_b': gm_ln_b[layer], 'gm_ws': gm_ws[layer], 'gm_bs': gm_bs[layer],
        }
        mod = jax.nn.silu(c) @ w_ada[layer] + b_ada[layer]
        mod_c = jax.nn.silu(c_ctx) @ w_ada[layer] + b_ada[layer]
        sh1, sc1, g1, sh2, sc2, g2 = [m[:, None, :] for m in jnp.split(mod, 6, axis=-1)]
        csh1, csc1, cg1, csh2, csc2, cg2 = jnp.split(mod_c, 6, axis=-1)

        y_lat, y_ctx = _token_mixer(_modulate(x, sh1, sc1), _modulate(ctx, csh1, csc1),
                                    row_idx, col_idx, lp, not last)
        x = _post_norm(x, g1 * y_lat, alpha, ln1_g[layer], ln1_b[layer])
        h_lat = _modulate(x, sh2, sc2).reshape(B * L, D)
        if last:
            f = _moe(h_lat, w_router, router_bias, w_gate[layer], w_up[layer], w_down[layer])
            x = _post_norm(x, g2 * f.reshape(B, L, D), alpha, ln2_g[layer], ln2_b[layer])
        else:
            ctx = _post_norm(ctx, cg1 * y_ctx, alpha, ln1_g[layer], ln1_b[layer])
            h_ctx = _modulate(ctx, csh2, csc2).reshape(B * C, D)
            f = _moe(jnp.concatenate([h_lat, h_ctx], axis=0), w_router, router_bias,
                     w_gate[layer], w_up[layer], w_down[layer])
            x = _post_norm(x, g2 * f[:B * L].reshape(B, L, D), alpha, ln2_g[layer], ln2_b[layer])
            ctx = _post_norm(ctx, cg2 * f[B * L:].reshape(B, C, D), alpha, ln2_g[layer], ln2_b[layer])
    return x
```

```python
import functools
import math

import jax
import jax.numpy as jnp
from jax import lax
from jax.experimental import pallas as pl
from jax.experimental.pallas import tpu as pltpu

F32 = jnp.float32
BF16 = jnp.bfloat16
HI = lax.Precision.HIGHEST

LN_EPS = 1e-6
MIXER_W = 512
HEAD_DIM = 64
N_HEADS = 8
N_KV_HEADS = 2
KV_GROUP = 4
KV_W = 128
WINDOW = 128
ROPE_THETA = 10000.0
POOL_WINDOWS = (2, 4, 8, 16)
POOL_GROUP = 128
POOL_HALO = 8
GMLP_CHUNK = 128
GMLP_GROUPS = 8
HY_EMB = 33
HY_EMB_PAD = 128
HY_TARGET = 1e-2
HY_FAST = 0.3
HY_SLOW = 1.5
N_EXPERTS = 32
N_EXPERT_GROUPS = 8
TOP_K = 2
D_EXPERT = 768
MOE_BLOCK = 256
LOGIT_PAD = 128

COL_HY = 0
COL_POOL = 3
COL_Q = 4
COL_GU = 5
COL_GV = 6
COL_K = 28
COL_V = 29
IN_W = 3840

VMEM_LIMIT = 56 * 1024 * 1024


def _cparams(sem):
    return pltpu.CompilerParams(dimension_semantics=sem, vmem_limit_bytes=VMEM_LIMIT)


def _ln(x):
    mu = jnp.mean(x, axis=-1, keepdims=True)
    xc = x - mu
    var = jnp.mean(xc * xc, axis=-1, keepdims=True)
    return xc * lax.rsqrt(var + LN_EPS)


def _rms_gain(y, g):
    return y * lax.rsqrt(jnp.mean(y * y, axis=-1, keepdims=True) + LN_EPS) * g


def _row_chunks(n_rows, chunk, body):
    n = n_rows // chunk
    if n == 1:
        body(0)
        return

    def step(i, carry):
        body(pl.multiple_of(i * chunk, chunk))
        return carry

    lax.fori_loop(0, n, step, 0)


def _ada_kernel(c_ref, w_ref, b_ref, o_ref):
    c = c_ref[...]
    s = c * jax.nn.sigmoid(c)
    o_ref[...] = jnp.dot(s, w_ref[...], precision=HI, preferred_element_type=F32) + b_ref[...]


def _ada_mods(cvec, w_ada, b_ada):
    depth, d, n = w_ada.shape
    tn = 1024
    return pl.pallas_call(
        _ada_kernel,
        grid=(depth, n // tn),
        in_specs=[
            pl.BlockSpec((8, d), lambda l, j: (0, 0)),
            pl.BlockSpec((None, d, tn), lambda l, j: (l, 0, j)),
            pl.BlockSpec((None, 1, tn), lambda l, j: (l, 0, j)),
        ],
        out_specs=pl.BlockSpec((None, 8, tn), lambda l, j: (l, 0, j)),
        out_shape=jax.ShapeDtypeStruct((depth, 8, n), F32),
        compiler_params=_cparams(("arbitrary", "arbitrary")),
        name="ada_mods",
    )(cvec, w_ada, b_ada.reshape(depth, 1, n))


def _inproj_kernel(x_ref, sh_ref, sc_ref, w_ref, o_ref, h_scr):
    @pl.when(pl.program_id(1) == 0)
    def _():
        sh = sh_ref[...]
        sc1 = 1.0 + sc_ref[...]

        def body(r0):
            x = x_ref[pl.ds(r0, 256), :]
            h_scr[pl.ds(r0, 256), :] = (_ln(x) * sc1 + sh).astype(BF16)

        _row_chunks(x_ref.shape[0], 256, body)

    o_ref[...] = jnp.dot(h_scr[...], w_ref[...], preferred_element_type=F32)


def _in_proj(xa, mod3, w_bf, n_batch, seq):
    t, d = xa.shape
    n = w_bf.shape[1]
    tm, tn = 1024, 768

    def mrow(i):
        return jnp.minimum((i * tm) // seq, n_batch)

    return pl.pallas_call(
        _inproj_kernel,
        grid=(t // tm, n // tn),
        in_specs=[
            pl.BlockSpec((tm, d), lambda i, j: (i, 0)),
            pl.BlockSpec((None, 1, d), lambda i, j: (mrow(i), 0, 0)),
            pl.BlockSpec((None, 1, d), lambda i, j: (mrow(i), 0, 1)),
            pl.BlockSpec((d, tn), lambda i, j: (0, j)),
        ],
        out_specs=pl.BlockSpec((tm, tn), lambda i, j: (i, j)),
        out_shape=jax.ShapeDtypeStruct((t, n), F32),
        scratch_shapes=[pltpu.VMEM((tm, d), BF16)],
        compiler_params=_cparams(("arbitrary", "arbitrary")),
        name="in_proj",
    )(xa, mod3, mod3, w_bf)


def _pool_kernel(z_ref, w_ref, ps_ref, g_ref, o_ref, buf):
    seq = z_ref.shape[0]
    rc = min(256, seq)
    zeros = jnp.zeros((POOL_HALO, MIXER_W), F32)
    buf[0:POOL_HALO, :] = zeros
    buf[seq + POOL_HALO:seq + 2 * POOL_HALO, :] = zeros

    def fill(r0):
        buf[pl.ds(r0 + POOL_HALO, rc), :] = z_ref[pl.ds(r0, rc), :]

    _row_chunks(seq, rc, fill)

    def body(r0):
        win = buf[pl.ds(r0, rc + 2 * POOL_HALO), :]
        t = r0 + lax.broadcasted_iota(jnp.int32, (rc, 1), 0)
        outs = []
        for gi, w in enumerate(POOL_WINDOWS):
            lanes = slice(gi * POOL_GROUP, (gi + 1) * POOL_GROUP)
            acc = None
            for j in range(-w // 2, w // 2):
                piece = win[POOL_HALO + j:POOL_HALO + j + rc, lanes]
                acc = piece if acc is None else acc + piece
            cnt = (jnp.minimum(t + w // 2, seq) - jnp.maximum(t - w // 2, 0)).astype(F32)
            dlt = acc / cnt - win[POOL_HALO:POOL_HALO + rc, lanes]
            outs.append(jnp.dot(dlt.astype(BF16), w_ref[gi], preferred_element_type=F32))
        y = jnp.concatenate(outs, axis=-1) * ps_ref[...]
        o_ref[pl.ds(r0, rc), :] = _rms_gain(y, g_ref[...]).astype(BF16)

    _row_chunks(seq, rc, body)


def _pool_mix(z, pool_w_bf, pool_scale, gain, n_seq, seq, row_blk0):
    return pl.pallas_call(
        _pool_kernel,
        grid=(n_seq,),
        in_specs=[
            pl.BlockSpec((seq, MIXER_W), lambda s: (row_blk0 + s, COL_POOL)),
            pl.BlockSpec((4, POOL_GROUP, POOL_GROUP), lambda s: (0, 0, 0)),
            pl.BlockSpec((1, MIXER_W), lambda s: (0, 0)),
            pl.BlockSpec((1, MIXER_W), lambda s: (0, 0)),
        ],
        out_specs=pl.BlockSpec((seq, MIXER_W), lambda s: (s, 0)),
        out_shape=jax.ShapeDtypeStruct((n_seq * seq, MIXER_W), BF16),
        scratch_shapes=[pltpu.VMEM((seq + 2 * POOL_HALO, MIXER_W), F32)],
        compiler_params=_cparams(("arbitrary",)),
        name="pool_mix",
    )(z, pool_w_bf, pool_scale, gain)


def _rope(x, cos, sin_signed):
    lane = lax.broadcasted_iota(jnp.int32, x.shape, 1)
    partner = jnp.where((lane & 31) < 16, pltpu.roll(x, 112, axis=1), pltpu.roll(x, 16, axis=1))
    return x * cos + partner * sin_signed


def _attend(qs, sink_col, parts):
    dn = (((1,), (1,)), ((), ()))
    scores = []
    for kk, _, mask in parts:
        s = lax.dot_general(qs, kk, dn, preferred_element_type=F32)
        if mask is not None:
            s = jnp.where(mask, s, -1e30)
        scores.append(s)
    m = sink_col
    for s in scores:
        m = jnp.maximum(m, jnp.max(s, axis=-1, keepdims=True))
    den = jnp.exp(sink_col - m)
    out = None
    for s, (_, vv, _) in zip(scores, parts):
        p = jnp.exp(s - m)
        den = den + jnp.sum(p, axis=-1, keepdims=True)
        o = jnp.dot(p.astype(BF16), vv, preferred_element_type=F32)
        out = o if out is None else out + o
    return out / den


def _heads_attend(q_all, sink_ref, tq, make_parts):
    cols = [None] * N_HEADS
    for kh in range(N_KV_HEADS):
        heads = [kh * KV_GROUP + g for g in range(KV_GROUP)]
        qs = jnp.concatenate([q_all[:, h * HEAD_DIM:(h + 1) * HEAD_DIM] for h in heads], axis=0)
        sink_col = jnp.concatenate([jnp.full((tq, 1), sink_ref[h], F32) for h in heads], axis=0)
        o = _attend(qs.astype(BF16), sink_col, make_parts(kh))
        for g, h in enumerate(heads):
            cols[h] = o[g * tq:(g + 1) * tq, :]
    return jnp.concatenate(cols, axis=-1)


def _attn_lat_kernel(sink_ref, q_ref, k_ref, v_ref, kc_ref, vc_ref, cos_ref, sin_ref, g_ref, o_ref):
    tq = q_ref.shape[0]
    seq = k_ref.shape[0]
    nband = tq + 2 * WINDOW
    i = pl.program_id(1)
    q0 = pl.multiple_of(i * tq, tq)
    k0 = pl.multiple_of(jnp.clip(q0 - WINDOW, 0, seq - nband), WINDOW)

    cq = cos_ref[pl.ds(q0, tq), :]
    sq = sin_ref[pl.ds(q0, tq), :]
    q = q_ref[...]
    q_all = jnp.concatenate(
        [_rope(q[:, c * 128:(c + 1) * 128], cq, sq) for c in range(MIXER_W // 128)], axis=-1)
    q_all = q_all * (HEAD_DIM ** -0.5)

    kb = _rope(k_ref[pl.ds(k0, nband), :], cos_ref[pl.ds(k0, nband), :],
               sin_ref[pl.ds(k0, nband), :]).astype(BF16)
    vb = v_ref[pl.ds(k0, nband), :].astype(BF16)
    kc = kc_ref[...].astype(BF16)
    vc = vc_ref[...].astype(BF16)

    qpos = q0 + (lax.broadcasted_iota(jnp.int32, (KV_GROUP * tq, nband), 0) & (tq - 1))
    kpos = k0 + lax.broadcasted_iota(jnp.int32, (KV_GROUP * tq, nband), 1)
    mask = jnp.abs(qpos - kpos) <= WINDOW

    def make_parts(kh):
        hs = slice(kh * HEAD_DIM, (kh + 1) * HEAD_DIM)
        return [(kb[:, hs], vb[:, hs], mask), (kc[:, hs], vc[:, hs], None)]

    y = _heads_attend(q_all, sink_ref, tq, make_parts)
    o_ref[...] = _rms_gain(y, g_ref[...]).astype(BF16)


def _attn_ctx_kernel(sink_ref, q_ref, kc_ref, vc_ref, g_ref, o_ref):
    tq = q_ref.shape[0]
    q_all = q_ref[...] * (HEAD_DIM ** -0.5)
    kc = kc_ref[...].astype(BF16)
    vc = vc_ref[...].astype(BF16)

    def make_parts(kh):
        hs = slice(kh * HEAD_DIM, (kh + 1) * HEAD_DIM)
        return [(kc[:, hs], vc[:, hs], None)]

    y = _heads_attend(q_all, sink_ref, tq, make_parts)
    o_ref[...] = _rms_gain(y, g_ref[...]).astype(BF16)


def _attn_lat(z, sink, cos_t, sin_t, gain, n_batch, seq, ctx_len):
    tq = 128
    nq = seq // tq
    ctx_blk0 = (n_batch * seq) // ctx_len
    smem = pl.BlockSpec(memory_space=pltpu.SMEM)
    return pl.pallas_call(
        _attn_lat_kernel,
        grid=(n_batch, nq),
        in_specs=[
            smem,
            pl.BlockSpec((tq, MIXER_W), lambda b, i: (b * nq + i, COL_Q)),
            pl.BlockSpec((seq, KV_W), lambda b, i: (b, COL_K)),
            pl.BlockSpec((seq, KV_W), lambda b, i: (b, COL_V)),
            pl.BlockSpec((ctx_len, KV_W), lambda b, i: (ctx_blk0 + b, COL_K)),
            pl.BlockSpec((ctx_len, KV_W), lambda b, i: (ctx_blk0 + b, COL_V)),
            pl.BlockSpec((seq, 128), lambda b, i: (0, 0)),
            pl.BlockSpec((seq, 128), lambda b, i: (0, 0)),
            pl.BlockSpec((1, MIXER_W), lambda b, i: (0, 0)),
        ],
        out_specs=pl.BlockSpec((tq, MIXER_W), lambda b, i: (b * nq + i, 0)),
        out_shape=jax.ShapeDtypeStruct((n_batch * seq, MIXER_W), BF16),
        compiler_params=_cparams(("arbitrary", "arbitrary")),
        name="attn_lat",
    )(sink, z, z, z, z, z, cos_t, sin_t, gain)


def _attn_ctx(z, sink, gain, n_batch, seq, ctx_len):
    ctx_blk0 = (n_batch * seq) // ctx_len
    smem = pl.BlockSpec(memory_space=pltpu.SMEM)
    return pl.pallas_call(
        _attn_ctx_kernel,
        grid=(n_batch,),
        in_specs=[
            smem,
            pl.BlockSpec((ctx_len, MIXER_W), lambda b: (ctx_blk0 + b, COL_Q)),
            pl.BlockSpec((ctx_len, KV_W), lambda b: (ctx_blk0 + b, COL_K)),
            pl.BlockSpec((ctx_len, KV_W), lambda b: (ctx_blk0 + b, COL_V)),
            pl.BlockSpec((1, MIXER_W), lambda b: (0, 0)),
        ],
        out_specs=pl.BlockSpec((ctx_len, MIXER_W), lambda b: (b, 0)),
        out_shape=jax.ShapeDtypeStruct((n_batch * ctx_len, MIXER_W), BF16),
        compiler_params=_cparams(("arbitrary",)),
        name="attn_ctx",
    )(sink, z, z, z, gain)


def _rope_tables(seq, grid_w):
    nf = HEAD_DIM // 4
    inv = ROPE_THETA ** (-jnp.arange(nf, dtype=F32) / nf)
    t = jnp.arange(seq)
    row = (t // grid_w).astype(F32)[:, None] * inv[None, :]
    col = (t % grid_w).astype(F32)[:, None] * inv[None, :]
    cos_h = jnp.concatenate([jnp.cos(row), jnp.cos(row), jnp.cos(col), jnp.cos(col)], axis=-1)
    sin_h = jnp.concatenate([-jnp.sin(row), jnp.sin(row), -jnp.sin(col), jnp.sin(col)], axis=-1)
    return jnp.tile(cos_h, (1, 2)), jnp.tile(sin_h, (1, 2))


def _hy_conv_kernel(z_ref, zp_ref, zn_ref, w_ref, b_ref, g1_ref, g2_ref, u_ref):
    i = pl.program_id(1)
    tl = z_ref.shape[0]
    has_prev = (i > 0).astype(F32)
    has_next = (i < pl.num_programs(1) - 1).astype(F32)
    row = lax.broadcasted_iota(jnp.int32, (tl, MIXER_W), 0)
    outs = (g1_ref, g2_ref, u_ref)
    for part in range(3):
        cs = slice(part * MIXER_W, (part + 1) * MIXER_W)
        zc = z_ref[:, cs]
        prev_row = zp_ref[POOL_HALO - 1:POOL_HALO, cs] * has_prev
        next_row = zn_ref[0:1, cs] * has_next
        zm = jnp.where(row == 0, prev_row, pltpu.roll(zc, 1, axis=0))
        zp = jnp.where(row == tl - 1, next_row, pltpu.roll(zc, tl - 1, axis=0))
        y = b_ref[:, cs] + zm * w_ref[0:1, cs] + zc * w_ref[1:2, cs] + zp * w_ref[2:3, cs]
        outs[part][...] = y.astype(outs[part].dtype)


def _hy_conv(z, conv_w, conv_b, n_seq, seq, row0):
    tl = min(512, seq)
    nt = seq // tl
    blk0 = row0 // tl
    hb0 = row0 // POOL_HALO
    hpt = tl // POOL_HALO
    last_halo = (row0 + n_seq * seq) // POOL_HALO - 1
    wide = 3 * MIXER_W

    def prev_map(s, i):
        return (jnp.maximum(hb0 + (s * nt + i) * hpt - 1, 0), COL_HY)

    def next_map(s, i):
        return (jnp.minimum(hb0 + (s * nt + i + 1) * hpt, last_halo), COL_HY)

    out_spec = pl.BlockSpec((tl, MIXER_W), lambda s, i: (i, s))
    shp = (seq, n_seq * MIXER_W)
    return pl.pallas_call(
        _hy_conv_kernel,
        grid=(n_seq, nt),
        in_specs=[
            pl.BlockSpec((tl, wide), lambda s, i: (blk0 + s * nt + i, COL_HY)),
            pl.BlockSpec((POOL_HALO, wide), prev_map),
            pl.BlockSpec((POOL_HALO, wide), next_map),
            pl.BlockSpec((3, wide), lambda s, i: (0, 0)),
            pl.BlockSpec((1, wide), lambda s, i: (0, 0)),
        ],
        out_specs=[out_spec, out_spec, out_spec],
        out_shape=[jax.ShapeDtypeStruct(shp, F32), jax.ShapeDtypeStruct(shp, F32),
                   jax.ShapeDtypeStruct(shp, BF16)],
        compiler_params=_cparams(("arbitrary", "arbitrary")),
        name="hy_conv",
    )(z, z, z, conv_w, conv_b)


def _hy_filter_kernel(feat_ref, dec_ref, w1_ref, b1_ref, f1_ref, w2_ref, b2_ref, f2_ref, w3_ref, o_ref):
    tl = feat_ref.shape[0]
    h = jnp.dot(feat_ref[...], w1_ref[...], precision=HI, preferred_element_type=F32) + b1_ref[...]
    h = jnp.sin(f1_ref[...] * h)
    h = jnp.dot(h, w2_ref[...], precision=HI, preferred_element_type=F32) + b2_ref[...]
    h = jnp.sin(f2_ref[...] * h)
    filt = jnp.dot(h, w3_ref[...], precision=HI, preferred_element_type=F32)
    dec = dec_ref[...]
    t = pl.program_id(0) * tl + lax.broadcasted_iota(jnp.int32, (tl, 1), 0)
    not_first = (t > 0).astype(F32)
    for o in range(2):
        hf = filt[:, (2 * o) * MIXER_W:(2 * o + 1) * MIXER_W] * dec
        hb = filt[:, (2 * o + 1) * MIXER_W:(2 * o + 2) * MIXER_W] * dec * not_first
        o_ref[:, o * MIXER_W:(o + 1) * MIXER_W] = (hf + hb).astype(BF16)
        o_ref[:, (2 + o) * MIXER_W:(3 + o) * MIXER_W] = (hb - hf).astype(BF16)


def _hy_filters(feat, decay, w1p, b1, f1, w2, b2, f2, w3):
    seq = feat.shape[0]
    tl = min(512, seq)
    hid = w2.shape[0]
    full = lambda shape: pl.BlockSpec(shape, lambda i: tuple(0 for _ in shape))
    return pl.pallas_call(
        _hy_filter_kernel,
        grid=(seq // tl,),
        in_specs=[
            pl.BlockSpec((tl, HY_EMB_PAD), lambda i: (i, 0)),
            pl.BlockSpec((tl, MIXER_W), lambda i: (i, 0)),
            full((HY_EMB_PAD, hid)), full((1, hid)), full((1, hid)),
            full((hid, hid)), full((1, hid)), full((1, hid)),
            full((hid, 4 * MIXER_W)),
        ],
        out_specs=pl.BlockSpec((tl, 4 * MIXER_W), lambda i: (i, 0)),
        out_shape=jax.ShapeDtypeStruct((seq, 4 * MIXER_W), BF16),
        compiler_params=_cparams(("arbitrary",)),
        name="hy_filters",
    )(feat, decay, w1p, b1, f1, w2, b2, f2, w3)


def _hy_kspec_kernel(fw_ref, ab_ref, o_ref, *, inv_len):
    hb = fw_ref.shape[0] // 2
    half = ab_ref.shape[1] // 2
    kr = jnp.dot(fw_ref[0:hb, :], ab_ref[:, 0:half], preferred_element_type=F32)
    ki = jnp.dot(fw_ref[hb:, :], ab_ref[:, half:], preferred_element_type=F32)
    o_ref[0:hb, :] = kr * inv_len
    o_ref[hb:, :] = ki * inv_len


def _hy_kspec(fw, ab, tm):
    two_l, seq = fw.shape
    return pl.pallas_call(
        functools.partial(_hy_kspec_kernel, inv_len=1.0 / seq),
        grid=(two_l // tm,),
        in_specs=[
            pl.BlockSpec((tm, seq), lambda i: (i, 0)),
            pl.BlockSpec((seq, 4 * MIXER_W), lambda i: (0, 0)),
        ],
        out_specs=pl.BlockSpec((tm, 2 * MIXER_W), lambda i: (i, 0)),
        out_shape=jax.ShapeDtypeStruct((two_l, 2 * MIXER_W), F32),
        compiler_params=_cparams(("arbitrary",)),
        name="hy_kspec",
    )(fw, ab)


def _hy_fwd_kernel(fw_ref, u_ref, k_ref, o_ref):
    hb = fw_ref.shape[0] // 2
    reps = u_ref.shape[1] // MIXER_W
    pq = jnp.dot(fw_ref[...], u_ref[...], preferred_element_type=F32)
    p, q = pq[0:hb, :], pq[hb:, :]
    kr = jnp.concatenate([k_ref[0:hb, :]] * reps, axis=-1)
    ki = jnp.concatenate([k_ref[hb:, :]] * reps, axis=-1)
    o_ref[0:hb, :] = (p * kr + q * ki).astype(BF16)
    o_ref[hb:, :] = (q * kr - p * ki).astype(BF16)


def _hy_fwd(fw, u, kspec, order, tm, tn):
    two_l, seq = fw.shape
    ncol = u.shape[1]
    return pl.pallas_call(
        _hy_fwd_kernel,
        grid=(ncol // tn, two_l // tm),
        in_specs=[
            pl.BlockSpec((tm, seq), lambda j, i: (i, 0)),
            pl.BlockSpec((seq, tn), lambda j, i: (0, j)),
            pl.BlockSpec((tm, MIXER_W), lambda j, i: (i, order)),
        ],
        out_specs=pl.BlockSpec((tm, tn), lambda j, i: (i, j)),
        out_shape=jax.ShapeDtypeStruct((two_l, ncol), BF16),
        compiler_params=_cparams(("arbitrary", "arbitrary")),
        name="hy_fwd",
    )(fw, u, kspec)


def _hy_inv_kernel(inv_ref, y_ref, u_ref, gate_ref, skip_ref, *rest, final):
    reps = u_ref.shape[1] // MIXER_W
    conv = jnp.dot(inv_ref[...], y_ref[...], preferred_element_type=F32)
    skip = jnp.concatenate([skip_ref[...]] * reps, axis=-1)
    out = gate_ref[...] * (conv + u_ref[...].astype(F32) * skip)
    if final:
        g_ref, o_ref = rest
        for r in range(reps):
            o_ref[r] = _rms_gain(out[:, r * MIXER_W:(r + 1) * MIXER_W], g_ref[...]).astype(BF16)
    else:
        (o_ref,) = rest
        o_ref[...] = out.astype(BF16)


def _hy_inv(inv, yspec, u, gate, skip, gain, final, tm, tn):
    seq, two_l = inv.shape
    ncol = u.shape[1]
    reps = tn // MIXER_W
    in_specs = [
        pl.BlockSpec((tm, two_l), lambda j, i: (i, 0)),
        pl.BlockSpec((two_l, tn), lambda j, i: (0, j)),
        pl.BlockSpec((tm, tn), lambda j, i: (i, j)),
        pl.BlockSpec((tm, tn), lambda j, i: (i, j)),
        pl.BlockSpec((1, MIXER_W), lambda j, i: (0, 0)),
    ]
    args = [inv, yspec, u, gate, skip]
    if final:
        in_specs.append(pl.BlockSpec((1, MIXER_W), lambda j, i: (0, 0)))
        args.append(gain)
        out_spec = pl.BlockSpec((reps, tm, MIXER_W), lambda j, i: (j, i, 0))
        out_shape = jax.ShapeDtypeStruct((ncol // MIXER_W, seq, MIXER_W), BF16)
    else:
        out_spec = pl.BlockSpec((tm, tn), lambda j, i: (i, j))
        out_shape = jax.ShapeDtypeStruct((seq, ncol), BF16)
    return pl.pallas_call(
        functools.partial(_hy_inv_kernel, final=final),
        grid=(ncol // tn, seq // tm),
        in_specs=in_specs,
        out_specs=out_spec,
        out_shape=out_shape,
        compiler_params=_cparams(("arbitrary", "arbitrary")),
        name="hy_inv",
    )(*args)


def _hy_tables(seq, tm):
    k = jnp.arange(seq, dtype=jnp.int32)
    m = ((2 * k + 1)[:, None] * k[None, :]) % (4 * seq)
    ang = m.astype(F32) * (2.0 * math.pi / (4 * seq))
    hb = tm // 2
    nb = seq // hb
    cs = jnp.stack([jnp.cos(ang).reshape(nb, hb, seq), jnp.sin(ang).reshape(nb, hb, seq)], axis=1)
    fw = cs.reshape(2 * seq, seq).astype(BF16)
    return fw, fw.T


def _hy_features(seq):
    t = jnp.linspace(0.0, 1.0, seq, dtype=F32)[:, None]
    bands = (HY_EMB - 1) // 2
    freqs = jnp.linspace(1e-4, bands - 1, bands, dtype=F32)[None, :]
    ang = (2.0 * math.pi / seq) * jnp.arange(seq, dtype=F32)[:, None] * freqs
    feat = jnp.concatenate([t, jnp.cos(ang), -jnp.sin(ang)], -1)
    feat = jnp.pad(feat, ((0, 0), (0, HY_EMB_PAD - HY_EMB)))
    deltas = jnp.abs(jnp.linspace(math.log(HY_TARGET) / HY_SLOW, math.log(HY_TARGET) / HY_FAST,
                                  MIXER_W, dtype=F32))
    return feat, jnp.exp(-t * deltas[None, :])


def _hyena_mix(z, lp, gain, n_seq, seq, row0):
    tm_f = min(512, 2 * seq)
    tm_i = min(256, seq)
    ncol = n_seq * MIXER_W
    tn = min(1024, ncol)
    fw, inv = _hy_tables(seq, tm_f)
    feat, decay = _hy_features(seq)
    ab = _hy_filters(feat, decay, lp["hy_w1p"], lp["hy_b1"], lp["hy_f1"], lp["hy_w2"], lp["hy_b2"],
                     lp["hy_f2"], lp["hy_w3"])
    kspec = _hy_kspec(fw, ab, tm_f)
    g1, g2, u = _hy_conv(z, lp["hy_conv_w"], lp["hy_conv_b"], n_seq, seq, row0)
    y0 = _hy_fwd(fw, u, kspec, 0, tm_f, tn)
    u1 = _hy_inv(inv, y0, u, g1, lp["hy_skip"][0:1], None, False, tm_i, tn)
    y1 = _hy_fwd(fw, u1, kspec, 1, tm_f, tn)
    out = _hy_inv(inv, y1, u1, g2, lp["hy_skip"][1:2], gain, True, tm_i, tn)
    return out.reshape(n_seq * seq, MIXER_W)


def _gelu(x):
    c = math.sqrt(2.0 / math.pi)
    return 0.5 * x * (1.0 + jnp.tanh(c * (x + 0.044715 * (x * x * x))))


def _gmlp_kernel(u_ref, v_ref, lg_ref, lb_ref, ws_ref, bs_ref, g_ref, o_ref):
    tl = u_ref.shape[0]
    gw = MIXER_W // GMLP_GROUPS
    lane_grp = lax.broadcasted_iota(jnp.int32, (GMLP_CHUNK, MIXER_W), 1) // gw
    for c in range(tl // GMLP_CHUNK):
        rows = slice(c * GMLP_CHUNK, (c + 1) * GMLP_CHUNK)
        v = _ln(_gelu(v_ref[rows, :])) * lg_ref[...] + lb_ref[...]
        vb = v.astype(BF16)
        stacked = jnp.concatenate(
            [jnp.where(lane_grp == g, vb, jnp.zeros_like(vb)) for g in range(GMLP_GROUPS)], axis=0)
        s = jnp.dot(ws_ref[...], stacked, preferred_element_type=F32) + bs_ref[...]
        y = _gelu(u_ref[rows, :]) * s
        o_ref[rows, :] = _rms_gain(y, g_ref[...]).astype(BF16)


def _gmlp_mix(z, ln_g, ln_b, ws_cat, bs_mat, gain, n_rows):
    tl = 512
    full = lambda shape: pl.BlockSpec(shape, lambda i: tuple(0 for _ in shape))
    return pl.pallas_call(
        _gmlp_kernel,
        grid=(n_rows // tl,),
        in_specs=[
            pl.BlockSpec((tl, MIXER_W), lambda i: (i, COL_GU)),
            pl.BlockSpec((tl, MIXER_W), lambda i: (i, COL_GV)),
            full((1, MIXER_W)), full((1, MIXER_W)),
            full((GMLP_CHUNK, GMLP_GROUPS * GMLP_CHUNK)),
            full((GMLP_CHUNK, MIXER_W)),
            full((1, MIXER_W)),
        ],
        out_specs=pl.BlockSpec((tl, MIXER_W), lambda i: (i, 0)),
        out_shape=jax.ShapeDtypeStruct((n_rows, MIXER_W), BF16),
        compiler_params=_cparams(("arbitrary",)),
        name="gmlp_mix",
    )(z, z, ln_g, ln_b, ws_cat, bs_mat, gain)


def _merge_kernel(yp_ref, ya_ref, yh_ref, yg_ref, w_ref, x_ref, g1_ref, lg_ref, lb_ref, sh_ref, sc_ref,
                  wr_ref, xo_ref, h_ref, lo_ref, *, alpha):
    y = None
    for gi, y_ref in enumerate((yp_ref, ya_ref, yh_ref, yg_ref)):
        part = jnp.dot(y_ref[...], w_ref[gi * MIXER_W:(gi + 1) * MIXER_W, :], preferred_element_type=F32)
        y = part if y is None else y + part
    x1 = _ln(alpha * x_ref[...] + g1_ref[...] * y) * lg_ref[...] + lb_ref[...]
    xo_ref[...] = x1
    h = _ln(x1) * (1.0 + sc_ref[...]) + sh_ref[...]
    h_ref[...] = h.astype(BF16)
    lo_ref[...] = jnp.dot(h, wr_ref[...], precision=HI, preferred_element_type=F32)


def _merge(ys, w_out_bf, xa, mod3, ln_g, ln_b, wr_pad, n_rows, n_batch, seq, alpha):
    d = xa.shape[1]
    tm = 256

    def mrow(i):
        return jnp.minimum((i * tm) // seq, n_batch)

    ymix = pl.BlockSpec((tm, MIXER_W), lambda i: (i, 0))
    rowvec = pl.BlockSpec((1, d), lambda i: (0, 0))
    return pl.pallas_call(
        functools.partial(_merge_kernel, alpha=alpha),
        grid=(n_rows // tm,),
        in_specs=[
            ymix, ymix, ymix, ymix,
            pl.BlockSpec((d, d), lambda i: (0, 0)),
            pl.BlockSpec((tm, d), lambda i: (i, 0)),
            pl.BlockSpec((None, 1, d), lambda i: (mrow(i), 0, 2)),
            rowvec, rowvec,
            pl.BlockSpec((None, 1, d), lambda i: (mrow(i), 0, 3)),
            pl.BlockSpec((None, 1, d), lambda i: (mrow(i), 0, 4)),
            pl.BlockSpec((d, LOGIT_PAD), lambda i: (0, 0)),
        ],
        out_specs=[
            pl.BlockSpec((tm, d), lambda i: (i, 0)),
            pl.BlockSpec((tm, d), lambda i: (i, 0)),
            pl.BlockSpec((tm, LOGIT_PAD), lambda i: (i, 0)),
        ],
        out_shape=[
            jax.ShapeDtypeStruct((n_rows, d), F32),
            jax.ShapeDtypeStruct((n_rows, d), BF16),
            jax.ShapeDtypeStruct((n_rows, LOGIT_PAD), F32),
        ],
        compiler_params=_cparams(("arbitrary",)),
        name="merge",
    )(*ys, w_out_bf, xa, mod3, ln_g, ln_b, mod3, mod3, wr_pad)


def _moe_kernel(be_ref, nu_ref, x_ref, sw_ref, wg_ref, wu_ref, wd_ref, o_ref):
    i = pl.program_id(0)

    @pl.when(i < nu_ref[0])
    def _():
        x = x_ref[...]
        g = jnp.dot(x, wg_ref[...], preferred_element_type=F32)
        u = jnp.dot(x, wu_ref[...], preferred_element_type=F32)
        a = (g * jax.nn.sigmoid(g) * u).astype(BF16)
        y = jnp.dot(a, wd_ref[...], preferred_element_type=F32)
        o_ref[...] = (y * sw_ref[...]).astype(o_ref.dtype)

    @pl.when(i >= nu_ref[0])
    def _():
        o_ref[...] = jnp.zeros_like(o_ref)


def _moe_experts(xs, slot_w, block_e, n_used, wg, wu, wd):
    n_slots, d = xs.shape
    de = wg.shape[2]
    n_blocks = n_slots // MOE_BLOCK
    grid_spec = pltpu.PrefetchScalarGridSpec(
        num_scalar_prefetch=2,
        grid=(n_blocks,),
        in_specs=[
            pl.BlockSpec((MOE_BLOCK, d), lambda i, be, nu: (i, 0)),
            pl.BlockSpec((MOE_BLOCK, 1), lambda i, be, nu: (i, 0)),
            pl.BlockSpec((None, d, de), lambda i, be, nu: (be[i], 0, 0)),
            pl.BlockSpec((None, d, de), lambda i, be, nu: (be[i], 0, 0)),
            pl.BlockSpec((None, de, d), lambda i, be, nu: (be[i], 0, 0)),
        ],
        out_specs=pl.BlockSpec((MOE_BLOCK, d), lambda i, be, nu: (i, 0)),
    )
    return pl.pallas_call(
        _moe_kernel,
        grid_spec=grid_spec,
        out_shape=jax.ShapeDtypeStruct((n_slots, d), BF16),
        compiler_params=_cparams(("arbitrary",)),
        name="moe_experts",
    )(block_e, n_used, xs, slot_w, wg, wu, wd)


def _route(logits, router_bias):
    t = logits.shape[0]
    per = N_EXPERTS // N_EXPERT_GROUPS
    s = jax.nn.sigmoid(logits[:, :N_EXPERTS])
    sel = s + router_bias.astype(F32)
    grp = lax.top_k(sel.reshape(t, N_EXPERT_GROUPS, per), 2)[0].sum(-1)
    best = jnp.argmax(grp, -1)
    in_grp = (jnp.arange(N_EXPERTS) // per)[None, :] == best[:, None]
    _, idx = lax.top_k(jnp.where(in_grp, sel, -jnp.inf), TOP_K)
    w = jnp.take_along_axis(s, idx, -1)
    return idx, w / jnp.sum(w, -1, keepdims=True)


def _dispatch_plan(idx, gate):
    t = idx.shape[0]
    a = t * TOP_K
    flat_e = idx.reshape(a)
    onehot = (flat_e[:, None] == jnp.arange(N_EXPERTS)[None, :]).astype(jnp.int32)
    csum = jnp.cumsum(onehot, axis=0)
    rank = jnp.take_along_axis(csum, flat_e[:, None], axis=1)[:, 0] - 1
    counts = csum[-1]
    padded = (counts + MOE_BLOCK - 1) // MOE_BLOCK * MOE_BLOCK
    pad_end = jnp.cumsum(padded)
    pad_start = pad_end - padded
    dest = pad_start[flat_e] + rank
    n_blocks = (a + N_EXPERTS * (MOE_BLOCK - 1) + MOE_BLOCK - 1) // MOE_BLOCK
    n_slots = n_blocks * MOE_BLOCK
    flat_tok = jnp.repeat(jnp.arange(t, dtype=jnp.int32), TOP_K)
    slot_tok = jnp.full((n_slots,), t, jnp.int32).at[dest].set(flat_tok)
    slot_w = jnp.zeros((n_slots,), F32).at[dest].set(gate.reshape(a))
    block_e = jnp.minimum(
        jnp.searchsorted(pad_end, jnp.arange(n_blocks) * MOE_BLOCK, side="right"), N_EXPERTS - 1)
    n_used = (pad_end[-1] // MOE_BLOCK).astype(jnp.int32).reshape(1)
    return slot_tok, slot_w, block_e.astype(jnp.int32), n_used, dest.reshape(t, TOP_K)


def _final_kernel(x_ref, ya_ref, yb_ref, g2_ref, lg_ref, lb_ref, o_ref, *, alpha):
    f = ya_ref[...].astype(F32) + yb_ref[...].astype(F32)
    o_ref[...] = _ln(alpha * x_ref[...] + g2_ref[...] * f) * lg_ref[...] + lb_ref[...]


def _final_norm(x1, ya, yb, mod3, ln_g, ln_b, n_batch, seq, alpha):
    n_rows, d = x1.shape
    tm = 512

    def mrow(i):
        return jnp.minimum((i * tm) // seq, n_batch)

    tile = pl.BlockSpec((tm, d), lambda i: (i, 0))
    rowvec = pl.BlockSpec((1, d), lambda i: (0, 0))
    return pl.pallas_call(
        functools.partial(_final_kernel, alpha=alpha),
        grid=(n_rows // tm,),
        in_specs=[tile, tile, tile, pl.BlockSpec((None, 1, d), lambda i: (mrow(i), 0, 5)), rowvec, rowvec],
        out_specs=tile,
        out_shape=jax.ShapeDtypeStruct((n_rows, d), F32),
        compiler_params=_cparams(("arbitrary",)),
        name="final_norm",
    )(x1, ya, yb, mod3, ln_g, ln_b)


def _permute_in_cols(w):
    pool, q, k, v, hy, gm = jnp.split(w, (512, 1024, 1152, 1280, 2816), axis=-1)
    return jnp.concatenate([hy, pool, q, gm, k, v], axis=-1)


def kernel(x, c, ctx, c_ctx, w_ada, b_ada, w_in, w_out, mix_norm_g, pool_w, pool_scale, attn_sink,
           hy_conv_w, hy_conv_b, hy_w1, hy_b1, hy_f1, hy_w2, hy_b2, hy_f2, hy_w3, hy_skip,
           gm_ln_g, gm_ln_b, gm_ws, gm_bs, ln1_g, ln1_b, ln2_g, ln2_b, w_router, router_bias,
           w_gate, w_up, w_down):
    n_batch, seq, d = x.shape
    ctx_len = ctx.shape[1]
    depth = w_in.shape[0]
    grid_w = 64
    alpha = (2 * depth) ** 0.25
    n_lat = n_batch * seq
    n_ctx = n_batch * ctx_len

    xa = jnp.concatenate([x.reshape(n_lat, d), ctx.reshape(n_ctx, d)], axis=0)
    cvec = jnp.concatenate([c, c_ctx[None, :], jnp.zeros((8 - n_batch - 1, d), F32)], axis=0)
    mods = _ada_mods(cvec, w_ada, b_ada)
    cos_t, sin_t = _rope_tables(seq, grid_w)
    wr_pad = jnp.pad(w_router, ((0, 0), (0, LOGIT_PAD - N_EXPERTS)))
    row = lambda v: v.reshape(1, -1)

    for layer in range(depth):
        last = layer == depth - 1
        mod3 = mods[layer].reshape(8, 1, 6 * d)
        gains = mix_norm_g[layer].reshape(4, 1, MIXER_W)
        lp = {
            "hy_conv_w": hy_conv_w[layer], "hy_conv_b": row(hy_conv_b[layer]),
            "hy_w1p": jnp.pad(hy_w1[layer], ((0, HY_EMB_PAD - HY_EMB), (0, 0))),
            "hy_b1": row(hy_b1[layer]), "hy_f1": row(hy_f1[layer]),
            "hy_w2": hy_w2[layer], "hy_b2": row(hy_b2[layer]), "hy_f2": row(hy_f2[layer]),
            "hy_w3": hy_w3[layer], "hy_skip": hy_skip[layer],
        }
        z = _in_proj(xa, mod3, _permute_in_cols(w_in[layer]).astype(BF16), n_batch, seq)

        pool_bf = pool_w[layer].astype(BF16)
        ps = row(pool_scale[layer])
        ws_cat = jnp.transpose(gm_ws[layer], (1, 0, 2)).reshape(GMLP_CHUNK, GMLP_GROUPS * GMLP_CHUNK)
        bs_mat = jnp.repeat(gm_bs[layer].T, MIXER_W // GMLP_GROUPS, axis=1)
        n_rows = n_lat if last else n_lat + n_ctx

        y_pool = _pool_mix(z, pool_bf, ps, gains[0], n_batch, seq, 0)
        y_attn = _attn_lat(z, attn_sink[layer], cos_t, sin_t, gains[1], n_batch, seq, ctx_len)
        y_hy = _hyena_mix(z, lp, gains[2], n_batch, seq, 0)
        y_gm = _gmlp_mix(z, row(gm_ln_g[layer]), row(gm_ln_b[layer]), ws_cat.astype(BF16), bs_mat,
                         gains[3], n_rows)
        if not last:
            y_pool = jnp.concatenate(
                [y_pool, _pool_mix(z, pool_bf, ps, gains[0], n_batch, ctx_len, n_lat // ctx_len)], axis=0)
            y_attn = jnp.concatenate(
                [y_attn, _attn_ctx(z, attn_sink[layer], gains[1], n_batch, seq, ctx_len)], axis=0)
            y_hy = jnp.concatenate([y_hy, _hyena_mix(z, lp, gains[2], n_batch, ctx_len, n_lat)], axis=0)

        x1, h2, logits = _merge((y_pool, y_attn, y_hy, y_gm), w_out[layer].astype(BF16), xa, mod3,
                                row(ln1_g[layer]), row(ln1_b[layer]), wr_pad, n_rows, n_batch, seq, alpha)

        idx, gate = _route(logits, router_bias)
        slot_tok, slot_w, block_e, n_used, dest = _dispatch_plan(idx, gate)
        h_pad = jnp.concatenate([h2, jnp.zeros((1, d), BF16)], axis=0)
        ys = _moe_experts(h_pad[slot_tok], slot_w[:, None], block_e, n_used,
                          w_gate[layer].astype(BF16), w_up[layer].astype(BF16), w_down[layer].astype(BF16))
        xa = _final_norm(x1, ys[dest[:, 0]], ys[dest[:, 1]], mod3, row(ln2_g[layer]), row(ln2_b[layer]),
                         n_batch, seq, alpha)
    return xa.reshape(n_batch, seq, d)
```

```python
import functools
import math

import jax
import jax.numpy as jnp
from jax import lax
from jax.experimental import pallas as pl
from jax.experimental.pallas import tpu as pltpu

F32 = jnp.float32
BF16 = jnp.bfloat16
HI = lax.Precision.HIGHEST

LN_EPS = 1e-6
MIXER_W = 512
HEAD_DIM = 64
N_HEADS = 8
N_KV_HEADS = 2
KV_GROUP = 4
KV_W = 128
WINDOW = 128
ROPE_THETA = 10000.0
POOL_WINDOWS = (2, 4, 8, 16)
POOL_GROUP = 128
POOL_HALO = 8
GMLP_CHUNK = 128
GMLP_GROUPS = 8
HY_EMB = 33
HY_EMB_PAD = 128
HY_TARGET = 1e-2
HY_FAST = 0.3
HY_SLOW = 1.5
N_EXPERTS = 32
N_EXPERT_GROUPS = 8
TOP_K = 2
D_EXPERT = 768
MOE_BLOCK = 256
LOGIT_PAD = 128

COL_HY = 0
COL_POOL = 3
COL_Q = 4
COL_GU = 5
COL_GV = 6
COL_K = 28
COL_V = 29
IN_W = 3840

VMEM_LIMIT = 56 * 1024 * 1024


def _cparams(sem):
    return pltpu.CompilerParams(dimension_semantics=sem, vmem_limit_bytes=VMEM_LIMIT)


def _ln(x):
    mu = jnp.mean(x, axis=-1, keepdims=True)
    xc = x - mu
    var = jnp.mean(xc * xc, axis=-1, keepdims=True)
    return xc * lax.rsqrt(var + LN_EPS)


def _rms_gain(y, g):
    return y * lax.rsqrt(jnp.mean(y * y, axis=-1, keepdims=True) + LN_EPS) * g


def _row_chunks(n_rows, chunk, body):
    n = n_rows // chunk
    if n == 1:
        body(0)
        return

    def step(i, carry):
        body(pl.multiple_of(i * chunk, chunk))
        return carry

    lax.fori_loop(0, n, step, 0)


def _ada_kernel(c_ref, w_ref, b_ref, o_ref):
    c = c_ref[...]
    s = c * jax.nn.sigmoid(c)
    o_ref[...] = jnp.dot(s, w_ref[...], precision=HI, preferred_element_type=F32) + b_ref[...]


def _ada_mods(cvec, w_ada, b_ada):
    depth, d, n = w_ada.shape
    tn = 1024
    return pl.pallas_call(
        _ada_kernel,
        grid=(depth, n // tn),
        in_specs=[
            pl.BlockSpec((8, d), lambda l, j: (0, 0)),
            pl.BlockSpec((None, d, tn), lambda l, j: (l, 0, j)),
            pl.BlockSpec((None, 1, tn), lambda l, j: (l, 0, j)),
        ],
        out_specs=pl.BlockSpec((None, 8, tn), lambda l, j: (l, 0, j)),
        out_shape=jax.ShapeDtypeStruct((depth, 8, n), F32),
        compiler_params=_cparams(("arbitrary", "arbitrary")),
        name="ada_mods",
    )(cvec, w_ada, b_ada.reshape(depth, 1, n))


def _inproj_kernel(x_ref, sh_ref, sc_ref, w_ref, o_ref, h_scr):
    @pl.when(pl.program_id(1) == 0)
    def _():
        sh = sh_ref[...]
        sc1 = 1.0 + sc_ref[...]

        def body(r0):
            x = x_ref[pl.ds(r0, 256), :]
            h_scr[pl.ds(r0, 256), :] = (_ln(x) * sc1 + sh).astype(BF16)

        _row_chunks(x_ref.shape[0], 256, body)

    o_ref[...] = jnp.dot(h_scr[...], w_ref[...], preferred_element_type=F32)


def _in_proj(xa, mod3, w_bf, n_batch, seq):
    t, d = xa.shape
    n = w_bf.shape[1]
    tn = 768
    tm = next(m for m in (1024, 512, 256) if t % m == 0 and seq % m == 0)

    def mrow(i):
        return jnp.minimum((i * tm) // seq, n_batch)

    return pl.pallas_call(
        _inproj_kernel,
        grid=(t // tm, n // tn),
        in_specs=[
            pl.BlockSpec((tm, d), lambda i, j: (i, 0)),
            pl.BlockSpec((None, 1, d), lambda i, j: (mrow(i), 0, 0)),
            pl.BlockSpec((None, 1, d), lambda i, j: (mrow(i), 0, 1)),
            pl.BlockSpec((d, tn), lambda i, j: (0, j)),
        ],
        out_specs=pl.BlockSpec((tm, tn), lambda i, j: (i, j)),
        out_shape=jax.ShapeDtypeStruct((t, n), F32),
        scratch_shapes=[pltpu.VMEM((tm, d), BF16)],
        compiler_params=_cparams(("arbitrary", "arbitrary")),
        name="in_proj",
    )(xa, mod3, mod3, w_bf)


def _pool_kernel(z_ref, w_ref, ps_ref, g_ref, o_ref, buf):
    seq = z_ref.shape[0]
    rc = min(256, seq)
    zeros = jnp.zeros((POOL_HALO, MIXER_W), F32)
    buf[0:POOL_HALO, :] = zeros
    buf[seq + POOL_HALO:seq + 2 * POOL_HALO, :] = zeros

    def fill(r0):
        buf[pl.ds(r0 + POOL_HALO, rc), :] = z_ref[pl.ds(r0, rc), :]

    _row_chunks(seq, rc, fill)

    def body(r0):
        win = buf[pl.ds(r0, rc + 2 * POOL_HALO), :]
        t = r0 + lax.broadcasted_iota(jnp.int32, (rc, 1), 0)
        outs = []
        for gi, w in enumerate(POOL_WINDOWS):
            lanes = slice(gi * POOL_GROUP, (gi + 1) * POOL_GROUP)
            acc = None
            for j in range(-w // 2, w // 2):
                piece = win[POOL_HALO + j:POOL_HALO + j + rc, lanes]
                acc = piece if acc is None else acc + piece
            cnt = (jnp.minimum(t + w // 2, seq) - jnp.maximum(t - w // 2, 0)).astype(F32)
            dlt = acc / cnt - win[POOL_HALO:POOL_HALO + rc, lanes]
            outs.append(jnp.dot(dlt.astype(BF16), w_ref[gi], preferred_element_type=F32))
        y = jnp.concatenate(outs, axis=-1) * ps_ref[...]
        o_ref[pl.ds(r0, rc), :] = _rms_gain(y, g_ref[...]).astype(BF16)

    _row_chunks(seq, rc, body)


def _pool_mix(z, pool_w_bf, pool_scale, gain, n_seq, seq, row_blk0):
    return pl.pallas_call(
        _pool_kernel,
        grid=(n_seq,),
        in_specs=[
            pl.BlockSpec((seq, MIXER_W), lambda s: (row_blk0 + s, COL_POOL)),
            pl.BlockSpec((4, POOL_GROUP, POOL_GROUP), lambda s: (0, 0, 0)),
            pl.BlockSpec((1, MIXER_W), lambda s: (0, 0)),
            pl.BlockSpec((1, MIXER_W), lambda s: (0, 0)),
        ],
        out_specs=pl.BlockSpec((seq, MIXER_W), lambda s: (s, 0)),
        out_shape=jax.ShapeDtypeStruct((n_seq * seq, MIXER_W), BF16),
        scratch_shapes=[pltpu.VMEM((seq + 2 * POOL_HALO, MIXER_W), F32)],
        compiler_params=_cparams(("arbitrary",)),
        name="pool_mix",
    )(z, pool_w_bf, pool_scale, gain)


def _rope(x, cos, sin_signed):
    lane = lax.broadcasted_iota(jnp.int32, x.shape, 1)
    partner = jnp.where((lane & 31) < 16, pltpu.roll(x, 112, axis=1), pltpu.roll(x, 16, axis=1))
    return x * cos + partner * sin_signed


def _attend(qs, sink_col, parts):
    dn = (((1,), (1,)), ((), ()))
    scores = []
    for kk, _, mask in parts:
        s = lax.dot_general(qs, kk, dn, preferred_element_type=F32)
        if mask is not None:
            s = jnp.where(mask, s, -1e30)
        scores.append(s)
    m = sink_col
    for s in scores:
        m = jnp.maximum(m, jnp.max(s, axis=-1, keepdims=True))
    den = jnp.exp(sink_col - m)
    out = None
    for s, (_, vv, _) in zip(scores, parts):
        p = jnp.exp(s - m)
        den = den + jnp.sum(p, axis=-1, keepdims=True)
        o = jnp.dot(p.astype(BF16), vv, preferred_element_type=F32)
        out = o if out is None else out + o
    return out / den


def _heads_attend(q_all, sink_ref, tq, make_parts):
    cols = [None] * N_HEADS
    for kh in range(N_KV_HEADS):
        heads = [kh * KV_GROUP + g for g in range(KV_GROUP)]
        qs = jnp.concatenate([q_all[:, h * HEAD_DIM:(h + 1) * HEAD_DIM] for h in heads], axis=0)
        sink_col = jnp.concatenate([jnp.full((tq, 1), sink_ref[h], F32) for h in heads], axis=0)
        o = _attend(qs.astype(BF16), sink_col, make_parts(kh))
        for g, h in enumerate(heads):
            cols[h] = o[g * tq:(g + 1) * tq, :]
    return jnp.concatenate(cols, axis=-1)


def _attn_lat_kernel(sink_ref, q_ref, k_ref, v_ref, kc_ref, vc_ref, cos_ref, sin_ref, g_ref, o_ref):
    tq = q_ref.shape[0]
    seq = k_ref.shape[0]
    nband = tq + 2 * WINDOW
    i = pl.program_id(1)
    q0 = pl.multiple_of(i * tq, tq)
    k0 = pl.multiple_of(jnp.clip(q0 - WINDOW, 0, seq - nband), WINDOW)

    cq = cos_ref[pl.ds(q0, tq), :]
    sq = sin_ref[pl.ds(q0, tq), :]
    q = q_ref[...]
    q_all = jnp.concatenate(
        [_rope(q[:, c * 128:(c + 1) * 128], cq, sq) for c in range(MIXER_W // 128)], axis=-1)
    q_all = q_all * (HEAD_DIM ** -0.5)

    kb = _rope(k_ref[pl.ds(k0, nband), :], cos_ref[pl.ds(k0, nband), :],
               sin_ref[pl.ds(k0, nband), :]).astype(BF16)
    vb = v_ref[pl.ds(k0, nband), :].astype(BF16)
    kc = kc_ref[...].astype(BF16)
    vc = vc_ref[...].astype(BF16)

    qpos = q0 + (lax.broadcasted_iota(jnp.int32, (KV_GROUP * tq, nband), 0) & (tq - 1))
    kpos = k0 + lax.broadcasted_iota(jnp.int32, (KV_GROUP * tq, nband), 1)
    mask = jnp.abs(qpos - kpos) <= WINDOW

    def make_parts(kh):
        hs = slice(kh * HEAD_DIM, (kh + 1) * HEAD_DIM)
        return [(kb[:, hs], vb[:, hs], mask), (kc[:, hs], vc[:, hs], None)]

    y = _heads_attend(q_all, sink_ref, tq, make_parts)
    o_ref[...] = _rms_gain(y, g_ref[...]).astype(BF16)


def _attn_ctx_kernel(sink_ref, q_ref, kc_ref, vc_ref, g_ref, o_ref):
    tq = q_ref.shape[0]
    q_all = q_ref[...] * (HEAD_DIM ** -0.5)
    kc = kc_ref[...].astype(BF16)
    vc = vc_ref[...].astype(BF16)

    def make_parts(kh):
        hs = slice(kh * HEAD_DIM, (kh + 1) * HEAD_DIM)
        return [(kc[:, hs], vc[:, hs], None)]

    y = _heads_attend(q_all, sink_ref, tq, make_parts)
    o_ref[...] = _rms_gain(y, g_ref[...]).astype(BF16)


def _attn_lat(z, sink, cos_t, sin_t, gain, n_batch, seq, ctx_len):
    tq = 128
    nq = seq // tq
    ctx_blk0 = (n_batch * seq) // ctx_len
    smem = pl.BlockSpec(memory_space=pltpu.SMEM)
    return pl.pallas_call(
        _attn_lat_kernel,
        grid=(n_batch, nq),
        in_specs=[
            smem,
            pl.BlockSpec((tq, MIXER_W), lambda b, i: (b * nq + i, COL_Q)),
            pl.BlockSpec((seq, KV_W), lambda b, i: (b, COL_K)),
            pl.BlockSpec((seq, KV_W), lambda b, i: (b, COL_V)),
            pl.BlockSpec((ctx_len, KV_W), lambda b, i: (ctx_blk0 + b, COL_K)),
            pl.BlockSpec((ctx_len, KV_W), lambda b, i: (ctx_blk0 + b, COL_V)),
            pl.BlockSpec((seq, 128), lambda b, i: (0, 0)),
            pl.BlockSpec((seq, 128), lambda b, i: (0, 0)),
            pl.BlockSpec((1, MIXER_W), lambda b, i: (0, 0)),
        ],
        out_specs=pl.BlockSpec((tq, MIXER_W), lambda b, i: (b * nq + i, 0)),
        out_shape=jax.ShapeDtypeStruct((n_batch * seq, MIXER_W), BF16),
        compiler_params=_cparams(("arbitrary", "arbitrary")),
        name="attn_lat",
    )(sink, z, z, z, z, z, cos_t, sin_t, gain)


def _attn_ctx(z, sink, gain, n_batch, seq, ctx_len):
    ctx_blk0 = (n_batch * seq) // ctx_len
    smem = pl.BlockSpec(memory_space=pltpu.SMEM)
    return pl.pallas_call(
        _attn_ctx_kernel,
        grid=(n_batch,),
        in_specs=[
            smem,
            pl.BlockSpec((ctx_len, MIXER_W), lambda b: (ctx_blk0 + b, COL_Q)),
            pl.BlockSpec((ctx_len, KV_W), lambda b: (ctx_blk0 + b, COL_K)),
            pl.BlockSpec((ctx_len, KV_W), lambda b: (ctx_blk0 + b, COL_V)),
            pl.BlockSpec((1, MIXER_W), lambda b: (0, 0)),
        ],
        out_specs=pl.BlockSpec((ctx_len, MIXER_W), lambda b: (b, 0)),
        out_shape=jax.ShapeDtypeStruct((n_batch * ctx_len, MIXER_W), BF16),
        compiler_params=_cparams(("arbitrary",)),
        name="attn_ctx",
    )(sink, z, z, z, gain)


def _rope_tables(seq, grid_w):
    nf = HEAD_DIM // 4
    inv = ROPE_THETA ** (-jnp.arange(nf, dtype=F32) / nf)
    t = jnp.arange(seq)
    row = (t // grid_w).astype(F32)[:, None] * inv[None, :]
    col = (t % grid_w).astype(F32)[:, None] * inv[None, :]
    cos_h = jnp.concatenate([jnp.cos(row), jnp.cos(row), jnp.cos(col), jnp.cos(col)], axis=-1)
    sin_h = jnp.concatenate([-jnp.sin(row), jnp.sin(row), -jnp.sin(col), jnp.sin(col)], axis=-1)
    return jnp.tile(cos_h, (1, 2)), jnp.tile(sin_h, (1, 2))


def _hy_conv_kernel(z_ref, zp_ref, zn_ref, w_ref, b_ref, g1_ref, g2_ref, u_ref):
    i = pl.program_id(1)
    tl = z_ref.shape[0]
    has_prev = (i > 0).astype(F32)
    has_next = (i < pl.num_programs(1) - 1).astype(F32)
    row = lax.broadcasted_iota(jnp.int32, (tl, MIXER_W), 0)
    outs = (g1_ref, g2_ref, u_ref)
    for part in range(3):
        cs = slice(part * MIXER_W, (part + 1) * MIXER_W)
        zc = z_ref[:, cs]
        prev_row = zp_ref[POOL_HALO - 1:POOL_HALO, cs] * has_prev
        next_row = zn_ref[0:1, cs] * has_next
        zm = jnp.where(row == 0, prev_row, pltpu.roll(zc, 1, axis=0))
        zp = jnp.where(row == tl - 1, next_row, pltpu.roll(zc, tl - 1, axis=0))
        y = b_ref[:, cs] + zm * w_ref[0:1, cs] + zc * w_ref[1:2, cs] + zp * w_ref[2:3, cs]
        outs[part][...] = y.astype(outs[part].dtype)


def _hy_conv(z, conv_w, conv_b, n_seq, seq, row0):
    tl = min(512, seq)
    nt = seq // tl
    blk0 = row0 // tl
    hb0 = row0 // POOL_HALO
    hpt = tl // POOL_HALO
    last_halo = (row0 + n_seq * seq) // POOL_HALO - 1
    wide = 3 * MIXER_W

    def prev_map(s, i):
        return (jnp.maximum(hb0 + (s * nt + i) * hpt - 1, 0), COL_HY)

    def next_map(s, i):
        return (jnp.minimum(hb0 + (s * nt + i + 1) * hpt, last_halo), COL_HY)

    out_spec = pl.BlockSpec((tl, MIXER_W), lambda s, i: (i, s))
    shp = (seq, n_seq * MIXER_W)
    return pl.pallas_call(
        _hy_conv_kernel,
        grid=(n_seq, nt),
        in_specs=[
            pl.BlockSpec((tl, wide), lambda s, i: (blk0 + s * nt + i, COL_HY)),
            pl.BlockSpec((POOL_HALO, wide), prev_map),
            pl.BlockSpec((POOL_HALO, wide), next_map),
            pl.BlockSpec((3, wide), lambda s, i: (0, 0)),
            pl.BlockSpec((1, wide), lambda s, i: (0, 0)),
        ],
        out_specs=[out_spec, out_spec, out_spec],
        out_shape=[jax.ShapeDtypeStruct(shp, F32), jax.ShapeDtypeStruct(shp, F32),
                   jax.ShapeDtypeStruct(shp, BF16)],
        compiler_params=_cparams(("arbitrary", "arbitrary")),
        name="hy_conv",
    )(z, z, z, conv_w, conv_b)


def _hy_filter_kernel(feat_ref, dec_ref, w1_ref, b1_ref, f1_ref, w2_ref, b2_ref, f2_ref, w3_ref, o_ref):
    tl = feat_ref.shape[0]
    h = jnp.dot(feat_ref[...], w1_ref[...], precision=HI, preferred_element_type=F32) + b1_ref[...]
    h = jnp.sin(f1_ref[...] * h)
    h = jnp.dot(h, w2_ref[...], precision=HI, preferred_element_type=F32) + b2_ref[...]
    h = jnp.sin(f2_ref[...] * h)
    filt = jnp.dot(h, w3_ref[...], precision=HI, preferred_element_type=F32)
    dec = dec_ref[...]
    t = pl.program_id(0) * tl + lax.broadcasted_iota(jnp.int32, (tl, 1), 0)
    not_first = (t > 0).astype(F32)
    for o in range(2):
        hf = filt[:, (2 * o) * MIXER_W:(2 * o + 1) * MIXER_W] * dec
        hb = filt[:, (2 * o + 1) * MIXER_W:(2 * o + 2) * MIXER_W] * dec * not_first
        o_ref[:, o * MIXER_W:(o + 1) * MIXER_W] = (hf + hb).astype(BF16)
        o_ref[:, (2 + o) * MIXER_W:(3 + o) * MIXER_W] = (hb - hf).astype(BF16)


def _hy_filters(feat, decay, w1p, b1, f1, w2, b2, f2, w3):
    seq = feat.shape[0]
    tl = min(512, seq)
    hid = w2.shape[0]
    full = lambda shape: pl.BlockSpec(shape, lambda i: tuple(0 for _ in shape))
    return pl.pallas_call(
        _hy_filter_kernel,
        grid=(seq // tl,),
        in_specs=[
            pl.BlockSpec((tl, HY_EMB_PAD), lambda i: (i, 0)),
            pl.BlockSpec((tl, MIXER_W), lambda i: (i, 0)),
            full((HY_EMB_PAD, hid)), full((1, hid)), full((1, hid)),
            full((hid, hid)), full((1, hid)), full((1, hid)),
            full((hid, 4 * MIXER_W)),
        ],
        out_specs=pl.BlockSpec((tl, 4 * MIXER_W), lambda i: (i, 0)),
        out_shape=jax.ShapeDtypeStruct((seq, 4 * MIXER_W), BF16),
        compiler_params=_cparams(("arbitrary",)),
        name="hy_filters",
    )(feat, decay, w1p, b1, f1, w2, b2, f2, w3)


def _hy_kspec_kernel(fw_ref, ab_ref, o_ref, *, inv_len):
    hb = fw_ref.shape[0] // 2
    half = ab_ref.shape[1] // 2
    kr = jnp.dot(fw_ref[0:hb, :], ab_ref[:, 0:half], preferred_element_type=F32)
    ki = jnp.dot(fw_ref[hb:, :], ab_ref[:, half:], preferred_element_type=F32)
    o_ref[0:hb, :] = kr * inv_len
    o_ref[hb:, :] = ki * inv_len


def _hy_kspec(fw, ab, tm):
    two_l, seq = fw.shape
    return pl.pallas_call(
        functools.partial(_hy_kspec_kernel, inv_len=1.0 / seq),
        grid=(two_l // tm,),
        in_specs=[
            pl.BlockSpec((tm, seq), lambda i: (i, 0)),
            pl.BlockSpec((seq, 4 * MIXER_W), lambda i: (0, 0)),
        ],
        out_specs=pl.BlockSpec((tm, 2 * MIXER_W), lambda i: (i, 0)),
        out_shape=jax.ShapeDtypeStruct((two_l, 2 * MIXER_W), F32),
        compiler_params=_cparams(("arbitrary",)),
        name="hy_kspec",
    )(fw, ab)


def _hy_fwd_kernel(fw_ref, u_ref, k_ref, o_ref):
    hb = fw_ref.shape[0] // 2
    reps = u_ref.shape[1] // MIXER_W
    pq = jnp.dot(fw_ref[...], u_ref[...], preferred_element_type=F32)
    p, q = pq[0:hb, :], pq[hb:, :]
    kr = jnp.concatenate([k_ref[0:hb, :]] * reps, axis=-1)
    ki = jnp.concatenate([k_ref[hb:, :]] * reps, axis=-1)
    o_ref[0:hb, :] = (p * kr + q * ki).astype(BF16)
    o_ref[hb:, :] = (q * kr - p * ki).astype(BF16)


def _hy_fwd(fw, u, kspec, order, tm, tn):
    two_l, seq = fw.shape
    ncol = u.shape[1]
    return pl.pallas_call(
        _hy_fwd_kernel,
        grid=(ncol // tn, two_l // tm),
        in_specs=[
            pl.BlockSpec((tm, seq), lambda j, i: (i, 0)),
            pl.BlockSpec((seq, tn), lambda j, i: (0, j)),
            pl.BlockSpec((tm, MIXER_W), lambda j, i: (i, order)),
        ],
        out_specs=pl.BlockSpec((tm, tn), lambda j, i: (i, j)),
        out_shape=jax.ShapeDtypeStruct((two_l, ncol), BF16),
        compiler_params=_cparams(("arbitrary", "arbitrary")),
        name="hy_fwd",
    )(fw, u, kspec)


def _hy_inv_kernel(inv_ref, y_ref, u_ref, gate_ref, skip_ref, *rest, final):
    reps = u_ref.shape[1] // MIXER_W
    conv = jnp.dot(inv_ref[...], y_ref[...], preferred_element_type=F32)
    skip = jnp.concatenate([skip_ref[...]] * reps, axis=-1)
    out = gate_ref[...] * (conv + u_ref[...].astype(F32) * skip)
    if final:
        g_ref, o_ref = rest
        for r in range(reps):
            o_ref[r] = _rms_gain(out[:, r * MIXER_W:(r + 1) * MIXER_W], g_ref[...]).astype(BF16)
    else:
        (o_ref,) = rest
        o_ref[...] = out.astype(BF16)


def _hy_inv(inv, yspec, u, gate, skip, gain, final, tm, tn):
    seq, two_l = inv.shape
    ncol = u.shape[1]
    reps = tn // MIXER_W
    in_specs = [
        pl.BlockSpec((tm, two_l), lambda j, i: (i, 0)),
        pl.BlockSpec((two_l, tn), lambda j, i: (0, j)),
        pl.BlockSpec((tm, tn), lambda j, i: (i, j)),
        pl.BlockSpec((tm, tn), lambda j, i: (i, j)),
        pl.BlockSpec((1, MIXER_W), lambda j, i: (0, 0)),
    ]
    args = [inv, yspec, u, gate, skip]
    if final:
        in_specs.append(pl.BlockSpec((1, MIXER_W), lambda j, i: (0, 0)))
        args.append(gain)
        out_spec = pl.BlockSpec((reps, tm, MIXER_W), lambda j, i: (j, i, 0))
        out_shape = jax.ShapeDtypeStruct((ncol // MIXER_W, seq, MIXER_W), BF16)
    else:
        out_spec = pl.BlockSpec((tm, tn), lambda j, i: (i, j))
        out_shape = jax.ShapeDtypeStruct((seq, ncol), BF16)
    return pl.pallas_call(
        functools.partial(_hy_inv_kernel, final=final),
        grid=(ncol // tn, seq // tm),
        in_specs=in_specs,
        out_specs=out_spec,
        out_shape=out_shape,
        compiler_params=_cparams(("arbitrary", "arbitrary")),
        name="hy_inv",
    )(*args)


def _hy_tables(seq, tm):
    k = jnp.arange(seq, dtype=jnp.int32)
    m = ((2 * k + 1)[:, None] * k[None, :]) % (4 * seq)
    ang = m.astype(F32) * (2.0 * math.pi / (4 * seq))
    hb = tm // 2
    nb = seq // hb
    cs = jnp.stack([jnp.cos(ang).reshape(nb, hb, seq), jnp.sin(ang).reshape(nb, hb, seq)], axis=1)
    fw = cs.reshape(2 * seq, seq).astype(BF16)
    return fw, fw.T


def _hy_features(seq):
    t = jnp.linspace(0.0, 1.0, seq, dtype=F32)[:, None]
    bands = (HY_EMB - 1) // 2
    freqs = jnp.linspace(1e-4, bands - 1, bands, dtype=F32)[None, :]
    ang = (2.0 * math.pi / seq) * jnp.arange(seq, dtype=F32)[:, None] * freqs
    feat = jnp.concatenate([t, jnp.cos(ang), -jnp.sin(ang)], -1)
    feat = jnp.pad(feat, ((0, 0), (0, HY_EMB_PAD - HY_EMB)))
    deltas = jnp.abs(jnp.linspace(math.log(HY_TARGET) / HY_SLOW, math.log(HY_TARGET) / HY_FAST,
                                  MIXER_W, dtype=F32))
    return feat, jnp.exp(-t * deltas[None, :])


def _hyena_mix(z, lp, gain, n_seq, seq, row0):
    tm_f = min(512, 2 * seq)
    tm_i = min(256, seq)
    ncol = n_seq * MIXER_W
    tn = min(1024, ncol)
    fw, inv = _hy_tables(seq, tm_f)
    feat, decay = _hy_features(seq)
    ab = _hy_filters(feat, decay, lp["hy_w1p"], lp["hy_b1"], lp["hy_f1"], lp["hy_w2"], lp["hy_b2"],
                     lp["hy_f2"], lp["hy_w3"])
    kspec = _hy_kspec(fw, ab, tm_f)
    g1, g2, u = _hy_conv(z, lp["hy_conv_w"], lp["hy_conv_b"], n_seq, seq, row0)
    y0 = _hy_fwd(fw, u, kspec, 0, tm_f, tn)
    u1 = _hy_inv(inv, y0, u, g1, lp["hy_skip"][0:1], None, False, tm_i, tn)
    y1 = _hy_fwd(fw, u1, kspec, 1, tm_f, tn)
    out = _hy_inv(inv, y1, u1, g2, lp["hy_skip"][1:2], gain, True, tm_i, tn)
    return out.reshape(n_seq * seq, MIXER_W)


def _gelu(x):
    c = math.sqrt(2.0 / math.pi)
    return 0.5 * x * (1.0 + jnp.tanh(c * (x + 0.044715 * (x * x * x))))


def _gmlp_kernel(u_ref, v_ref, lg_ref, lb_ref, ws_ref, bs_ref, g_ref, o_ref):
    tl = u_ref.shape[0]
    gw = MIXER_W // GMLP_GROUPS
    lane_grp = lax.broadcasted_iota(jnp.int32, (GMLP_CHUNK, MIXER_W), 1) // gw
    for c in range(tl // GMLP_CHUNK):
        rows = slice(c * GMLP_CHUNK, (c + 1) * GMLP_CHUNK)
        v = _ln(_gelu(v_ref[rows, :])) * lg_ref[...] + lb_ref[...]
        vb = v.astype(BF16)
        stacked = jnp.concatenate(
            [jnp.where(lane_grp == g, vb, jnp.zeros_like(vb)) for g in range(GMLP_GROUPS)], axis=0)
        s = jnp.dot(ws_ref[...], stacked, preferred_element_type=F32) + bs_ref[...]
        y = _gelu(u_ref[rows, :]) * s
        o_ref[rows, :] = _rms_gain(y, g_ref[...]).astype(BF16)


def _gmlp_mix(z, ln_g, ln_b, ws_cat, bs_mat, gain, n_rows):
    tl = 512
    full = lambda shape: pl.BlockSpec(shape, lambda i: tuple(0 for _ in shape))
    return pl.pallas_call(
        _gmlp_kernel,
        grid=(n_rows // tl,),
        in_specs=[
            pl.BlockSpec((tl, MIXER_W), lambda i: (i, COL_GU)),
            pl.BlockSpec((tl, MIXER_W), lambda i: (i, COL_GV)),
            full((1, MIXER_W)), full((1, MIXER_W)),
            full((GMLP_CHUNK, GMLP_GROUPS * GMLP_CHUNK)),
            full((GMLP_CHUNK, MIXER_W)),
            full((1, MIXER_W)),
        ],
        out_specs=pl.BlockSpec((tl, MIXER_W), lambda i: (i, 0)),
        out_shape=jax.ShapeDtypeStruct((n_rows, MIXER_W), BF16),
        compiler_params=_cparams(("arbitrary",)),
        name="gmlp_mix",
    )(z, z, ln_g, ln_b, ws_cat, bs_mat, gain)


def _merge_kernel(yp_ref, ya_ref, yh_ref, yg_ref, w_ref, x_ref, g1_ref, lg_ref, lb_ref, sh_ref, sc_ref,
                  wrh_ref, wrl_ref, rb_ref, xo_ref, h_ref, idx_ref, gate_ref, *, alpha):
    y = None
    for gi, y_ref in enumerate((yp_ref, ya_ref, yh_ref, yg_ref)):
        part = jnp.dot(y_ref[...], w_ref[gi * MIXER_W:(gi + 1) * MIXER_W, :], preferred_element_type=F32)
        y = part if y is None else y + part
    x1 = _ln(alpha * x_ref[...] + g1_ref[...] * y) * lg_ref[...] + lb_ref[...]
    xo_ref[...] = x1
    h = _ln(x1) * (1.0 + sc_ref[...]) + sh_ref[...]
    h_hi = h.astype(BF16)
    h_ref[...] = h_hi
    h_lo = (h - h_hi.astype(F32)).astype(BF16)
    logits = (jnp.dot(h_hi, wrh_ref[...], preferred_element_type=F32)
              + (jnp.dot(h_lo, wrh_ref[...], preferred_element_type=F32)
                 + jnp.dot(h_hi, wrl_ref[...], preferred_element_type=F32)))
    idx, gate = _route_rows(logits, rb_ref[...])
    idx_ref[...] = idx
    gate_ref[...] = gate


def _route_rows(logits, bias):
    neg = -1e30
    lane = lax.broadcasted_iota(jnp.int32, logits.shape, 1)
    lane_f = lane.astype(F32)
    valid = lane < N_EXPERTS
    per = N_EXPERTS // N_EXPERT_GROUPS
    s = jax.nn.sigmoid(logits)
    sel = jnp.where(valid, s + bias, neg)
    sh = [sel] + [pltpu.roll(sel, LOGIT_PAD - j, axis=1) for j in range(1, per)]
    pair = None
    for a in range(per):
        for b in range(a + 1, per):
            t = sh[a] + sh[b]
            pair = t if pair is None else jnp.maximum(pair, t)
    grp = jnp.where(valid & ((lane & (per - 1)) == 0), pair, neg)

    def first_max(v):
        m = jnp.max(v, axis=-1, keepdims=True)
        return jnp.min(jnp.where(v == m, lane_f, float(LOGIT_PAD)), axis=-1, keepdims=True)

    best = first_max(grp).astype(jnp.int32)
    shift = per.bit_length() - 1
    cand = jnp.where(valid & ((lane >> shift) == (best >> shift)), sel, neg)
    i1 = first_max(cand)
    i2 = first_max(jnp.where(lane_f == i1, neg, cand))
    w1 = jnp.sum(jnp.where(lane_f == i1, s, 0.0), axis=-1, keepdims=True)
    w2 = jnp.sum(jnp.where(lane_f == i2, s, 0.0), axis=-1, keepdims=True)
    tot = w1 + w2
    idx = jnp.where(lane == 0, i1, jnp.where(lane == 1, i2, 0.0)).astype(jnp.int32)
    gate = jnp.where(lane == 0, w1 / tot, jnp.where(lane == 1, w2 / tot, 0.0))
    return idx, gate


def _merge(ys, w_out_bf, xa, mod3, ln_g, ln_b, wr_hi, wr_lo, rb_pad, n_rows, n_batch, seq, alpha):
    d = xa.shape[1]
    tm = 512

    def mrow(i):
        return jnp.minimum((i * tm) // seq, n_batch)

    ymix = pl.BlockSpec((tm, MIXER_W), lambda i: (i, 0))
    rowvec = pl.BlockSpec((1, d), lambda i: (0, 0))
    return pl.pallas_call(
        functools.partial(_merge_kernel, alpha=alpha),
        grid=(n_rows // tm,),
        in_specs=[
            ymix, ymix, ymix, ymix,
            pl.BlockSpec((d, d), lambda i: (0, 0)),
            pl.BlockSpec((tm, d), lambda i: (i, 0)),
            pl.BlockSpec((None, 1, d), lambda i: (mrow(i), 0, 2)),
            rowvec, rowvec,
            pl.BlockSpec((None, 1, d), lambda i: (mrow(i), 0, 3)),
            pl.BlockSpec((None, 1, d), lambda i: (mrow(i), 0, 4)),
            pl.BlockSpec((d, LOGIT_PAD), lambda i: (0, 0)),
            pl.BlockSpec((d, LOGIT_PAD), lambda i: (0, 0)),
            pl.BlockSpec((1, LOGIT_PAD), lambda i: (0, 0)),
        ],
        out_specs=[
            pl.BlockSpec((tm, d), lambda i: (i, 0)),
            pl.BlockSpec((tm, d), lambda i: (i, 0)),
            pl.BlockSpec((tm, LOGIT_PAD), lambda i: (i, 0)),
            pl.BlockSpec((tm, LOGIT_PAD), lambda i: (i, 0)),
        ],
        out_shape=[
            jax.ShapeDtypeStruct((n_rows, d), F32),
            jax.ShapeDtypeStruct((n_rows, d), BF16),
            jax.ShapeDtypeStruct((n_rows, LOGIT_PAD), jnp.int32),
            jax.ShapeDtypeStruct((n_rows, LOGIT_PAD), F32),
        ],
        compiler_params=_cparams(("arbitrary",)),
        name="merge",
    )(*ys, w_out_bf, xa, mod3, ln_g, ln_b, mod3, mod3, wr_hi, wr_lo, rb_pad)


def _moe_kernel(be_ref, nu_ref, x_ref, sw_ref, wg_ref, wu_ref, wd_ref, o_ref, wg_s, wu_s, wd_s):
    i = pl.program_id(0)
    active = i < nu_ref[0]
    new_expert = jnp.logical_or(i == 0, be_ref[i] != be_ref[jnp.maximum(i - 1, 0)])

    @pl.when(jnp.logical_and(active, new_expert))
    def _():
        for src, dst in ((wg_ref, wg_s), (wu_ref, wu_s), (wd_ref, wd_s)):
            def cast(r0, src=src, dst=dst):
                dst[pl.ds(r0, 256), :] = src[pl.ds(r0, 256), :].astype(BF16)

            _row_chunks(src.shape[0], 256, cast)

    @pl.when(active)
    def _():
        x = x_ref[...]
        g = jnp.dot(x, wg_s[...], preferred_element_type=F32)
        u = jnp.dot(x, wu_s[...], preferred_element_type=F32)
        a = (g * jax.nn.sigmoid(g) * u).astype(BF16)
        y = jnp.dot(a, wd_s[...], preferred_element_type=F32)
        o_ref[...] = (y * sw_ref[...]).astype(o_ref.dtype)

    @pl.when(jnp.logical_not(active))
    def _():
        o_ref[...] = jnp.zeros_like(o_ref)


def _moe_experts(xs, slot_w, block_e, n_used, wg, wu, wd, layer):
    n_slots, d = xs.shape
    de = wg.shape[3]
    n_blocks = n_slots // MOE_BLOCK
    grid_spec = pltpu.PrefetchScalarGridSpec(
        num_scalar_prefetch=2,
        grid=(n_blocks,),
        in_specs=[
            pl.BlockSpec((MOE_BLOCK, d), lambda i, be, nu: (i, 0)),
            pl.BlockSpec((MOE_BLOCK, 1), lambda i, be, nu: (i, 0)),
            pl.BlockSpec((None, None, d, de), lambda i, be, nu: (layer, be[i], 0, 0)),
            pl.BlockSpec((None, None, d, de), lambda i, be, nu: (layer, be[i], 0, 0)),
            pl.BlockSpec((None, None, de, d), lambda i, be, nu: (layer, be[i], 0, 0)),
        ],
        out_specs=pl.BlockSpec((MOE_BLOCK, d), lambda i, be, nu: (i, 0)),
        scratch_shapes=[pltpu.VMEM((d, de), BF16), pltpu.VMEM((d, de), BF16), pltpu.VMEM((de, d), BF16)],
    )
    return pl.pallas_call(
        _moe_kernel,
        grid_spec=grid_spec,
        out_shape=jax.ShapeDtypeStruct((n_slots, d), BF16),
        compiler_params=_cparams(("arbitrary",)),
        name="moe_experts",
    )(block_e, n_used, xs, slot_w, wg, wu, wd)


def _assignment_ranks(flat_e):
    a = flat_e.shape[0]
    blk = 128
    nb = a // blk
    onehot = (flat_e[:, None] == jnp.arange(N_EXPERTS)[None, :]).astype(BF16).reshape(nb, blk, N_EXPERTS)
    tri = (jnp.arange(blk)[:, None] >= jnp.arange(blk)[None, :]).astype(BF16)
    intra = jnp.einsum("ij,bjk->bik", tri, onehot, preferred_element_type=F32)
    bsum = intra[:, -1, :]
    before = (jnp.arange(nb)[:, None] > jnp.arange(nb)[None, :]).astype(BF16)
    offs = jnp.dot(before, bsum.astype(BF16), preferred_element_type=F32)
    csum = intra + offs[:, None, :]
    rank = jnp.sum(csum * onehot.astype(F32), axis=-1).reshape(a) - 1.0
    counts = offs[-1] + bsum[-1]
    return rank.astype(jnp.int32), counts.astype(jnp.int32)


def _dispatch_plan(idx, gate):
    t = idx.shape[0]
    a = t * TOP_K
    flat_e = idx.reshape(a)
    rank, counts = _assignment_ranks(flat_e)
    padded = (counts + MOE_BLOCK - 1) // MOE_BLOCK * MOE_BLOCK
    pad_end = jnp.cumsum(padded)
    pad_start = pad_end - padded
    dest = pad_start[flat_e] + rank
    n_blocks = (a + N_EXPERTS * (MOE_BLOCK - 1) + MOE_BLOCK - 1) // MOE_BLOCK
    n_slots = n_blocks * MOE_BLOCK
    flat_tok = jnp.repeat(jnp.arange(t, dtype=jnp.int32), TOP_K)
    slot_tok = jnp.full((n_slots,), t, jnp.int32).at[dest].set(flat_tok)
    slot_w = jnp.zeros((n_slots,), F32).at[dest].set(gate.reshape(a))
    blk_start = jnp.arange(n_blocks, dtype=jnp.int32) * MOE_BLOCK
    block_e = jnp.minimum(
        jnp.sum((pad_end[None, :] <= blk_start[:, None]).astype(jnp.int32), axis=1), N_EXPERTS - 1)
    n_used = (pad_end[-1] // MOE_BLOCK).astype(jnp.int32).reshape(1)
    return slot_tok, slot_w, block_e.astype(jnp.int32), n_used, dest.reshape(t, TOP_K)


def _final_kernel(x_ref, ya_ref, yb_ref, g2_ref, lg_ref, lb_ref, o_ref, *, alpha):
    f = ya_ref[...].astype(F32) + yb_ref[...].astype(F32)
    o_ref[...] = _ln(alpha * x_ref[...] + g2_ref[...] * f) * lg_ref[...] + lb_ref[...]


def _final_norm(x1, ya, yb, mod3, ln_g, ln_b, n_batch, seq, alpha):
    n_rows, d = x1.shape
    tm = 512

    def mrow(i):
        return jnp.minimum((i * tm) // seq, n_batch)

    tile = pl.BlockSpec((tm, d), lambda i: (i, 0))
    rowvec = pl.BlockSpec((1, d), lambda i: (0, 0))
    return pl.pallas_call(
        functools.partial(_final_kernel, alpha=alpha),
        grid=(n_rows // tm,),
        in_specs=[tile, tile, tile, pl.BlockSpec((None, 1, d), lambda i: (mrow(i), 0, 5)), rowvec, rowvec],
        out_specs=tile,
        out_shape=jax.ShapeDtypeStruct((n_rows, d), F32),
        compiler_params=_cparams(("arbitrary",)),
        name="final_norm",
    )(x1, ya, yb, mod3, ln_g, ln_b)


def _permute_in_cols(w):
    pool, q, k, v, hy, gm = jnp.split(w, (512, 1024, 1152, 1280, 2816), axis=-1)
    return jnp.concatenate([hy, pool, q, gm, k, v], axis=-1)


def kernel(x, c, ctx, c_ctx, w_ada, b_ada, w_in, w_out, mix_norm_g, pool_w, pool_scale, attn_sink,
           hy_conv_w, hy_conv_b, hy_w1, hy_b1, hy_f1, hy_w2, hy_b2, hy_f2, hy_w3, hy_skip,
           gm_ln_g, gm_ln_b, gm_ws, gm_bs, ln1_g, ln1_b, ln2_g, ln2_b, w_router, router_bias,
           w_gate, w_up, w_down):
    n_batch, seq, d = x.shape
    ctx_len = ctx.shape[1]
    depth = w_in.shape[0]
    grid_w = 64
    alpha = (2 * depth) ** 0.25
    n_lat = n_batch * seq
    n_ctx = n_batch * ctx_len

    xa = jnp.concatenate([x.reshape(n_lat, d), ctx.reshape(n_ctx, d)], axis=0)
    cvec = jnp.concatenate([c, c_ctx[None, :], jnp.zeros((8 - n_batch - 1, d), F32)], axis=0)
    mods = _ada_mods(cvec, w_ada, b_ada)
    cos_t, sin_t = _rope_tables(seq, grid_w)
    wr_pad = jnp.pad(w_router, ((0, 0), (0, LOGIT_PAD - N_EXPERTS)))
    wr_hi = wr_pad.astype(BF16)
    wr_lo = (wr_pad - wr_hi.astype(F32)).astype(BF16)
    rb_pad = jnp.pad(router_bias.astype(F32), (0, LOGIT_PAD - N_EXPERTS)).reshape(1, LOGIT_PAD)
    row = lambda v: v.reshape(1, -1)

    for layer in range(depth):
        last = layer == depth - 1
        mod3 = mods[layer].reshape(8, 1, 6 * d)
        gains = mix_norm_g[layer].reshape(4, 1, MIXER_W)
        lp = {
            "hy_conv_w": hy_conv_w[layer], "hy_conv_b": row(hy_conv_b[layer]),
            "hy_w1p": jnp.pad(hy_w1[layer], ((0, HY_EMB_PAD - HY_EMB), (0, 0))),
            "hy_b1": row(hy_b1[layer]), "hy_f1": row(hy_f1[layer]),
            "hy_w2": hy_w2[layer], "hy_b2": row(hy_b2[layer]), "hy_f2": row(hy_f2[layer]),
            "hy_w3": hy_w3[layer], "hy_skip": hy_skip[layer],
        }
        z = _in_proj(xa, mod3, _permute_in_cols(w_in[layer]).astype(BF16), n_batch, seq)

        pool_bf = pool_w[layer].astype(BF16)
        ps = row(pool_scale[layer])
        ws_cat = jnp.transpose(gm_ws[layer], (1, 0, 2)).reshape(GMLP_CHUNK, GMLP_GROUPS * GMLP_CHUNK)
        bs_mat = jnp.repeat(gm_bs[layer].T, MIXER_W // GMLP_GROUPS, axis=1)
        n_rows = n_lat if last else n_lat + n_ctx

        y_pool = _pool_mix(z, pool_bf, ps, gains[0], n_batch, seq, 0)
        y_attn = _attn_lat(z, attn_sink[layer], cos_t, sin_t, gains[1], n_batch, seq, ctx_len)
        y_hy = _hyena_mix(z, lp, gains[2], n_batch, seq, 0)
        y_gm = _gmlp_mix(z, row(gm_ln_g[layer]), row(gm_ln_b[layer]), ws_cat.astype(BF16), bs_mat,
                         gains[3], n_rows)
        if not last:
            y_pool = jnp.concatenate(
                [y_pool, _pool_mix(z, pool_bf, ps, gains[0], n_batch, ctx_len, n_lat // ctx_len)], axis=0)
            y_attn = jnp.concatenate(
                [y_attn, _attn_ctx(z, attn_sink[layer], gains[1], n_batch, seq, ctx_len)], axis=0)
            y_hy = jnp.concatenate([y_hy, _hyena_mix(z, lp, gains[2], n_batch, ctx_len, n_lat)], axis=0)

        x1, h2, idx, gate = _merge((y_pool, y_attn, y_hy, y_gm), w_out[layer].astype(BF16), xa, mod3,
                                   row(ln1_g[layer]), row(ln1_b[layer]), wr_hi, wr_lo, rb_pad,
                                   n_rows, n_batch, seq, alpha)

        slot_tok, slot_w, block_e, n_used, dest = _dispatch_plan(idx[:, :TOP_K], gate[:, :TOP_K])
        h_pad = jnp.concatenate([h2, jnp.zeros((1, d), BF16)], axis=0)
        ys = _moe_experts(h_pad[slot_tok], slot_w[:, None], block_e, n_used, w_gate, w_up, w_down, layer)
        xa = _final_norm(x1, ys[dest[:, 0]], ys[dest[:, 1]], mod3, row(ln2_g[layer]), row(ln2_b[layer]),
                         n_batch, seq, alpha)
    return xa.reshape(n_batch, seq, d)
```

```python
import functools
import math

import jax
import jax.numpy as jnp
from jax import lax
from jax.experimental import pallas as pl
from jax.experimental.pallas import tpu as pltpu

F32 = jnp.float32
BF16 = jnp.bfloat16
HI = lax.Precision.HIGHEST

LN_EPS = 1e-6
MIXER_W = 512
HEAD_DIM = 64
N_HEADS = 8
N_KV_HEADS = 2
KV_GROUP = 4
KV_W = 128
WINDOW = 128
ROPE_THETA = 10000.0
POOL_WINDOWS = (2, 4, 8, 16)
POOL_GROUP = 128
POOL_HALO = 8
GMLP_CHUNK = 128
GMLP_GROUPS = 8
HY_EMB = 33
HY_EMB_PAD = 128
HY_TARGET = 1e-2
HY_FAST = 0.3
HY_SLOW = 1.5
N_EXPERTS = 32
N_EXPERT_GROUPS = 8
TOP_K = 2
D_EXPERT = 768
MOE_BLOCK = 256
LOGIT_PAD = 128

COL_HY = 0
COL_POOL = 3
COL_Q = 4
COL_GU = 5
COL_GV = 6
COL_K = 28
COL_V = 29
IN_W = 3840

VMEM_LIMIT = 56 * 1024 * 1024


def _cparams(sem):
    return pltpu.CompilerParams(dimension_semantics=sem, vmem_limit_bytes=VMEM_LIMIT)


def _ln(x):
    mu = jnp.mean(x, axis=-1, keepdims=True)
    xc = x - mu
    var = jnp.mean(xc * xc, axis=-1, keepdims=True)
    return xc * lax.rsqrt(var + LN_EPS)


def _rms_gain(y, g):
    return y * lax.rsqrt(jnp.mean(y * y, axis=-1, keepdims=True) + LN_EPS) * g


def _row_chunks(n_rows, chunk, body):
    n = n_rows // chunk
    if n == 1:
        body(0)
        return

    def step(i, carry):
        body(pl.multiple_of(i * chunk, chunk))
        return carry

    lax.fori_loop(0, n, step, 0)


def _ada_kernel(c_ref, w_ref, b_ref, o_ref):
    c = c_ref[...]
    s = c * jax.nn.sigmoid(c)
    o_ref[...] = jnp.dot(s, w_ref[...], precision=HI, preferred_element_type=F32) + b_ref[...]


def _ada_mods(cvec, w_ada, b_ada):
    depth, d, n = w_ada.shape
    tn = 1024
    return pl.pallas_call(
        _ada_kernel,
        grid=(depth, n // tn),
        in_specs=[
            pl.BlockSpec((8, d), lambda l, j: (0, 0)),
            pl.BlockSpec((None, d, tn), lambda l, j: (l, 0, j)),
            pl.BlockSpec((None, 1, tn), lambda l, j: (l, 0, j)),
        ],
        out_specs=pl.BlockSpec((None, 8, tn), lambda l, j: (l, 0, j)),
        out_shape=jax.ShapeDtypeStruct((depth, 8, n), F32),
        compiler_params=_cparams(("arbitrary", "arbitrary")),
        name="ada_mods",
    )(cvec, w_ada, b_ada.reshape(depth, 1, n))


def _inproj_kernel(x_ref, sh_ref, sc_ref, w_ref, o_ref, h_scr):
    @pl.when(pl.program_id(1) == 0)
    def _():
        sh = sh_ref[...]
        sc1 = 1.0 + sc_ref[...]

        def body(r0):
            x = x_ref[pl.ds(r0, 256), :]
            h_scr[pl.ds(r0, 256), :] = (_ln(x) * sc1 + sh).astype(BF16)

        _row_chunks(x_ref.shape[0], 256, body)

    o_ref[...] = jnp.dot(h_scr[...], w_ref[...], preferred_element_type=F32)


def _in_proj(xa, mod3, w_bf, n_batch, seq):
    t, d = xa.shape
    n = w_bf.shape[1]
    tn = 768
    tm = next(m for m in (1024, 512, 256) if t % m == 0 and seq % m == 0)

    def mrow(i):
        return jnp.minimum((i * tm) // seq, n_batch)

    return pl.pallas_call(
        _inproj_kernel,
        grid=(t // tm, n // tn),
        in_specs=[
            pl.BlockSpec((tm, d), lambda i, j: (i, 0)),
            pl.BlockSpec((None, 1, d), lambda i, j: (mrow(i), 0, 0)),
            pl.BlockSpec((None, 1, d), lambda i, j: (mrow(i), 0, 1)),
            pl.BlockSpec((d, tn), lambda i, j: (0, j)),
        ],
        out_specs=pl.BlockSpec((tm, tn), lambda i, j: (i, j)),
        out_shape=jax.ShapeDtypeStruct((t, n), F32),
        scratch_shapes=[pltpu.VMEM((tm, d), BF16)],
        compiler_params=_cparams(("arbitrary", "arbitrary")),
        name="in_proj",
    )(xa, mod3, mod3, w_bf)


def _pool_kernel(z_ref, w_ref, ps_ref, g_ref, o_ref, buf):
    seq = z_ref.shape[0]
    rc = min(256, seq)
    zeros = jnp.zeros((POOL_HALO, MIXER_W), F32)
    buf[0:POOL_HALO, :] = zeros
    buf[seq + POOL_HALO:seq + 2 * POOL_HALO, :] = zeros

    def fill(r0):
        buf[pl.ds(r0 + POOL_HALO, rc), :] = z_ref[pl.ds(r0, rc), :]

    _row_chunks(seq, rc, fill)

    def body(r0):
        win = buf[pl.ds(r0, rc + 2 * POOL_HALO), :]
        t = r0 + lax.broadcasted_iota(jnp.int32, (rc, 1), 0)
        outs = []
        for gi, w in enumerate(POOL_WINDOWS):
            lanes = slice(gi * POOL_GROUP, (gi + 1) * POOL_GROUP)
            acc = None
            for j in range(-w // 2, w // 2):
                piece = win[POOL_HALO + j:POOL_HALO + j + rc, lanes]
                acc = piece if acc is None else acc + piece
            cnt = (jnp.minimum(t + w // 2, seq) - jnp.maximum(t - w // 2, 0)).astype(F32)
            dlt = acc / cnt - win[POOL_HALO:POOL_HALO + rc, lanes]
            outs.append(jnp.dot(dlt.astype(BF16), w_ref[gi], preferred_element_type=F32))
        y = jnp.concatenate(outs, axis=-1) * ps_ref[...]
        o_ref[pl.ds(r0, rc), :] = _rms_gain(y, g_ref[...]).astype(BF16)

    _row_chunks(seq, rc, body)


def _pool_mix(z, pool_w_bf, pool_scale, gain, n_seq, seq, row_blk0):
    return pl.pallas_call(
        _pool_kernel,
        grid=(n_seq,),
        in_specs=[
            pl.BlockSpec((seq, MIXER_W), lambda s: (row_blk0 + s, COL_POOL)),
            pl.BlockSpec((4, POOL_GROUP, POOL_GROUP), lambda s: (0, 0, 0)),
            pl.BlockSpec((1, MIXER_W), lambda s: (0, 0)),
            pl.BlockSpec((1, MIXER_W), lambda s: (0, 0)),
        ],
        out_specs=pl.BlockSpec((seq, MIXER_W), lambda s: (s, 0)),
        out_shape=jax.ShapeDtypeStruct((n_seq * seq, MIXER_W), BF16),
        scratch_shapes=[pltpu.VMEM((seq + 2 * POOL_HALO, MIXER_W), F32)],
        compiler_params=_cparams(("arbitrary",)),
        name="pool_mix",
    )(z, pool_w_bf, pool_scale, gain)


def _rope(x, cos, sin_signed):
    lane = lax.broadcasted_iota(jnp.int32, x.shape, 1)
    partner = jnp.where((lane & 31) < 16, pltpu.roll(x, 112, axis=1), pltpu.roll(x, 16, axis=1))
    return x * cos + partner * sin_signed


def _attend(qs, sink_col, parts):
    dn = (((1,), (1,)), ((), ()))
    scores = []
    for kk, _, mask in parts:
        s = lax.dot_general(qs, kk, dn, preferred_element_type=F32)
        if mask is not None:
            s = jnp.where(mask, s, -1e30)
        scores.append(s)
    m = sink_col
    for s in scores:
        m = jnp.maximum(m, jnp.max(s, axis=-1, keepdims=True))
    den = jnp.exp(sink_col - m)
    out = None
    for s, (_, vv, _) in zip(scores, parts):
        p = jnp.exp(s - m)
        den = den + jnp.sum(p, axis=-1, keepdims=True)
        o = jnp.dot(p.astype(BF16), vv, preferred_element_type=F32)
        out = o if out is None else out + o
    return out / den


def _heads_attend(q_all, sink_ref, tq, make_parts):
    cols = [None] * N_HEADS
    for kh in range(N_KV_HEADS):
        heads = [kh * KV_GROUP + g for g in range(KV_GROUP)]
        qs = jnp.concatenate([q_all[:, h * HEAD_DIM:(h + 1) * HEAD_DIM] for h in heads], axis=0)
        sink_col = jnp.concatenate([jnp.full((tq, 1), sink_ref[h], F32) for h in heads], axis=0)
        o = _attend(qs.astype(BF16), sink_col, make_parts(kh))
        for g, h in enumerate(heads):
            cols[h] = o[g * tq:(g + 1) * tq, :]
    return jnp.concatenate(cols, axis=-1)


def _attn_lat_kernel(sink_ref, q_ref, k_ref, v_ref, kc_ref, vc_ref, cos_ref, sin_ref, g_ref, o_ref):
    tq = q_ref.shape[0]
    seq = k_ref.shape[0]
    nband = tq + 2 * WINDOW
    i = pl.program_id(1)
    q0 = pl.multiple_of(i * tq, tq)
    k0 = pl.multiple_of(jnp.clip(q0 - WINDOW, 0, seq - nband), WINDOW)

    cq = cos_ref[pl.ds(q0, tq), :]
    sq = sin_ref[pl.ds(q0, tq), :]
    q = q_ref[...]
    q_all = jnp.concatenate(
        [_rope(q[:, c * 128:(c + 1) * 128], cq, sq) for c in range(MIXER_W // 128)], axis=-1)
    q_all = q_all * (HEAD_DIM ** -0.5)

    kb = _rope(k_ref[pl.ds(k0, nband), :], cos_ref[pl.ds(k0, nband), :],
               sin_ref[pl.ds(k0, nband), :]).astype(BF16)
    vb = v_ref[pl.ds(k0, nband), :].astype(BF16)
    kc = kc_ref[...].astype(BF16)
    vc = vc_ref[...].astype(BF16)

    qpos = q0 + (lax.broadcasted_iota(jnp.int32, (KV_GROUP * tq, nband), 0) & (tq - 1))
    kpos = k0 + lax.broadcasted_iota(jnp.int32, (KV_GROUP * tq, nband), 1)
    mask = jnp.abs(qpos - kpos) <= WINDOW

    def make_parts(kh):
        hs = slice(kh * HEAD_DIM, (kh + 1) * HEAD_DIM)
        return [(kb[:, hs], vb[:, hs], mask), (kc[:, hs], vc[:, hs], None)]

    y = _heads_attend(q_all, sink_ref, tq, make_parts)
    o_ref[...] = _rms_gain(y, g_ref[...]).astype(BF16)


def _attn_ctx_kernel(sink_ref, q_ref, kc_ref, vc_ref, g_ref, o_ref):
    tq = q_ref.shape[0]
    q_all = q_ref[...] * (HEAD_DIM ** -0.5)
    kc = kc_ref[...].astype(BF16)
    vc = vc_ref[...].astype(BF16)

    def make_parts(kh):
        hs = slice(kh * HEAD_DIM, (kh + 1) * HEAD_DIM)
        return [(kc[:, hs], vc[:, hs], None)]

    y = _heads_attend(q_all, sink_ref, tq, make_parts)
    o_ref[...] = _rms_gain(y, g_ref[...]).astype(BF16)


def _attn_lat(z, sink, cos_t, sin_t, gain, n_batch, seq, ctx_len):
    tq = 128
    nq = seq // tq
    ctx_blk0 = (n_batch * seq) // ctx_len
    smem = pl.BlockSpec(memory_space=pltpu.SMEM)
    return pl.pallas_call(
        _attn_lat_kernel,
        grid=(n_batch, nq),
        in_specs=[
            smem,
            pl.BlockSpec((tq, MIXER_W), lambda b, i: (b * nq + i, COL_Q)),
            pl.BlockSpec((seq, KV_W), lambda b, i: (b, COL_K)),
            pl.BlockSpec((seq, KV_W), lambda b, i: (b, COL_V)),
            pl.BlockSpec((ctx_len, KV_W), lambda b, i: (ctx_blk0 + b, COL_K)),
            pl.BlockSpec((ctx_len, KV_W), lambda b, i: (ctx_blk0 + b, COL_V)),
            pl.BlockSpec((seq, 128), lambda b, i: (0, 0)),
            pl.BlockSpec((seq, 128), lambda b, i: (0, 0)),
            pl.BlockSpec((1, MIXER_W), lambda b, i: (0, 0)),
        ],
        out_specs=pl.BlockSpec((tq, MIXER_W), lambda b, i: (b * nq + i, 0)),
        out_shape=jax.ShapeDtypeStruct((n_batch * seq, MIXER_W), BF16),
        compiler_params=_cparams(("arbitrary", "arbitrary")),
        name="attn_lat",
    )(sink, z, z, z, z, z, cos_t, sin_t, gain)


def _attn_ctx(z, sink, gain, n_batch, seq, ctx_len):
    ctx_blk0 = (n_batch * seq) // ctx_len
    smem = pl.BlockSpec(memory_space=pltpu.SMEM)
    return pl.pallas_call(
        _attn_ctx_kernel,
        grid=(n_batch,),
        in_specs=[
            smem,
            pl.BlockSpec((ctx_len, MIXER_W), lambda b: (ctx_blk0 + b, COL_Q)),
            pl.BlockSpec((ctx_len, KV_W), lambda b: (ctx_blk0 + b, COL_K)),
            pl.BlockSpec((ctx_len, KV_W), lambda b: (ctx_blk0 + b, COL_V)),
            pl.BlockSpec((1, MIXER_W), lambda b: (0, 0)),
        ],
        out_specs=pl.BlockSpec((ctx_len, MIXER_W), lambda b: (b, 0)),
        out_shape=jax.ShapeDtypeStruct((n_batch * ctx_len, MIXER_W), BF16),
        compiler_params=_cparams(("arbitrary",)),
        name="attn_ctx",
    )(sink, z, z, z, gain)


def _rope_tables(seq, grid_w):
    nf = HEAD_DIM // 4
    inv = ROPE_THETA ** (-jnp.arange(nf, dtype=F32) / nf)
    t = jnp.arange(seq)
    row = (t // grid_w).astype(F32)[:, None] * inv[None, :]
    col = (t % grid_w).astype(F32)[:, None] * inv[None, :]
    cos_h = jnp.concatenate([jnp.cos(row), jnp.cos(row), jnp.cos(col), jnp.cos(col)], axis=-1)
    sin_h = jnp.concatenate([-jnp.sin(row), jnp.sin(row), -jnp.sin(col), jnp.sin(col)], axis=-1)
    return jnp.tile(cos_h, (1, 2)), jnp.tile(sin_h, (1, 2))


def _hy_conv_kernel(z_ref, zp_ref, zn_ref, w_ref, b_ref, g1_ref, g2_ref, u_ref):
    i = pl.program_id(1)
    tl = z_ref.shape[0]
    has_prev = (i > 0).astype(F32)
    has_next = (i < pl.num_programs(1) - 1).astype(F32)
    row = lax.broadcasted_iota(jnp.int32, (tl, MIXER_W), 0)
    outs = (g1_ref, g2_ref, u_ref)
    for part in range(3):
        cs = slice(part * MIXER_W, (part + 1) * MIXER_W)
        zc = z_ref[:, cs]
        prev_row = zp_ref[POOL_HALO - 1:POOL_HALO, cs] * has_prev
        next_row = zn_ref[0:1, cs] * has_next
        zm = jnp.where(row == 0, prev_row, pltpu.roll(zc, 1, axis=0))
        zp = jnp.where(row == tl - 1, next_row, pltpu.roll(zc, tl - 1, axis=0))
        y = b_ref[:, cs] + zm * w_ref[0:1, cs] + zc * w_ref[1:2, cs] + zp * w_ref[2:3, cs]
        outs[part][...] = y.astype(outs[part].dtype)


def _hy_conv(z, conv_w, conv_b, n_seq, seq, row0):
    tl = min(512, seq)
    nt = seq // tl
    blk0 = row0 // tl
    hb0 = row0 // POOL_HALO
    hpt = tl // POOL_HALO
    last_halo = (row0 + n_seq * seq) // POOL_HALO - 1
    wide = 3 * MIXER_W

    def prev_map(s, i):
        return (jnp.maximum(hb0 + (s * nt + i) * hpt - 1, 0), COL_HY)

    def next_map(s, i):
        return (jnp.minimum(hb0 + (s * nt + i + 1) * hpt, last_halo), COL_HY)

    out_spec = pl.BlockSpec((tl, MIXER_W), lambda s, i: (i, s))
    shp = (seq, n_seq * MIXER_W)
    return pl.pallas_call(
        _hy_conv_kernel,
        grid=(n_seq, nt),
        in_specs=[
            pl.BlockSpec((tl, wide), lambda s, i: (blk0 + s * nt + i, COL_HY)),
            pl.BlockSpec((POOL_HALO, wide), prev_map),
            pl.BlockSpec((POOL_HALO, wide), next_map),
            pl.BlockSpec((3, wide), lambda s, i: (0, 0)),
            pl.BlockSpec((1, wide), lambda s, i: (0, 0)),
        ],
        out_specs=[out_spec, out_spec, out_spec],
        out_shape=[jax.ShapeDtypeStruct(shp, F32), jax.ShapeDtypeStruct(shp, F32),
                   jax.ShapeDtypeStruct(shp, BF16)],
        compiler_params=_cparams(("arbitrary", "arbitrary")),
        name="hy_conv",
    )(z, z, z, conv_w, conv_b)


def _hy_filter_kernel(feat_ref, dec_ref, w1_ref, b1_ref, f1_ref, w2_ref, b2_ref, f2_ref, w3_ref, o_ref):
    tl = feat_ref.shape[0]
    h = jnp.dot(feat_ref[...], w1_ref[...], precision=HI, preferred_element_type=F32) + b1_ref[...]
    h = jnp.sin(f1_ref[...] * h)
    h = jnp.dot(h, w2_ref[...], precision=HI, preferred_element_type=F32) + b2_ref[...]
    h = jnp.sin(f2_ref[...] * h)
    filt = jnp.dot(h, w3_ref[...], precision=HI, preferred_element_type=F32)
    dec = dec_ref[...]
    t = pl.program_id(0) * tl + lax.broadcasted_iota(jnp.int32, (tl, 1), 0)
    not_first = (t > 0).astype(F32)
    for o in range(2):
        hf = filt[:, (2 * o) * MIXER_W:(2 * o + 1) * MIXER_W] * dec
        hb = filt[:, (2 * o + 1) * MIXER_W:(2 * o + 2) * MIXER_W] * dec * not_first
        o_ref[:, o * MIXER_W:(o + 1) * MIXER_W] = (hf + hb).astype(BF16)
        o_ref[:, (2 + o) * MIXER_W:(3 + o) * MIXER_W] = (hb - hf).astype(BF16)


def _hy_filters(feat, decay, w1p, b1, f1, w2, b2, f2, w3):
    seq = feat.shape[0]
    tl = min(512, seq)
    hid = w2.shape[0]
    full = lambda shape: pl.BlockSpec(shape, lambda i: tuple(0 for _ in shape))
    return pl.pallas_call(
        _hy_filter_kernel,
        grid=(seq // tl,),
        in_specs=[
            pl.BlockSpec((tl, HY_EMB_PAD), lambda i: (i, 0)),
            pl.BlockSpec((tl, MIXER_W), lambda i: (i, 0)),
            full((HY_EMB_PAD, hid)), full((1, hid)), full((1, hid)),
            full((hid, hid)), full((1, hid)), full((1, hid)),
            full((hid, 4 * MIXER_W)),
        ],
        out_specs=pl.BlockSpec((tl, 4 * MIXER_W), lambda i: (i, 0)),
        out_shape=jax.ShapeDtypeStruct((seq, 4 * MIXER_W), BF16),
        compiler_params=_cparams(("arbitrary",)),
        name="hy_filters",
    )(feat, decay, w1p, b1, f1, w2, b2, f2, w3)


def _hy_kspec_kernel(fw_ref, ab_ref, o_ref, *, inv_len):
    hb = fw_ref.shape[0] // 2
    half = ab_ref.shape[1] // 2
    kr = jnp.dot(fw_ref[0:hb, :], ab_ref[:, 0:half], preferred_element_type=F32)
    ki = jnp.dot(fw_ref[hb:, :], ab_ref[:, half:], preferred_element_type=F32)
    o_ref[0:hb, :] = kr * inv_len
    o_ref[hb:, :] = ki * inv_len


def _hy_kspec(fw, ab, tm):
    two_l, seq = fw.shape
    return pl.pallas_call(
        functools.partial(_hy_kspec_kernel, inv_len=1.0 / seq),
        grid=(two_l // tm,),
        in_specs=[
            pl.BlockSpec((tm, seq), lambda i: (i, 0)),
            pl.BlockSpec((seq, 4 * MIXER_W), lambda i: (0, 0)),
        ],
        out_specs=pl.BlockSpec((tm, 2 * MIXER_W), lambda i: (i, 0)),
        out_shape=jax.ShapeDtypeStruct((two_l, 2 * MIXER_W), F32),
        compiler_params=_cparams(("arbitrary",)),
        name="hy_kspec",
    )(fw, ab)


def _hy_fwd_kernel(fw_ref, u_ref, k_ref, o_ref):
    hb = fw_ref.shape[0] // 2
    reps = u_ref.shape[1] // MIXER_W
    pq = jnp.dot(fw_ref[...], u_ref[...], preferred_element_type=F32)
    p, q = pq[0:hb, :], pq[hb:, :]
    kr = jnp.concatenate([k_ref[0:hb, :]] * reps, axis=-1)
    ki = jnp.concatenate([k_ref[hb:, :]] * reps, axis=-1)
    o_ref[0:hb, :] = (p * kr + q * ki).astype(BF16)
    o_ref[hb:, :] = (q * kr - p * ki).astype(BF16)


def _hy_fwd(fw, u, kspec, order, tm, tn):
    two_l, seq = fw.shape
    ncol = u.shape[1]
    return pl.pallas_call(
        _hy_fwd_kernel,
        grid=(ncol // tn, two_l // tm),
        in_specs=[
            pl.BlockSpec((tm, seq), lambda j, i: (i, 0)),
            pl.BlockSpec((seq, tn), lambda j, i: (0, j)),
            pl.BlockSpec((tm, MIXER_W), lambda j, i: (i, order)),
        ],
        out_specs=pl.BlockSpec((tm, tn), lambda j, i: (i, j)),
        out_shape=jax.ShapeDtypeStruct((two_l, ncol), BF16),
        compiler_params=_cparams(("arbitrary", "arbitrary")),
        name="hy_fwd",
    )(fw, u, kspec)


def _hy_inv_kernel(inv_ref, y_ref, u_ref, gate_ref, skip_ref, *rest, final):
    reps = u_ref.shape[1] // MIXER_W
    conv = jnp.dot(inv_ref[...], y_ref[...], preferred_element_type=F32)
    skip = jnp.concatenate([skip_ref[...]] * reps, axis=-1)
    out = gate_ref[...] * (conv + u_ref[...].astype(F32) * skip)
    if final:
        g_ref, o_ref = rest
        for r in range(reps):
            o_ref[r] = _rms_gain(out[:, r * MIXER_W:(r + 1) * MIXER_W], g_ref[...]).astype(BF16)
    else:
        (o_ref,) = rest
        o_ref[...] = out.astype(BF16)


def _hy_inv(inv, yspec, u, gate, skip, gain, final, tm, tn):
    seq, two_l = inv.shape
    ncol = u.shape[1]
    reps = tn // MIXER_W
    in_specs = [
        pl.BlockSpec((tm, two_l), lambda j, i: (i, 0)),
        pl.BlockSpec((two_l, tn), lambda j, i: (0, j)),
        pl.BlockSpec((tm, tn), lambda j, i: (i, j)),
        pl.BlockSpec((tm, tn), lambda j, i: (i, j)),
        pl.BlockSpec((1, MIXER_W), lambda j, i: (0, 0)),
    ]
    args = [inv, yspec, u, gate, skip]
    if final:
        in_specs.append(pl.BlockSpec((1, MIXER_W), lambda j, i: (0, 0)))
        args.append(gain)
        out_spec = pl.BlockSpec((reps, tm, MIXER_W), lambda j, i: (j, i, 0))
        out_shape = jax.ShapeDtypeStruct((ncol // MIXER_W, seq, MIXER_W), BF16)
    else:
        out_spec = pl.BlockSpec((tm, tn), lambda j, i: (i, j))
        out_shape = jax.ShapeDtypeStruct((seq, ncol), BF16)
    return pl.pallas_call(
        functools.partial(_hy_inv_kernel, final=final),
        grid=(ncol // tn, seq // tm),
        in_specs=in_specs,
        out_specs=out_spec,
        out_shape=out_shape,
        compiler_params=_cparams(("arbitrary", "arbitrary")),
        name="hy_inv",
    )(*args)


def _hy_tables(seq, tm):
    k = jnp.arange(seq, dtype=jnp.int32)
    m = ((2 * k + 1)[:, None] * k[None, :]) % (4 * seq)
    ang = m.astype(F32) * (2.0 * math.pi / (4 * seq))
    hb = tm // 2
    nb = seq // hb
    cs = jnp.stack([jnp.cos(ang).reshape(nb, hb, seq), jnp.sin(ang).reshape(nb, hb, seq)], axis=1)
    fw = cs.reshape(2 * seq, seq).astype(BF16)
    return fw, fw.T


def _hy_features(seq):
    t = jnp.linspace(0.0, 1.0, seq, dtype=F32)[:, None]
    bands = (HY_EMB - 1) // 2
    freqs = jnp.linspace(1e-4, bands - 1, bands, dtype=F32)[None, :]
    ang = (2.0 * math.pi / seq) * jnp.arange(seq, dtype=F32)[:, None] * freqs
    feat = jnp.concatenate([t, jnp.cos(ang), -jnp.sin(ang)], -1)
    feat = jnp.pad(feat, ((0, 0), (0, HY_EMB_PAD - HY_EMB)))
    deltas = jnp.abs(jnp.linspace(math.log(HY_TARGET) / HY_SLOW, math.log(HY_TARGET) / HY_FAST,
                                  MIXER_W, dtype=F32))
    return feat, jnp.exp(-t * deltas[None, :])


def _hyena_mix(z, lp, gain, n_seq, seq, row0):
    tm_f = min(512, 2 * seq)
    tm_i = min(256, seq)
    ncol = n_seq * MIXER_W
    tn = min(1024, ncol)
    fw, inv = _hy_tables(seq, tm_f)
    feat, decay = _hy_features(seq)
    ab = _hy_filters(feat, decay, lp["hy_w1p"], lp["hy_b1"], lp["hy_f1"], lp["hy_w2"], lp["hy_b2"],
                     lp["hy_f2"], lp["hy_w3"])
    kspec = _hy_kspec(fw, ab, tm_f)
    g1, g2, u = _hy_conv(z, lp["hy_conv_w"], lp["hy_conv_b"], n_seq, seq, row0)
    y0 = _hy_fwd(fw, u, kspec, 0, tm_f, tn)
    u1 = _hy_inv(inv, y0, u, g1, lp["hy_skip"][0:1], None, False, tm_i, tn)
    y1 = _hy_fwd(fw, u1, kspec, 1, tm_f, tn)
    out = _hy_inv(inv, y1, u1, g2, lp["hy_skip"][1:2], gain, True, tm_i, tn)
    return out.reshape(n_seq * seq, MIXER_W)


def _gelu(x):
    c = math.sqrt(2.0 / math.pi)
    return 0.5 * x * (1.0 + jnp.tanh(c * (x + 0.044715 * (x * x * x))))


def _gmlp_kernel(u_ref, v_ref, lg_ref, lb_ref, ws_ref, bs_ref, g_ref, o_ref):
    tl = u_ref.shape[0]
    gw = MIXER_W // GMLP_GROUPS
    lane_grp = lax.broadcasted_iota(jnp.int32, (GMLP_CHUNK, MIXER_W), 1) // gw
    for c in range(tl // GMLP_CHUNK):
        rows = slice(c * GMLP_CHUNK, (c + 1) * GMLP_CHUNK)
        v = _ln(_gelu(v_ref[rows, :])) * lg_ref[...] + lb_ref[...]
        vb = v.astype(BF16)
        stacked = jnp.concatenate(
            [jnp.where(lane_grp == g, vb, jnp.zeros_like(vb)) for g in range(GMLP_GROUPS)], axis=0)
        s = jnp.dot(ws_ref[...], stacked, preferred_element_type=F32) + bs_ref[...]
        y = _gelu(u_ref[rows, :]) * s
        o_ref[rows, :] = _rms_gain(y, g_ref[...]).astype(BF16)


def _gmlp_mix(z, ln_g, ln_b, ws_cat, bs_mat, gain, n_rows):
    tl = 512
    full = lambda shape: pl.BlockSpec(shape, lambda i: tuple(0 for _ in shape))
    return pl.pallas_call(
        _gmlp_kernel,
        grid=(n_rows // tl,),
        in_specs=[
            pl.BlockSpec((tl, MIXER_W), lambda i: (i, COL_GU)),
            pl.BlockSpec((tl, MIXER_W), lambda i: (i, COL_GV)),
            full((1, MIXER_W)), full((1, MIXER_W)),
            full((GMLP_CHUNK, GMLP_GROUPS * GMLP_CHUNK)),
            full((GMLP_CHUNK, MIXER_W)),
            full((1, MIXER_W)),
        ],
        out_specs=pl.BlockSpec((tl, MIXER_W), lambda i: (i, 0)),
        out_shape=jax.ShapeDtypeStruct((n_rows, MIXER_W), BF16),
        compiler_params=_cparams(("arbitrary",)),
        name="gmlp_mix",
    )(z, z, ln_g, ln_b, ws_cat, bs_mat, gain)


def _merge_kernel(yp_ref, ya_ref, yh_ref, yg_ref, w_ref, x_ref, g1_ref, lg_ref, lb_ref, sh_ref, sc_ref,
                  wrh_ref, wrl_ref, rb_ref, xo_ref, h_ref, idx_ref, gate_ref, *, alpha):
    y = None
    for gi, y_ref in enumerate((yp_ref, ya_ref, yh_ref, yg_ref)):
        part = jnp.dot(y_ref[...], w_ref[gi * MIXER_W:(gi + 1) * MIXER_W, :], preferred_element_type=F32)
        y = part if y is None else y + part
    x1 = _ln(alpha * x_ref[...] + g1_ref[...] * y) * lg_ref[...] + lb_ref[...]
    xo_ref[...] = x1
    h = _ln(x1) * (1.0 + sc_ref[...]) + sh_ref[...]
    h_hi = h.astype(BF16)
    h_ref[...] = h
    h_lo = (h - h_hi.astype(F32)).astype(BF16)
    logits = (jnp.dot(h_hi, wrh_ref[...], preferred_element_type=F32)
              + (jnp.dot(h_lo, wrh_ref[...], preferred_element_type=F32)
                 + jnp.dot(h_hi, wrl_ref[...], preferred_element_type=F32)))
    idx, gate = _route_rows(logits, rb_ref[...])
    idx_ref[...] = idx
    gate_ref[...] = gate


def _route_rows(logits, bias):
    neg = -1e30
    lane = lax.broadcasted_iota(jnp.int32, logits.shape, 1)
    lane_f = lane.astype(F32)
    valid = lane < N_EXPERTS
    per = N_EXPERTS // N_EXPERT_GROUPS
    s = jax.nn.sigmoid(logits)
    sel = jnp.where(valid, s + bias, neg)
    sh = [sel] + [pltpu.roll(sel, LOGIT_PAD - j, axis=1) for j in range(1, per)]
    pair = None
    for a in range(per):
        for b in range(a + 1, per):
            t = sh[a] + sh[b]
            pair = t if pair is None else jnp.maximum(pair, t)
    grp = jnp.where(valid & ((lane & (per - 1)) == 0), pair, neg)

    def first_max(v):
        m = jnp.max(v, axis=-1, keepdims=True)
        return jnp.min(jnp.where(v == m, lane_f, float(LOGIT_PAD)), axis=-1, keepdims=True)

    best = first_max(grp).astype(jnp.int32)
    shift = per.bit_length() - 1
    cand = jnp.where(valid & ((lane >> shift) == (best >> shift)), sel, neg)
    i1 = first_max(cand)
    i2 = first_max(jnp.where(lane_f == i1, neg, cand))
    w1 = jnp.sum(jnp.where(lane_f == i1, s, 0.0), axis=-1, keepdims=True)
    w2 = jnp.sum(jnp.where(lane_f == i2, s, 0.0), axis=-1, keepdims=True)
    tot = w1 + w2
    idx = jnp.where(lane == 0, i1, jnp.where(lane == 1, i2, 0.0)).astype(jnp.int32)
    gate = jnp.where(lane == 0, w1 / tot, jnp.where(lane == 1, w2 / tot, 0.0))
    return idx, gate


def _merge(ys, w_out_bf, xa, mod3, ln_g, ln_b, wr_hi, wr_lo, rb_pad, n_rows, n_batch, seq, alpha):
    d = xa.shape[1]
    tm = 512

    def mrow(i):
        return jnp.minimum((i * tm) // seq, n_batch)

    ymix = pl.BlockSpec((tm, MIXER_W), lambda i: (i, 0))
    rowvec = pl.BlockSpec((1, d), lambda i: (0, 0))
    return pl.pallas_call(
        functools.partial(_merge_kernel, alpha=alpha),
        grid=(n_rows // tm,),
        in_specs=[
            ymix, ymix, ymix, ymix,
            pl.BlockSpec((d, d), lambda i: (0, 0)),
            pl.BlockSpec((tm, d), lambda i: (i, 0)),
            pl.BlockSpec((None, 1, d), lambda i: (mrow(i), 0, 2)),
            rowvec, rowvec,
            pl.BlockSpec((None, 1, d), lambda i: (mrow(i), 0, 3)),
            pl.BlockSpec((None, 1, d), lambda i: (mrow(i), 0, 4)),
            pl.BlockSpec((d, LOGIT_PAD), lambda i: (0, 0)),
            pl.BlockSpec((d, LOGIT_PAD), lambda i: (0, 0)),
            pl.BlockSpec((1, LOGIT_PAD), lambda i: (0, 0)),
        ],
        out_specs=[
            pl.BlockSpec((tm, d), lambda i: (i, 0)),
            pl.BlockSpec((tm, d), lambda i: (i, 0)),
            pl.BlockSpec((tm, LOGIT_PAD), lambda i: (i, 0)),
            pl.BlockSpec((tm, LOGIT_PAD), lambda i: (i, 0)),
        ],
        out_shape=[
            jax.ShapeDtypeStruct((n_rows, d), F32),
            jax.ShapeDtypeStruct((n_rows, d), F32),
            jax.ShapeDtypeStruct((n_rows, LOGIT_PAD), jnp.int32),
            jax.ShapeDtypeStruct((n_rows, LOGIT_PAD), F32),
        ],
        compiler_params=_cparams(("arbitrary",)),
        name="merge",
    )(*ys, w_out_bf, xa, mod3, ln_g, ln_b, mod3, mod3, wr_hi, wr_lo, rb_pad)


def _moe_kernel(be_ref, nu_ref, tok_ref, h_hbm, wg_ref, wu_ref, wd_ref, o_ref, xbuf, sem, wg_s, wu_s, wd_s):
    i = pl.program_id(0)
    n_used = nu_ref[0]
    slot = lax.rem(i, 2)

    def row_copy(blk, r, sl):
        tok = tok_ref[blk * MOE_BLOCK + r]
        return pltpu.make_async_copy(h_hbm.at[pl.ds(tok, 1)], xbuf.at[sl, pl.ds(r, 1)], sem.at[sl])

    def gather_start(blk, sl):
        def body(r, carry):
            row_copy(blk, r, sl).start()
            return carry

        lax.fori_loop(0, MOE_BLOCK, body, 0, unroll=8)

    def gather_wait(sl):
        pltpu.make_async_copy(h_hbm.at[pl.ds(0, MOE_BLOCK)], xbuf.at[sl], sem.at[sl]).wait()

    @pl.when(i == 0)
    def _():
        gather_start(0, 0)

    @pl.when(i + 1 < n_used)
    def _():
        gather_start(i + 1, 1 - slot)

    active = i < n_used
    new_expert = jnp.logical_or(i == 0, be_ref[i] != be_ref[jnp.maximum(i - 1, 0)])

    @pl.when(jnp.logical_and(active, new_expert))
    def _():
        for src, dst in ((wg_ref, wg_s), (wu_ref, wu_s), (wd_ref, wd_s)):
            def cast(r0, src=src, dst=dst):
                dst[pl.ds(r0, 256), :] = src[pl.ds(r0, 256), :].astype(BF16)

            _row_chunks(src.shape[0], 256, cast)

    @pl.when(active)
    def _():
        gather_wait(slot)
        x = xbuf[slot].astype(BF16)
        g = jnp.dot(x, wg_s[...], preferred_element_type=F32)
        u = jnp.dot(x, wu_s[...], preferred_element_type=F32)
        a = (g * jax.nn.sigmoid(g) * u).astype(BF16)
        o_ref[...] = jnp.dot(a, wd_s[...], preferred_element_type=F32).astype(o_ref.dtype)

    @pl.when(jnp.logical_not(active))
    def _():
        o_ref[...] = jnp.zeros_like(o_ref)


def _moe_experts(h, slot_tok, block_e, n_used, wg, wu, wd, layer):
    d = h.shape[1]
    n_slots = slot_tok.shape[0]
    de = wg.shape[3]
    n_blocks = n_slots // MOE_BLOCK
    grid_spec = pltpu.PrefetchScalarGridSpec(
        num_scalar_prefetch=3,
        grid=(n_blocks,),
        in_specs=[
            pl.BlockSpec(memory_space=pl.ANY),
            pl.BlockSpec((None, None, d, de), lambda i, be, nu, tok: (layer, be[i], 0, 0)),
            pl.BlockSpec((None, None, d, de), lambda i, be, nu, tok: (layer, be[i], 0, 0)),
            pl.BlockSpec((None, None, de, d), lambda i, be, nu, tok: (layer, be[i], 0, 0)),
        ],
        out_specs=pl.BlockSpec((MOE_BLOCK, d), lambda i, be, nu, tok: (i, 0)),
        scratch_shapes=[
            pltpu.VMEM((2, MOE_BLOCK, d), F32),
            pltpu.SemaphoreType.DMA((2,)),
            pltpu.VMEM((d, de), BF16), pltpu.VMEM((d, de), BF16), pltpu.VMEM((de, d), BF16),
        ],
    )
    return pl.pallas_call(
        _moe_kernel,
        grid_spec=grid_spec,
        out_shape=jax.ShapeDtypeStruct((n_slots, d), BF16),
        compiler_params=_cparams(("arbitrary",)),
        name="moe_experts",
    )(block_e, n_used, slot_tok, h, wg, wu, wd)


def _assignment_ranks(flat_e):
    a = flat_e.shape[0]
    blk = 128
    nb = a // blk
    onehot = (flat_e[:, None] == jnp.arange(N_EXPERTS)[None, :]).astype(BF16).reshape(nb, blk, N_EXPERTS)
    tri = (jnp.arange(blk)[:, None] >= jnp.arange(blk)[None, :]).astype(BF16)
    intra = jnp.einsum("ij,bjk->bik", tri, onehot, preferred_element_type=F32)
    bsum = intra[:, -1, :]
    before = (jnp.arange(nb)[:, None] > jnp.arange(nb)[None, :]).astype(BF16)
    offs = jnp.dot(before, bsum.astype(BF16), preferred_element_type=F32)
    csum = intra + offs[:, None, :]
    rank = jnp.sum(csum * onehot.astype(F32), axis=-1).reshape(a) - 1.0
    counts = offs[-1] + bsum[-1]
    return rank.astype(jnp.int32), counts.astype(jnp.int32)


def _dispatch_plan(idx):
    t = idx.shape[0]
    a = t * TOP_K
    flat_e = idx.reshape(a)
    rank, counts = _assignment_ranks(flat_e)
    padded = (counts + MOE_BLOCK - 1) // MOE_BLOCK * MOE_BLOCK
    pad_end = jnp.cumsum(padded)
    pad_start = pad_end - padded
    dest = pad_start[flat_e] + rank
    n_blocks = (a + N_EXPERTS * (MOE_BLOCK - 1) + MOE_BLOCK - 1) // MOE_BLOCK
    n_slots = n_blocks * MOE_BLOCK
    flat_tok = jnp.repeat(jnp.arange(t, dtype=jnp.int32), TOP_K)
    slot_tok = jnp.zeros((n_slots,), jnp.int32).at[dest].set(flat_tok)
    blk_start = jnp.arange(n_blocks, dtype=jnp.int32) * MOE_BLOCK
    block_e = jnp.minimum(
        jnp.sum((pad_end[None, :] <= blk_start[:, None]).astype(jnp.int32), axis=1), N_EXPERTS - 1)
    n_used = (pad_end[-1] // MOE_BLOCK).astype(jnp.int32).reshape(1)
    return slot_tok, block_e.astype(jnp.int32), n_used, dest.reshape(t, TOP_K)


def _final_kernel(x_ref, ya_ref, yb_ref, gate_ref, g2_ref, lg_ref, lb_ref, o_ref, *, alpha):
    f = ya_ref[...].astype(F32) * gate_ref[:, 0:1] + yb_ref[...].astype(F32) * gate_ref[:, 1:2]
    o_ref[...] = _ln(alpha * x_ref[...] + g2_ref[...] * f) * lg_ref[...] + lb_ref[...]


def _final_norm(x1, ya, yb, gate, mod3, ln_g, ln_b, n_batch, seq, alpha):
    n_rows, d = x1.shape
    tm = 512

    def mrow(i):
        return jnp.minimum((i * tm) // seq, n_batch)

    tile = pl.BlockSpec((tm, d), lambda i: (i, 0))
    rowvec = pl.BlockSpec((1, d), lambda i: (0, 0))
    return pl.pallas_call(
        functools.partial(_final_kernel, alpha=alpha),
        grid=(n_rows // tm,),
        in_specs=[tile, tile, tile, pl.BlockSpec((tm, LOGIT_PAD), lambda i: (i, 0)),
                  pl.BlockSpec((None, 1, d), lambda i: (mrow(i), 0, 5)), rowvec, rowvec],
        out_specs=tile,
        out_shape=jax.ShapeDtypeStruct((n_rows, d), F32),
        compiler_params=_cparams(("arbitrary",)),
        name="final_norm",
    )(x1, ya, yb, gate, mod3, ln_g, ln_b)


def _permute_in_cols(w):
    pool, q, k, v, hy, gm = jnp.split(w, (512, 1024, 1152, 1280, 2816), axis=-1)
    return jnp.concatenate([hy, pool, q, gm, k, v], axis=-1)


def kernel(x, c, ctx, c_ctx, w_ada, b_ada, w_in, w_out, mix_norm_g, pool_w, pool_scale, attn_sink,
           hy_conv_w, hy_conv_b, hy_w1, hy_b1, hy_f1, hy_w2, hy_b2, hy_f2, hy_w3, hy_skip,
           gm_ln_g, gm_ln_b, gm_ws, gm_bs, ln1_g, ln1_b, ln2_g, ln2_b, w_router, router_bias,
           w_gate, w_up, w_down):
    n_batch, seq, d = x.shape
    ctx_len = ctx.shape[1]
    depth = w_in.shape[0]
    grid_w = 64
    alpha = (2 * depth) ** 0.25
    n_lat = n_batch * seq
    n_ctx = n_batch * ctx_len

    xa = jnp.concatenate([x.reshape(n_lat, d), ctx.reshape(n_ctx, d)], axis=0)
    cvec = jnp.concatenate([c, c_ctx[None, :], jnp.zeros((8 - n_batch - 1, d), F32)], axis=0)
    mods = _ada_mods(cvec, w_ada, b_ada)
    cos_t, sin_t = _rope_tables(seq, grid_w)
    wr_pad = jnp.pad(w_router, ((0, 0), (0, LOGIT_PAD - N_EXPERTS)))
    wr_hi = wr_pad.astype(BF16)
    wr_lo = (wr_pad - wr_hi.astype(F32)).astype(BF16)
    rb_pad = jnp.pad(router_bias.astype(F32), (0, LOGIT_PAD - N_EXPERTS)).reshape(1, LOGIT_PAD)
    row = lambda v: v.reshape(1, -1)

    for layer in range(depth):
        last = layer == depth - 1
        mod3 = mods[layer].reshape(8, 1, 6 * d)
        gains = mix_norm_g[layer].reshape(4, 1, MIXER_W)
        lp = {
            "hy_conv_w": hy_conv_w[layer], "hy_conv_b": row(hy_conv_b[layer]),
            "hy_w1p": jnp.pad(hy_w1[layer], ((0, HY_EMB_PAD - HY_EMB), (0, 0))),
            "hy_b1": row(hy_b1[layer]), "hy_f1": row(hy_f1[layer]),
            "hy_w2": hy_w2[layer], "hy_b2": row(hy_b2[layer]), "hy_f2": row(hy_f2[layer]),
            "hy_w3": hy_w3[layer], "hy_skip": hy_skip[layer],
        }
        z = _in_proj(xa, mod3, _permute_in_cols(w_in[layer]).astype(BF16), n_batch, seq)

        pool_bf = pool_w[layer].astype(BF16)
        ps = row(pool_scale[layer])
        ws_cat = jnp.transpose(gm_ws[layer], (1, 0, 2)).reshape(GMLP_CHUNK, GMLP_GROUPS * GMLP_CHUNK)
        bs_mat = jnp.repeat(gm_bs[layer].T, MIXER_W // GMLP_GROUPS, axis=1)
        n_rows = n_lat if last else n_lat + n_ctx

        y_pool = _pool_mix(z, pool_bf, ps, gains[0], n_batch, seq, 0)
        y_attn = _attn_lat(z, attn_sink[layer], cos_t, sin_t, gains[1], n_batch, seq, ctx_len)
        y_hy = _hyena_mix(z, lp, gains[2], n_batch, seq, 0)
        y_gm = _gmlp_mix(z, row(gm_ln_g[layer]), row(gm_ln_b[layer]), ws_cat.astype(BF16), bs_mat,
                         gains[3], n_rows)
        if not last:
            y_pool = jnp.concatenate(
                [y_pool, _pool_mix(z, pool_bf, ps, gains[0], n_batch, ctx_len, n_lat // ctx_len)], axis=0)
            y_attn = jnp.concatenate(
                [y_attn, _attn_ctx(z, attn_sink[layer], gains[1], n_batch, seq, ctx_len)], axis=0)
            y_hy = jnp.concatenate([y_hy, _hyena_mix(z, lp, gains[2], n_batch, ctx_len, n_lat)], axis=0)

        x1, h2, idx, gate = _merge((y_pool, y_attn, y_hy, y_gm), w_out[layer].astype(BF16), xa, mod3,
                                   row(ln1_g[layer]), row(ln1_b[layer]), wr_hi, wr_lo, rb_pad,
                                   n_rows, n_batch, seq, alpha)

        slot_tok, block_e, n_used, dest = _dispatch_plan(idx[:, :TOP_K])
        ys = _moe_experts(h2, slot_tok, block_e, n_used, w_gate, w_up, w_down, layer)
        xa = _final_norm(x1, ys[dest[:, 0]], ys[dest[:, 1]], gate, mod3, row(ln2_g[layer]), row(ln2_b[layer]),
                         n_batch, seq, alpha)
    return xa.reshape(n_batch, seq, d)
```

```python
import functools
import math

import jax
import jax.numpy as jnp
from jax import lax
from jax.experimental import pallas as pl
from jax.experimental.pallas import tpu as pltpu

F32 = jnp.float32
BF16 = jnp.bfloat16
HI = lax.Precision.HIGHEST

LN_EPS = 1e-6
MIXER_W = 512
HEAD_DIM = 64
N_HEADS = 8
N_KV_HEADS = 2
KV_GROUP = 4
KV_W = 128
WINDOW = 128
ROPE_THETA = 10000.0
POOL_WINDOWS = (2, 4, 8, 16)
POOL_GROUP = 128
POOL_HALO = 8
GMLP_CHUNK = 128
GMLP_GROUPS = 8
HY_EMB = 33
HY_EMB_PAD = 128
HY_TARGET = 1e-2
HY_FAST = 0.3
HY_SLOW = 1.5
N_EXPERTS = 32
N_EXPERT_GROUPS = 8
TOP_K = 2
D_EXPERT = 768
MOE_BLOCK = 256
LOGIT_PAD = 128

COL_HY = 0
COL_POOL = 3
COL_Q = 4
COL_GU = 5
COL_GV = 6
COL_K = 28
COL_V = 29
IN_W = 3840

VMEM_LIMIT = 56 * 1024 * 1024


def _cparams(sem):
    return pltpu.CompilerParams(dimension_semantics=sem, vmem_limit_bytes=VMEM_LIMIT)


def _ln(x):
    mu = jnp.mean(x, axis=-1, keepdims=True)
    xc = x - mu
    var = jnp.mean(xc * xc, axis=-1, keepdims=True)
    return xc * lax.rsqrt(var + LN_EPS)


def _rms_gain(y, g):
    return y * lax.rsqrt(jnp.mean(y * y, axis=-1, keepdims=True) + LN_EPS) * g


def _row_chunks(n_rows, chunk, body):
    n = n_rows // chunk
    if n == 1:
        body(0)
        return

    def step(i, carry):
        body(pl.multiple_of(i * chunk, chunk))
        return carry

    lax.fori_loop(0, n, step, 0)


def _ada_kernel(c_ref, w_ref, b_ref, o_ref):
    c = c_ref[...]
    s = c * jax.nn.sigmoid(c)
    o_ref[...] = jnp.dot(s, w_ref[...], precision=HI, preferred_element_type=F32) + b_ref[...]


def _ada_mods(cvec, w_ada, b_ada):
    depth, d, n = w_ada.shape
    tn = 1024
    return pl.pallas_call(
        _ada_kernel,
        grid=(depth, n // tn),
        in_specs=[
            pl.BlockSpec((8, d), lambda l, j: (0, 0)),
            pl.BlockSpec((None, d, tn), lambda l, j: (l, 0, j)),
            pl.BlockSpec((None, 1, tn), lambda l, j: (l, 0, j)),
        ],
        out_specs=pl.BlockSpec((None, 8, tn), lambda l, j: (l, 0, j)),
        out_shape=jax.ShapeDtypeStruct((depth, 8, n), F32),
        compiler_params=_cparams(("arbitrary", "arbitrary")),
        name="ada_mods",
    )(cvec, w_ada, b_ada.reshape(depth, 1, n))


def _inproj_kernel(x_ref, sh_ref, sc_ref, w_ref, o_ref, h_scr):
    @pl.when(pl.program_id(1) == 0)
    def _():
        sh = sh_ref[...]
        sc1 = 1.0 + sc_ref[...]

        def body(r0):
            x = x_ref[pl.ds(r0, 256), :]
            h_scr[pl.ds(r0, 256), :] = (_ln(x) * sc1 + sh).astype(BF16)

        _row_chunks(x_ref.shape[0], 256, body)

    o_ref[...] = jnp.dot(h_scr[...], w_ref[...], preferred_element_type=F32)


def _in_proj(xa, mod3, w_bf, n_batch, seq):
    t, d = xa.shape
    n = w_bf.shape[1]
    tn = 768
    tm = next(m for m in (1024, 512, 256) if t % m == 0 and seq % m == 0)

    def mrow(i):
        return jnp.minimum((i * tm) // seq, n_batch)

    return pl.pallas_call(
        _inproj_kernel,
        grid=(t // tm, n // tn),
        in_specs=[
            pl.BlockSpec((tm, d), lambda i, j: (i, 0)),
            pl.BlockSpec((None, 1, d), lambda i, j: (mrow(i), 0, 0)),
            pl.BlockSpec((None, 1, d), lambda i, j: (mrow(i), 0, 1)),
            pl.BlockSpec((d, tn), lambda i, j: (0, j)),
        ],
        out_specs=pl.BlockSpec((tm, tn), lambda i, j: (i, j)),
        out_shape=jax.ShapeDtypeStruct((t, n), F32),
        scratch_shapes=[pltpu.VMEM((tm, d), BF16)],
        compiler_params=_cparams(("arbitrary", "arbitrary")),
        name="in_proj",
    )(xa, mod3, mod3, w_bf)


def _pool_kernel(z_ref, w_ref, ps_ref, g_ref, o_ref, buf):
    seq = z_ref.shape[0]
    rc = min(256, seq)
    zeros = jnp.zeros((POOL_HALO, MIXER_W), F32)
    buf[0:POOL_HALO, :] = zeros
    buf[seq + POOL_HALO:seq + 2 * POOL_HALO, :] = zeros

    def fill(r0):
        buf[pl.ds(r0 + POOL_HALO, rc), :] = z_ref[pl.ds(r0, rc), :]

    _row_chunks(seq, rc, fill)

    def body(r0):
        win = buf[pl.ds(r0, rc + 2 * POOL_HALO), :]
        t = r0 + lax.broadcasted_iota(jnp.int32, (rc, 1), 0)
        outs = []
        for gi, w in enumerate(POOL_WINDOWS):
            lanes = slice(gi * POOL_GROUP, (gi + 1) * POOL_GROUP)
            acc = None
            for j in range(-w // 2, w // 2):
                piece = win[POOL_HALO + j:POOL_HALO + j + rc, lanes]
                acc = piece if acc is None else acc + piece
            cnt = (jnp.minimum(t + w // 2, seq) - jnp.maximum(t - w // 2, 0)).astype(F32)
            dlt = acc / cnt - win[POOL_HALO:POOL_HALO + rc, lanes]
            outs.append(jnp.dot(dlt.astype(BF16), w_ref[gi], preferred_element_type=F32))
        y = jnp.concatenate(outs, axis=-1) * ps_ref[...]
        o_ref[pl.ds(r0, rc), :] = _rms_gain(y, g_ref[...]).astype(BF16)

    _row_chunks(seq, rc, body)


def _pool_mix(z, pool_w_bf, pool_scale, gain, n_seq, seq, row_blk0):
    return pl.pallas_call(
        _pool_kernel,
        grid=(n_seq,),
        in_specs=[
            pl.BlockSpec((seq, MIXER_W), lambda s: (row_blk0 + s, COL_POOL)),
            pl.BlockSpec((4, POOL_GROUP, POOL_GROUP), lambda s: (0, 0, 0)),
            pl.BlockSpec((1, MIXER_W), lambda s: (0, 0)),
            pl.BlockSpec((1, MIXER_W), lambda s: (0, 0)),
        ],
        out_specs=pl.BlockSpec((seq, MIXER_W), lambda s: (s, 0)),
        out_shape=jax.ShapeDtypeStruct((n_seq * seq, MIXER_W), BF16),
        scratch_shapes=[pltpu.VMEM((seq + 2 * POOL_HALO, MIXER_W), F32)],
        compiler_params=_cparams(("arbitrary",)),
        name="pool_mix",
    )(z, pool_w_bf, pool_scale, gain)


def _rope(x, cos, sin_signed):
    lane = lax.broadcasted_iota(jnp.int32, x.shape, 1)
    partner = jnp.where((lane & 31) < 16, pltpu.roll(x, 112, axis=1), pltpu.roll(x, 16, axis=1))
    return x * cos + partner * sin_signed


def _attend(qs, sink_col, parts):
    dn = (((1,), (1,)), ((), ()))
    scores = []
    for kk, _, mask in parts:
        s = lax.dot_general(qs, kk, dn, preferred_element_type=F32)
        if mask is not None:
            s = jnp.where(mask, s, -1e30)
        scores.append(s)
    m = sink_col
    for s in scores:
        m = jnp.maximum(m, jnp.max(s, axis=-1, keepdims=True))
    den = jnp.exp(sink_col - m)
    out = None
    for s, (_, vv, _) in zip(scores, parts):
        p = jnp.exp(s - m)
        den = den + jnp.sum(p, axis=-1, keepdims=True)
        o = jnp.dot(p.astype(BF16), vv, preferred_element_type=F32)
        out = o if out is None else out + o
    return out / den


def _heads_attend(q_all, sink_ref, tq, make_parts):
    cols = [None] * N_HEADS
    for kh in range(N_KV_HEADS):
        heads = [kh * KV_GROUP + g for g in range(KV_GROUP)]
        qs = jnp.concatenate([q_all[:, h * HEAD_DIM:(h + 1) * HEAD_DIM] for h in heads], axis=0)
        sink_col = jnp.concatenate([jnp.full((tq, 1), sink_ref[h], F32) for h in heads], axis=0)
        o = _attend(qs.astype(BF16), sink_col, make_parts(kh))
        for g, h in enumerate(heads):
            cols[h] = o[g * tq:(g + 1) * tq, :]
    return jnp.concatenate(cols, axis=-1)


def _attn_lat_kernel(sink_ref, q_ref, k_ref, v_ref, kc_ref, vc_ref, cos_ref, sin_ref, g_ref, o_ref):
    tq = q_ref.shape[0]
    seq = k_ref.shape[0]
    nband = tq + 2 * WINDOW
    i = pl.program_id(1)
    q0 = pl.multiple_of(i * tq, tq)
    k0 = pl.multiple_of(jnp.clip(q0 - WINDOW, 0, seq - nband), WINDOW)

    cq = cos_ref[pl.ds(q0, tq), :]
    sq = sin_ref[pl.ds(q0, tq), :]
    q = q_ref[...]
    q_all = jnp.concatenate(
        [_rope(q[:, c * 128:(c + 1) * 128], cq, sq) for c in range(MIXER_W // 128)], axis=-1)
    q_all = q_all * (HEAD_DIM ** -0.5)

    kb = _rope(k_ref[pl.ds(k0, nband), :], cos_ref[pl.ds(k0, nband), :],
               sin_ref[pl.ds(k0, nband), :]).astype(BF16)
    vb = v_ref[pl.ds(k0, nband), :].astype(BF16)
    kc = kc_ref[...].astype(BF16)
    vc = vc_ref[...].astype(BF16)

    qpos = q0 + (lax.broadcasted_iota(jnp.int32, (KV_GROUP * tq, nband), 0) & (tq - 1))
    kpos = k0 + lax.broadcasted_iota(jnp.int32, (KV_GROUP * tq, nband), 1)
    mask = jnp.abs(qpos - kpos) <= WINDOW

    def make_parts(kh):
        hs = slice(kh * HEAD_DIM, (kh + 1) * HEAD_DIM)
        return [(kb[:, hs], vb[:, hs], mask), (kc[:, hs], vc[:, hs], None)]

    y = _heads_attend(q_all, sink_ref, tq, make_parts)
    o_ref[...] = _rms_gain(y, g_ref[...]).astype(BF16)


def _attn_ctx_kernel(sink_ref, q_ref, kc_ref, vc_ref, g_ref, o_ref):
    tq = q_ref.shape[0]
    q_all = q_ref[...] * (HEAD_DIM ** -0.5)
    kc = kc_ref[...].astype(BF16)
    vc = vc_ref[...].astype(BF16)

    def make_parts(kh):
        hs = slice(kh * HEAD_DIM, (kh + 1) * HEAD_DIM)
        return [(kc[:, hs], vc[:, hs], None)]

    y = _heads_attend(q_all, sink_ref, tq, make_parts)
    o_ref[...] = _rms_gain(y, g_ref[...]).astype(BF16)


def _attn_lat(z, sink, cos_t, sin_t, gain, n_batch, seq, ctx_len):
    tq = 128
    nq = seq // tq
    ctx_blk0 = (n_batch * seq) // ctx_len
    smem = pl.BlockSpec(memory_space=pltpu.SMEM)
    return pl.pallas_call(
        _attn_lat_kernel,
        grid=(n_batch, nq),
        in_specs=[
            smem,
            pl.BlockSpec((tq, MIXER_W), lambda b, i: (b * nq + i, COL_Q)),
            pl.BlockSpec((seq, KV_W), lambda b, i: (b, COL_K)),
            pl.BlockSpec((seq, KV_W), lambda b, i: (b, COL_V)),
            pl.BlockSpec((ctx_len, KV_W), lambda b, i: (ctx_blk0 + b, COL_K)),
            pl.BlockSpec((ctx_len, KV_W), lambda b, i: (ctx_blk0 + b, COL_V)),
            pl.BlockSpec((seq, 128), lambda b, i: (0, 0)),
            pl.BlockSpec((seq, 128), lambda b, i: (0, 0)),
            pl.BlockSpec((1, MIXER_W), lambda b, i: (0, 0)),
        ],
        out_specs=pl.BlockSpec((tq, MIXER_W), lambda b, i: (b * nq + i, 0)),
        out_shape=jax.ShapeDtypeStruct((n_batch * seq, MIXER_W), BF16),
        compiler_params=_cparams(("arbitrary", "arbitrary")),
        name="attn_lat",
    )(sink, z, z, z, z, z, cos_t, sin_t, gain)


def _attn_ctx(z, sink, gain, n_batch, seq, ctx_len):
    ctx_blk0 = (n_batch * seq) // ctx_len
    smem = pl.BlockSpec(memory_space=pltpu.SMEM)
    return pl.pallas_call(
        _attn_ctx_kernel,
        grid=(n_batch,),
        in_specs=[
            smem,
            pl.BlockSpec((ctx_len, MIXER_W), lambda b: (ctx_blk0 + b, COL_Q)),
            pl.BlockSpec((ctx_len, KV_W), lambda b: (ctx_blk0 + b, COL_K)),
            pl.BlockSpec((ctx_len, KV_W), lambda b: (ctx_blk0 + b, COL_V)),
            pl.BlockSpec((1, MIXER_W), lambda b: (0, 0)),
        ],
        out_specs=pl.BlockSpec((ctx_len, MIXER_W), lambda b: (b, 0)),
        out_shape=jax.ShapeDtypeStruct((n_batch * ctx_len, MIXER_W), BF16),
        compiler_params=_cparams(("arbitrary",)),
        name="attn_ctx",
    )(sink, z, z, z, gain)


def _rope_tables(seq, grid_w):
    nf = HEAD_DIM // 4
    inv = ROPE_THETA ** (-jnp.arange(nf, dtype=F32) / nf)
    t = jnp.arange(seq)
    row = (t // grid_w).astype(F32)[:, None] * inv[None, :]
    col = (t % grid_w).astype(F32)[:, None] * inv[None, :]
    cos_h = jnp.concatenate([jnp.cos(row), jnp.cos(row), jnp.cos(col), jnp.cos(col)], axis=-1)
    sin_h = jnp.concatenate([-jnp.sin(row), jnp.sin(row), -jnp.sin(col), jnp.sin(col)], axis=-1)
    return jnp.tile(cos_h, (1, 2)), jnp.tile(sin_h, (1, 2))


def _hy_conv_kernel(*refs):
    z_refs, zp_refs, zn_refs = refs[0:3], refs[3:6], refs[6:9]
    w_ref, b_ref, g_ref, u_ref = refs[9:]
    i = pl.program_id(1)
    th = z_refs[0].shape[0] // 2
    has_prev = (i > 0).astype(F32)
    has_next = (i < pl.num_programs(1) - 1).astype(F32)
    row = lax.broadcasted_iota(jnp.int32, (th, 128), 0)
    for part in range(3):
        ze = z_refs[part][pl.ds(0, th, stride=2), :]
        zo = z_refs[part][pl.ds(1, th, stride=2), :]
        prev_row = zp_refs[part][POOL_HALO - 1:POOL_HALO, :] * has_prev
        next_row = zn_refs[part][0:1, :] * has_next
        zo_m = jnp.where(row == 0, prev_row, pltpu.roll(zo, 1, axis=0))
        ze_p = jnp.where(row == th - 1, next_row, pltpu.roll(ze, th - 1, axis=0))
        w0, w1, w2, b = w_ref[part, 0:1, :], w_ref[part, 1:2, :], w_ref[part, 2:3, :], b_ref[part]
        ye = b + zo_m * w0 + ze * w1 + zo * w2
        yo = b + ze * w0 + zo * w1 + ze_p * w2
        g_ref[part, 0] = ye
        g_ref[part, 1] = yo
        if part == 2:
            u_ref[0] = ye.astype(BF16)
            u_ref[1] = yo.astype(BF16)


def _hy_conv(z, conv_w, conv_b, n_seq, seq, row0):
    tl = min(2048, seq)
    nt = seq // tl
    blk0 = row0 // tl
    hb0 = row0 // POOL_HALO
    hpt = tl // POOL_HALO
    last_halo = (row0 + n_seq * seq) // POOL_HALO - 1
    cpp = MIXER_W // 128
    ncol = n_seq * MIXER_W

    def main_spec(part):
        return pl.BlockSpec((tl, 128), lambda s, i, c: (blk0 + s * nt + i, part * cpp + c))

    def prev_spec(part):
        return pl.BlockSpec((POOL_HALO, 128),
                            lambda s, i, c: (jnp.maximum(hb0 + (s * nt + i) * hpt - 1, 0), part * cpp + c))

    def next_spec(part):
        return pl.BlockSpec((POOL_HALO, 128),
                            lambda s, i, c: (jnp.minimum(hb0 + (s * nt + i + 1) * hpt, last_halo), part * cpp + c))

    w3 = conv_w.reshape(3, 3, MIXER_W).transpose(1, 0, 2)
    b3 = conv_b.reshape(3, 1, MIXER_W)
    return pl.pallas_call(
        _hy_conv_kernel,
        grid=(n_seq, nt, cpp),
        in_specs=[main_spec(p) for p in range(3)] + [prev_spec(p) for p in range(3)]
        + [next_spec(p) for p in range(3)]
        + [pl.BlockSpec((3, 3, 128), lambda s, i, c: (0, 0, c)),
           pl.BlockSpec((3, 1, 128), lambda s, i, c: (0, 0, c))],
        out_specs=[
            pl.BlockSpec((3, 2, tl // 2, 128), lambda s, i, c: (0, 0, i, s * cpp + c)),
            pl.BlockSpec((2, tl // 2, 128), lambda s, i, c: (0, i, s * cpp + c)),
        ],
        out_shape=[jax.ShapeDtypeStruct((3, 2, seq // 2, ncol), F32),
                   jax.ShapeDtypeStruct((2, seq // 2, ncol), BF16)],
        compiler_params=_cparams(("arbitrary", "arbitrary", "arbitrary")),
        name="hy_conv",
    )(*([z] * 9), w3, b3)


def _hy_filter_kernel(feat_ref, dec_ref, w1_ref, b1_ref, f1_ref, w2_ref, b2_ref, f2_ref, w3_ref, o_ref):
    tl = feat_ref.shape[0]
    h = jnp.dot(feat_ref[...], w1_ref[...], precision=HI, preferred_element_type=F32) + b1_ref[...]
    h = jnp.sin(f1_ref[...] * h)
    h = jnp.dot(h, w2_ref[...], precision=HI, preferred_element_type=F32) + b2_ref[...]
    h = jnp.sin(f2_ref[...] * h)
    filt = jnp.dot(h, w3_ref[...], precision=HI, preferred_element_type=F32)
    dec = dec_ref[...]
    t = 2 * (pl.program_id(1) * tl + lax.broadcasted_iota(jnp.int32, (tl, 1), 0)) + pl.program_id(0)
    not_first = (t > 0).astype(F32)
    for o in range(2):
        hf = filt[:, (2 * o) * MIXER_W:(2 * o + 1) * MIXER_W] * dec
        hb = filt[:, (2 * o + 1) * MIXER_W:(2 * o + 2) * MIXER_W] * dec * not_first
        o_ref[:, o * MIXER_W:(o + 1) * MIXER_W] = (hf + hb).astype(BF16)
        o_ref[:, (2 + o) * MIXER_W:(3 + o) * MIXER_W] = (hb - hf).astype(BF16)


def _hy_filters(feat, decay, w1p, b1, f1, w2, b2, f2, w3):
    half = feat.shape[1]
    tl = min(512, half)
    hid = w2.shape[0]
    full = lambda shape: pl.BlockSpec(shape, lambda p, i: tuple(0 for _ in shape))
    return pl.pallas_call(
        _hy_filter_kernel,
        grid=(2, half // tl),
        in_specs=[
            pl.BlockSpec((None, tl, HY_EMB_PAD), lambda p, i: (p, i, 0)),
            pl.BlockSpec((None, tl, MIXER_W), lambda p, i: (p, i, 0)),
            full((HY_EMB_PAD, hid)), full((1, hid)), full((1, hid)),
            full((hid, hid)), full((1, hid)), full((1, hid)),
            full((hid, 4 * MIXER_W)),
        ],
        out_specs=pl.BlockSpec((None, tl, 4 * MIXER_W), lambda p, i: (p, i, 0)),
        out_shape=jax.ShapeDtypeStruct((2, half, 4 * MIXER_W), BF16),
        compiler_params=_cparams(("arbitrary", "arbitrary")),
        name="hy_filters",
    )(feat, decay, w1p, b1, f1, w2, b2, f2, w3)


def _hy_kspec_kernel(te_ref, to_ref, ab_ref, o_ref, *, inv_len):
    hb = te_ref.shape[0] // 2
    half = ab_ref.shape[2] // 2
    ec = jnp.dot(te_ref[0:hb, :], ab_ref[0, :, 0:half], preferred_element_type=F32)
    oc = jnp.dot(to_ref[0:hb, :], ab_ref[1, :, 0:half], preferred_element_type=F32)
    es = jnp.dot(te_ref[hb:, :], ab_ref[0, :, half:], preferred_element_type=F32)
    os_ = jnp.dot(to_ref[hb:, :], ab_ref[1, :, half:], preferred_element_type=F32)
    o_ref[0, 0:hb, :] = (ec + oc) * inv_len
    o_ref[0, hb:, :] = (es + os_) * inv_len
    o_ref[1, 0:hb, :] = (ec - oc) * inv_len
    o_ref[1, hb:, :] = (os_ - es) * inv_len


def _hy_kspec(t_e, t_o, ab, tm):
    seq, half = t_e.shape
    return pl.pallas_call(
        functools.partial(_hy_kspec_kernel, inv_len=1.0 / seq),
        grid=(seq // tm,),
        in_specs=[
            pl.BlockSpec((tm, half), lambda i: (i, 0)),
            pl.BlockSpec((tm, half), lambda i: (i, 0)),
            pl.BlockSpec((2, half, 4 * MIXER_W), lambda i: (0, 0, 0)),
        ],
        out_specs=pl.BlockSpec((2, tm, 2 * MIXER_W), lambda i: (0, i, 0)),
        out_shape=jax.ShapeDtypeStruct((2, seq, 2 * MIXER_W), F32),
        compiler_params=_cparams(("arbitrary",)),
        name="hy_kspec",
    )(t_e, t_o, ab)


def _hy_fwd_kernel(te_ref, to_ref, u_ref, k_ref, o_ref):
    hb = te_ref.shape[0] // 2
    reps = u_ref.shape[2] // MIXER_W
    ep = jnp.dot(te_ref[...], u_ref[0], preferred_element_type=F32)
    op = jnp.dot(to_ref[...], u_ref[1], preferred_element_type=F32)
    ec, es, oc, os_ = ep[0:hb, :], ep[hb:, :], op[0:hb, :], op[hb:, :]
    tile = lambda v: jnp.concatenate([v] * reps, axis=-1)
    p, q, pm, qm = ec + oc, es + os_, ec - oc, os_ - es
    kr, ki = tile(k_ref[0, 0:hb, :]), tile(k_ref[0, hb:, :])
    krm, kim = tile(k_ref[1, 0:hb, :]), tile(k_ref[1, hb:, :])
    yr, yn = p * kr + q * ki, q * kr - p * ki
    yrm, ynm = pm * krm + qm * kim, qm * krm - pm * kim
    o_ref[0, 0:hb, :] = (yr + yrm).astype(BF16)
    o_ref[0, hb:, :] = (yn - ynm).astype(BF16)
    o_ref[1, 0:hb, :] = (yr - yrm).astype(BF16)
    o_ref[1, hb:, :] = (yn + ynm).astype(BF16)


def _hy_fwd(t_e, t_o, u, kspec, order, tm, tn):
    seq, half = t_e.shape
    ncol = u.shape[2]
    return pl.pallas_call(
        _hy_fwd_kernel,
        grid=(ncol // tn, seq // tm),
        in_specs=[
            pl.BlockSpec((tm, half), lambda j, i: (i, 0)),
            pl.BlockSpec((tm, half), lambda j, i: (i, 0)),
            pl.BlockSpec((2, half, tn), lambda j, i: (0, 0, j)),
            pl.BlockSpec((2, tm, MIXER_W), lambda j, i: (0, i, order)),
        ],
        out_specs=pl.BlockSpec((2, tm, tn), lambda j, i: (0, i, j)),
        out_shape=jax.ShapeDtypeStruct((2, seq, ncol), BF16),
        compiler_params=_cparams(("arbitrary", "arbitrary")),
        name="hy_fwd",
    )(t_e, t_o, u, kspec)


def _hy_inv_kernel(ie_ref, io_ref, y_ref, u_ref, gate_ref, skip_ref, *rest, final):
    reps = u_ref.shape[2] // MIXER_W
    skip = jnp.concatenate([skip_ref[...]] * reps, axis=-1)
    for par, inv_ref in enumerate((ie_ref, io_ref)):
        conv = jnp.dot(inv_ref[...], y_ref[par], preferred_element_type=F32)
        out = gate_ref[par] * (conv + u_ref[par].astype(F32) * skip)
        if final:
            g_ref, o_ref = rest
            for r in range(reps):
                o_ref[r, par] = _rms_gain(out[:, r * MIXER_W:(r + 1) * MIXER_W], g_ref[...]).astype(BF16)
        else:
            (o_ref,) = rest
            o_ref[par] = out.astype(BF16)


def _hy_inv(inv_e, inv_o, yspec, u, conv, part, skip, gain, final, tm, tn):
    half, seq = inv_e.shape
    ncol = u.shape[2]
    reps = tn // MIXER_W
    in_specs = [
        pl.BlockSpec((tm, seq), lambda j, i: (i, 0)),
        pl.BlockSpec((tm, seq), lambda j, i: (i, 0)),
        pl.BlockSpec((2, seq, tn), lambda j, i: (0, 0, j)),
        pl.BlockSpec((2, tm, tn), lambda j, i: (0, i, j)),
        pl.BlockSpec((None, 2, tm, tn), lambda j, i: (part, 0, i, j)),
        pl.BlockSpec((1, MIXER_W), lambda j, i: (0, 0)),
    ]
    args = [inv_e, inv_o, yspec, u, conv, skip]
    if final:
        in_specs.append(pl.BlockSpec((1, MIXER_W), lambda j, i: (0, 0)))
        args.append(gain)
        out_spec = pl.BlockSpec((reps, 2, tm, MIXER_W), lambda j, i: (j, 0, i, 0))
        out_shape = jax.ShapeDtypeStruct((ncol // MIXER_W, 2, half, MIXER_W), BF16)
    else:
        out_spec = pl.BlockSpec((2, tm, tn), lambda j, i: (0, i, j))
        out_shape = jax.ShapeDtypeStruct((2, half, ncol), BF16)
    return pl.pallas_call(
        functools.partial(_hy_inv_kernel, final=final),
        grid=(ncol // tn, half // tm),
        in_specs=in_specs,
        out_specs=out_spec,
        out_shape=out_shape,
        compiler_params=_cparams(("arbitrary", "arbitrary")),
        name="hy_inv",
    )(*args)


def _hy_tables(seq, tm):
    half = seq // 2
    k = jnp.arange(half, dtype=jnp.int32)
    hb = tm // 2
    nb = half // hb
    tabs = []
    for par in range(2):
        n = 2 * k + par
        m = ((2 * k + 1)[:, None] * n[None, :]) % (4 * seq)
        ang = m.astype(F32) * (2.0 * math.pi / (4 * seq))
        cs = jnp.stack([jnp.cos(ang).reshape(nb, hb, half), jnp.sin(ang).reshape(nb, hb, half)], axis=1)
        tabs.append(cs.reshape(seq, half).astype(BF16))
    return tabs[0], tabs[1]


def _hy_features(seq):
    t = jnp.linspace(0.0, 1.0, seq, dtype=F32)[:, None]
    bands = (HY_EMB - 1) // 2
    freqs = jnp.linspace(1e-4, bands - 1, bands, dtype=F32)[None, :]
    ang = (2.0 * math.pi / seq) * jnp.arange(seq, dtype=F32)[:, None] * freqs
    feat = jnp.concatenate([t, jnp.cos(ang), -jnp.sin(ang)], -1)
    feat = jnp.pad(feat, ((0, 0), (0, HY_EMB_PAD - HY_EMB)))
    deltas = jnp.abs(jnp.linspace(math.log(HY_TARGET) / HY_SLOW, math.log(HY_TARGET) / HY_FAST,
                                  MIXER_W, dtype=F32))
    decay = jnp.exp(-t * deltas[None, :])
    split = lambda a: jnp.stack([a[0::2], a[1::2]], axis=0)
    return split(feat), split(decay)


def _hyena_mix(z, lp, gain, n_seq, seq, row0):
    half = seq // 2
    tm_f = min(512, seq)
    tm_i = min(256, half)
    ncol = n_seq * MIXER_W
    tn = min(1024, ncol)
    t_e, t_o = _hy_tables(seq, tm_f)
    inv_e, inv_o = t_e.T, t_o.T
    feat, decay = _hy_features(seq)
    ab = _hy_filters(feat, decay, lp["hy_w1p"], lp["hy_b1"], lp["hy_f1"], lp["hy_w2"], lp["hy_b2"],
                     lp["hy_f2"], lp["hy_w3"])
    kspec = _hy_kspec(t_e, t_o, ab, tm_f)
    conv, u = _hy_conv(z, lp["hy_conv_w"], lp["hy_conv_b"], n_seq, seq, row0)
    y0 = _hy_fwd(t_e, t_o, u, kspec, 0, tm_f, tn)
    u1 = _hy_inv(inv_e, inv_o, y0, u, conv, 0, lp["hy_skip"][0:1], None, False, tm_i, tn)
    y1 = _hy_fwd(t_e, t_o, u1, kspec, 1, tm_f, tn)
    out = _hy_inv(inv_e, inv_o, y1, u1, conv, 1, lp["hy_skip"][1:2], gain, True, tm_i, tn)
    return jnp.transpose(out, (0, 2, 1, 3)).reshape(n_seq * seq, MIXER_W)


def _gelu(x):
    c = math.sqrt(2.0 / math.pi)
    return 0.5 * x * (1.0 + jnp.tanh(c * (x + 0.044715 * (x * x * x))))


def _gmlp_kernel(u_ref, v_ref, lg_ref, lb_ref, ws_ref, bs_ref, g_ref, o_ref):
    tl = u_ref.shape[0]
    gw = MIXER_W // GMLP_GROUPS
    lane_grp = lax.broadcasted_iota(jnp.int32, (GMLP_CHUNK, MIXER_W), 1) // gw
    for c in range(tl // GMLP_CHUNK):
        rows = slice(c * GMLP_CHUNK, (c + 1) * GMLP_CHUNK)
        v = _ln(_gelu(v_ref[rows, :])) * lg_ref[...] + lb_ref[...]
        vb = v.astype(BF16)
        stacked = jnp.concatenate(
            [jnp.where(lane_grp == g, vb, jnp.zeros_like(vb)) for g in range(GMLP_GROUPS)], axis=0)
        s = jnp.dot(ws_ref[...], stacked, preferred_element_type=F32) + bs_ref[...]
        y = _gelu(u_ref[rows, :]) * s
        o_ref[rows, :] = _rms_gain(y, g_ref[...]).astype(BF16)


def _gmlp_mix(z, ln_g, ln_b, ws_cat, bs_mat, gain, n_rows):
    tl = 512
    full = lambda shape: pl.BlockSpec(shape, lambda i: tuple(0 for _ in shape))
    return pl.pallas_call(
        _gmlp_kernel,
        grid=(n_rows // tl,),
        in_specs=[
            pl.BlockSpec((tl, MIXER_W), lambda i: (i, COL_GU)),
            pl.BlockSpec((tl, MIXER_W), lambda i: (i, COL_GV)),
            full((1, MIXER_W)), full((1, MIXER_W)),
            full((GMLP_CHUNK, GMLP_GROUPS * GMLP_CHUNK)),
            full((GMLP_CHUNK, MIXER_W)),
            full((1, MIXER_W)),
        ],
        out_specs=pl.BlockSpec((tl, MIXER_W), lambda i: (i, 0)),
        out_shape=jax.ShapeDtypeStruct((n_rows, MIXER_W), BF16),
        compiler_params=_cparams(("arbitrary",)),
        name="gmlp_mix",
    )(z, z, ln_g, ln_b, ws_cat, bs_mat, gain)


def _merge_kernel(yp_ref, ya_ref, yh_ref, yg_ref, w_ref, x_ref, g1_ref, lg_ref, lb_ref, sh_ref, sc_ref,
                  wrh_ref, wrl_ref, rb_ref, xo_ref, h_ref, idx_ref, gate_ref, *, alpha):
    ymix = jnp.concatenate([yp_ref[...], ya_ref[...], yh_ref[...], yg_ref[...]], axis=-1)
    y = jnp.dot(ymix, w_ref[...], preferred_element_type=F32)
    x1 = _ln(alpha * x_ref[...] + g1_ref[...] * y) * lg_ref[...] + lb_ref[...]
    xo_ref[...] = x1
    h = _ln(x1) * (1.0 + sc_ref[...]) + sh_ref[...]
    h_ref[...] = h
    h_hi = h.astype(BF16)
    h_lo = (h - h_hi.astype(F32)).astype(BF16)
    logits = (jnp.dot(h_hi, wrh_ref[...], preferred_element_type=F32)
              + (jnp.dot(h_lo, wrh_ref[...], preferred_element_type=F32)
                 + jnp.dot(h_hi, wrl_ref[...], preferred_element_type=F32)))
    idx, gate = _route_rows(logits, rb_ref[...])
    idx_ref[...] = idx
    gate_ref[...] = gate


def _route_rows(logits, bias):
    neg = -1e30
    lane = lax.broadcasted_iota(jnp.int32, logits.shape, 1)
    lane_f = lane.astype(F32)
    valid = lane < N_EXPERTS
    per = N_EXPERTS // N_EXPERT_GROUPS
    s = jax.nn.sigmoid(logits)
    sel = jnp.where(valid, s + bias, neg)
    sh = [sel] + [pltpu.roll(sel, LOGIT_PAD - j, axis=1) for j in range(1, per)]
    pair = None
    for a in range(per):
        for b in range(a + 1, per):
            t = sh[a] + sh[b]
            pair = t if pair is None else jnp.maximum(pair, t)
    grp = jnp.where(valid & ((lane & (per - 1)) == 0), pair, neg)

    def first_max(v):
        m = jnp.max(v, axis=-1, keepdims=True)
        return jnp.min(jnp.where(v == m, lane_f, float(LOGIT_PAD)), axis=-1, keepdims=True)

    best = first_max(grp).astype(jnp.int32)
    shift = per.bit_length() - 1
    cand = jnp.where(valid & ((lane >> shift) == (best >> shift)), sel, neg)
    i1 = first_max(cand)
    i2 = first_max(jnp.where(lane_f == i1, neg, cand))
    w1 = jnp.sum(jnp.where(lane_f == i1, s, 0.0), axis=-1, keepdims=True)
    w2 = jnp.sum(jnp.where(lane_f == i2, s, 0.0), axis=-1, keepdims=True)
    tot = w1 + w2
    idx = jnp.where(lane == 0, i1, jnp.where(lane == 1, i2, 0.0)).astype(jnp.int32)
    gate = jnp.where(lane == 0, w1 / tot, jnp.where(lane == 1, w2 / tot, 0.0))
    return idx, gate


def _merge(ys, w_out_bf, xa, mod3, ln_g, ln_b, wr_hi, wr_lo, rb_pad, n_rows, n_batch, seq, alpha):
    d = xa.shape[1]
    tm = 512

    def mrow(i):
        return jnp.minimum((i * tm) // seq, n_batch)

    ymix = pl.BlockSpec((tm, MIXER_W), lambda i: (i, 0))
    rowvec = pl.BlockSpec((1, d), lambda i: (0, 0))
    return pl.pallas_call(
        functools.partial(_merge_kernel, alpha=alpha),
        grid=(n_rows // tm,),
        in_specs=[
            ymix, ymix, ymix, ymix,
            pl.BlockSpec((d, d), lambda i: (0, 0)),
            pl.BlockSpec((tm, d), lambda i: (i, 0)),
            pl.BlockSpec((None, 1, d), lambda i: (mrow(i), 0, 2)),
            rowvec, rowvec,
            pl.BlockSpec((None, 1, d), lambda i: (mrow(i), 0, 3)),
            pl.BlockSpec((None, 1, d), lambda i: (mrow(i), 0, 4)),
            pl.BlockSpec((d, LOGIT_PAD), lambda i: (0, 0)),
            pl.BlockSpec((d, LOGIT_PAD), lambda i: (0, 0)),
            pl.BlockSpec((1, LOGIT_PAD), lambda i: (0, 0)),
        ],
        out_specs=[
            pl.BlockSpec((tm, d), lambda i: (i, 0)),
            pl.BlockSpec((tm, d), lambda i: (i, 0)),
            pl.BlockSpec((tm, LOGIT_PAD), lambda i: (i, 0)),
            pl.BlockSpec((tm, LOGIT_PAD), lambda i: (i, 0)),
        ],
        out_shape=[
            jax.ShapeDtypeStruct((n_rows, d), F32),
            jax.ShapeDtypeStruct((n_rows, d), F32),
            jax.ShapeDtypeStruct((n_rows, LOGIT_PAD), jnp.int32),
            jax.ShapeDtypeStruct((n_rows, LOGIT_PAD), F32),
        ],
        compiler_params=_cparams(("arbitrary",)),
        name="merge",
    )(*ys, w_out_bf, xa, mod3, ln_g, ln_b, mod3, mod3, wr_hi, wr_lo, rb_pad)


def _moe_kernel(be_ref, nu_ref, tok_ref, h_hbm, wg_ref, wu_ref, wd_ref, o_ref, xbuf, sem, wg_s, wu_s, wd_s):
    i = pl.program_id(0)
    n_used = nu_ref[0]
    slot = lax.rem(i, 2)

    def row_copy(blk, r, sl):
        tok = tok_ref[blk * MOE_BLOCK + r]
        return pltpu.make_async_copy(h_hbm.at[pl.ds(tok, 1)], xbuf.at[sl, pl.ds(r, 1)], sem.at[sl])

    def gather_start(blk, sl):
        def body(r, carry):
            row_copy(blk, r, sl).start()
            return carry

        lax.fori_loop(0, MOE_BLOCK, body, 0, unroll=8)

    def gather_wait(sl):
        pltpu.make_async_copy(h_hbm.at[pl.ds(0, MOE_BLOCK)], xbuf.at[sl], sem.at[sl]).wait()

    @pl.when(i == 0)
    def _():
        gather_start(0, 0)

    active = i < n_used
    new_expert = jnp.logical_or(i == 0, be_ref[i] != be_ref[jnp.maximum(i - 1, 0)])

    @pl.when(jnp.logical_and(active, new_expert))
    def _():
        for src, dst in ((wg_ref, wg_s), (wu_ref, wu_s), (wd_ref, wd_s)):
            def cast(r0, src=src, dst=dst):
                dst[pl.ds(r0, 256), :] = src[pl.ds(r0, 256), :].astype(BF16)

            _row_chunks(src.shape[0], 256, cast)

    def ffn_block(prefetch_next):
        gather_wait(slot)
        x = xbuf[slot].astype(BF16)
        if prefetch_next:
            for r in range(MOE_BLOCK):
                row_copy(i + 1, r, 1 - slot).start()
        g = jnp.dot(x, wg_s[...], preferred_element_type=F32)
        u = jnp.dot(x, wu_s[...], preferred_element_type=F32)
        a = (g * jax.nn.sigmoid(g) * u).astype(BF16)
        o_ref[...] = jnp.dot(a, wd_s[...], preferred_element_type=F32).astype(o_ref.dtype)

    has_next = i + 1 < n_used
    pl.when(jnp.logical_and(active, has_next))(functools.partial(ffn_block, True))
    pl.when(jnp.logical_and(active, jnp.logical_not(has_next)))(functools.partial(ffn_block, False))

    @pl.when(jnp.logical_not(active))
    def _():
        o_ref[...] = jnp.zeros_like(o_ref)


def _moe_experts(h, slot_tok, block_e, n_used, wg, wu, wd, layer):
    d = h.shape[1]
    n_slots = slot_tok.shape[0]
    de = wg.shape[3]
    n_blocks = n_slots // MOE_BLOCK
    grid_spec = pltpu.PrefetchScalarGridSpec(
        num_scalar_prefetch=3,
        grid=(n_blocks,),
        in_specs=[
            pl.BlockSpec(memory_space=pl.ANY),
            pl.BlockSpec((None, None, d, de), lambda i, be, nu, tok: (layer, be[i], 0, 0)),
            pl.BlockSpec((None, None, d, de), lambda i, be, nu, tok: (layer, be[i], 0, 0)),
            pl.BlockSpec((None, None, de, d), lambda i, be, nu, tok: (layer, be[i], 0, 0)),
        ],
        out_specs=pl.BlockSpec((MOE_BLOCK, d), lambda i, be, nu, tok: (i, 0)),
        scratch_shapes=[
            pltpu.VMEM((2, MOE_BLOCK, d), F32),
            pltpu.SemaphoreType.DMA((2,)),
            pltpu.VMEM((d, de), BF16), pltpu.VMEM((d, de), BF16), pltpu.VMEM((de, d), BF16),
        ],
    )
    return pl.pallas_call(
        _moe_kernel,
        grid_spec=grid_spec,
        out_shape=jax.ShapeDtypeStruct((n_slots, d), BF16),
        compiler_params=_cparams(("arbitrary",)),
        name="moe_experts",
    )(block_e, n_used, slot_tok, h, wg, wu, wd)


def _assignment_ranks(flat_e):
    a = flat_e.shape[0]
    blk = 128
    nb = a // blk
    onehot = (flat_e[:, None] == jnp.arange(N_EXPERTS)[None, :]).astype(BF16).reshape(nb, blk, N_EXPERTS)
    tri = (jnp.arange(blk)[:, None] >= jnp.arange(blk)[None, :]).astype(BF16)
    intra = jnp.einsum("ij,bjk->bik", tri, onehot, preferred_element_type=F32)
    bsum = intra[:, -1, :]
    before = (jnp.arange(nb)[:, None] > jnp.arange(nb)[None, :]).astype(BF16)
    offs = jnp.dot(before, bsum.astype(BF16), preferred_element_type=F32)
    csum = intra + offs[:, None, :]
    rank = jnp.sum(csum * onehot.astype(F32), axis=-1).reshape(a) - 1.0
    counts = offs[-1] + bsum[-1]
    return rank.astype(jnp.int32), counts.astype(jnp.int32)


def _dispatch_plan(idx):
    t = idx.shape[0]
    a = t * TOP_K
    flat_e = idx.reshape(a)
    rank, counts = _assignment_ranks(flat_e)
    padded = (counts + MOE_BLOCK - 1) // MOE_BLOCK * MOE_BLOCK
    pad_end = jnp.cumsum(padded)
    pad_start = pad_end - padded
    dest = pad_start[flat_e] + rank
    n_blocks = (a + N_EXPERTS * (MOE_BLOCK - 1) + MOE_BLOCK - 1) // MOE_BLOCK
    n_slots = n_blocks * MOE_BLOCK
    flat_tok = jnp.repeat(jnp.arange(t, dtype=jnp.int32), TOP_K)
    slot_tok = jnp.zeros((n_slots,), jnp.int32).at[dest].set(flat_tok)
    blk_start = jnp.arange(n_blocks, dtype=jnp.int32) * MOE_BLOCK
    block_e = jnp.minimum(
        jnp.sum((pad_end[None, :] <= blk_start[:, None]).astype(jnp.int32), axis=1), N_EXPERTS - 1)
    n_used = (pad_end[-1] // MOE_BLOCK).astype(jnp.int32).reshape(1)
    return slot_tok, block_e.astype(jnp.int32), n_used, dest.reshape(t, TOP_K)


def _final_kernel(x_ref, ya_ref, yb_ref, gate_ref, g2_ref, lg_ref, lb_ref, o_ref, *, alpha):
    f = ya_ref[...].astype(F32) * gate_ref[:, 0:1] + yb_ref[...].astype(F32) * gate_ref[:, 1:2]
    o_ref[...] = _ln(alpha * x_ref[...] + g2_ref[...] * f) * lg_ref[...] + lb_ref[...]


def _final_norm(x1, ya, yb, gate, mod3, ln_g, ln_b, n_batch, seq, alpha):
    n_rows, d = x1.shape
    tm = 512

    def mrow(i):
        return jnp.minimum((i * tm) // seq, n_batch)

    tile = pl.BlockSpec((tm, d), lambda i: (i, 0))
    rowvec = pl.BlockSpec((1, d), lambda i: (0, 0))
    return pl.pallas_call(
        functools.partial(_final_kernel, alpha=alpha),
        grid=(n_rows // tm,),
        in_specs=[tile, tile, tile, pl.BlockSpec((tm, LOGIT_PAD), lambda i: (i, 0)),
                  pl.BlockSpec((None, 1, d), lambda i: (mrow(i), 0, 5)), rowvec, rowvec],
        out_specs=tile,
        out_shape=jax.ShapeDtypeStruct((n_rows, d), F32),
        compiler_params=_cparams(("arbitrary",)),
        name="final_norm",
    )(x1, ya, yb, gate, mod3, ln_g, ln_b)


def _permute_in_cols(w):
    pool, q, k, v, hy, gm = jnp.split(w, (512, 1024, 1152, 1280, 2816), axis=-1)
    return jnp.concatenate([hy, pool, q, gm, k, v], axis=-1)


def kernel(x, c, ctx, c_ctx, w_ada, b_ada, w_in, w_out, mix_norm_g, pool_w, pool_scale, attn_sink,
           hy_conv_w, hy_conv_b, hy_w1, hy_b1, hy_f1, hy_w2, hy_b2, hy_f2, hy_w3, hy_skip,
           gm_ln_g, gm_ln_b, gm_ws, gm_bs, ln1_g, ln1_b, ln2_g, ln2_b, w_router, router_bias,
           w_gate, w_up, w_down):
    n_batch, seq, d = x.shape
    ctx_len = ctx.shape[1]
    depth = w_in.shape[0]
    grid_w = 64
    alpha = (2 * depth) ** 0.25
    n_lat = n_batch * seq
    n_ctx = n_batch * ctx_len

    xa = jnp.concatenate([x.reshape(n_lat, d), ctx.reshape(n_ctx, d)], axis=0)
    cvec = jnp.concatenate([c, c_ctx[None, :], jnp.zeros((8 - n_batch - 1, d), F32)], axis=0)
    mods = _ada_mods(cvec, w_ada, b_ada)
    cos_t, sin_t = _rope_tables(seq, grid_w)
    wr_pad = jnp.pad(w_router, ((0, 0), (0, LOGIT_PAD - N_EXPERTS)))
    wr_hi = wr_pad.astype(BF16)
    wr_lo = (wr_pad - wr_hi.astype(F32)).astype(BF16)
    rb_pad = jnp.pad(router_bias.astype(F32), (0, LOGIT_PAD - N_EXPERTS)).reshape(1, LOGIT_PAD)
    row = lambda v: v.reshape(1, -1)

    for layer in range(depth):
        last = layer == depth - 1
        mod3 = mods[layer].reshape(8, 1, 6 * d)
        gains = mix_norm_g[layer].reshape(4, 1, MIXER_W)
        lp = {
            "hy_conv_w": hy_conv_w[layer], "hy_conv_b": row(hy_conv_b[layer]),
            "hy_w1p": jnp.pad(hy_w1[layer], ((0, HY_EMB_PAD - HY_EMB), (0, 0))),
            "hy_b1": row(hy_b1[layer]), "hy_f1": row(hy_f1[layer]),
            "hy_w2": hy_w2[layer], "hy_b2": row(hy_b2[layer]), "hy_f2": row(hy_f2[layer]),
            "hy_w3": hy_w3[layer], "hy_skip": hy_skip[layer],
        }
        z = _in_proj(xa, mod3, _permute_in_cols(w_in[layer]).astype(BF16), n_batch, seq)

        pool_bf = pool_w[layer].astype(BF16)
        ps = row(pool_scale[layer])
        ws_cat = jnp.transpose(gm_ws[layer], (1, 0, 2)).reshape(GMLP_CHUNK, GMLP_GROUPS * GMLP_CHUNK)
        bs_mat = jnp.repeat(gm_bs[layer].T, MIXER_W // GMLP_GROUPS, axis=1)
        n_rows = n_lat if last else n_lat + n_ctx

        y_pool = _pool_mix(z, pool_bf, ps, gains[0], n_batch, seq, 0)
        y_attn = _attn_lat(z, attn_sink[layer], cos_t, sin_t, gains[1], n_batch, seq, ctx_len)
        y_hy = _hyena_mix(z, lp, gains[2], n_batch, seq, 0)
        y_gm = _gmlp_mix(z, row(gm_ln_g[layer]), row(gm_ln_b[layer]), ws_cat.astype(BF16), bs_mat,
                         gains[3], n_rows)
        if not last:
            y_pool = jnp.concatenate(
                [y_pool, _pool_mix(z, pool_bf, ps, gains[0], n_batch, ctx_len, n_lat // ctx_len)], axis=0)
            y_attn = jnp.concatenate(
                [y_attn, _attn_ctx(z, attn_sink[layer], gains[1], n_batch, seq, ctx_len)], axis=0)
            y_hy = jnp.concatenate([y_hy, _hyena_mix(z, lp, gains[2], n_batch, ctx_len, n_lat)], axis=0)

        x1, h2, idx, gate = _merge((y_pool, y_attn, y_hy, y_gm), w_out[layer].astype(BF16), xa, mod3,
                                   row(ln1_g[layer]), row(ln1_b[layer]), wr_hi, wr_lo, rb_pad,
                                   n_rows, n_batch, seq, alpha)

        slot_tok, block_e, n_used, dest = _dispatch_plan(idx[:, :TOP_K])
        ys = _moe_experts(h2, slot_tok, block_e, n_used, w_gate, w_up, w_down, layer)
        xa = _final_norm(x1, ys[dest[:, 0]], ys[dest[:, 1]], gate, mod3, row(ln2_g[layer]), row(ln2_b[layer]),
                         n_batch, seq, alpha)
    return xa.reshape(n_batch, seq, d)
```

```python
import functools
import math

import jax
import jax.numpy as jnp
from jax import lax
from jax.experimental import pallas as pl
from jax.experimental.pallas import tpu as pltpu

F32 = jnp.float32
BF16 = jnp.bfloat16
HI = lax.Precision.HIGHEST

LN_EPS = 1e-6
MIXER_W = 512
HEAD_DIM = 64
N_HEADS = 8
N_KV_HEADS = 2
KV_GROUP = 4
KV_W = 128
WINDOW = 128
ROPE_THETA = 10000.0
POOL_WINDOWS = (2, 4, 8, 16)
POOL_GROUP = 128
POOL_HALO = 8
GMLP_CHUNK = 128
GMLP_GROUPS = 8
HY_EMB = 33
HY_EMB_PAD = 128
HY_TARGET = 1e-2
HY_FAST = 0.3
HY_SLOW = 1.5
N_EXPERTS = 32
N_EXPERT_GROUPS = 8
TOP_K = 2
D_EXPERT = 768
MOE_BLOCK = 256
LOGIT_PAD = 128

COL_HY = 0
COL_POOL = 3
COL_Q = 4
COL_GU = 5
COL_GV = 6
COL_K = 28
COL_V = 29
IN_W = 3840

VMEM_LIMIT = 56 * 1024 * 1024


def _cparams(sem):
    return pltpu.CompilerParams(dimension_semantics=sem, vmem_limit_bytes=VMEM_LIMIT)


def _ln(x):
    mu = jnp.mean(x, axis=-1, keepdims=True)
    xc = x - mu
    var = jnp.mean(xc * xc, axis=-1, keepdims=True)
    return xc * lax.rsqrt(var + LN_EPS)


def _rms_gain(y, g):
    return y * lax.rsqrt(jnp.mean(y * y, axis=-1, keepdims=True) + LN_EPS) * g


def _row_chunks(n_rows, chunk, body):
    n = n_rows // chunk
    if n == 1:
        body(0)
        return

    def step(i, carry):
        body(pl.multiple_of(i * chunk, chunk))
        return carry

    lax.fori_loop(0, n, step, 0)


def _ada_kernel(c_ref, w_ref, b_ref, o_ref):
    c = c_ref[...]
    s = c * jax.nn.sigmoid(c)
    o_ref[...] = jnp.dot(s, w_ref[...], precision=HI, preferred_element_type=F32) + b_ref[...]


def _ada_mods(cvec, w_ada, b_ada):
    depth, d, n = w_ada.shape
    tn = 1024
    return pl.pallas_call(
        _ada_kernel,
        grid=(depth, n // tn),
        in_specs=[
            pl.BlockSpec((8, d), lambda l, j: (0, 0)),
            pl.BlockSpec((None, d, tn), lambda l, j: (l, 0, j)),
            pl.BlockSpec((None, 1, tn), lambda l, j: (l, 0, j)),
        ],
        out_specs=pl.BlockSpec((None, 8, tn), lambda l, j: (l, 0, j)),
        out_shape=jax.ShapeDtypeStruct((depth, 8, n), F32),
        compiler_params=_cparams(("arbitrary", "arbitrary")),
        name="ada_mods",
    )(cvec, w_ada, b_ada.reshape(depth, 1, n))


def _inproj_kernel(x_ref, sh_ref, sc_ref, w_ref, o_ref, h_scr):
    @pl.when(pl.program_id(1) == 0)
    def _():
        sh = sh_ref[...]
        sc1 = 1.0 + sc_ref[...]

        def body(r0):
            x = x_ref[pl.ds(r0, 256), :]
            h_scr[pl.ds(r0, 256), :] = (_ln(x) * sc1 + sh).astype(BF16)

        _row_chunks(x_ref.shape[0], 256, body)

    o_ref[...] = jnp.dot(h_scr[...], w_ref[...], preferred_element_type=F32)


def _in_proj(xa, mod3, w_bf, n_batch, seq):
    t, d = xa.shape
    n = w_bf.shape[1]
    tn = 768
    tm = next(m for m in (1024, 512, 256) if t % m == 0 and seq % m == 0)

    def mrow(i):
        return jnp.minimum((i * tm) // seq, n_batch)

    return pl.pallas_call(
        _inproj_kernel,
        grid=(t // tm, n // tn),
        in_specs=[
            pl.BlockSpec((tm, d), lambda i, j: (i, 0)),
            pl.BlockSpec((None, 1, d), lambda i, j: (mrow(i), 0, 0)),
            pl.BlockSpec((None, 1, d), lambda i, j: (mrow(i), 0, 1)),
            pl.BlockSpec((d, tn), lambda i, j: (0, j)),
        ],
        out_specs=pl.BlockSpec((tm, tn), lambda i, j: (i, j)),
        out_shape=jax.ShapeDtypeStruct((t, n), F32),
        scratch_shapes=[pltpu.VMEM((tm, d), BF16)],
        compiler_params=_cparams(("arbitrary", "arbitrary")),
        name="in_proj",
    )(xa, mod3, mod3, w_bf)


def _pool_kernel(z_ref, w_ref, ps_ref, g_ref, o_ref, buf):
    seq = z_ref.shape[0]
    rc = min(256, seq)
    zeros = jnp.zeros((POOL_HALO, MIXER_W), F32)
    buf[0:POOL_HALO, :] = zeros
    buf[seq + POOL_HALO:seq + 2 * POOL_HALO, :] = zeros

    def fill(r0):
        buf[pl.ds(r0 + POOL_HALO, rc), :] = z_ref[pl.ds(r0, rc), :]

    _row_chunks(seq, rc, fill)

    def body(r0):
        win = buf[pl.ds(r0, rc + 2 * POOL_HALO), :]
        t = r0 + lax.broadcasted_iota(jnp.int32, (rc, 1), 0)
        outs = []
        for gi, w in enumerate(POOL_WINDOWS):
            lanes = slice(gi * POOL_GROUP, (gi + 1) * POOL_GROUP)
            acc = None
            for j in range(-w // 2, w // 2):
                piece = win[POOL_HALO + j:POOL_HALO + j + rc, lanes]
                acc = piece if acc is None else acc + piece
            cnt = (jnp.minimum(t + w // 2, seq) - jnp.maximum(t - w // 2, 0)).astype(F32)
            dlt = acc / cnt - win[POOL_HALO:POOL_HALO + rc, lanes]
            outs.append(jnp.dot(dlt.astype(BF16), w_ref[gi], preferred_element_type=F32))
        y = jnp.concatenate(outs, axis=-1) * ps_ref[...]
        o_ref[pl.ds(r0, rc), :] = _rms_gain(y, g_ref[...]).astype(BF16)

    _row_chunks(seq, rc, body)


def _pool_mix(z, pool_w_bf, pool_scale, gain, n_seq, seq, row_blk0):
    return pl.pallas_call(
        _pool_kernel,
        grid=(n_seq,),
        in_specs=[
            pl.BlockSpec((seq, MIXER_W), lambda s: (row_blk0 + s, COL_POOL)),
            pl.BlockSpec((4, POOL_GROUP, POOL_GROUP), lambda s: (0, 0, 0)),
            pl.BlockSpec((1, MIXER_W), lambda s: (0, 0)),
            pl.BlockSpec((1, MIXER_W), lambda s: (0, 0)),
        ],
        out_specs=pl.BlockSpec((seq, MIXER_W), lambda s: (s, 0)),
        out_shape=jax.ShapeDtypeStruct((n_seq * seq, MIXER_W), BF16),
        scratch_shapes=[pltpu.VMEM((seq + 2 * POOL_HALO, MIXER_W), F32)],
        compiler_params=_cparams(("arbitrary",)),
        name="pool_mix",
    )(z, pool_w_bf, pool_scale, gain)


def _rope(x, cos, sin_signed):
    lane = lax.broadcasted_iota(jnp.int32, x.shape, 1)
    partner = jnp.where((lane & 31) < 16, pltpu.roll(x, 112, axis=1), pltpu.roll(x, 16, axis=1))
    return x * cos + partner * sin_signed


def _attend(qs, sink_col, parts):
    dn = (((1,), (1,)), ((), ()))
    scores = []
    for kk, _, mask in parts:
        s = lax.dot_general(qs, kk, dn, preferred_element_type=F32)
        if mask is not None:
            s = s + mask
        scores.append(s)
    m = sink_col
    for s in scores:
        m = jnp.maximum(m, jnp.max(s, axis=-1, keepdims=True))
    den = jnp.exp(sink_col - m)
    out = None
    for s, (_, vv, _) in zip(scores, parts):
        p = jnp.exp(s - m)
        den = den + jnp.sum(p, axis=-1, keepdims=True)
        o = jnp.dot(p.astype(BF16), vv, preferred_element_type=F32)
        out = o if out is None else out + o
    return out / den


def _heads_attend(q_all, sink_ref, tq, make_parts):
    cols = [None] * N_HEADS
    for kh in range(N_KV_HEADS):
        heads = [kh * KV_GROUP + g for g in range(KV_GROUP)]
        qs = jnp.concatenate([q_all[:, h * HEAD_DIM:(h + 1) * HEAD_DIM] for h in heads], axis=0)
        sink_col = jnp.concatenate([jnp.full((tq, 1), sink_ref[h], F32) for h in heads], axis=0)
        o = _attend(qs.astype(BF16), sink_col, make_parts(kh))
        for g, h in enumerate(heads):
            cols[h] = o[g * tq:(g + 1) * tq, :]
    return jnp.concatenate(cols, axis=-1)


def _attn_lat_kernel(sink_ref, q_ref, k_ref, v_ref, kc_ref, vc_ref, cos_ref, sin_ref, g_ref, o_ref):
    tq = q_ref.shape[0]
    seq = k_ref.shape[0]
    nband = tq + 2 * WINDOW
    i = pl.program_id(1)
    q0 = pl.multiple_of(i * tq, tq)
    k0 = pl.multiple_of(jnp.clip(q0 - WINDOW, 0, seq - nband), WINDOW)

    cq = cos_ref[pl.ds(q0, tq), :]
    sq = sin_ref[pl.ds(q0, tq), :]
    q = q_ref[...]
    q_all = jnp.concatenate(
        [_rope(q[:, c * 128:(c + 1) * 128], cq, sq) for c in range(MIXER_W // 128)], axis=-1)
    q_all = q_all * (HEAD_DIM ** -0.5)

    kb = _rope(k_ref[pl.ds(k0, nband), :], cos_ref[pl.ds(k0, nband), :],
               sin_ref[pl.ds(k0, nband), :]).astype(BF16)
    vb = v_ref[pl.ds(k0, nband), :].astype(BF16)
    kc = kc_ref[...].astype(BF16)
    vc = vc_ref[...].astype(BF16)

    qpos = q0 + lax.broadcasted_iota(jnp.int32, (tq, nband), 0)
    kpos = k0 + lax.broadcasted_iota(jnp.int32, (tq, nband), 1)
    band = jnp.where(jnp.abs(qpos - kpos) <= WINDOW, 0.0, -1e30)
    mask = jnp.concatenate([band] * KV_GROUP, axis=0)

    def make_parts(kh):
        hs = slice(kh * HEAD_DIM, (kh + 1) * HEAD_DIM)
        return [(kb[:, hs], vb[:, hs], mask), (kc[:, hs], vc[:, hs], None)]

    y = _heads_attend(q_all, sink_ref, tq, make_parts)
    o_ref[...] = _rms_gain(y, g_ref[...]).astype(BF16)


def _attn_ctx_kernel(sink_ref, q_ref, kc_ref, vc_ref, g_ref, o_ref):
    tq = q_ref.shape[0]
    q_all = q_ref[...] * (HEAD_DIM ** -0.5)
    kc = kc_ref[...].astype(BF16)
    vc = vc_ref[...].astype(BF16)

    def make_parts(kh):
        hs = slice(kh * HEAD_DIM, (kh + 1) * HEAD_DIM)
        return [(kc[:, hs], vc[:, hs], None)]

    y = _heads_attend(q_all, sink_ref, tq, make_parts)
    o_ref[...] = _rms_gain(y, g_ref[...]).astype(BF16)


def _attn_lat(z, sink, cos_t, sin_t, gain, n_batch, seq, ctx_len):
    tq = 256
    nq = seq // tq
    ctx_blk0 = (n_batch * seq) // ctx_len
    smem = pl.BlockSpec(memory_space=pltpu.SMEM)
    return pl.pallas_call(
        _attn_lat_kernel,
        grid=(n_batch, nq),
        in_specs=[
            smem,
            pl.BlockSpec((tq, MIXER_W), lambda b, i: (b * nq + i, COL_Q)),
            pl.BlockSpec((seq, KV_W), lambda b, i: (b, COL_K)),
            pl.BlockSpec((seq, KV_W), lambda b, i: (b, COL_V)),
            pl.BlockSpec((ctx_len, KV_W), lambda b, i: (ctx_blk0 + b, COL_K)),
            pl.BlockSpec((ctx_len, KV_W), lambda b, i: (ctx_blk0 + b, COL_V)),
            pl.BlockSpec((seq, 128), lambda b, i: (0, 0)),
            pl.BlockSpec((seq, 128), lambda b, i: (0, 0)),
            pl.BlockSpec((1, MIXER_W), lambda b, i: (0, 0)),
        ],
        out_specs=pl.BlockSpec((tq, MIXER_W), lambda b, i: (b * nq + i, 0)),
        out_shape=jax.ShapeDtypeStruct((n_batch * seq, MIXER_W), BF16),
        compiler_params=_cparams(("arbitrary", "arbitrary")),
        name="attn_lat",
    )(sink, z, z, z, z, z, cos_t, sin_t, gain)


def _attn_ctx(z, sink, gain, n_batch, seq, ctx_len):
    ctx_blk0 = (n_batch * seq) // ctx_len
    smem = pl.BlockSpec(memory_space=pltpu.SMEM)
    return pl.pallas_call(
        _attn_ctx_kernel,
        grid=(n_batch,),
        in_specs=[
            smem,
            pl.BlockSpec((ctx_len, MIXER_W), lambda b: (ctx_blk0 + b, COL_Q)),
            pl.BlockSpec((ctx_len, KV_W), lambda b: (ctx_blk0 + b, COL_K)),
            pl.BlockSpec((ctx_len, KV_W), lambda b: (ctx_blk0 + b, COL_V)),
            pl.BlockSpec((1, MIXER_W), lambda b: (0, 0)),
        ],
        out_specs=pl.BlockSpec((ctx_len, MIXER_W), lambda b: (b, 0)),
        out_shape=jax.ShapeDtypeStruct((n_batch * ctx_len, MIXER_W), BF16),
        compiler_params=_cparams(("arbitrary",)),
        name="attn_ctx",
    )(sink, z, z, z, gain)


def _rope_tables(seq, grid_w):
    nf = HEAD_DIM // 4
    inv = ROPE_THETA ** (-jnp.arange(nf, dtype=F32) / nf)
    t = jnp.arange(seq)
    row = (t // grid_w).astype(F32)[:, None] * inv[None, :]
    col = (t % grid_w).astype(F32)[:, None] * inv[None, :]
    cos_h = jnp.concatenate([jnp.cos(row), jnp.cos(row), jnp.cos(col), jnp.cos(col)], axis=-1)
    sin_h = jnp.concatenate([-jnp.sin(row), jnp.sin(row), -jnp.sin(col), jnp.sin(col)], axis=-1)
    return jnp.tile(cos_h, (1, 2)), jnp.tile(sin_h, (1, 2))


def _hy_conv_kernel(*refs):
    z_refs, zp_refs, zn_refs = refs[0:3], refs[3:6], refs[6:9]
    w_ref, b_ref, g_ref, u_ref = refs[9:]
    i = pl.program_id(1)
    th = z_refs[0].shape[0] // 2
    has_prev = (i > 0).astype(F32)
    has_next = (i < pl.num_programs(1) - 1).astype(F32)
    row = lax.broadcasted_iota(jnp.int32, (th, 128), 0)
    for part in range(3):
        ze = z_refs[part][pl.ds(0, th, stride=2), :]
        zo = z_refs[part][pl.ds(1, th, stride=2), :]
        prev_row = zp_refs[part][POOL_HALO - 1:POOL_HALO, :] * has_prev
        next_row = zn_refs[part][0:1, :] * has_next
        zo_m = jnp.where(row == 0, prev_row, pltpu.roll(zo, 1, axis=0))
        ze_p = jnp.where(row == th - 1, next_row, pltpu.roll(ze, th - 1, axis=0))
        w0, w1, w2, b = w_ref[part, 0:1, :], w_ref[part, 1:2, :], w_ref[part, 2:3, :], b_ref[part]
        ye = b + zo_m * w0 + ze * w1 + zo * w2
        yo = b + ze * w0 + zo * w1 + ze_p * w2
        g_ref[part, 0] = ye
        g_ref[part, 1] = yo
        if part == 2:
            u_ref[0] = ye.astype(BF16)
            u_ref[1] = yo.astype(BF16)


def _hy_conv(z, conv_w, conv_b, n_seq, seq, row0):
    tl = min(2048, seq)
    nt = seq // tl
    blk0 = row0 // tl
    hb0 = row0 // POOL_HALO
    hpt = tl // POOL_HALO
    last_halo = (row0 + n_seq * seq) // POOL_HALO - 1
    cpp = MIXER_W // 128
    ncol = n_seq * MIXER_W

    def main_spec(part):
        return pl.BlockSpec((tl, 128), lambda s, i, c: (blk0 + s * nt + i, part * cpp + c))

    def prev_spec(part):
        return pl.BlockSpec((POOL_HALO, 128),
                            lambda s, i, c: (jnp.maximum(hb0 + (s * nt + i) * hpt - 1, 0), part * cpp + c))

    def next_spec(part):
        return pl.BlockSpec((POOL_HALO, 128),
                            lambda s, i, c: (jnp.minimum(hb0 + (s * nt + i + 1) * hpt, last_halo), part * cpp + c))

    w3 = conv_w.reshape(3, 3, MIXER_W).transpose(1, 0, 2)
    b3 = conv_b.reshape(3, 1, MIXER_W)
    return pl.pallas_call(
        _hy_conv_kernel,
        grid=(n_seq, nt, cpp),
        in_specs=[main_spec(p) for p in range(3)] + [prev_spec(p) for p in range(3)]
        + [next_spec(p) for p in range(3)]
        + [pl.BlockSpec((3, 3, 128), lambda s, i, c: (0, 0, c)),
           pl.BlockSpec((3, 1, 128), lambda s, i, c: (0, 0, c))],
        out_specs=[
            pl.BlockSpec((3, 2, tl // 2, 128), lambda s, i, c: (0, 0, i, s * cpp + c)),
            pl.BlockSpec((2, tl // 2, 128), lambda s, i, c: (0, i, s * cpp + c)),
        ],
        out_shape=[jax.ShapeDtypeStruct((3, 2, seq // 2, ncol), F32),
                   jax.ShapeDtypeStruct((2, seq // 2, ncol), BF16)],
        compiler_params=_cparams(("arbitrary", "arbitrary", "arbitrary")),
        name="hy_conv",
    )(*([z] * 9), w3, b3)


def _hy_filter_kernel(feat_ref, dec_ref, w1_ref, b1_ref, f1_ref, w2_ref, b2_ref, f2_ref, w3_ref, o_ref):
    tl = feat_ref.shape[0]
    h = jnp.dot(feat_ref[...], w1_ref[...], precision=HI, preferred_element_type=F32) + b1_ref[...]
    h = jnp.sin(f1_ref[...] * h)
    h = jnp.dot(h, w2_ref[...], precision=HI, preferred_element_type=F32) + b2_ref[...]
    h = jnp.sin(f2_ref[...] * h)
    filt = jnp.dot(h, w3_ref[...], precision=HI, preferred_element_type=F32)
    dec = dec_ref[...]
    t = 2 * (pl.program_id(1) * tl + lax.broadcasted_iota(jnp.int32, (tl, 1), 0)) + pl.program_id(0)
    not_first = (t > 0).astype(F32)
    for o in range(2):
        hf = filt[:, (2 * o) * MIXER_W:(2 * o + 1) * MIXER_W] * dec
        hb = filt[:, (2 * o + 1) * MIXER_W:(2 * o + 2) * MIXER_W] * dec * not_first
        o_ref[:, o * MIXER_W:(o + 1) * MIXER_W] = (hf + hb).astype(BF16)
        o_ref[:, (2 + o) * MIXER_W:(3 + o) * MIXER_W] = (hb - hf).astype(BF16)


def _hy_filters(feat, decay, w1p, b1, f1, w2, b2, f2, w3):
    half = feat.shape[1]
    tl = min(512, half)
    hid = w2.shape[0]
    full = lambda shape: pl.BlockSpec(shape, lambda p, i: tuple(0 for _ in shape))
    return pl.pallas_call(
        _hy_filter_kernel,
        grid=(2, half // tl),
        in_specs=[
            pl.BlockSpec((None, tl, HY_EMB_PAD), lambda p, i: (p, i, 0)),
            pl.BlockSpec((None, tl, MIXER_W), lambda p, i: (p, i, 0)),
            full((HY_EMB_PAD, hid)), full((1, hid)), full((1, hid)),
            full((hid, hid)), full((1, hid)), full((1, hid)),
            full((hid, 4 * MIXER_W)),
        ],
        out_specs=pl.BlockSpec((None, tl, 4 * MIXER_W), lambda p, i: (p, i, 0)),
        out_shape=jax.ShapeDtypeStruct((2, half, 4 * MIXER_W), BF16),
        compiler_params=_cparams(("arbitrary", "arbitrary")),
        name="hy_filters",
    )(feat, decay, w1p, b1, f1, w2, b2, f2, w3)


def _hy_kspec_kernel(te_ref, to_ref, ab_ref, o_ref, *, inv_len):
    hb = te_ref.shape[0] // 2
    half = ab_ref.shape[2] // 2
    ec = jnp.dot(te_ref[0:hb, :], ab_ref[0, :, 0:half], preferred_element_type=F32)
    oc = jnp.dot(to_ref[0:hb, :], ab_ref[1, :, 0:half], preferred_element_type=F32)
    es = jnp.dot(te_ref[hb:, :], ab_ref[0, :, half:], preferred_element_type=F32)
    os_ = jnp.dot(to_ref[hb:, :], ab_ref[1, :, half:], preferred_element_type=F32)
    o_ref[0, 0:hb, :] = (ec + oc) * inv_len
    o_ref[0, hb:, :] = (es + os_) * inv_len
    o_ref[1, 0:hb, :] = (ec - oc) * inv_len
    o_ref[1, hb:, :] = (os_ - es) * inv_len


def _hy_kspec(t_e, t_o, ab, tm):
    seq, half = t_e.shape
    return pl.pallas_call(
        functools.partial(_hy_kspec_kernel, inv_len=1.0 / seq),
        grid=(seq // tm,),
        in_specs=[
            pl.BlockSpec((tm, half), lambda i: (i, 0)),
            pl.BlockSpec((tm, half), lambda i: (i, 0)),
            pl.BlockSpec((2, half, 4 * MIXER_W), lambda i: (0, 0, 0)),
        ],
        out_specs=pl.BlockSpec((2, tm, 2 * MIXER_W), lambda i: (0, i, 0)),
        out_shape=jax.ShapeDtypeStruct((2, seq, 2 * MIXER_W), F32),
        compiler_params=_cparams(("arbitrary",)),
        name="hy_kspec",
    )(t_e, t_o, ab)


def _hy_fwd_kernel(te_ref, to_ref, u_ref, k_ref, o_ref):
    hb = te_ref.shape[0] // 2
    reps = u_ref.shape[2] // MIXER_W
    ep = jnp.dot(te_ref[...], u_ref[0], preferred_element_type=F32)
    op = jnp.dot(to_ref[...], u_ref[1], preferred_element_type=F32)
    ec, es, oc, os_ = ep[0:hb, :], ep[hb:, :], op[0:hb, :], op[hb:, :]
    tile = lambda v: jnp.concatenate([v] * reps, axis=-1)
    p, q, pm, qm = ec + oc, es + os_, ec - oc, os_ - es
    kr, ki = tile(k_ref[0, 0:hb, :]), tile(k_ref[0, hb:, :])
    krm, kim = tile(k_ref[1, 0:hb, :]), tile(k_ref[1, hb:, :])
    yr, yn = p * kr + q * ki, q * kr - p * ki
    yrm, ynm = pm * krm + qm * kim, qm * krm - pm * kim
    o_ref[0, 0:hb, :] = (yr + yrm).astype(BF16)
    o_ref[0, hb:, :] = (yn - ynm).astype(BF16)
    o_ref[1, 0:hb, :] = (yr - yrm).astype(BF16)
    o_ref[1, hb:, :] = (yn + ynm).astype(BF16)


def _hy_fwd(t_e, t_o, u, kspec, order, tm, tn):
    seq, half = t_e.shape
    ncol = u.shape[2]
    return pl.pallas_call(
        _hy_fwd_kernel,
        grid=(ncol // tn, seq // tm),
        in_specs=[
            pl.BlockSpec((tm, half), lambda j, i: (i, 0)),
            pl.BlockSpec((tm, half), lambda j, i: (i, 0)),
            pl.BlockSpec((2, half, tn), lambda j, i: (0, 0, j)),
            pl.BlockSpec((2, tm, MIXER_W), lambda j, i: (0, i, order)),
        ],
        out_specs=pl.BlockSpec((2, tm, tn), lambda j, i: (0, i, j)),
        out_shape=jax.ShapeDtypeStruct((2, seq, ncol), BF16),
        compiler_params=_cparams(("arbitrary", "arbitrary")),
        name="hy_fwd",
    )(t_e, t_o, u, kspec)


def _hy_inv_kernel(ie_ref, io_ref, y_ref, u_ref, gate_ref, skip_ref, *rest, final):
    reps = u_ref.shape[2] // MIXER_W
    skip = jnp.concatenate([skip_ref[...]] * reps, axis=-1)
    for par, inv_ref in enumerate((ie_ref, io_ref)):
        conv = jnp.dot(inv_ref[...], y_ref[par], preferred_element_type=F32)
        out = gate_ref[par] * (conv + u_ref[par].astype(F32) * skip)
        if final:
            g_ref, o_ref = rest
            for r in range(reps):
                o_ref[r, par] = _rms_gain(out[:, r * MIXER_W:(r + 1) * MIXER_W], g_ref[...]).astype(BF16)
        else:
            (o_ref,) = rest
            o_ref[par] = out.astype(BF16)


def _hy_inv(inv_e, inv_o, yspec, u, conv, part, skip, gain, final, tm, tn):
    half, seq = inv_e.shape
    ncol = u.shape[2]
    reps = tn // MIXER_W
    in_specs = [
        pl.BlockSpec((tm, seq), lambda j, i: (i, 0)),
        pl.BlockSpec((tm, seq), lambda j, i: (i, 0)),
        pl.BlockSpec((2, seq, tn), lambda j, i: (0, 0, j)),
        pl.BlockSpec((2, tm, tn), lambda j, i: (0, i, j)),
        pl.BlockSpec((None, 2, tm, tn), lambda j, i: (part, 0, i, j)),
        pl.BlockSpec((1, MIXER_W), lambda j, i: (0, 0)),
    ]
    args = [inv_e, inv_o, yspec, u, conv, skip]
    if final:
        in_specs.append(pl.BlockSpec((1, MIXER_W), lambda j, i: (0, 0)))
        args.append(gain)
        out_spec = pl.BlockSpec((reps, 2, tm, MIXER_W), lambda j, i: (j, 0, i, 0))
        out_shape = jax.ShapeDtypeStruct((ncol // MIXER_W, 2, half, MIXER_W), BF16)
    else:
        out_spec = pl.BlockSpec((2, tm, tn), lambda j, i: (0, i, j))
        out_shape = jax.ShapeDtypeStruct((2, half, ncol), BF16)
    return pl.pallas_call(
        functools.partial(_hy_inv_kernel, final=final),
        grid=(ncol // tn, half // tm),
        in_specs=in_specs,
        out_specs=out_spec,
        out_shape=out_shape,
        compiler_params=_cparams(("arbitrary", "arbitrary")),
        name="hy_inv",
    )(*args)


def _hy_tables(seq, tm):
    half = seq // 2
    k = jnp.arange(half, dtype=jnp.int32)
    hb = tm // 2
    nb = half // hb
    tabs = []
    for par in range(2):
        n = 2 * k + par
        m = ((2 * k + 1)[:, None] * n[None, :]) % (4 * seq)
        ang = m.astype(F32) * (2.0 * math.pi / (4 * seq))
        cs = jnp.stack([jnp.cos(ang).reshape(nb, hb, half), jnp.sin(ang).reshape(nb, hb, half)], axis=1)
        tabs.append(cs.reshape(seq, half).astype(BF16))
    return tabs[0], tabs[1]


def _hy_features(seq):
    t = jnp.linspace(0.0, 1.0, seq, dtype=F32)[:, None]
    bands = (HY_EMB - 1) // 2
    freqs = jnp.linspace(1e-4, bands - 1, bands, dtype=F32)[None, :]
    ang = (2.0 * math.pi / seq) * jnp.arange(seq, dtype=F32)[:, None] * freqs
    feat = jnp.concatenate([t, jnp.cos(ang), -jnp.sin(ang)], -1)
    feat = jnp.pad(feat, ((0, 0), (0, HY_EMB_PAD - HY_EMB)))
    deltas = jnp.abs(jnp.linspace(math.log(HY_TARGET) / HY_SLOW, math.log(HY_TARGET) / HY_FAST,
                                  MIXER_W, dtype=F32))
    decay = jnp.exp(-t * deltas[None, :])
    split = lambda a: jnp.stack([a[0::2], a[1::2]], axis=0)
    return split(feat), split(decay)


def _hyena_mix(z, lp, gain, n_seq, seq, row0):
    half = seq // 2
    tm_f = min(512, seq)
    tm_i = min(256, half)
    ncol = n_seq * MIXER_W
    tn = min(1024, ncol)
    t_e, t_o = _hy_tables(seq, tm_f)
    inv_e, inv_o = t_e.T, t_o.T
    feat, decay = _hy_features(seq)
    ab = _hy_filters(feat, decay, lp["hy_w1p"], lp["hy_b1"], lp["hy_f1"], lp["hy_w2"], lp["hy_b2"],
                     lp["hy_f2"], lp["hy_w3"])
    kspec = _hy_kspec(t_e, t_o, ab, tm_f)
    conv, u = _hy_conv(z, lp["hy_conv_w"], lp["hy_conv_b"], n_seq, seq, row0)
    y0 = _hy_fwd(t_e, t_o, u, kspec, 0, tm_f, tn)
    u1 = _hy_inv(inv_e, inv_o, y0, u, conv, 0, lp["hy_skip"][0:1], None, False, tm_i, tn)
    y1 = _hy_fwd(t_e, t_o, u1, kspec, 1, tm_f, tn)
    out = _hy_inv(inv_e, inv_o, y1, u1, conv, 1, lp["hy_skip"][1:2], gain, True, tm_i, tn)
    return jnp.transpose(out, (0, 2, 1, 3)).reshape(n_seq * seq, MIXER_W)


def _gelu(x):
    c = math.sqrt(2.0 / math.pi)
    return 0.5 * x * (1.0 + jnp.tanh(c * (x + 0.044715 * (x * x * x))))


def _gmlp_kernel(u_ref, v_ref, lg_ref, lb_ref, ws_ref, bs_ref, g_ref, o_ref):
    tl = u_ref.shape[0]
    gw = MIXER_W // GMLP_GROUPS
    lane_grp = lax.broadcasted_iota(jnp.int32, (GMLP_CHUNK, MIXER_W), 1) // gw
    for c in range(tl // GMLP_CHUNK):
        rows = slice(c * GMLP_CHUNK, (c + 1) * GMLP_CHUNK)
        v = _ln(_gelu(v_ref[rows, :])) * lg_ref[...] + lb_ref[...]
        vb = v.astype(BF16)
        stacked = jnp.concatenate(
            [jnp.where(lane_grp == g, vb, jnp.zeros_like(vb)) for g in range(GMLP_GROUPS)], axis=0)
        s = jnp.dot(ws_ref[...], stacked, preferred_element_type=F32) + bs_ref[...]
        y = _gelu(u_ref[rows, :]) * s
        o_ref[rows, :] = _rms_gain(y, g_ref[...]).astype(BF16)


def _gmlp_mix(z, ln_g, ln_b, ws_cat, bs_mat, gain, n_rows):
    tl = 512
    full = lambda shape: pl.BlockSpec(shape, lambda i: tuple(0 for _ in shape))
    return pl.pallas_call(
        _gmlp_kernel,
        grid=(n_rows // tl,),
        in_specs=[
            pl.BlockSpec((tl, MIXER_W), lambda i: (i, COL_GU)),
            pl.BlockSpec((tl, MIXER_W), lambda i: (i, COL_GV)),
            full((1, MIXER_W)), full((1, MIXER_W)),
            full((GMLP_CHUNK, GMLP_GROUPS * GMLP_CHUNK)),
            full((GMLP_CHUNK, MIXER_W)),
            full((1, MIXER_W)),
        ],
        out_specs=pl.BlockSpec((tl, MIXER_W), lambda i: (i, 0)),
        out_shape=jax.ShapeDtypeStruct((n_rows, MIXER_W), BF16),
        compiler_params=_cparams(("arbitrary",)),
        name="gmlp_mix",
    )(z, z, ln_g, ln_b, ws_cat, bs_mat, gain)


def _merge_kernel(yp_ref, ya_ref, yh_ref, yg_ref, w_ref, x_ref, g1_ref, lg_ref, lb_ref, sh_ref, sc_ref,
                  wrh_ref, wrl_ref, rb_ref, xo_ref, h_ref, idx_ref, gate_ref, y_a, y_b, *, alpha):
    s = pl.program_id(0)

    @pl.when(s == 0)
    def _():
        y_b[...] = jnp.zeros_like(y_b)

    def step(y_new, y_old):
        ymix = jnp.concatenate([yp_ref[...], ya_ref[...], yh_ref[...], yg_ref[...]], axis=-1)
        y_new[...] = jnp.dot(ymix, w_ref[...], preferred_element_type=F32)

        x1 = _ln(alpha * x_ref[...] + g1_ref[...] * y_old[...]) * lg_ref[...] + lb_ref[...]
        xo_ref[...] = x1
        h = _ln(x1) * (1.0 + sc_ref[...]) + sh_ref[...]
        h_ref[...] = h
        h_hi = h.astype(BF16)
        h_lo = (h - h_hi.astype(F32)).astype(BF16)
        logits = (jnp.dot(h_hi, wrh_ref[...], preferred_element_type=F32)
                  + (jnp.dot(h_lo, wrh_ref[...], preferred_element_type=F32)
                     + jnp.dot(h_hi, wrl_ref[...], preferred_element_type=F32)))
        idx, gate = _route_rows(logits, rb_ref[...])
        idx_ref[...] = idx
        gate_ref[...] = gate

    even = lax.rem(s, 2) == 0
    pl.when(even)(functools.partial(step, y_a, y_b))
    pl.when(jnp.logical_not(even))(functools.partial(step, y_b, y_a))


def _route_rows(logits, bias):
    neg = -1e30
    lane = lax.broadcasted_iota(jnp.int32, logits.shape, 1)
    lane_f = lane.astype(F32)
    valid = lane < N_EXPERTS
    per = N_EXPERTS // N_EXPERT_GROUPS
    s = jax.nn.sigmoid(logits)
    sel = jnp.where(valid, s + bias, neg)
    sh = [sel] + [pltpu.roll(sel, LOGIT_PAD - j, axis=1) for j in range(1, per)]
    pair = None
    for a in range(per):
        for b in range(a + 1, per):
            t = sh[a] + sh[b]
            pair = t if pair is None else jnp.maximum(pair, t)
    grp = jnp.where(valid & ((lane & (per - 1)) == 0), pair, neg)

    def first_max(v):
        m = jnp.max(v, axis=-1, keepdims=True)
        return jnp.min(jnp.where(v == m, lane_f, float(LOGIT_PAD)), axis=-1, keepdims=True)

    best = first_max(grp).astype(jnp.int32)
    shift = per.bit_length() - 1
    cand = jnp.where(valid & ((lane >> shift) == (best >> shift)), sel, neg)
    i1 = first_max(cand)
    i2 = first_max(jnp.where(lane_f == i1, neg, cand))
    w1 = jnp.sum(jnp.where(lane_f == i1, s, 0.0), axis=-1, keepdims=True)
    w2 = jnp.sum(jnp.where(lane_f == i2, s, 0.0), axis=-1, keepdims=True)
    tot = w1 + w2
    idx = jnp.where(lane == 0, i1, jnp.where(lane == 1, i2, 0.0)).astype(jnp.int32)
    gate = jnp.where(lane == 0, w1 / tot, jnp.where(lane == 1, w2 / tot, 0.0))
    return idx, gate


def _merge(ys, w_out_bf, xa, mod3, ln_g, ln_b, wr_hi, wr_lo, rb_pad, n_rows, n_batch, seq, alpha):
    d = xa.shape[1]
    tm = 512
    nt = n_rows // tm

    def cur(i):
        return jnp.minimum(i, nt - 1)

    def prv(i):
        return jnp.maximum(i - 1, 0)

    def mrow(i):
        return jnp.minimum((prv(i) * tm) // seq, n_batch)

    ymix = pl.BlockSpec((tm, MIXER_W), lambda i: (cur(i), 0))
    rowvec = pl.BlockSpec((1, d), lambda i: (0, 0))
    return pl.pallas_call(
        functools.partial(_merge_kernel, alpha=alpha),
        grid=(nt + 1,),
        in_specs=[
            ymix, ymix, ymix, ymix,
            pl.BlockSpec((d, d), lambda i: (0, 0)),
            pl.BlockSpec((tm, d), lambda i: (prv(i), 0)),
            pl.BlockSpec((None, 1, d), lambda i: (mrow(i), 0, 2)),
            rowvec, rowvec,
            pl.BlockSpec((None, 1, d), lambda i: (mrow(i), 0, 3)),
            pl.BlockSpec((None, 1, d), lambda i: (mrow(i), 0, 4)),
            pl.BlockSpec((d, LOGIT_PAD), lambda i: (0, 0)),
            pl.BlockSpec((d, LOGIT_PAD), lambda i: (0, 0)),
            pl.BlockSpec((1, LOGIT_PAD), lambda i: (0, 0)),
        ],
        out_specs=[
            pl.BlockSpec((tm, d), lambda i: (prv(i), 0)),
            pl.BlockSpec((tm, d), lambda i: (prv(i), 0)),
            pl.BlockSpec((tm, LOGIT_PAD), lambda i: (prv(i), 0)),
            pl.BlockSpec((tm, LOGIT_PAD), lambda i: (prv(i), 0)),
        ],
        out_shape=[
            jax.ShapeDtypeStruct((n_rows, d), F32),
            jax.ShapeDtypeStruct((n_rows, d), F32),
            jax.ShapeDtypeStruct((n_rows, LOGIT_PAD), jnp.int32),
            jax.ShapeDtypeStruct((n_rows, LOGIT_PAD), F32),
        ],
        scratch_shapes=[pltpu.VMEM((tm, d), F32), pltpu.VMEM((tm, d), F32)],
        compiler_params=_cparams(("arbitrary",)),
        name="merge",
    )(*ys, w_out_bf, xa, mod3, ln_g, ln_b, mod3, mod3, wr_hi, wr_lo, rb_pad)


def _moe_kernel(be_ref, nu_ref, tok_ref, h_hbm, wg_ref, wu_ref, wd_ref, o_ref, xbuf, sem, wg_s, wu_s, wd_s):
    i = pl.program_id(0)
    n_used = nu_ref[0]
    slot = lax.rem(i, 2)

    def row_copy(blk, r, sl):
        tok = tok_ref[blk * MOE_BLOCK + r]
        return pltpu.make_async_copy(h_hbm.at[pl.ds(tok, 1)], xbuf.at[sl, pl.ds(r, 1)], sem.at[sl])

    def gather_start(blk, sl):
        def body(r, carry):
            row_copy(blk, r, sl).start()
            return carry

        lax.fori_loop(0, MOE_BLOCK, body, 0, unroll=8)

    def gather_wait(sl):
        pltpu.make_async_copy(h_hbm.at[pl.ds(0, MOE_BLOCK)], xbuf.at[sl], sem.at[sl]).wait()

    @pl.when(i == 0)
    def _():
        gather_start(0, 0)

    active = i < n_used
    new_expert = jnp.logical_or(i == 0, be_ref[i] != be_ref[jnp.maximum(i - 1, 0)])

    @pl.when(jnp.logical_and(active, new_expert))
    def _():
        for src, dst in ((wg_ref, wg_s), (wu_ref, wu_s), (wd_ref, wd_s)):
            def cast(r0, src=src, dst=dst):
                dst[pl.ds(r0, 256), :] = src[pl.ds(r0, 256), :].astype(BF16)

            _row_chunks(src.shape[0], 256, cast)

    def ffn_block(prefetch_next):
        gather_wait(slot)
        x = xbuf[slot].astype(BF16)
        if prefetch_next:
            for r in range(MOE_BLOCK):
                row_copy(i + 1, r, 1 - slot).start()
        g = jnp.dot(x, wg_s[...], preferred_element_type=F32)
        u = jnp.dot(x, wu_s[...], preferred_element_type=F32)
        a = (g * jax.nn.sigmoid(g) * u).astype(BF16)
        o_ref[...] = jnp.dot(a, wd_s[...], preferred_element_type=F32).astype(o_ref.dtype)

    has_next = i + 1 < n_used
    pl.when(jnp.logical_and(active, has_next))(functools.partial(ffn_block, True))
    pl.when(jnp.logical_and(active, jnp.logical_not(has_next)))(functools.partial(ffn_block, False))

    @pl.when(jnp.logical_not(active))
    def _():
        o_ref[...] = jnp.zeros_like(o_ref)


def _moe_experts(h, slot_tok, block_e, n_used, wg, wu, wd, layer):
    d = h.shape[1]
    n_slots = slot_tok.shape[0]
    de = wg.shape[3]
    n_blocks = n_slots // MOE_BLOCK
    grid_spec = pltpu.PrefetchScalarGridSpec(
        num_scalar_prefetch=3,
        grid=(n_blocks,),
        in_specs=[
            pl.BlockSpec(memory_space=pl.ANY),
            pl.BlockSpec((None, None, d, de), lambda i, be, nu, tok: (layer, be[i], 0, 0)),
            pl.BlockSpec((None, None, d, de), lambda i, be, nu, tok: (layer, be[i], 0, 0)),
            pl.BlockSpec((None, None, de, d), lambda i, be, nu, tok: (layer, be[i], 0, 0)),
        ],
        out_specs=pl.BlockSpec((MOE_BLOCK, d), lambda i, be, nu, tok: (i, 0)),
        scratch_shapes=[
            pltpu.VMEM((2, MOE_BLOCK, d), F32),
            pltpu.SemaphoreType.DMA((2,)),
            pltpu.VMEM((d, de), BF16), pltpu.VMEM((d, de), BF16), pltpu.VMEM((de, d), BF16),
        ],
    )
    return pl.pallas_call(
        _moe_kernel,
        grid_spec=grid_spec,
        out_shape=jax.ShapeDtypeStruct((n_slots, d), BF16),
        compiler_params=_cparams(("arbitrary",)),
        name="moe_experts",
    )(block_e, n_used, slot_tok, h, wg, wu, wd)


def _assignment_ranks(flat_e):
    a = flat_e.shape[0]
    blk = 128
    nb = a // blk
    onehot = (flat_e[:, None] == jnp.arange(N_EXPERTS)[None, :]).astype(BF16).reshape(nb, blk, N_EXPERTS)
    tri = (jnp.arange(blk)[:, None] >= jnp.arange(blk)[None, :]).astype(BF16)
    intra = jnp.einsum("ij,bjk->bik", tri, onehot, preferred_element_type=F32)
    bsum = intra[:, -1, :]
    before = (jnp.arange(nb)[:, None] > jnp.arange(nb)[None, :]).astype(BF16)
    offs = jnp.dot(before, bsum.astype(BF16), preferred_element_type=F32)
    csum = intra + offs[:, None, :]
    rank = jnp.sum(csum * onehot.astype(F32), axis=-1).reshape(a) - 1.0
    counts = offs[-1] + bsum[-1]
    return rank.astype(jnp.int32), counts.astype(jnp.int32)


def _dispatch_plan(idx):
    t = idx.shape[0]
    a = t * TOP_K
    flat_e = idx.reshape(a)
    rank, counts = _assignment_ranks(flat_e)
    padded = (counts + MOE_BLOCK - 1) // MOE_BLOCK * MOE_BLOCK
    pad_end = jnp.cumsum(padded)
    pad_start = pad_end - padded
    dest = pad_start[flat_e] + rank
    n_blocks = (a + N_EXPERTS * (MOE_BLOCK - 1) + MOE_BLOCK - 1) // MOE_BLOCK
    n_slots = n_blocks * MOE_BLOCK
    flat_tok = jnp.repeat(jnp.arange(t, dtype=jnp.int32), TOP_K)
    slot_tok = jnp.zeros((n_slots,), jnp.int32).at[dest].set(flat_tok)
    blk_start = jnp.arange(n_blocks, dtype=jnp.int32) * MOE_BLOCK
    block_e = jnp.minimum(
        jnp.sum((pad_end[None, :] <= blk_start[:, None]).astype(jnp.int32), axis=1), N_EXPERTS - 1)
    n_used = (pad_end[-1] // MOE_BLOCK).astype(jnp.int32).reshape(1)
    return slot_tok, block_e.astype(jnp.int32), n_used, dest.reshape(t, TOP_K)


def _final_kernel(x_ref, ya_ref, yb_ref, gate_ref, g2_ref, lg_ref, lb_ref, o_ref, *, alpha):
    f = ya_ref[...].astype(F32) * gate_ref[:, 0:1] + yb_ref[...].astype(F32) * gate_ref[:, 1:2]
    o_ref[...] = _ln(alpha * x_ref[...] + g2_ref[...] * f) * lg_ref[...] + lb_ref[...]


def _final_norm(x1, ya, yb, gate, mod3, ln_g, ln_b, n_batch, seq, alpha):
    n_rows, d = x1.shape
    tm = 512

    def mrow(i):
        return jnp.minimum((i * tm) // seq, n_batch)

    tile = pl.BlockSpec((tm, d), lambda i: (i, 0))
    rowvec = pl.BlockSpec((1, d), lambda i: (0, 0))
    return pl.pallas_call(
        functools.partial(_final_kernel, alpha=alpha),
        grid=(n_rows // tm,),
        in_specs=[tile, tile, tile, pl.BlockSpec((tm, LOGIT_PAD), lambda i: (i, 0)),
                  pl.BlockSpec((None, 1, d), lambda i: (mrow(i), 0, 5)), rowvec, rowvec],
        out_specs=tile,
        out_shape=jax.ShapeDtypeStruct((n_rows, d), F32),
        compiler_params=_cparams(("arbitrary",)),
        name="final_norm",
    )(x1, ya, yb, gate, mod3, ln_g, ln_b)


def _permute_in_cols(w):
    pool, q, k, v, hy, gm = jnp.split(w, (512, 1024, 1152, 1280, 2816), axis=-1)
    return jnp.concatenate([hy, pool, q, gm, k, v], axis=-1)


def kernel(x, c, ctx, c_ctx, w_ada, b_ada, w_in, w_out, mix_norm_g, pool_w, pool_scale, attn_sink,
           hy_conv_w, hy_conv_b, hy_w1, hy_b1, hy_f1, hy_w2, hy_b2, hy_f2, hy_w3, hy_skip,
           gm_ln_g, gm_ln_b, gm_ws, gm_bs, ln1_g, ln1_b, ln2_g, ln2_b, w_router, router_bias,
           w_gate, w_up, w_down):
    n_batch, seq, d = x.shape
    ctx_len = ctx.shape[1]
    depth = w_in.shape[0]
    grid_w = 64
    alpha = (2 * depth) ** 0.25
    n_lat = n_batch * seq
    n_ctx = n_batch * ctx_len

    xa = jnp.concatenate([x.reshape(n_lat, d), ctx.reshape(n_ctx, d)], axis=0)
    cvec = jnp.concatenate([c, c_ctx[None, :], jnp.zeros((8 - n_batch - 1, d), F32)], axis=0)
    mods = _ada_mods(cvec, w_ada, b_ada)
    cos_t, sin_t = _rope_tables(seq, grid_w)
    wr_pad = jnp.pad(w_router, ((0, 0), (0, LOGIT_PAD - N_EXPERTS)))
    wr_hi = wr_pad.astype(BF16)
    wr_lo = (wr_pad - wr_hi.astype(F32)).astype(BF16)
    rb_pad = jnp.pad(router_bias.astype(F32), (0, LOGIT_PAD - N_EXPERTS)).reshape(1, LOGIT_PAD)
    row = lambda v: v.reshape(1, -1)

    for layer in range(depth):
        last = layer == depth - 1
        mod3 = mods[layer].reshape(8, 1, 6 * d)
        gains = mix_norm_g[layer].reshape(4, 1, MIXER_W)
        lp = {
            "hy_conv_w": hy_conv_w[layer], "hy_conv_b": row(hy_conv_b[layer]),
            "hy_w1p": jnp.pad(hy_w1[layer], ((0, HY_EMB_PAD - HY_EMB), (0, 0))),
            "hy_b1": row(hy_b1[layer]), "hy_f1": row(hy_f1[layer]),
            "hy_w2": hy_w2[layer], "hy_b2": row(hy_b2[layer]), "hy_f2": row(hy_f2[layer]),
            "hy_w3": hy_w3[layer], "hy_skip": hy_skip[layer],
        }
        z = _in_proj(xa, mod3, _permute_in_cols(w_in[layer]).astype(BF16), n_batch, seq)

        pool_bf = pool_w[layer].astype(BF16)
        ps = row(pool_scale[layer])
        ws_cat = jnp.transpose(gm_ws[layer], (1, 0, 2)).reshape(GMLP_CHUNK, GMLP_GROUPS * GMLP_CHUNK)
        bs_mat = jnp.repeat(gm_bs[layer].T, MIXER_W // GMLP_GROUPS, axis=1)
        n_rows = n_lat if last else n_lat + n_ctx

        y_pool = _pool_mix(z, pool_bf, ps, gains[0], n_batch, seq, 0)
        y_attn = _attn_lat(z, attn_sink[layer], cos_t, sin_t, gains[1], n_batch, seq, ctx_len)
        y_hy = _hyena_mix(z, lp, gains[2], n_batch, seq, 0)
        y_gm = _gmlp_mix(z, row(gm_ln_g[layer]), row(gm_ln_b[layer]), ws_cat.astype(BF16), bs_mat,
                         gains[3], n_rows)
        if not last:
            y_pool = jnp.concatenate(
                [y_pool, _pool_mix(z, pool_bf, ps, gains[0], n_batch, ctx_len, n_lat // ctx_len)], axis=0)
            y_attn = jnp.concatenate(
                [y_attn, _attn_ctx(z, attn_sink[layer], gains[1], n_batch, seq, ctx_len)], axis=0)
            y_hy = jnp.concatenate([y_hy, _hyena_mix(z, lp, gains[2], n_batch, ctx_len, n_lat)], axis=0)

        x1, h2, idx, gate = _merge((y_pool, y_attn, y_hy, y_gm), w_out[layer].astype(BF16), xa, mod3,
                                   row(ln1_g[layer]), row(ln1_b[layer]), wr_hi, wr_lo, rb_pad,
                                   n_rows, n_batch, seq, alpha)

        slot_tok, block_e, n_used, dest = _dispatch_plan(idx[:, :TOP_K])
        ys = _moe_experts(h2, slot_tok, block_e, n_used, w_gate, w_up, w_down, layer)
        xa = _final_norm(x1, ys[dest[:, 0]], ys[dest[:, 1]], gate, mod3, row(ln2_g[layer]), row(ln2_b[layer]),
                         n_batch, seq, alpha)
    return xa.reshape(n_batch, seq, d)
```

```python
import functools
import math

import jax
import jax.numpy as jnp
from jax import lax
from jax.experimental import pallas as pl
from jax.experimental.pallas import tpu as pltpu

F32 = jnp.float32
BF16 = jnp.bfloat16
HI = lax.Precision.HIGHEST

LN_EPS = 1e-6
MIXER_W = 512
HEAD_DIM = 64
N_HEADS = 8
N_KV_HEADS = 2
KV_GROUP = 4
KV_W = 128
WINDOW = 128
ROPE_THETA = 10000.0
POOL_WINDOWS = (2, 4, 8, 16)
POOL_GROUP = 128
POOL_HALO = 8
GMLP_CHUNK = 128
GMLP_GROUPS = 8
HY_EMB = 33
HY_EMB_PAD = 128
HY_TARGET = 1e-2
HY_FAST = 0.3
HY_SLOW = 1.5
N_EXPERTS = 32
N_EXPERT_GROUPS = 8
TOP_K = 2
D_EXPERT = 768
MOE_BLOCK = 256
LOGIT_PAD = 128

COL_HY = 0
COL_POOL = 3
COL_Q = 4
COL_GU = 5
COL_GV = 6
COL_K = 28
COL_V = 29
IN_W = 3840

VMEM_LIMIT = 56 * 1024 * 1024


def _cparams(sem):
    return pltpu.CompilerParams(dimension_semantics=sem, vmem_limit_bytes=VMEM_LIMIT)


def _ln(x):
    mu = jnp.mean(x, axis=-1, keepdims=True)
    xc = x - mu
    var = jnp.mean(xc * xc, axis=-1, keepdims=True)
    return xc * lax.rsqrt(var + LN_EPS)


def _rms_gain(y, g):
    return y * lax.rsqrt(jnp.mean(y * y, axis=-1, keepdims=True) + LN_EPS) * g


def _row_chunks(n_rows, chunk, body):
    n = n_rows // chunk
    if n == 1:
        body(0)
        return

    def step(i, carry):
        body(pl.multiple_of(i * chunk, chunk))
        return carry

    lax.fori_loop(0, n, step, 0)


def _ada_kernel(c_ref, w_ref, b_ref, o_ref):
    c = c_ref[...]
    s = c * jax.nn.sigmoid(c)
    o_ref[...] = jnp.dot(s, w_ref[...], precision=HI, preferred_element_type=F32) + b_ref[...]


def _ada_mods(cvec, w_ada, b_ada):
    depth, d, n = w_ada.shape
    tn = 1024
    return pl.pallas_call(
        _ada_kernel,
        grid=(depth, n // tn),
        in_specs=[
            pl.BlockSpec((8, d), lambda l, j: (0, 0)),
            pl.BlockSpec((None, d, tn), lambda l, j: (l, 0, j)),
            pl.BlockSpec((None, 1, tn), lambda l, j: (l, 0, j)),
        ],
        out_specs=pl.BlockSpec((None, 8, tn), lambda l, j: (l, 0, j)),
        out_shape=jax.ShapeDtypeStruct((depth, 8, n), F32),
        compiler_params=_cparams(("arbitrary", "arbitrary")),
        name="ada_mods",
    )(cvec, w_ada, b_ada.reshape(depth, 1, n))


def _inproj_kernel(x_ref, sh_ref, sc_ref, w_ref, o_ref, h_scr):
    @pl.when(pl.program_id(1) == 0)
    def _():
        sh = sh_ref[...]
        sc1 = 1.0 + sc_ref[...]

        def body(r0):
            x = x_ref[pl.ds(r0, 256), :]
            h_scr[pl.ds(r0, 256), :] = (_ln(x) * sc1 + sh).astype(BF16)

        _row_chunks(x_ref.shape[0], 256, body)

    o_ref[...] = jnp.dot(h_scr[...], w_ref[...], preferred_element_type=F32)


def _in_proj(xa, mod3, w_bf, n_batch, seq):
    t, d = xa.shape
    n = w_bf.shape[1]
    tn = 768
    tm = next(m for m in (1024, 512, 256) if t % m == 0 and seq % m == 0)

    def mrow(i):
        return jnp.minimum((i * tm) // seq, n_batch)

    return pl.pallas_call(
        _inproj_kernel,
        grid=(t // tm, n // tn),
        in_specs=[
            pl.BlockSpec((tm, d), lambda i, j: (i, 0)),
            pl.BlockSpec((None, 1, d), lambda i, j: (mrow(i), 0, 0)),
            pl.BlockSpec((None, 1, d), lambda i, j: (mrow(i), 0, 1)),
            pl.BlockSpec((d, tn), lambda i, j: (0, j)),
        ],
        out_specs=pl.BlockSpec((tm, tn), lambda i, j: (i, j)),
        out_shape=jax.ShapeDtypeStruct((t, n), F32),
        scratch_shapes=[pltpu.VMEM((tm, d), BF16)],
        compiler_params=_cparams(("arbitrary", "arbitrary")),
        name="in_proj",
    )(xa, mod3, mod3, w_bf)


def _pool_kernel(z_ref, w_ref, ps_ref, g_ref, o_ref, buf):
    seq = z_ref.shape[0]
    rc = min(256, seq)
    zeros = jnp.zeros((POOL_HALO, MIXER_W), F32)
    buf[0:POOL_HALO, :] = zeros
    buf[seq + POOL_HALO:seq + 2 * POOL_HALO, :] = zeros

    def fill(r0):
        buf[pl.ds(r0 + POOL_HALO, rc), :] = z_ref[pl.ds(r0, rc), :]

    _row_chunks(seq, rc, fill)

    def body(r0):
        win = buf[pl.ds(r0, rc + 2 * POOL_HALO), :]
        t = r0 + lax.broadcasted_iota(jnp.int32, (rc, 1), 0)
        outs = []
        for gi, w in enumerate(POOL_WINDOWS):
            lanes = slice(gi * POOL_GROUP, (gi + 1) * POOL_GROUP)
            acc = None
            for j in range(-w // 2, w // 2):
                piece = win[POOL_HALO + j:POOL_HALO + j + rc, lanes]
                acc = piece if acc is None else acc + piece
            cnt = (jnp.minimum(t + w // 2, seq) - jnp.maximum(t - w // 2, 0)).astype(F32)
            dlt = acc / cnt - win[POOL_HALO:POOL_HALO + rc, lanes]
            outs.append(jnp.dot(dlt.astype(BF16), w_ref[gi], preferred_element_type=F32))
        y = jnp.concatenate(outs, axis=-1) * ps_ref[...]
        o_ref[pl.ds(r0, rc), :] = _rms_gain(y, g_ref[...]).astype(BF16)

    _row_chunks(seq, rc, body)


def _pool_mix(z, pool_w_bf, pool_scale, gain, n_seq, seq, row_blk0):
    return pl.pallas_call(
        _pool_kernel,
        grid=(n_seq,),
        in_specs=[
            pl.BlockSpec((seq, MIXER_W), lambda s: (row_blk0 + s, COL_POOL)),
            pl.BlockSpec((4, POOL_GROUP, POOL_GROUP), lambda s: (0, 0, 0)),
            pl.BlockSpec((1, MIXER_W), lambda s: (0, 0)),
            pl.BlockSpec((1, MIXER_W), lambda s: (0, 0)),
        ],
        out_specs=pl.BlockSpec((seq, MIXER_W), lambda s: (s, 0)),
        out_shape=jax.ShapeDtypeStruct((n_seq * seq, MIXER_W), BF16),
        scratch_shapes=[pltpu.VMEM((seq + 2 * POOL_HALO, MIXER_W), F32)],
        compiler_params=_cparams(("arbitrary",)),
        name="pool_mix",
    )(z, pool_w_bf, pool_scale, gain)


def _rope(x, cos, sin_signed):
    lane = lax.broadcasted_iota(jnp.int32, x.shape, 1)
    partner = jnp.where((lane & 31) < 16, pltpu.roll(x, 112, axis=1), pltpu.roll(x, 16, axis=1))
    return x * cos + partner * sin_signed


def _attend(qs, sink_col, parts):
    dn = (((1,), (1,)), ((), ()))
    scores = []
    for kk, _, mask in parts:
        s = lax.dot_general(qs, kk, dn, preferred_element_type=F32)
        if mask is not None:
            s = s + mask
        scores.append(s)
    m = sink_col
    for s in scores:
        m = jnp.maximum(m, jnp.max(s, axis=-1, keepdims=True))
    den = jnp.exp(sink_col - m)
    out = None
    for s, (_, vv, _) in zip(scores, parts):
        p = jnp.exp(s - m)
        den = den + jnp.sum(p, axis=-1, keepdims=True)
        o = jnp.dot(p.astype(BF16), vv, preferred_element_type=F32)
        out = o if out is None else out + o
    return out / den


def _heads_attend(q_all, sink_ref, tq, make_parts):
    cols = [None] * N_HEADS
    for kh in range(N_KV_HEADS):
        heads = [kh * KV_GROUP + g for g in range(KV_GROUP)]
        qs = jnp.concatenate([q_all[:, h * HEAD_DIM:(h + 1) * HEAD_DIM] for h in heads], axis=0)
        sink_col = jnp.concatenate([jnp.full((tq, 1), sink_ref[h], F32) for h in heads], axis=0)
        o = _attend(qs.astype(BF16), sink_col, make_parts(kh))
        for g, h in enumerate(heads):
            cols[h] = o[g * tq:(g + 1) * tq, :]
    return jnp.concatenate(cols, axis=-1)


def _attn_lat_kernel(sink_ref, q_ref, k_ref, v_ref, kc_ref, vc_ref, cos_ref, sin_ref, g_ref, o_ref):
    tq = q_ref.shape[0]
    seq = k_ref.shape[0]
    nband = tq + 2 * WINDOW
    i = pl.program_id(1)
    q0 = pl.multiple_of(i * tq, tq)
    k0 = pl.multiple_of(jnp.clip(q0 - WINDOW, 0, seq - nband), WINDOW)

    cq = cos_ref[pl.ds(q0, tq), :]
    sq = sin_ref[pl.ds(q0, tq), :]
    q = q_ref[...]
    q_all = jnp.concatenate(
        [_rope(q[:, c * 128:(c + 1) * 128], cq, sq) for c in range(MIXER_W // 128)], axis=-1)
    q_all = q_all * (HEAD_DIM ** -0.5)

    kb = _rope(k_ref[pl.ds(k0, nband), :], cos_ref[pl.ds(k0, nband), :],
               sin_ref[pl.ds(k0, nband), :]).astype(BF16)
    vb = v_ref[pl.ds(k0, nband), :].astype(BF16)
    kc = kc_ref[...].astype(BF16)
    vc = vc_ref[...].astype(BF16)

    qpos = q0 + lax.broadcasted_iota(jnp.int32, (tq, nband), 0)
    kpos = k0 + lax.broadcasted_iota(jnp.int32, (tq, nband), 1)
    band = jnp.where(jnp.abs(qpos - kpos) <= WINDOW, 0.0, -1e30)
    mask = jnp.concatenate([band] * KV_GROUP, axis=0)

    def make_parts(kh):
        hs = slice(kh * HEAD_DIM, (kh + 1) * HEAD_DIM)
        return [(kb[:, hs], vb[:, hs], mask), (kc[:, hs], vc[:, hs], None)]

    y = _heads_attend(q_all, sink_ref, tq, make_parts)
    o_ref[...] = _rms_gain(y, g_ref[...]).astype(BF16)


def _attn_ctx_kernel(sink_ref, q_ref, kc_ref, vc_ref, g_ref, o_ref):
    tq = q_ref.shape[0]
    q_all = q_ref[...] * (HEAD_DIM ** -0.5)
    kc = kc_ref[...].astype(BF16)
    vc = vc_ref[...].astype(BF16)

    def make_parts(kh):
        hs = slice(kh * HEAD_DIM, (kh + 1) * HEAD_DIM)
        return [(kc[:, hs], vc[:, hs], None)]

    y = _heads_attend(q_all, sink_ref, tq, make_parts)
    o_ref[...] = _rms_gain(y, g_ref[...]).astype(BF16)


def _attn_lat(z, sink, cos_t, sin_t, gain, n_batch, seq, ctx_len):
    tq = 256
    nq = seq // tq
    ctx_blk0 = (n_batch * seq) // ctx_len
    smem = pl.BlockSpec(memory_space=pltpu.SMEM)
    return pl.pallas_call(
        _attn_lat_kernel,
        grid=(n_batch, nq),
        in_specs=[
            smem,
            pl.BlockSpec((tq, MIXER_W), lambda b, i: (b * nq + i, COL_Q)),
            pl.BlockSpec((seq, KV_W), lambda b, i: (b, COL_K)),
            pl.BlockSpec((seq, KV_W), lambda b, i: (b, COL_V)),
            pl.BlockSpec((ctx_len, KV_W), lambda b, i: (ctx_blk0 + b, COL_K)),
            pl.BlockSpec((ctx_len, KV_W), lambda b, i: (ctx_blk0 + b, COL_V)),
            pl.BlockSpec((seq, 128), lambda b, i: (0, 0)),
            pl.BlockSpec((seq, 128), lambda b, i: (0, 0)),
            pl.BlockSpec((1, MIXER_W), lambda b, i: (0, 0)),
        ],
        out_specs=pl.BlockSpec((tq, MIXER_W), lambda b, i: (b * nq + i, 0)),
        out_shape=jax.ShapeDtypeStruct((n_batch * seq, MIXER_W), BF16),
        compiler_params=_cparams(("arbitrary", "arbitrary")),
        name="attn_lat",
    )(sink, z, z, z, z, z, cos_t, sin_t, gain)


def _attn_ctx(z, sink, gain, n_batch, seq, ctx_len):
    ctx_blk0 = (n_batch * seq) // ctx_len
    smem = pl.BlockSpec(memory_space=pltpu.SMEM)
    return pl.pallas_call(
        _attn_ctx_kernel,
        grid=(n_batch,),
        in_specs=[
            smem,
            pl.BlockSpec((ctx_len, MIXER_W), lambda b: (ctx_blk0 + b, COL_Q)),
            pl.BlockSpec((ctx_len, KV_W), lambda b: (ctx_blk0 + b, COL_K)),
            pl.BlockSpec((ctx_len, KV_W), lambda b: (ctx_blk0 + b, COL_V)),
            pl.BlockSpec((1, MIXER_W), lambda b: (0, 0)),
        ],
        out_specs=pl.BlockSpec((ctx_len, MIXER_W), lambda b: (b, 0)),
        out_shape=jax.ShapeDtypeStruct((n_batch * ctx_len, MIXER_W), BF16),
        compiler_params=_cparams(("arbitrary",)),
        name="attn_ctx",
    )(sink, z, z, z, gain)


def _rope_tables(seq, grid_w):
    nf = HEAD_DIM // 4
    inv = ROPE_THETA ** (-jnp.arange(nf, dtype=F32) / nf)
    t = jnp.arange(seq)
    row = (t // grid_w).astype(F32)[:, None] * inv[None, :]
    col = (t % grid_w).astype(F32)[:, None] * inv[None, :]
    cos_h = jnp.concatenate([jnp.cos(row), jnp.cos(row), jnp.cos(col), jnp.cos(col)], axis=-1)
    sin_h = jnp.concatenate([-jnp.sin(row), jnp.sin(row), -jnp.sin(col), jnp.sin(col)], axis=-1)
    return jnp.tile(cos_h, (1, 2)), jnp.tile(sin_h, (1, 2))


def _hy_conv_kernel(*refs):
    z_refs, zp_refs, zn_refs = refs[0:3], refs[3:6], refs[6:9]
    w_ref, b_ref, g_ref, u_ref = refs[9:]
    i = pl.program_id(1)
    th = z_refs[0].shape[0] // 2
    has_prev = (i > 0).astype(F32)
    has_next = (i < pl.num_programs(1) - 1).astype(F32)
    row = lax.broadcasted_iota(jnp.int32, (th, 128), 0)
    for part in range(3):
        ze = z_refs[part][pl.ds(0, th, stride=2), :]
        zo = z_refs[part][pl.ds(1, th, stride=2), :]
        prev_row = zp_refs[part][POOL_HALO - 1:POOL_HALO, :] * has_prev
        next_row = zn_refs[part][0:1, :] * has_next
        zo_m = jnp.where(row == 0, prev_row, pltpu.roll(zo, 1, axis=0))
        ze_p = jnp.where(row == th - 1, next_row, pltpu.roll(ze, th - 1, axis=0))
        w0, w1, w2, b = w_ref[part, 0:1, :], w_ref[part, 1:2, :], w_ref[part, 2:3, :], b_ref[part]
        ye = b + zo_m * w0 + ze * w1 + zo * w2
        yo = b + ze * w0 + zo * w1 + ze_p * w2
        g_ref[part, 0] = ye
        g_ref[part, 1] = yo
        if part == 2:
            u_ref[0] = ye.astype(BF16)
            u_ref[1] = yo.astype(BF16)


def _hy_conv(z, conv_w, conv_b, n_seq, seq, row0):
    tl = min(2048, seq)
    nt = seq // tl
    blk0 = row0 // tl
    hb0 = row0 // POOL_HALO
    hpt = tl // POOL_HALO
    last_halo = (row0 + n_seq * seq) // POOL_HALO - 1
    cpp = MIXER_W // 128
    ncol = n_seq * MIXER_W

    def main_spec(part):
        return pl.BlockSpec((tl, 128), lambda s, i, c: (blk0 + s * nt + i, part * cpp + c))

    def prev_spec(part):
        return pl.BlockSpec((POOL_HALO, 128),
                            lambda s, i, c: (jnp.maximum(hb0 + (s * nt + i) * hpt - 1, 0), part * cpp + c))

    def next_spec(part):
        return pl.BlockSpec((POOL_HALO, 128),
                            lambda s, i, c: (jnp.minimum(hb0 + (s * nt + i + 1) * hpt, last_halo), part * cpp + c))

    w3 = conv_w.reshape(3, 3, MIXER_W).transpose(1, 0, 2)
    b3 = conv_b.reshape(3, 1, MIXER_W)
    return pl.pallas_call(
        _hy_conv_kernel,
        grid=(n_seq, nt, cpp),
        in_specs=[main_spec(p) for p in range(3)] + [prev_spec(p) for p in range(3)]
        + [next_spec(p) for p in range(3)]
        + [pl.BlockSpec((3, 3, 128), lambda s, i, c: (0, 0, c)),
           pl.BlockSpec((3, 1, 128), lambda s, i, c: (0, 0, c))],
        out_specs=[
            pl.BlockSpec((3, 2, tl // 2, 128), lambda s, i, c: (0, 0, i, s * cpp + c)),
            pl.BlockSpec((2, tl // 2, 128), lambda s, i, c: (0, i, s * cpp + c)),
        ],
        out_shape=[jax.ShapeDtypeStruct((3, 2, seq // 2, ncol), F32),
                   jax.ShapeDtypeStruct((2, seq // 2, ncol), BF16)],
        compiler_params=_cparams(("arbitrary", "arbitrary", "arbitrary")),
        name="hy_conv",
    )(*([z] * 9), w3, b3)


def _hy_filter_kernel(feat_ref, dec_ref, w1_ref, b1_ref, f1_ref, w2_ref, b2_ref, f2_ref, w3_ref, o_ref):
    tl = feat_ref.shape[0]
    h = jnp.dot(feat_ref[...], w1_ref[...], precision=HI, preferred_element_type=F32) + b1_ref[...]
    h = jnp.sin(f1_ref[...] * h)
    h = jnp.dot(h, w2_ref[...], precision=HI, preferred_element_type=F32) + b2_ref[...]
    h = jnp.sin(f2_ref[...] * h)
    filt = jnp.dot(h, w3_ref[...], precision=HI, preferred_element_type=F32)
    dec = dec_ref[...]
    t = 2 * (pl.program_id(1) * tl + lax.broadcasted_iota(jnp.int32, (tl, 1), 0)) + pl.program_id(0)
    not_first = (t > 0).astype(F32)
    for o in range(2):
        hf = filt[:, (2 * o) * MIXER_W:(2 * o + 1) * MIXER_W] * dec
        hb = filt[:, (2 * o + 1) * MIXER_W:(2 * o + 2) * MIXER_W] * dec * not_first
        o_ref[:, o * MIXER_W:(o + 1) * MIXER_W] = (hf + hb).astype(BF16)
        o_ref[:, (2 + o) * MIXER_W:(3 + o) * MIXER_W] = (hb - hf).astype(BF16)


def _hy_filters(feat, decay, w1p, b1, f1, w2, b2, f2, w3):
    half = feat.shape[1]
    tl = min(512, half)
    hid = w2.shape[0]
    full = lambda shape: pl.BlockSpec(shape, lambda p, i: tuple(0 for _ in shape))
    return pl.pallas_call(
        _hy_filter_kernel,
        grid=(2, half // tl),
        in_specs=[
            pl.BlockSpec((None, tl, HY_EMB_PAD), lambda p, i: (p, i, 0)),
            pl.BlockSpec((None, tl, MIXER_W), lambda p, i: (p, i, 0)),
            full((HY_EMB_PAD, hid)), full((1, hid)), full((1, hid)),
            full((hid, hid)), full((1, hid)), full((1, hid)),
            full((hid, 4 * MIXER_W)),
        ],
        out_specs=pl.BlockSpec((None, tl, 4 * MIXER_W), lambda p, i: (p, i, 0)),
        out_shape=jax.ShapeDtypeStruct((2, half, 4 * MIXER_W), BF16),
        compiler_params=_cparams(("arbitrary", "arbitrary")),
        name="hy_filters",
    )(feat, decay, w1p, b1, f1, w2, b2, f2, w3)


def _hy_kspec_kernel(te_ref, to_ref, ab_ref, o_ref, *, inv_len):
    hb = te_ref.shape[0] // 2
    half = ab_ref.shape[2] // 2
    ec = jnp.dot(te_ref[0:hb, :], ab_ref[0, :, 0:half], preferred_element_type=F32)
    oc = jnp.dot(to_ref[0:hb, :], ab_ref[1, :, 0:half], preferred_element_type=F32)
    es = jnp.dot(te_ref[hb:, :], ab_ref[0, :, half:], preferred_element_type=F32)
    os_ = jnp.dot(to_ref[hb:, :], ab_ref[1, :, half:], preferred_element_type=F32)
    o_ref[0, 0:hb, :] = (ec + oc) * inv_len
    o_ref[0, hb:, :] = (es + os_) * inv_len
    o_ref[1, 0:hb, :] = (ec - oc) * inv_len
    o_ref[1, hb:, :] = (os_ - es) * inv_len


def _hy_kspec(t_e, t_o, ab, tm):
    seq, half = t_e.shape
    return pl.pallas_call(
        functools.partial(_hy_kspec_kernel, inv_len=1.0 / seq),
        grid=(seq // tm,),
        in_specs=[
            pl.BlockSpec((tm, half), lambda i: (i, 0)),
            pl.BlockSpec((tm, half), lambda i: (i, 0)),
            pl.BlockSpec((2, half, 4 * MIXER_W), lambda i: (0, 0, 0)),
        ],
        out_specs=pl.BlockSpec((2, tm, 2 * MIXER_W), lambda i: (0, i, 0)),
        out_shape=jax.ShapeDtypeStruct((2, seq, 2 * MIXER_W), F32),
        compiler_params=_cparams(("arbitrary",)),
        name="hy_kspec",
    )(t_e, t_o, ab)


def _hy_fwd_kernel(te_ref, to_ref, u_ref, k_ref, o_ref):
    hb = te_ref.shape[0] // 2
    reps = u_ref.shape[2] // MIXER_W
    ep = jnp.dot(te_ref[...], u_ref[0], preferred_element_type=F32)
    op = jnp.dot(to_ref[...], u_ref[1], preferred_element_type=F32)
    ec, es, oc, os_ = ep[0:hb, :], ep[hb:, :], op[0:hb, :], op[hb:, :]
    tile = lambda v: jnp.concatenate([v] * reps, axis=-1)
    p, q, pm, qm = ec + oc, es + os_, ec - oc, os_ - es
    kr, ki = tile(k_ref[0, 0:hb, :]), tile(k_ref[0, hb:, :])
    krm, kim = tile(k_ref[1, 0:hb, :]), tile(k_ref[1, hb:, :])
    yr, yn = p * kr + q * ki, q * kr - p * ki
    yrm, ynm = pm * krm + qm * kim, qm * krm - pm * kim
    o_ref[0, 0:hb, :] = (yr + yrm).astype(BF16)
    o_ref[0, hb:, :] = (yn - ynm).astype(BF16)
    o_ref[1, 0:hb, :] = (yr - yrm).astype(BF16)
    o_ref[1, hb:, :] = (yn + ynm).astype(BF16)


def _hy_fwd(t_e, t_o, u, kspec, order, tm, tn):
    seq, half = t_e.shape
    ncol = u.shape[2]
    return pl.pallas_call(
        _hy_fwd_kernel,
        grid=(ncol // tn, seq // tm),
        in_specs=[
            pl.BlockSpec((tm, half), lambda j, i: (i, 0)),
            pl.BlockSpec((tm, half), lambda j, i: (i, 0)),
            pl.BlockSpec((2, half, tn), lambda j, i: (0, 0, j)),
            pl.BlockSpec((2, tm, MIXER_W), lambda j, i: (0, i, order)),
        ],
        out_specs=pl.BlockSpec((2, tm, tn), lambda j, i: (0, i, j)),
        out_shape=jax.ShapeDtypeStruct((2, seq, ncol), BF16),
        compiler_params=_cparams(("arbitrary", "arbitrary")),
        name="hy_fwd",
    )(t_e, t_o, u, kspec)


def _hy_inv_kernel(ie_ref, io_ref, y_ref, u_ref, gate_ref, skip_ref, *rest, final):
    reps = u_ref.shape[2] // MIXER_W
    skip = jnp.concatenate([skip_ref[...]] * reps, axis=-1)
    for par, inv_ref in enumerate((ie_ref, io_ref)):
        conv = jnp.dot(inv_ref[...], y_ref[par], preferred_element_type=F32)
        out = gate_ref[par] * (conv + u_ref[par].astype(F32) * skip)
        if final:
            g_ref, o_ref = rest
            for r in range(reps):
                o_ref[r, par] = _rms_gain(out[:, r * MIXER_W:(r + 1) * MIXER_W], g_ref[...]).astype(BF16)
        else:
            (o_ref,) = rest
            o_ref[par] = out.astype(BF16)


def _hy_inv(inv_e, inv_o, yspec, u, conv, part, skip, gain, final, tm, tn):
    half, seq = inv_e.shape
    ncol = u.shape[2]
    reps = tn // MIXER_W
    in_specs = [
        pl.BlockSpec((tm, seq), lambda j, i: (i, 0)),
        pl.BlockSpec((tm, seq), lambda j, i: (i, 0)),
        pl.BlockSpec((2, seq, tn), lambda j, i: (0, 0, j)),
        pl.BlockSpec((2, tm, tn), lambda j, i: (0, i, j)),
        pl.BlockSpec((None, 2, tm, tn), lambda j, i: (part, 0, i, j)),
        pl.BlockSpec((1, MIXER_W), lambda j, i: (0, 0)),
    ]
    args = [inv_e, inv_o, yspec, u, conv, skip]
    if final:
        in_specs.append(pl.BlockSpec((1, MIXER_W), lambda j, i: (0, 0)))
        args.append(gain)
        out_spec = pl.BlockSpec((reps, 2, tm, MIXER_W), lambda j, i: (j, 0, i, 0))
        out_shape = jax.ShapeDtypeStruct((ncol // MIXER_W, 2, half, MIXER_W), BF16)
    else:
        out_spec = pl.BlockSpec((2, tm, tn), lambda j, i: (0, i, j))
        out_shape = jax.ShapeDtypeStruct((2, half, ncol), BF16)
    return pl.pallas_call(
        functools.partial(_hy_inv_kernel, final=final),
        grid=(ncol // tn, half // tm),
        in_specs=in_specs,
        out_specs=out_spec,
        out_shape=out_shape,
        compiler_params=_cparams(("arbitrary", "arbitrary")),
        name="hy_inv",
    )(*args)


def _hy_tables(seq, tm):
    half = seq // 2
    hb = tm // 2
    nb = half // hb
    split = 64 if half % 64 == 0 else 1
    unit = 2.0 * math.pi / (4 * seq)
    k2 = (2 * jnp.arange(half, dtype=jnp.int32) + 1)[:, None]

    def cos_sin(n):
        ang = ((k2 * n[None, :]) % (4 * seq)).astype(F32) * unit
        return jnp.cos(ang), jnp.sin(ang)

    tabs = []
    for par in range(2):
        ch, sh = cos_sin(2 * split * jnp.arange(half // split, dtype=jnp.int32))
        cl, sl = cos_sin(2 * jnp.arange(split, dtype=jnp.int32) + par)
        cos_t = (ch[:, :, None] * cl[:, None, :] - sh[:, :, None] * sl[:, None, :]).reshape(half, half)
        sin_t = (sh[:, :, None] * cl[:, None, :] + ch[:, :, None] * sl[:, None, :]).reshape(half, half)
        cs = jnp.stack([cos_t.reshape(nb, hb, half), sin_t.reshape(nb, hb, half)], axis=1)
        tabs.append(cs.reshape(seq, half).astype(BF16))
    return tabs[0], tabs[1]


def _hy_features(seq):
    t = jnp.linspace(0.0, 1.0, seq, dtype=F32)[:, None]
    bands = (HY_EMB - 1) // 2
    freqs = jnp.linspace(1e-4, bands - 1, bands, dtype=F32)[None, :]
    ang = (2.0 * math.pi / seq) * jnp.arange(seq, dtype=F32)[:, None] * freqs
    feat = jnp.concatenate([t, jnp.cos(ang), -jnp.sin(ang)], -1)
    feat = jnp.pad(feat, ((0, 0), (0, HY_EMB_PAD - HY_EMB)))
    deltas = jnp.abs(jnp.linspace(math.log(HY_TARGET) / HY_SLOW, math.log(HY_TARGET) / HY_FAST,
                                  MIXER_W, dtype=F32))
    decay = jnp.exp(-t * deltas[None, :])
    split = lambda a: jnp.stack([a[0::2], a[1::2]], axis=0)
    return split(feat), split(decay)


def _hyena_mix(z, lp, gain, n_seq, seq, row0):
    half = seq // 2
    tm_f = min(512, seq)
    tm_i = min(256, half)
    ncol = n_seq * MIXER_W
    tn = min(1024, ncol)
    t_e, t_o = _hy_tables(seq, tm_f)
    inv_e, inv_o = t_e.T, t_o.T
    feat, decay = _hy_features(seq)
    ab = _hy_filters(feat, decay, lp["hy_w1p"], lp["hy_b1"], lp["hy_f1"], lp["hy_w2"], lp["hy_b2"],
                     lp["hy_f2"], lp["hy_w3"])
    kspec = _hy_kspec(t_e, t_o, ab, tm_f)
    conv, u = _hy_conv(z, lp["hy_conv_w"], lp["hy_conv_b"], n_seq, seq, row0)
    y0 = _hy_fwd(t_e, t_o, u, kspec, 0, tm_f, tn)
    u1 = _hy_inv(inv_e, inv_o, y0, u, conv, 0, lp["hy_skip"][0:1], None, False, tm_i, tn)
    y1 = _hy_fwd(t_e, t_o, u1, kspec, 1, tm_f, tn)
    out = _hy_inv(inv_e, inv_o, y1, u1, conv, 1, lp["hy_skip"][1:2], gain, True, tm_i, tn)
    return jnp.transpose(out, (0, 2, 1, 3)).reshape(n_seq * seq, MIXER_W)


def _gelu(x):
    c = math.sqrt(2.0 / math.pi)
    return 0.5 * x * (1.0 + jnp.tanh(c * (x + 0.044715 * (x * x * x))))


def _gmlp_kernel(u_ref, v_ref, lg_ref, lb_ref, ws_ref, bs_ref, g_ref, o_ref):
    tl = u_ref.shape[0]
    gw = MIXER_W // GMLP_GROUPS
    lane_grp = lax.broadcasted_iota(jnp.int32, (GMLP_CHUNK, MIXER_W), 1) // gw
    for c in range(tl // GMLP_CHUNK):
        rows = slice(c * GMLP_CHUNK, (c + 1) * GMLP_CHUNK)
        v = _ln(_gelu(v_ref[rows, :])) * lg_ref[...] + lb_ref[...]
        vb = v.astype(BF16)
        stacked = jnp.concatenate(
            [jnp.where(lane_grp == g, vb, jnp.zeros_like(vb)) for g in range(GMLP_GROUPS)], axis=0)
        s = jnp.dot(ws_ref[...], stacked, preferred_element_type=F32) + bs_ref[...]
        y = _gelu(u_ref[rows, :]) * s
        o_ref[rows, :] = _rms_gain(y, g_ref[...]).astype(BF16)


def _gmlp_mix(z, ln_g, ln_b, ws_cat, bs_mat, gain, n_rows):
    tl = 512
    full = lambda shape: pl.BlockSpec(shape, lambda i: tuple(0 for _ in shape))
    return pl.pallas_call(
        _gmlp_kernel,
        grid=(n_rows // tl,),
        in_specs=[
            pl.BlockSpec((tl, MIXER_W), lambda i: (i, COL_GU)),
            pl.BlockSpec((tl, MIXER_W), lambda i: (i, COL_GV)),
            full((1, MIXER_W)), full((1, MIXER_W)),
            full((GMLP_CHUNK, GMLP_GROUPS * GMLP_CHUNK)),
            full((GMLP_CHUNK, MIXER_W)),
            full((1, MIXER_W)),
        ],
        out_specs=pl.BlockSpec((tl, MIXER_W), lambda i: (i, 0)),
        out_shape=jax.ShapeDtypeStruct((n_rows, MIXER_W), BF16),
        compiler_params=_cparams(("arbitrary",)),
        name="gmlp_mix",
    )(z, z, ln_g, ln_b, ws_cat, bs_mat, gain)


def _merge_kernel(yp_ref, ya_ref, yh_ref, yg_ref, w_ref, x_ref, g1_ref, lg_ref, lb_ref, sh_ref, sc_ref,
                  wrh_ref, wrl_ref, rb_ref, xo_ref, h_ref, idx_ref, gate_ref, y_a, y_b, *, alpha):
    s = pl.program_id(0)

    @pl.when(s == 0)
    def _():
        y_b[...] = jnp.zeros_like(y_b)

    def step(y_new, y_old):
        ymix = jnp.concatenate([yp_ref[...], ya_ref[...], yh_ref[...], yg_ref[...]], axis=-1)
        y_new[...] = jnp.dot(ymix, w_ref[...], preferred_element_type=F32)

        x1 = _ln(alpha * x_ref[...] + g1_ref[...] * y_old[...]) * lg_ref[...] + lb_ref[...]
        xo_ref[...] = x1
        h = _ln(x1) * (1.0 + sc_ref[...]) + sh_ref[...]
        h_ref[...] = h
        h_hi = h.astype(BF16)
        h_lo = (h - h_hi.astype(F32)).astype(BF16)
        logits = (jnp.dot(h_hi, wrh_ref[...], preferred_element_type=F32)
                  + (jnp.dot(h_lo, wrh_ref[...], preferred_element_type=F32)
                     + jnp.dot(h_hi, wrl_ref[...], preferred_element_type=F32)))
        idx, gate = _route_rows(logits, rb_ref[...])
        idx_ref[...] = idx
        gate_ref[...] = gate

    even = lax.rem(s, 2) == 0
    pl.when(even)(functools.partial(step, y_a, y_b))
    pl.when(jnp.logical_not(even))(functools.partial(step, y_b, y_a))


def _route_rows(logits, bias):
    neg = -1e30
    lane = lax.broadcasted_iota(jnp.int32, logits.shape, 1)
    lane_f = lane.astype(F32)
    valid = lane < N_EXPERTS
    per = N_EXPERTS // N_EXPERT_GROUPS
    s = jax.nn.sigmoid(logits)
    sel = jnp.where(valid, s + bias, neg)
    sh = [sel] + [pltpu.roll(sel, LOGIT_PAD - j, axis=1) for j in range(1, per)]
    pair = None
    for a in range(per):
        for b in range(a + 1, per):
            t = sh[a] + sh[b]
            pair = t if pair is None else jnp.maximum(pair, t)
    grp = jnp.where(valid & ((lane & (per - 1)) == 0), pair, neg)

    def first_max(v):
        m = jnp.max(v, axis=-1, keepdims=True)
        return jnp.min(jnp.where(v == m, lane_f, float(LOGIT_PAD)), axis=-1, keepdims=True)

    best = first_max(grp).astype(jnp.int32)
    shift = per.bit_length() - 1
    cand = jnp.where(valid & ((lane >> shift) == (best >> shift)), sel, neg)
    i1 = first_max(cand)
    i2 = first_max(jnp.where(lane_f == i1, neg, cand))
    w1 = jnp.sum(jnp.where(lane_f == i1, s, 0.0), axis=-1, keepdims=True)
    w2 = jnp.sum(jnp.where(lane_f == i2, s, 0.0), axis=-1, keepdims=True)
    tot = w1 + w2
    idx = jnp.where(lane == 0, i1, jnp.where(lane == 1, i2, 0.0)).astype(jnp.int32)
    gate = jnp.where(lane == 0, w1 / tot, jnp.where(lane == 1, w2 / tot, 0.0))
    return idx, gate


def _merge(ys, w_out_bf, xa, mod3, ln_g, ln_b, wr_hi, wr_lo, rb_pad, n_rows, n_batch, seq, alpha):
    d = xa.shape[1]
    tm = 512
    nt = n_rows // tm

    def cur(i):
        return jnp.minimum(i, nt - 1)

    def prv(i):
        return jnp.maximum(i - 1, 0)

    def mrow(i):
        return jnp.minimum((prv(i) * tm) // seq, n_batch)

    ymix = pl.BlockSpec((tm, MIXER_W), lambda i: (cur(i), 0))
    rowvec = pl.BlockSpec((1, d), lambda i: (0, 0))
    return pl.pallas_call(
        functools.partial(_merge_kernel, alpha=alpha),
        grid=(nt + 1,),
        in_specs=[
            ymix, ymix, ymix, ymix,
            pl.BlockSpec((d, d), lambda i: (0, 0)),
            pl.BlockSpec((tm, d), lambda i: (prv(i), 0)),
            pl.BlockSpec((None, 1, d), lambda i: (mrow(i), 0, 2)),
            rowvec, rowvec,
            pl.BlockSpec((None, 1, d), lambda i: (mrow(i), 0, 3)),
            pl.BlockSpec((None, 1, d), lambda i: (mrow(i), 0, 4)),
            pl.BlockSpec((d, LOGIT_PAD), lambda i: (0, 0)),
            pl.BlockSpec((d, LOGIT_PAD), lambda i: (0, 0)),
            pl.BlockSpec((1, LOGIT_PAD), lambda i: (0, 0)),
        ],
        out_specs=[
            pl.BlockSpec((tm, d), lambda i: (prv(i), 0)),
            pl.BlockSpec((tm, d), lambda i: (prv(i), 0)),
            pl.BlockSpec((tm, LOGIT_PAD), lambda i: (prv(i), 0)),
            pl.BlockSpec((tm, LOGIT_PAD), lambda i: (prv(i), 0)),
        ],
        out_shape=[
            jax.ShapeDtypeStruct((n_rows, d), F32),
            jax.ShapeDtypeStruct((n_rows, d), F32),
            jax.ShapeDtypeStruct((n_rows, LOGIT_PAD), jnp.int32),
            jax.ShapeDtypeStruct((n_rows, LOGIT_PAD), F32),
        ],
        scratch_shapes=[pltpu.VMEM((tm, d), F32), pltpu.VMEM((tm, d), F32)],
        compiler_params=_cparams(("arbitrary",)),
        name="merge",
    )(*ys, w_out_bf, xa, mod3, ln_g, ln_b, mod3, mod3, wr_hi, wr_lo, rb_pad)


def _moe_kernel(be_ref, nu_ref, tok_ref, ws_ref, nxt_ref, h_hbm, wg_hbm, wu_hbm, wd_hbm, o_ref,
                xbuf, xsem, stage_g, stage_u, stage_d, wsem, wg_s, wu_s, wd_s, *, layer):
    i = pl.program_id(0)
    n_used = nu_ref[0]
    slot = lax.rem(i, 2)

    def row_copy(blk, r, sl):
        tok = tok_ref[blk * MOE_BLOCK + r]
        return pltpu.make_async_copy(h_hbm.at[pl.ds(tok, 1)], xbuf.at[sl, pl.ds(r, 1)], xsem.at[sl])

    def gather_start(blk, sl):
        def body(r, carry):
            row_copy(blk, r, sl).start()
            return carry

        lax.fori_loop(0, MOE_BLOCK, body, 0, unroll=8)

    def gather_wait(sl):
        pltpu.make_async_copy(h_hbm.at[pl.ds(0, MOE_BLOCK)], xbuf.at[sl], xsem.at[sl]).wait()

    def weight_copies(e, ws):
        return (pltpu.make_async_copy(wg_hbm.at[layer, e], stage_g.at[ws], wsem.at[ws]),
                pltpu.make_async_copy(wu_hbm.at[layer, e], stage_u.at[ws], wsem.at[ws]),
                pltpu.make_async_copy(wd_hbm.at[layer, e], stage_d.at[ws], wsem.at[ws]))

    @pl.when(i == 0)
    def _():
        gather_start(0, 0)
        for cp in weight_copies(be_ref[0], ws_ref[0]):
            cp.start()

    active = i < n_used
    new_expert = jnp.logical_or(i == 0, be_ref[i] != be_ref[jnp.maximum(i - 1, 0)])

    @pl.when(jnp.logical_and(active, new_expert))
    def _():
        ws = ws_ref[i]
        for cp in weight_copies(be_ref[i], ws):
            cp.wait()
        nxt = nxt_ref[i]

        @pl.when(nxt >= 0)
        def _():
            for cp in weight_copies(nxt, 1 - ws):
                cp.start()

        for src, dst in ((stage_g, wg_s), (stage_u, wu_s), (stage_d, wd_s)):
            def cast(r0, src=src, dst=dst):
                dst[pl.ds(r0, 256), :] = src[ws, pl.ds(r0, 256), :].astype(BF16)

            _row_chunks(dst.shape[0], 256, cast)

    def ffn_block(prefetch_next):
        gather_wait(slot)
        x = xbuf[slot].astype(BF16)
        if prefetch_next:
            for r in range(MOE_BLOCK):
                row_copy(i + 1, r, 1 - slot).start()
        g = jnp.dot(x, wg_s[...], preferred_element_type=F32)
        u = jnp.dot(x, wu_s[...], preferred_element_type=F32)
        a = (g * jax.nn.sigmoid(g) * u).astype(BF16)
        o_ref[...] = jnp.dot(a, wd_s[...], preferred_element_type=F32).astype(o_ref.dtype)

    has_next = i + 1 < n_used
    pl.when(jnp.logical_and(active, has_next))(functools.partial(ffn_block, True))
    pl.when(jnp.logical_and(active, jnp.logical_not(has_next)))(functools.partial(ffn_block, False))

    @pl.when(jnp.logical_not(active))
    def _():
        o_ref[...] = jnp.zeros_like(o_ref)


def _moe_experts(h, plan, wg, wu, wd, layer):
    d = h.shape[1]
    n_slots = plan["slot_tok"].shape[0]
    de = wg.shape[3]
    n_blocks = n_slots // MOE_BLOCK
    hbm = pl.BlockSpec(memory_space=pl.ANY)
    grid_spec = pltpu.PrefetchScalarGridSpec(
        num_scalar_prefetch=5,
        grid=(n_blocks,),
        in_specs=[hbm, hbm, hbm, hbm],
        out_specs=pl.BlockSpec((MOE_BLOCK, d), lambda i, *_: (i, 0)),
        scratch_shapes=[
            pltpu.VMEM((2, MOE_BLOCK, d), F32),
            pltpu.SemaphoreType.DMA((2,)),
            pltpu.VMEM((2, d, de), F32), pltpu.VMEM((2, d, de), F32), pltpu.VMEM((2, de, d), F32),
            pltpu.SemaphoreType.DMA((2,)),
            pltpu.VMEM((d, de), BF16), pltpu.VMEM((d, de), BF16), pltpu.VMEM((de, d), BF16),
        ],
    )
    return pl.pallas_call(
        functools.partial(_moe_kernel, layer=layer),
        grid_spec=grid_spec,
        out_shape=jax.ShapeDtypeStruct((n_slots, d), BF16),
        compiler_params=_cparams(("arbitrary",)),
        name="moe_experts",
    )(plan["block_e"], plan["n_used"], plan["slot_tok"], plan["wslot"], plan["next_e"], h, wg, wu, wd)


def _assignment_ranks(flat_e):
    a = flat_e.shape[0]
    blk = 128
    nb = a // blk
    onehot = (flat_e[:, None] == jnp.arange(N_EXPERTS)[None, :]).astype(BF16).reshape(nb, blk, N_EXPERTS)
    tri = (jnp.arange(blk)[:, None] >= jnp.arange(blk)[None, :]).astype(BF16)
    intra = jnp.einsum("ij,bjk->bik", tri, onehot, preferred_element_type=F32)
    bsum = intra[:, -1, :]
    before = (jnp.arange(nb)[:, None] > jnp.arange(nb)[None, :]).astype(BF16)
    offs = jnp.dot(before, bsum.astype(BF16), preferred_element_type=F32)
    csum = intra + offs[:, None, :]
    rank = jnp.sum(csum * onehot.astype(F32), axis=-1).reshape(a) - 1.0
    counts = offs[-1] + bsum[-1]
    return rank.astype(jnp.int32), counts.astype(jnp.int32)


def _dispatch_plan(idx):
    t = idx.shape[0]
    a = t * TOP_K
    flat_e = idx.reshape(a)
    rank, counts = _assignment_ranks(flat_e)
    padded = (counts + MOE_BLOCK - 1) // MOE_BLOCK * MOE_BLOCK
    pad_end = jnp.cumsum(padded)
    pad_start = pad_end - padded
    dest = pad_start[flat_e] + rank
    n_blocks = (a + N_EXPERTS * (MOE_BLOCK - 1) + MOE_BLOCK - 1) // MOE_BLOCK
    n_slots = n_blocks * MOE_BLOCK
    flat_tok = jnp.repeat(jnp.arange(t, dtype=jnp.int32), TOP_K)
    slot_tok = jnp.zeros((n_slots,), jnp.int32).at[dest].set(flat_tok)
    blk_start = jnp.arange(n_blocks, dtype=jnp.int32) * MOE_BLOCK
    block_e = jnp.minimum(
        jnp.sum((pad_end[None, :] <= blk_start[:, None]).astype(jnp.int32), axis=1), N_EXPERTS - 1)
    n_used = (pad_end[-1] // MOE_BLOCK).astype(jnp.int32).reshape(1)
    nonempty = counts > 0
    eid = jnp.arange(N_EXPERTS, dtype=jnp.int32)
    run_of_e = jnp.cumsum(nonempty.astype(jnp.int32)) - 1
    later = jnp.logical_and(eid[None, :] > eid[:, None], nonempty[None, :])
    nxt_of_e = jnp.min(jnp.where(later, eid[None, :], N_EXPERTS), axis=1)
    nxt_of_e = jnp.where(nxt_of_e == N_EXPERTS, -1, nxt_of_e)
    plan = {
        "slot_tok": slot_tok, "block_e": block_e.astype(jnp.int32), "n_used": n_used,
        "wslot": (run_of_e[block_e] % 2).astype(jnp.int32), "next_e": nxt_of_e[block_e].astype(jnp.int32),
    }
    return plan, dest.reshape(t, TOP_K)


def _final_kernel(x_ref, ya_ref, yb_ref, gate_ref, g2_ref, lg_ref, lb_ref, o_ref, *, alpha):
    f = ya_ref[...].astype(F32) * gate_ref[:, 0:1] + yb_ref[...].astype(F32) * gate_ref[:, 1:2]
    o_ref[...] = _ln(alpha * x_ref[...] + g2_ref[...] * f) * lg_ref[...] + lb_ref[...]


def _final_norm(x1, ya, yb, gate, mod3, ln_g, ln_b, n_batch, seq, alpha):
    n_rows, d = x1.shape
    tm = 512

    def mrow(i):
        return jnp.minimum((i * tm) // seq, n_batch)

    tile = pl.BlockSpec((tm, d), lambda i: (i, 0))
    rowvec = pl.BlockSpec((1, d), lambda i: (0, 0))
    return pl.pallas_call(
        functools.partial(_final_kernel, alpha=alpha),
        grid=(n_rows // tm,),
        in_specs=[tile, tile, tile, pl.BlockSpec((tm, LOGIT_PAD), lambda i: (i, 0)),
                  pl.BlockSpec((None, 1, d), lambda i: (mrow(i), 0, 5)), rowvec, rowvec],
        out_specs=tile,
        out_shape=jax.ShapeDtypeStruct((n_rows, d), F32),
        compiler_params=_cparams(("arbitrary",)),
        name="final_norm",
    )(x1, ya, yb, gate, mod3, ln_g, ln_b)


def _permute_in_cols(w):
    pool, q, k, v, hy, gm = jnp.split(w, (512, 1024, 1152, 1280, 2816), axis=-1)
    return jnp.concatenate([hy, pool, q, gm, k, v], axis=-1)


def kernel(x, c, ctx, c_ctx, w_ada, b_ada, w_in, w_out, mix_norm_g, pool_w, pool_scale, attn_sink,
           hy_conv_w, hy_conv_b, hy_w1, hy_b1, hy_f1, hy_w2, hy_b2, hy_f2, hy_w3, hy_skip,
           gm_ln_g, gm_ln_b, gm_ws, gm_bs, ln1_g, ln1_b, ln2_g, ln2_b, w_router, router_bias,
           w_gate, w_up, w_down):
    n_batch, seq, d = x.shape
    ctx_len = ctx.shape[1]
    depth = w_in.shape[0]
    grid_w = 64
    alpha = (2 * depth) ** 0.25
    n_lat = n_batch * seq
    n_ctx = n_batch * ctx_len

    xa = jnp.concatenate([x.reshape(n_lat, d), ctx.reshape(n_ctx, d)], axis=0)
    cvec = jnp.concatenate([c, c_ctx[None, :], jnp.zeros((8 - n_batch - 1, d), F32)], axis=0)
    mods = _ada_mods(cvec, w_ada, b_ada)
    cos_t, sin_t = _rope_tables(seq, grid_w)
    wr_pad = jnp.pad(w_router, ((0, 0), (0, LOGIT_PAD - N_EXPERTS)))
    wr_hi = wr_pad.astype(BF16)
    wr_lo = (wr_pad - wr_hi.astype(F32)).astype(BF16)
    rb_pad = jnp.pad(router_bias.astype(F32), (0, LOGIT_PAD - N_EXPERTS)).reshape(1, LOGIT_PAD)
    row = lambda v: v.reshape(1, -1)

    for layer in range(depth):
        last = layer == depth - 1
        mod3 = mods[layer].reshape(8, 1, 6 * d)
        gains = mix_norm_g[layer].reshape(4, 1, MIXER_W)
        lp = {
            "hy_conv_w": hy_conv_w[layer], "hy_conv_b": row(hy_conv_b[layer]),
            "hy_w1p": jnp.pad(hy_w1[layer], ((0, HY_EMB_PAD - HY_EMB), (0, 0))),
            "hy_b1": row(hy_b1[layer]), "hy_f1": row(hy_f1[layer]),
            "hy_w2": hy_w2[layer], "hy_b2": row(hy_b2[layer]), "hy_f2": row(hy_f2[layer]),
            "hy_w3": hy_w3[layer], "hy_skip": hy_skip[layer],
        }
        z = _in_proj(xa, mod3, _permute_in_cols(w_in[layer]).astype(BF16), n_batch, seq)

        pool_bf = pool_w[layer].astype(BF16)
        ps = row(pool_scale[layer])
        ws_cat = jnp.transpose(gm_ws[layer], (1, 0, 2)).reshape(GMLP_CHUNK, GMLP_GROUPS * GMLP_CHUNK)
        bs_mat = jnp.repeat(gm_bs[layer].T, MIXER_W // GMLP_GROUPS, axis=1)
        n_rows = n_lat if last else n_lat + n_ctx

        y_pool = _pool_mix(z, pool_bf, ps, gains[0], n_batch, seq, 0)
        y_attn = _attn_lat(z, attn_sink[layer], cos_t, sin_t, gains[1], n_batch, seq, ctx_len)
        y_hy = _hyena_mix(z, lp, gains[2], n_batch, seq, 0)
        y_gm = _gmlp_mix(z, row(gm_ln_g[layer]), row(gm_ln_b[layer]), ws_cat.astype(BF16), bs_mat,
                         gains[3], n_rows)
        if not last:
            y_pool = jnp.concatenate(
                [y_pool, _pool_mix(z, pool_bf, ps, gains[0], n_batch, ctx_len, n_lat // ctx_len)], axis=0)
            y_attn = jnp.concatenate(
                [y_attn, _attn_ctx(z, attn_sink[layer], gains[1], n_batch, seq, ctx_len)], axis=0)
            y_hy = jnp.concatenate([y_hy, _hyena_mix(z, lp, gains[2], n_batch, ctx_len, n_lat)], axis=0)

        x1, h2, idx, gate = _merge((y_pool, y_attn, y_hy, y_gm), w_out[layer].astype(BF16), xa, mod3,
                                   row(ln1_g[layer]), row(ln1_b[layer]), wr_hi, wr_lo, rb_pad,
                                   n_rows, n_batch, seq, alpha)

        plan, dest = _dispatch_plan(idx[:, :TOP_K])
        ys = _moe_experts(h2, plan, w_gate, w_up, w_down, layer)
        xa = _final_norm(x1, ys[dest[:, 0]], ys[dest[:, 1]], gate, mod3, row(ln2_g[layer]), row(ln2_b[layer]),
                         n_batch, seq, alpha)
    return xa.reshape(n_batch, seq, d)
```

```python
import functools
import math

import jax
import jax.numpy as jnp
from jax import lax
from jax.experimental import pallas as pl
from jax.experimental.pallas import tpu as pltpu

F32 = jnp.float32
BF16 = jnp.bfloat16
HI = lax.Precision.HIGHEST

LN_EPS = 1e-6
MIXER_W = 512
HEAD_DIM = 64
N_HEADS = 8
N_KV_HEADS = 2
KV_GROUP = 4
KV_W = 128
WINDOW = 128
ROPE_THETA = 10000.0
POOL_WINDOWS = (2, 4, 8, 16)
POOL_GROUP = 128
POOL_HALO = 8
GMLP_CHUNK = 128
GMLP_GROUPS = 8
HY_EMB = 33
HY_EMB_PAD = 128
HY_TARGET = 1e-2
HY_FAST = 0.3
HY_SLOW = 1.5
N_EXPERTS = 32
N_EXPERT_GROUPS = 8
TOP_K = 2
D_EXPERT = 768
MOE_BLOCK = 256
LOGIT_PAD = 128

COL_HY = 0
COL_POOL = 3
COL_Q = 4
COL_GU = 5
COL_GV = 6
COL_K = 28
COL_V = 29
IN_W = 3840

VMEM_LIMIT = 56 * 1024 * 1024


def _cparams(sem):
    return pltpu.CompilerParams(dimension_semantics=sem, vmem_limit_bytes=VMEM_LIMIT)


def _ln(x):
    mu = jnp.mean(x, axis=-1, keepdims=True)
    xc = x - mu
    var = jnp.mean(xc * xc, axis=-1, keepdims=True)
    return xc * lax.rsqrt(var + LN_EPS)


def _rms_gain(y, g):
    return y * lax.rsqrt(jnp.mean(y * y, axis=-1, keepdims=True) + LN_EPS) * g


def _row_chunks(n_rows, chunk, body):
    n = n_rows // chunk
    if n == 1:
        body(0)
        return

    def step(i, carry):
        body(pl.multiple_of(i * chunk, chunk))
        return carry

    lax.fori_loop(0, n, step, 0)


def _ada_kernel(c_ref, w_ref, b_ref, o_ref):
    c = c_ref[...]
    s = c * jax.nn.sigmoid(c)
    o_ref[...] = jnp.dot(s, w_ref[...], precision=HI, preferred_element_type=F32) + b_ref[...]


def _ada_mods(cvec, w_ada, b_ada):
    depth, d, n = w_ada.shape
    tn = 1024
    return pl.pallas_call(
        _ada_kernel,
        grid=(depth, n // tn),
        in_specs=[
            pl.BlockSpec((8, d), lambda l, j: (0, 0)),
            pl.BlockSpec((None, d, tn), lambda l, j: (l, 0, j)),
            pl.BlockSpec((None, 1, tn), lambda l, j: (l, 0, j)),
        ],
        out_specs=pl.BlockSpec((None, 8, tn), lambda l, j: (l, 0, j)),
        out_shape=jax.ShapeDtypeStruct((depth, 8, n), F32),
        compiler_params=_cparams(("arbitrary", "arbitrary")),
        name="ada_mods",
    )(cvec, w_ada, b_ada.reshape(depth, 1, n))


def _inproj_kernel(x_ref, sh_ref, sc_ref, w_ref, o_ref, h_scr):
    @pl.when(pl.program_id(1) == 0)
    def _():
        sh = sh_ref[...]
        sc1 = 1.0 + sc_ref[...]

        def body(r0):
            x = x_ref[pl.ds(r0, 256), :]
            h_scr[pl.ds(r0, 256), :] = (_ln(x) * sc1 + sh).astype(BF16)

        _row_chunks(x_ref.shape[0], 256, body)

    o_ref[...] = jnp.dot(h_scr[...], w_ref[...], preferred_element_type=F32)


def _in_proj(xa, mod3, w_bf, n_batch, seq):
    t, d = xa.shape
    n = w_bf.shape[1]
    tn = 768
    tm = next(m for m in (1024, 512, 256) if t % m == 0 and seq % m == 0)

    def mrow(i):
        return jnp.minimum((i * tm) // seq, n_batch)

    return pl.pallas_call(
        _inproj_kernel,
        grid=(t // tm, n // tn),
        in_specs=[
            pl.BlockSpec((tm, d), lambda i, j: (i, 0)),
            pl.BlockSpec((None, 1, d), lambda i, j: (mrow(i), 0, 0)),
            pl.BlockSpec((None, 1, d), lambda i, j: (mrow(i), 0, 1)),
            pl.BlockSpec((d, tn), lambda i, j: (0, j)),
        ],
        out_specs=pl.BlockSpec((tm, tn), lambda i, j: (i, j)),
        out_shape=jax.ShapeDtypeStruct((t, n), F32),
        scratch_shapes=[pltpu.VMEM((tm, d), BF16)],
        compiler_params=_cparams(("arbitrary", "arbitrary")),
        name="in_proj",
    )(xa, mod3, mod3, w_bf)


def _pool_kernel(z_ref, w_ref, ps_ref, g_ref, o_ref, buf):
    seq = z_ref.shape[0]
    rc = min(256, seq)
    zeros = jnp.zeros((POOL_HALO, MIXER_W), F32)
    buf[0:POOL_HALO, :] = zeros
    buf[seq + POOL_HALO:seq + 2 * POOL_HALO, :] = zeros

    def fill(r0):
        buf[pl.ds(r0 + POOL_HALO, rc), :] = z_ref[pl.ds(r0, rc), :]

    _row_chunks(seq, rc, fill)

    def body(r0):
        win = buf[pl.ds(r0, rc + 2 * POOL_HALO), :]
        t = r0 + lax.broadcasted_iota(jnp.int32, (rc, 1), 0)
        outs = []
        for gi, w in enumerate(POOL_WINDOWS):
            lanes = slice(gi * POOL_GROUP, (gi + 1) * POOL_GROUP)
            acc = None
            for j in range(-w // 2, w // 2):
                piece = win[POOL_HALO + j:POOL_HALO + j + rc, lanes]
                acc = piece if acc is None else acc + piece
            cnt = (jnp.minimum(t + w // 2, seq) - jnp.maximum(t - w // 2, 0)).astype(F32)
            dlt = acc / cnt - win[POOL_HALO:POOL_HALO + rc, lanes]
            outs.append(jnp.dot(dlt.astype(BF16), w_ref[gi], preferred_element_type=F32))
        y = jnp.concatenate(outs, axis=-1) * ps_ref[...]
        o_ref[pl.ds(r0, rc), :] = _rms_gain(y, g_ref[...]).astype(BF16)

    _row_chunks(seq, rc, body)


def _pool_mix(z, pool_w_bf, pool_scale, gain, n_seq, seq, row_blk0):
    return pl.pallas_call(
        _pool_kernel,
        grid=(n_seq,),
        in_specs=[
            pl.BlockSpec((seq, MIXER_W), lambda s: (row_blk0 + s, COL_POOL)),
            pl.BlockSpec((4, POOL_GROUP, POOL_GROUP), lambda s: (0, 0, 0)),
            pl.BlockSpec((1, MIXER_W), lambda s: (0, 0)),
            pl.BlockSpec((1, MIXER_W), lambda s: (0, 0)),
        ],
        out_specs=pl.BlockSpec((seq, MIXER_W), lambda s: (s, 0)),
        out_shape=jax.ShapeDtypeStruct((n_seq * seq, MIXER_W), BF16),
        scratch_shapes=[pltpu.VMEM((seq + 2 * POOL_HALO, MIXER_W), F32)],
        compiler_params=_cparams(("arbitrary",)),
        name="pool_mix",
    )(z, pool_w_bf, pool_scale, gain)


def _rope(x, cos, sin_signed):
    lane = lax.broadcasted_iota(jnp.int32, x.shape, 1)
    partner = jnp.where((lane & 31) < 16, pltpu.roll(x, 112, axis=1), pltpu.roll(x, 16, axis=1))
    return x * cos + partner * sin_signed


def _attend(qs, sink_col, parts):
    dn = (((1,), (1,)), ((), ()))
    scores = []
    for kk, _, mask in parts:
        s = lax.dot_general(qs, kk, dn, preferred_element_type=F32)
        if mask is not None:
            s = s + mask
        scores.append(s)
    m = sink_col
    for s in scores:
        m = jnp.maximum(m, jnp.max(s, axis=-1, keepdims=True))
    den = jnp.exp(sink_col - m)
    out = None
    for s, (_, vv, _) in zip(scores, parts):
        p = jnp.exp(s - m)
        den = den + jnp.sum(p, axis=-1, keepdims=True)
        o = jnp.dot(p.astype(BF16), vv, preferred_element_type=F32)
        out = o if out is None else out + o
    return out / den


def _heads_attend(q_all, sink_ref, tq, make_parts):
    cols = [None] * N_HEADS
    for kh in range(N_KV_HEADS):
        heads = [kh * KV_GROUP + g for g in range(KV_GROUP)]
        qs = jnp.concatenate([q_all[:, h * HEAD_DIM:(h + 1) * HEAD_DIM] for h in heads], axis=0)
        sink_col = jnp.concatenate([jnp.full((tq, 1), sink_ref[h], F32) for h in heads], axis=0)
        o = _attend(qs.astype(BF16), sink_col, make_parts(kh))
        for g, h in enumerate(heads):
            cols[h] = o[g * tq:(g + 1) * tq, :]
    return jnp.concatenate(cols, axis=-1)


def _attn_lat_kernel(sink_ref, q_ref, k_ref, v_ref, kc_ref, vc_ref, cos_ref, sin_ref, g_ref, o_ref):
    tq = q_ref.shape[0]
    seq = k_ref.shape[0]
    nband = tq + 2 * WINDOW
    i = pl.program_id(1)
    q0 = pl.multiple_of(i * tq, tq)
    k0 = pl.multiple_of(jnp.clip(q0 - WINDOW, 0, seq - nband), WINDOW)

    cq = cos_ref[pl.ds(q0, tq), :]
    sq = sin_ref[pl.ds(q0, tq), :]
    q = q_ref[...]
    q_all = jnp.concatenate(
        [_rope(q[:, c * 128:(c + 1) * 128], cq, sq) for c in range(MIXER_W // 128)], axis=-1)
    q_all = q_all * (HEAD_DIM ** -0.5)

    kb = _rope(k_ref[pl.ds(k0, nband), :], cos_ref[pl.ds(k0, nband), :],
               sin_ref[pl.ds(k0, nband), :]).astype(BF16)
    vb = v_ref[pl.ds(k0, nband), :].astype(BF16)
    kc = kc_ref[...].astype(BF16)
    vc = vc_ref[...].astype(BF16)

    qpos = q0 + lax.broadcasted_iota(jnp.int32, (tq, nband), 0)
    kpos = k0 + lax.broadcasted_iota(jnp.int32, (tq, nband), 1)
    band = jnp.where(jnp.abs(qpos - kpos) <= WINDOW, 0.0, -1e30)
    mask = jnp.concatenate([band] * KV_GROUP, axis=0)

    def make_parts(kh):
        hs = slice(kh * HEAD_DIM, (kh + 1) * HEAD_DIM)
        return [(kb[:, hs], vb[:, hs], mask), (kc[:, hs], vc[:, hs], None)]

    y = _heads_attend(q_all, sink_ref, tq, make_parts)
    o_ref[...] = _rms_gain(y, g_ref[...]).astype(BF16)


def _attn_ctx_kernel(sink_ref, q_ref, kc_ref, vc_ref, g_ref, o_ref):
    tq = q_ref.shape[0]
    q_all = q_ref[...] * (HEAD_DIM ** -0.5)
    kc = kc_ref[...].astype(BF16)
    vc = vc_ref[...].astype(BF16)

    def make_parts(kh):
        hs = slice(kh * HEAD_DIM, (kh + 1) * HEAD_DIM)
        return [(kc[:, hs], vc[:, hs], None)]

    y = _heads_attend(q_all, sink_ref, tq, make_parts)
    o_ref[...] = _rms_gain(y, g_ref[...]).astype(BF16)


def _attn_lat(z, sink, cos_t, sin_t, gain, n_batch, seq, ctx_len):
    tq = 256
    nq = seq // tq
    ctx_blk0 = (n_batch * seq) // ctx_len
    smem = pl.BlockSpec(memory_space=pltpu.SMEM)
    return pl.pallas_call(
        _attn_lat_kernel,
        grid=(n_batch, nq),
        in_specs=[
            smem,
            pl.BlockSpec((tq, MIXER_W), lambda b, i: (b * nq + i, COL_Q)),
            pl.BlockSpec((seq, KV_W), lambda b, i: (b, COL_K)),
            pl.BlockSpec((seq, KV_W), lambda b, i: (b, COL_V)),
            pl.BlockSpec((ctx_len, KV_W), lambda b, i: (ctx_blk0 + b, COL_K)),
            pl.BlockSpec((ctx_len, KV_W), lambda b, i: (ctx_blk0 + b, COL_V)),
            pl.BlockSpec((seq, 128), lambda b, i: (0, 0)),
            pl.BlockSpec((seq, 128), lambda b, i: (0, 0)),
            pl.BlockSpec((1, MIXER_W), lambda b, i: (0, 0)),
        ],
        out_specs=pl.BlockSpec((tq, MIXER_W), lambda b, i: (b * nq + i, 0)),
        out_shape=jax.ShapeDtypeStruct((n_batch * seq, MIXER_W), BF16),
        compiler_params=_cparams(("arbitrary", "arbitrary")),
        name="attn_lat",
    )(sink, z, z, z, z, z, cos_t, sin_t, gain)


def _attn_ctx(z, sink, gain, n_batch, seq, ctx_len):
    ctx_blk0 = (n_batch * seq) // ctx_len
    smem = pl.BlockSpec(memory_space=pltpu.SMEM)
    return pl.pallas_call(
        _attn_ctx_kernel,
        grid=(n_batch,),
        in_specs=[
            smem,
            pl.BlockSpec((ctx_len, MIXER_W), lambda b: (ctx_blk0 + b, COL_Q)),
            pl.BlockSpec((ctx_len, KV_W), lambda b: (ctx_blk0 + b, COL_K)),
            pl.BlockSpec((ctx_len, KV_W), lambda b: (ctx_blk0 + b, COL_V)),
            pl.BlockSpec((1, MIXER_W), lambda b: (0, 0)),
        ],
        out_specs=pl.BlockSpec((ctx_len, MIXER_W), lambda b: (b, 0)),
        out_shape=jax.ShapeDtypeStruct((n_batch * ctx_len, MIXER_W), BF16),
        compiler_params=_cparams(("arbitrary",)),
        name="attn_ctx",
    )(sink, z, z, z, gain)


def _rope_tables(seq, grid_w):
    nf = HEAD_DIM // 4
    inv = ROPE_THETA ** (-jnp.arange(nf, dtype=F32) / nf)
    t = jnp.arange(seq)
    row = (t // grid_w).astype(F32)[:, None] * inv[None, :]
    col = (t % grid_w).astype(F32)[:, None] * inv[None, :]
    cos_h = jnp.concatenate([jnp.cos(row), jnp.cos(row), jnp.cos(col), jnp.cos(col)], axis=-1)
    sin_h = jnp.concatenate([-jnp.sin(row), jnp.sin(row), -jnp.sin(col), jnp.sin(col)], axis=-1)
    return jnp.tile(cos_h, (1, 2)), jnp.tile(sin_h, (1, 2))


def _hy_conv_kernel(*refs):
    z_refs, zp_refs, zn_refs = refs[0:3], refs[3:6], refs[6:9]
    w_ref, b_ref, g_ref, u_ref = refs[9:]
    i = pl.program_id(1)
    th = z_refs[0].shape[0] // 2
    has_prev = (i > 0).astype(F32)
    has_next = (i < pl.num_programs(1) - 1).astype(F32)
    row = lax.broadcasted_iota(jnp.int32, (th, 128), 0)
    for part in range(3):
        ze = z_refs[part][pl.ds(0, th, stride=2), :]
        zo = z_refs[part][pl.ds(1, th, stride=2), :]
        prev_row = zp_refs[part][POOL_HALO - 1:POOL_HALO, :] * has_prev
        next_row = zn_refs[part][0:1, :] * has_next
        zo_m = jnp.where(row == 0, prev_row, pltpu.roll(zo, 1, axis=0))
        ze_p = jnp.where(row == th - 1, next_row, pltpu.roll(ze, th - 1, axis=0))
        w0, w1, w2, b = w_ref[part, 0:1, :], w_ref[part, 1:2, :], w_ref[part, 2:3, :], b_ref[part]
        ye = b + zo_m * w0 + ze * w1 + zo * w2
        yo = b + ze * w0 + zo * w1 + ze_p * w2
        g_ref[part, 0] = ye
        g_ref[part, 1] = yo
        if part == 2:
            u_ref[0] = ye.astype(BF16)
            u_ref[1] = yo.astype(BF16)


def _hy_conv(z, conv_w, conv_b, n_seq, seq, row0):
    tl = min(2048, seq)
    nt = seq // tl
    blk0 = row0 // tl
    hb0 = row0 // POOL_HALO
    hpt = tl // POOL_HALO
    last_halo = (row0 + n_seq * seq) // POOL_HALO - 1
    cpp = MIXER_W // 128
    ncol = n_seq * MIXER_W

    def main_spec(part):
        return pl.BlockSpec((tl, 128), lambda s, i, c: (blk0 + s * nt + i, part * cpp + c))

    def prev_spec(part):
        return pl.BlockSpec((POOL_HALO, 128),
                            lambda s, i, c: (jnp.maximum(hb0 + (s * nt + i) * hpt - 1, 0), part * cpp + c))

    def next_spec(part):
        return pl.BlockSpec((POOL_HALO, 128),
                            lambda s, i, c: (jnp.minimum(hb0 + (s * nt + i + 1) * hpt, last_halo), part * cpp + c))

    w3 = conv_w.reshape(3, 3, MIXER_W).transpose(1, 0, 2)
    b3 = conv_b.reshape(3, 1, MIXER_W)
    return pl.pallas_call(
        _hy_conv_kernel,
        grid=(n_seq, nt, cpp),
        in_specs=[main_spec(p) for p in range(3)] + [prev_spec(p) for p in range(3)]
        + [next_spec(p) for p in range(3)]
        + [pl.BlockSpec((3, 3, 128), lambda s, i, c: (0, 0, c)),
           pl.BlockSpec((3, 1, 128), lambda s, i, c: (0, 0, c))],
        out_specs=[
            pl.BlockSpec((3, 2, tl // 2, 128), lambda s, i, c: (0, 0, i, s * cpp + c)),
            pl.BlockSpec((2, tl // 2, 128), lambda s, i, c: (0, i, s * cpp + c)),
        ],
        out_shape=[jax.ShapeDtypeStruct((3, 2, seq // 2, ncol), F32),
                   jax.ShapeDtypeStruct((2, seq // 2, ncol), BF16)],
        compiler_params=_cparams(("arbitrary", "arbitrary", "arbitrary")),
        name="hy_conv",
    )(*([z] * 9), w3, b3)


def _hy_filter_kernel(feat_ref, dec_ref, w1_ref, b1_ref, f1_ref, w2_ref, b2_ref, f2_ref, w3_ref, o_ref):
    tl = feat_ref.shape[0]
    h = jnp.dot(feat_ref[...], w1_ref[...], precision=HI, preferred_element_type=F32) + b1_ref[...]
    h = jnp.sin(f1_ref[...] * h)
    h = jnp.dot(h, w2_ref[...], precision=HI, preferred_element_type=F32) + b2_ref[...]
    h = jnp.sin(f2_ref[...] * h)
    filt = jnp.dot(h, w3_ref[...], precision=HI, preferred_element_type=F32)
    dec = dec_ref[...]
    t = 2 * (pl.program_id(1) * tl + lax.broadcasted_iota(jnp.int32, (tl, 1), 0)) + pl.program_id(0)
    not_first = (t > 0).astype(F32)
    for o in range(2):
        hf = filt[:, (2 * o) * MIXER_W:(2 * o + 1) * MIXER_W] * dec
        hb = filt[:, (2 * o + 1) * MIXER_W:(2 * o + 2) * MIXER_W] * dec * not_first
        o_ref[:, o * MIXER_W:(o + 1) * MIXER_W] = (hf + hb).astype(BF16)
        o_ref[:, (2 + o) * MIXER_W:(3 + o) * MIXER_W] = (hb - hf).astype(BF16)


def _hy_filters(feat, decay, w1p, b1, f1, w2, b2, f2, w3):
    half = feat.shape[1]
    tl = min(512, half)
    hid = w2.shape[0]
    full = lambda shape: pl.BlockSpec(shape, lambda p, i: tuple(0 for _ in shape))
    return pl.pallas_call(
        _hy_filter_kernel,
        grid=(2, half // tl),
        in_specs=[
            pl.BlockSpec((None, tl, HY_EMB_PAD), lambda p, i: (p, i, 0)),
            pl.BlockSpec((None, tl, MIXER_W), lambda p, i: (p, i, 0)),
            full((HY_EMB_PAD, hid)), full((1, hid)), full((1, hid)),
            full((hid, hid)), full((1, hid)), full((1, hid)),
            full((hid, 4 * MIXER_W)),
        ],
        out_specs=pl.BlockSpec((None, tl, 4 * MIXER_W), lambda p, i: (p, i, 0)),
        out_shape=jax.ShapeDtypeStruct((2, half, 4 * MIXER_W), BF16),
        compiler_params=_cparams(("arbitrary", "arbitrary")),
        name="hy_filters",
    )(feat, decay, w1p, b1, f1, w2, b2, f2, w3)


def _hy_kspec_kernel(te_ref, to_ref, ab_ref, o_ref, *, inv_len):
    hb = te_ref.shape[0] // 2
    half = ab_ref.shape[2] // 2
    ec = jnp.dot(te_ref[0:hb, :], ab_ref[0, :, 0:half], preferred_element_type=F32)
    oc = jnp.dot(to_ref[0:hb, :], ab_ref[1, :, 0:half], preferred_element_type=F32)
    es = jnp.dot(te_ref[hb:, :], ab_ref[0, :, half:], preferred_element_type=F32)
    os_ = jnp.dot(to_ref[hb:, :], ab_ref[1, :, half:], preferred_element_type=F32)
    o_ref[0, 0:hb, :] = (ec + oc) * inv_len
    o_ref[0, hb:, :] = (es + os_) * inv_len
    o_ref[1, 0:hb, :] = (ec - oc) * inv_len
    o_ref[1, hb:, :] = (os_ - es) * inv_len


def _hy_kspec(t_e, t_o, ab, tm):
    seq, half = t_e.shape
    return pl.pallas_call(
        functools.partial(_hy_kspec_kernel, inv_len=1.0 / seq),
        grid=(seq // tm,),
        in_specs=[
            pl.BlockSpec((tm, half), lambda i: (i, 0)),
            pl.BlockSpec((tm, half), lambda i: (i, 0)),
            pl.BlockSpec((2, half, 4 * MIXER_W), lambda i: (0, 0, 0)),
        ],
        out_specs=pl.BlockSpec((2, tm, 2 * MIXER_W), lambda i: (0, i, 0)),
        out_shape=jax.ShapeDtypeStruct((2, seq, 2 * MIXER_W), F32),
        compiler_params=_cparams(("arbitrary",)),
        name="hy_kspec",
    )(t_e, t_o, ab)


def _hy_fwd_kernel(te_ref, to_ref, u_ref, k_ref, o_ref):
    hb = te_ref.shape[0] // 2
    reps = u_ref.shape[2] // MIXER_W
    ep = jnp.dot(te_ref[...], u_ref[0], preferred_element_type=F32)
    op = jnp.dot(to_ref[...], u_ref[1], preferred_element_type=F32)
    ec, es, oc, os_ = ep[0:hb, :], ep[hb:, :], op[0:hb, :], op[hb:, :]
    tile = lambda v: jnp.concatenate([v] * reps, axis=-1)
    p, q, pm, qm = ec + oc, es + os_, ec - oc, os_ - es
    kr, ki = tile(k_ref[0, 0:hb, :]), tile(k_ref[0, hb:, :])
    krm, kim = tile(k_ref[1, 0:hb, :]), tile(k_ref[1, hb:, :])
    yr, yn = p * kr + q * ki, q * kr - p * ki
    yrm, ynm = pm * krm + qm * kim, qm * krm - pm * kim
    o_ref[0, 0:hb, :] = (yr + yrm).astype(BF16)
    o_ref[0, hb:, :] = (yn - ynm).astype(BF16)
    o_ref[1, 0:hb, :] = (yr - yrm).astype(BF16)
    o_ref[1, hb:, :] = (yn + ynm).astype(BF16)


def _hy_fwd(t_e, t_o, u, kspec, order, tm, tn):
    seq, half = t_e.shape
    ncol = u.shape[2]
    return pl.pallas_call(
        _hy_fwd_kernel,
        grid=(ncol // tn, seq // tm),
        in_specs=[
            pl.BlockSpec((tm, half), lambda j, i: (i, 0)),
            pl.BlockSpec((tm, half), lambda j, i: (i, 0)),
            pl.BlockSpec((2, half, tn), lambda j, i: (0, 0, j)),
            pl.BlockSpec((2, tm, MIXER_W), lambda j, i: (0, i, order)),
        ],
        out_specs=pl.BlockSpec((2, tm, tn), lambda j, i: (0, i, j)),
        out_shape=jax.ShapeDtypeStruct((2, seq, ncol), BF16),
        compiler_params=_cparams(("arbitrary", "arbitrary")),
        name="hy_fwd",
    )(t_e, t_o, u, kspec)


def _hy_inv_kernel(ie_ref, io_ref, y_ref, u_ref, gate_ref, skip_ref, *rest, final):
    reps = u_ref.shape[2] // MIXER_W
    skip = jnp.concatenate([skip_ref[...]] * reps, axis=-1)
    for par, inv_ref in enumerate((ie_ref, io_ref)):
        conv = jnp.dot(inv_ref[...], y_ref[par], preferred_element_type=F32)
        out = gate_ref[par] * (conv + u_ref[par].astype(F32) * skip)
        if final:
            g_ref, o_ref = rest
            for r in range(reps):
                o_ref[r, par] = _rms_gain(out[:, r * MIXER_W:(r + 1) * MIXER_W], g_ref[...]).astype(BF16)
        else:
            (o_ref,) = rest
            o_ref[par] = out.astype(BF16)


def _hy_inv(inv_e, inv_o, yspec, u, conv, part, skip, gain, final, tm, tn):
    half, seq = inv_e.shape
    ncol = u.shape[2]
    reps = tn // MIXER_W
    in_specs = [
        pl.BlockSpec((tm, seq), lambda j, i: (i, 0)),
        pl.BlockSpec((tm, seq), lambda j, i: (i, 0)),
        pl.BlockSpec((2, seq, tn), lambda j, i: (0, 0, j)),
        pl.BlockSpec((2, tm, tn), lambda j, i: (0, i, j)),
        pl.BlockSpec((None, 2, tm, tn), lambda j, i: (part, 0, i, j)),
        pl.BlockSpec((1, MIXER_W), lambda j, i: (0, 0)),
    ]
    args = [inv_e, inv_o, yspec, u, conv, skip]
    if final:
        in_specs.append(pl.BlockSpec((1, MIXER_W), lambda j, i: (0, 0)))
        args.append(gain)
        out_spec = pl.BlockSpec((reps, 2, tm, MIXER_W), lambda j, i: (j, 0, i, 0))
        out_shape = jax.ShapeDtypeStruct((ncol // MIXER_W, 2, half, MIXER_W), BF16)
    else:
        out_spec = pl.BlockSpec((2, tm, tn), lambda j, i: (0, i, j))
        out_shape = jax.ShapeDtypeStruct((2, half, ncol), BF16)
    return pl.pallas_call(
        functools.partial(_hy_inv_kernel, final=final),
        grid=(ncol // tn, half // tm),
        in_specs=in_specs,
        out_specs=out_spec,
        out_shape=out_shape,
        compiler_params=_cparams(("arbitrary", "arbitrary")),
        name="hy_inv",
    )(*args)


def _hy_tables(seq, tm):
    half = seq // 2
    k = jnp.arange(half, dtype=jnp.int32)
    hb = tm // 2
    nb = half // hb
    tabs = []
    for par in range(2):
        n = 2 * k + par
        m = ((2 * k + 1)[:, None] * n[None, :]) % (4 * seq)
        ang = m.astype(F32) * (2.0 * math.pi / (4 * seq))
        cs = jnp.stack([jnp.cos(ang).reshape(nb, hb, half), jnp.sin(ang).reshape(nb, hb, half)], axis=1)
        tabs.append(cs.reshape(seq, half).astype(BF16))
    return tabs[0], tabs[1]


def _hy_features(seq):
    t = jnp.linspace(0.0, 1.0, seq, dtype=F32)[:, None]
    bands = (HY_EMB - 1) // 2
    freqs = jnp.linspace(1e-4, bands - 1, bands, dtype=F32)[None, :]
    ang = (2.0 * math.pi / seq) * jnp.arange(seq, dtype=F32)[:, None] * freqs
    feat = jnp.concatenate([t, jnp.cos(ang), -jnp.sin(ang)], -1)
    feat = jnp.pad(feat, ((0, 0), (0, HY_EMB_PAD - HY_EMB)))
    deltas = jnp.abs(jnp.linspace(math.log(HY_TARGET) / HY_SLOW, math.log(HY_TARGET) / HY_FAST,
                                  MIXER_W, dtype=F32))
    decay = jnp.exp(-t * deltas[None, :])
    split = lambda a: jnp.stack([a[0::2], a[1::2]], axis=0)
    return split(feat), split(decay)


def _hyena_mix(z, lp, gain, n_seq, seq, row0):
    half = seq // 2
    tm_f = min(512, seq)
    tm_i = min(256, half)
    ncol = n_seq * MIXER_W
    tn = min(1024, ncol)
    t_e, t_o = _hy_tables(seq, tm_f)
    inv_e, inv_o = t_e.T, t_o.T
    feat, decay = _hy_features(seq)
    ab = _hy_filters(feat, decay, lp["hy_w1p"], lp["hy_b1"], lp["hy_f1"], lp["hy_w2"], lp["hy_b2"],
                     lp["hy_f2"], lp["hy_w3"])
    kspec = _hy_kspec(t_e, t_o, ab, tm_f)
    conv, u = _hy_conv(z, lp["hy_conv_w"], lp["hy_conv_b"], n_seq, seq, row0)
    y0 = _hy_fwd(t_e, t_o, u, kspec, 0, tm_f, tn)
    u1 = _hy_inv(inv_e, inv_o, y0, u, conv, 0, lp["hy_skip"][0:1], None, False, tm_i, tn)
    y1 = _hy_fwd(t_e, t_o, u1, kspec, 1, tm_f, tn)
    out = _hy_inv(inv_e, inv_o, y1, u1, conv, 1, lp["hy_skip"][1:2], gain, True, tm_i, tn)
    return jnp.transpose(out, (0, 2, 1, 3)).reshape(n_seq * seq, MIXER_W)


def _gelu(x):
    c = math.sqrt(2.0 / math.pi)
    return 0.5 * x * (1.0 + jnp.tanh(c * (x + 0.044715 * (x * x * x))))


def _gmlp_kernel(u_ref, v_ref, lg_ref, lb_ref, ws_ref, bs_ref, g_ref, o_ref):
    tl = u_ref.shape[0]
    gw = MIXER_W // GMLP_GROUPS
    lane_grp = lax.broadcasted_iota(jnp.int32, (GMLP_CHUNK, MIXER_W), 1) // gw
    for c in range(tl // GMLP_CHUNK):
        rows = slice(c * GMLP_CHUNK, (c + 1) * GMLP_CHUNK)
        v = _ln(_gelu(v_ref[rows, :])) * lg_ref[...] + lb_ref[...]
        vb = v.astype(BF16)
        stacked = jnp.concatenate(
            [jnp.where(lane_grp == g, vb, jnp.zeros_like(vb)) for g in range(GMLP_GROUPS)], axis=0)
        s = jnp.dot(ws_ref[...], stacked, preferred_element_type=F32) + bs_ref[...]
        y = _gelu(u_ref[rows, :]) * s
        o_ref[rows, :] = _rms_gain(y, g_ref[...]).astype(BF16)


def _gmlp_mix(z, ln_g, ln_b, ws_cat, bs_mat, gain, n_rows):
    tl = 512
    full = lambda shape: pl.BlockSpec(shape, lambda i: tuple(0 for _ in shape))
    return pl.pallas_call(
        _gmlp_kernel,
        grid=(n_rows // tl,),
        in_specs=[
            pl.BlockSpec((tl, MIXER_W), lambda i: (i, COL_GU)),
            pl.BlockSpec((tl, MIXER_W), lambda i: (i, COL_GV)),
            full((1, MIXER_W)), full((1, MIXER_W)),
            full((GMLP_CHUNK, GMLP_GROUPS * GMLP_CHUNK)),
            full((GMLP_CHUNK, MIXER_W)),
            full((1, MIXER_W)),
        ],
        out_specs=pl.BlockSpec((tl, MIXER_W), lambda i: (i, 0)),
        out_shape=jax.ShapeDtypeStruct((n_rows, MIXER_W), BF16),
        compiler_params=_cparams(("arbitrary",)),
        name="gmlp_mix",
    )(z, z, ln_g, ln_b, ws_cat, bs_mat, gain)


def _merge_kernel(yp_ref, ya_ref, yh_ref, yg_ref, w_ref, x_ref, g1_ref, lg_ref, lb_ref, sh_ref, sc_ref,
                  wrh_ref, wrl_ref, rb_ref, xo_ref, h_ref, idx_ref, gate_ref, y_a, y_b, *, alpha):
    s = pl.program_id(0)

    @pl.when(s == 0)
    def _():
        y_b[...] = jnp.zeros_like(y_b)

    def step(y_new, y_old):
        ymix = jnp.concatenate([yp_ref[...], ya_ref[...], yh_ref[...], yg_ref[...]], axis=-1)
        y_new[...] = jnp.dot(ymix, w_ref[...], preferred_element_type=F32)

        x1 = _ln(alpha * x_ref[...] + g1_ref[...] * y_old[...]) * lg_ref[...] + lb_ref[...]
        xo_ref[...] = x1
        h = _ln(x1) * (1.0 + sc_ref[...]) + sh_ref[...]
        h_ref[...] = h
        h_hi = h.astype(BF16)
        h_lo = (h - h_hi.astype(F32)).astype(BF16)
        logits = (jnp.dot(h_hi, wrh_ref[...], preferred_element_type=F32)
                  + (jnp.dot(h_lo, wrh_ref[...], preferred_element_type=F32)
                     + jnp.dot(h_hi, wrl_ref[...], preferred_element_type=F32)))
        idx, gate = _route_rows(logits, rb_ref[...])
        idx_ref[...] = idx
        gate_ref[...] = gate

    even = lax.rem(s, 2) == 0
    pl.when(even)(functools.partial(step, y_a, y_b))
    pl.when(jnp.logical_not(even))(functools.partial(step, y_b, y_a))


def _route_rows(logits, bias):
    neg = -1e30
    lane = lax.broadcasted_iota(jnp.int32, logits.shape, 1)
    lane_f = lane.astype(F32)
    valid = lane < N_EXPERTS
    per = N_EXPERTS // N_EXPERT_GROUPS
    s = jax.nn.sigmoid(logits)
    sel = jnp.where(valid, s + bias, neg)
    sh = [sel] + [pltpu.roll(sel, LOGIT_PAD - j, axis=1) for j in range(1, per)]
    pair = None
    for a in range(per):
        for b in range(a + 1, per):
            t = sh[a] + sh[b]
            pair = t if pair is None else jnp.maximum(pair, t)
    grp = jnp.where(valid & ((lane & (per - 1)) == 0), pair, neg)

    def first_max(v):
        m = jnp.max(v, axis=-1, keepdims=True)
        return jnp.min(jnp.where(v == m, lane_f, float(LOGIT_PAD)), axis=-1, keepdims=True)

    best = first_max(grp).astype(jnp.int32)
    shift = per.bit_length() - 1
    cand = jnp.where(valid & ((lane >> shift) == (best >> shift)), sel, neg)
    i1 = first_max(cand)
    i2 = first_max(jnp.where(lane_f == i1, neg, cand))
    w1 = jnp.sum(jnp.where(lane_f == i1, s, 0.0), axis=-1, keepdims=True)
    w2 = jnp.sum(jnp.where(lane_f == i2, s, 0.0), axis=-1, keepdims=True)
    tot = w1 + w2
    idx = jnp.where(lane == 0, i1, jnp.where(lane == 1, i2, 0.0)).astype(jnp.int32)
    gate = jnp.where(lane == 0, w1 / tot, jnp.where(lane == 1, w2 / tot, 0.0))
    return idx, gate


def _merge(ys, w_out_bf, xa, mod3, ln_g, ln_b, wr_hi, wr_lo, rb_pad, n_rows, n_batch, seq, alpha):
    d = xa.shape[1]
    tm = 512
    nt = n_rows // tm

    def cur(i):
        return jnp.minimum(i, nt - 1)

    def prv(i):
        return jnp.maximum(i - 1, 0)

    def mrow(i):
        return jnp.minimum((prv(i) * tm) // seq, n_batch)

    ymix = pl.BlockSpec((tm, MIXER_W), lambda i: (cur(i), 0))
    rowvec = pl.BlockSpec((1, d), lambda i: (0, 0))
    return pl.pallas_call(
        functools.partial(_merge_kernel, alpha=alpha),
        grid=(nt + 1,),
        in_specs=[
            ymix, ymix, ymix, ymix,
            pl.BlockSpec((d, d), lambda i: (0, 0)),
            pl.BlockSpec((tm, d), lambda i: (prv(i), 0)),
            pl.BlockSpec((None, 1, d), lambda i: (mrow(i), 0, 2)),
            rowvec, rowvec,
            pl.BlockSpec((None, 1, d), lambda i: (mrow(i), 0, 3)),
            pl.BlockSpec((None, 1, d), lambda i: (mrow(i), 0, 4)),
            pl.BlockSpec((d, LOGIT_PAD), lambda i: (0, 0)),
            pl.BlockSpec((d, LOGIT_PAD), lambda i: (0, 0)),
            pl.BlockSpec((1, LOGIT_PAD), lambda i: (0, 0)),
        ],
        out_specs=[
            pl.BlockSpec((tm, d), lambda i: (prv(i), 0)),
            pl.BlockSpec((tm, d), lambda i: (prv(i), 0)),
            pl.BlockSpec((tm, LOGIT_PAD), lambda i: (prv(i), 0)),
            pl.BlockSpec((tm, LOGIT_PAD), lambda i: (prv(i), 0)),
        ],
        out_shape=[
            jax.ShapeDtypeStruct((n_rows, d), F32),
            jax.ShapeDtypeStruct((n_rows, d), F32),
            jax.ShapeDtypeStruct((n_rows, LOGIT_PAD), jnp.int32),
            jax.ShapeDtypeStruct((n_rows, LOGIT_PAD), F32),
        ],
        scratch_shapes=[pltpu.VMEM((tm, d), F32), pltpu.VMEM((tm, d), F32)],
        compiler_params=_cparams(("arbitrary",)),
        name="merge",
    )(*ys, w_out_bf, xa, mod3, ln_g, ln_b, mod3, mod3, wr_hi, wr_lo, rb_pad)


def _moe_kernel(be_ref, nu_ref, tok_ref, h_hbm, wg_ref, wu_ref, wd_ref, o_ref, xbuf, sem, wg_s, wu_s, wd_s):
    i = pl.program_id(0)
    n_used = nu_ref[0]
    ring = xbuf.shape[0]
    slot = lax.rem(i, ring)

    def row_copy(blk, r, sl):
        tok = tok_ref[blk * MOE_BLOCK + r]
        return pltpu.make_async_copy(h_hbm.at[pl.ds(tok, 1)], xbuf.at[sl, pl.ds(r, 1)], sem.at[sl])

    def gather_start(blk, sl):
        def body(r, carry):
            row_copy(blk, r, sl).start()
            return carry

        lax.fori_loop(0, MOE_BLOCK, body, 0, unroll=8)

    def gather_wait(sl):
        pltpu.make_async_copy(h_hbm.at[pl.ds(0, MOE_BLOCK)], xbuf.at[sl], sem.at[sl]).wait()

    @pl.when(i == 0)
    def _():
        gather_start(0, 0)

        @pl.when(n_used > 1)
        def _():
            gather_start(1, 1)

    active = i < n_used
    new_expert = jnp.logical_or(i == 0, be_ref[i] != be_ref[jnp.maximum(i - 1, 0)])

    @pl.when(jnp.logical_and(active, new_expert))
    def _():
        for src, dst in ((wg_ref, wg_s), (wu_ref, wu_s), (wd_ref, wd_s)):
            def cast(r0, src=src, dst=dst):
                dst[pl.ds(r0, 256), :] = src[pl.ds(r0, 256), :].astype(BF16)

            _row_chunks(src.shape[0], 256, cast)

    def ffn_block(prefetch_next):
        gather_wait(slot)
        x = xbuf[slot].astype(BF16)
        if prefetch_next:
            nxt = lax.rem(i + 2, ring)
            for r in range(MOE_BLOCK):
                row_copy(i + 2, r, nxt).start()
        g = jnp.dot(x, wg_s[...], preferred_element_type=F32)
        u = jnp.dot(x, wu_s[...], preferred_element_type=F32)
        a = (g * jax.nn.sigmoid(g) * u).astype(BF16)
        o_ref[...] = jnp.dot(a, wd_s[...], preferred_element_type=F32).astype(o_ref.dtype)

    has_next = i + 2 < n_used
    pl.when(jnp.logical_and(active, has_next))(functools.partial(ffn_block, True))
    pl.when(jnp.logical_and(active, jnp.logical_not(has_next)))(functools.partial(ffn_block, False))

    @pl.when(jnp.logical_not(active))
    def _():
        o_ref[...] = jnp.zeros_like(o_ref)


def _moe_experts(h, slot_tok, block_e, n_used, wg, wu, wd, layer):
    d = h.shape[1]
    n_slots = slot_tok.shape[0]
    de = wg.shape[3]
    n_blocks = n_slots // MOE_BLOCK
    grid_spec = pltpu.PrefetchScalarGridSpec(
        num_scalar_prefetch=3,
        grid=(n_blocks,),
        in_specs=[
            pl.BlockSpec(memory_space=pl.ANY),
            pl.BlockSpec((None, None, d, de), lambda i, be, nu, tok: (layer, be[i], 0, 0)),
            pl.BlockSpec((None, None, d, de), lambda i, be, nu, tok: (layer, be[i], 0, 0)),
            pl.BlockSpec((None, None, de, d), lambda i, be, nu, tok: (layer, be[i], 0, 0)),
        ],
        out_specs=pl.BlockSpec((MOE_BLOCK, d), lambda i, be, nu, tok: (i, 0)),
        scratch_shapes=[
            pltpu.VMEM((3, MOE_BLOCK, d), F32),
            pltpu.SemaphoreType.DMA((3,)),
            pltpu.VMEM((d, de), BF16), pltpu.VMEM((d, de), BF16), pltpu.VMEM((de, d), BF16),
        ],
    )
    return pl.pallas_call(
        _moe_kernel,
        grid_spec=grid_spec,
        out_shape=jax.ShapeDtypeStruct((n_slots, d), BF16),
        compiler_params=_cparams(("arbitrary",)),
        name="moe_experts",
    )(block_e, n_used, slot_tok, h, wg, wu, wd)


def _assignment_ranks(flat_e):
    a = flat_e.shape[0]
    blk = 128
    nb = a // blk
    onehot = (flat_e[:, None] == jnp.arange(N_EXPERTS)[None, :]).astype(BF16).reshape(nb, blk, N_EXPERTS)
    tri = (jnp.arange(blk)[:, None] >= jnp.arange(blk)[None, :]).astype(BF16)
    intra = jnp.einsum("ij,bjk->bik", tri, onehot, preferred_element_type=F32)
    bsum = intra[:, -1, :]
    before = (jnp.arange(nb)[:, None] > jnp.arange(nb)[None, :]).astype(BF16)
    offs = jnp.dot(before, bsum.astype(BF16), preferred_element_type=F32)
    csum = intra + offs[:, None, :]
    rank = jnp.sum(csum * onehot.astype(F32), axis=-1).reshape(a) - 1.0
    counts = offs[-1] + bsum[-1]
    return rank.astype(jnp.int32), counts.astype(jnp.int32)


def _dispatch_plan(idx):
    t = idx.shape[0]
    a = t * TOP_K
    flat_e = idx.reshape(a)
    rank, counts = _assignment_ranks(flat_e)
    padded = (counts + MOE_BLOCK - 1) // MOE_BLOCK * MOE_BLOCK
    pad_end = jnp.cumsum(padded)
    pad_start = pad_end - padded
    dest = pad_start[flat_e] + rank
    n_blocks = (a + N_EXPERTS * (MOE_BLOCK - 1) + MOE_BLOCK - 1) // MOE_BLOCK
    n_slots = n_blocks * MOE_BLOCK
    flat_tok = jnp.repeat(jnp.arange(t, dtype=jnp.int32), TOP_K)
    slot_tok = jnp.zeros((n_slots,), jnp.int32).at[dest].set(flat_tok)
    blk_start = jnp.arange(n_blocks, dtype=jnp.int32) * MOE_BLOCK
    block_e = jnp.minimum(
        jnp.sum((pad_end[None, :] <= blk_start[:, None]).astype(jnp.int32), axis=1), N_EXPERTS - 1)
    n_used = (pad_end[-1] // MOE_BLOCK).astype(jnp.int32).reshape(1)
    return slot_tok, block_e.astype(jnp.int32), n_used, dest.reshape(t, TOP_K)


def _final_kernel(x_ref, ya_ref, yb_ref, gate_ref, g2_ref, lg_ref, lb_ref, o_ref, *, alpha):
    f = ya_ref[...].astype(F32) * gate_ref[:, 0:1] + yb_ref[...].astype(F32) * gate_ref[:, 1:2]
    o_ref[...] = _ln(alpha * x_ref[...] + g2_ref[...] * f) * lg_ref[...] + lb_ref[...]


def _final_norm(x1, ya, yb, gate, mod3, ln_g, ln_b, n_batch, seq, alpha):
    n_rows, d = x1.shape
    tm = 512

    def mrow(i):
        return jnp.minimum((i * tm) // seq, n_batch)

    tile = pl.BlockSpec((tm, d), lambda i: (i, 0))
    rowvec = pl.BlockSpec((1, d), lambda i: (0, 0))
    return pl.pallas_call(
        functools.partial(_final_kernel, alpha=alpha),
        grid=(n_rows // tm,),
        in_specs=[tile, tile, tile, pl.BlockSpec((tm, LOGIT_PAD), lambda i: (i, 0)),
                  pl.BlockSpec((None, 1, d), lambda i: (mrow(i), 0, 5)), rowvec, rowvec],
        out_specs=tile,
        out_shape=jax.ShapeDtypeStruct((n_rows, d), F32),
        compiler_params=_cparams(("arbitrary",)),
        name="final_norm",
    )(x1, ya, yb, gate, mod3, ln_g, ln_b)


def _permute_in_cols(w):
    pool, q, k, v, hy, gm = jnp.split(w, (512, 1024, 1152, 1280, 2816), axis=-1)
    return jnp.concatenate([hy, pool, q, gm, k, v], axis=-1)


def kernel(x, c, ctx, c_ctx, w_ada, b_ada, w_in, w_out, mix_norm_g, pool_w, pool_scale, attn_sink,
           hy_conv_w, hy_conv_b, hy_w1, hy_b1, hy_f1, hy_w2, hy_b2, hy_f2, hy_w3, hy_skip,
           gm_ln_g, gm_ln_b, gm_ws, gm_bs, ln1_g, ln1_b, ln2_g, ln2_b, w_router, router_bias,
           w_gate, w_up, w_down):
    n_batch, seq, d = x.shape
    ctx_len = ctx.shape[1]
    depth = w_in.shape[0]
    grid_w = 64
    alpha = (2 * depth) ** 0.25
    n_lat = n_batch * seq
    n_ctx = n_batch * ctx_len

    xa = jnp.concatenate([x.reshape(n_lat, d), ctx.reshape(n_ctx, d)], axis=0)
    cvec = jnp.concatenate([c, c_ctx[None, :], jnp.zeros((8 - n_batch - 1, d), F32)], axis=0)
    mods = _ada_mods(cvec, w_ada, b_ada)
    cos_t, sin_t = _rope_tables(seq, grid_w)
    wr_pad = jnp.pad(w_router, ((0, 0), (0, LOGIT_PAD - N_EXPERTS)))
    wr_hi = wr_pad.astype(BF16)
    wr_lo = (wr_pad - wr_hi.astype(F32)).astype(BF16)
    rb_pad = jnp.pad(router_bias.astype(F32), (0, LOGIT_PAD - N_EXPERTS)).reshape(1, LOGIT_PAD)
    row = lambda v: v.reshape(1, -1)

    for layer in range(depth):
        last = layer == depth - 1
        mod3 = mods[layer].reshape(8, 1, 6 * d)
        gains = mix_norm_g[layer].reshape(4, 1, MIXER_W)
        lp = {
            "hy_conv_w": hy_conv_w[layer], "hy_conv_b": row(hy_conv_b[layer]),
            "hy_w1p": jnp.pad(hy_w1[layer], ((0, HY_EMB_PAD - HY_EMB), (0, 0))),
            "hy_b1": row(hy_b1[layer]), "hy_f1": row(hy_f1[layer]),
            "hy_w2": hy_w2[layer], "hy_b2": row(hy_b2[layer]), "hy_f2": row(hy_f2[layer]),
            "hy_w3": hy_w3[layer], "hy_skip": hy_skip[layer],
        }
        z = _in_proj(xa, mod3, _permute_in_cols(w_in[layer]).astype(BF16), n_batch, seq)

        pool_bf = pool_w[layer].astype(BF16)
        ps = row(pool_scale[layer])
        ws_cat = jnp.transpose(gm_ws[layer], (1, 0, 2)).reshape(GMLP_CHUNK, GMLP_GROUPS * GMLP_CHUNK)
        bs_mat = jnp.repeat(gm_bs[layer].T, MIXER_W // GMLP_GROUPS, axis=1)
        n_rows = n_lat if last else n_lat + n_ctx

        y_pool = _pool_mix(z, pool_bf, ps, gains[0], n_batch, seq, 0)
        y_attn = _attn_lat(z, attn_sink[layer], cos_t, sin_t, gains[1], n_batch, seq, ctx_len)
        y_hy = _hyena_mix(z, lp, gains[2], n_batch, seq, 0)
        y_gm = _gmlp_mix(z, row(gm_ln_g[layer]), row(gm_ln_b[layer]), ws_cat.astype(BF16), bs_mat,
                         gains[3], n_rows)
        if not last:
            y_pool = jnp.concatenate(
                [y_pool, _pool_mix(z, pool_bf, ps, gains[0], n_batch, ctx_len, n_lat // ctx_len)], axis=0)
            y_attn = jnp.concatenate(
                [y_attn, _attn_ctx(z, attn_sink[layer], gains[1], n_batch, seq, ctx_len)], axis=0)
            y_hy = jnp.concatenate([y_hy, _hyena_mix(z, lp, gains[2], n_batch, ctx_len, n_lat)], axis=0)

        x1, h2, idx, gate = _merge((y_pool, y_attn, y_hy, y_gm), w_out[layer].astype(BF16), xa, mod3,
                                   row(ln1_g[layer]), row(ln1_b[layer]), wr_hi, wr_lo, rb_pad,
                                   n_rows, n_batch, seq, alpha)

        slot_tok, block_e, n_used, dest = _dispatch_plan(idx[:, :TOP_K])
        ys = _moe_experts(h2, slot_tok, block_e, n_used, w_gate, w_up, w_down, layer)
        xa = _final_norm(x1, ys[dest[:, 0]], ys[dest[:, 1]], gate, mod3, row(ln2_g[layer]), row(ln2_b[layer]),
                         n_batch, seq, alpha)
    return xa.reshape(n_batch, seq, d)
```

```python
import functools
import math

import jax
import jax.numpy as jnp
from jax import lax
from jax.experimental import pallas as pl
from jax.experimental.pallas import tpu as pltpu

F32 = jnp.float32
BF16 = jnp.bfloat16
HI = lax.Precision.HIGHEST

LN_EPS = 1e-6
MIXER_W = 512
HEAD_DIM = 64
N_HEADS = 8
N_KV_HEADS = 2
KV_GROUP = 4
KV_W = 128
WINDOW = 128
ROPE_THETA = 10000.0
POOL_WINDOWS = (2, 4, 8, 16)
POOL_GROUP = 128
POOL_HALO = 8
GMLP_CHUNK = 128
GMLP_GROUPS = 8
HY_EMB = 33
HY_EMB_PAD = 128
HY_TARGET = 1e-2
HY_FAST = 0.3
HY_SLOW = 1.5
N_EXPERTS = 32
N_EXPERT_GROUPS = 8
TOP_K = 2
D_EXPERT = 768
MOE_BLOCK = 256
LOGIT_PAD = 128

COL_HY = 0
COL_POOL = 3
COL_Q = 4
COL_GU = 5
COL_GV = 6
COL_K = 28
COL_V = 29
IN_W = 3840

VMEM_LIMIT = 56 * 1024 * 1024


def _cparams(sem):
    return pltpu.CompilerParams(dimension_semantics=sem, vmem_limit_bytes=VMEM_LIMIT)


def _ln(x):
    mu = jnp.mean(x, axis=-1, keepdims=True)
    xc = x - mu
    var = jnp.mean(xc * xc, axis=-1, keepdims=True)
    return xc * lax.rsqrt(var + LN_EPS)


def _rms_gain(y, g):
    return y * lax.rsqrt(jnp.mean(y * y, axis=-1, keepdims=True) + LN_EPS) * g


def _row_chunks(n_rows, chunk, body):
    n = n_rows // chunk
    if n == 1:
        body(0)
        return

    def step(i, carry):
        body(pl.multiple_of(i * chunk, chunk))
        return carry

    lax.fori_loop(0, n, step, 0)


def _ada_kernel(c_ref, w_ref, b_ref, o_ref):
    c = c_ref[...]
    s = c * jax.nn.sigmoid(c)
    o_ref[...] = jnp.dot(s, w_ref[...], precision=HI, preferred_element_type=F32) + b_ref[...]


def _ada_mods(cvec, w_ada, b_ada):
    depth, d, n = w_ada.shape
    tn = 1024
    return pl.pallas_call(
        _ada_kernel,
        grid=(depth, n // tn),
        in_specs=[
            pl.BlockSpec((8, d), lambda l, j: (0, 0)),
            pl.BlockSpec((None, d, tn), lambda l, j: (l, 0, j)),
            pl.BlockSpec((None, 1, tn), lambda l, j: (l, 0, j)),
        ],
        out_specs=pl.BlockSpec((None, 8, tn), lambda l, j: (l, 0, j)),
        out_shape=jax.ShapeDtypeStruct((depth, 8, n), F32),
        compiler_params=_cparams(("arbitrary", "arbitrary")),
        name="ada_mods",
    )(cvec, w_ada, b_ada.reshape(depth, 1, n))


def _inproj_kernel(x_ref, sh_ref, sc_ref, w_ref, o_ref, h_scr):
    @pl.when(pl.program_id(1) == 0)
    def _():
        sh = sh_ref[...]
        sc1 = 1.0 + sc_ref[...]

        def body(r0):
            x = x_ref[pl.ds(r0, 256), :]
            h_scr[pl.ds(r0, 256), :] = (_ln(x) * sc1 + sh).astype(BF16)

        _row_chunks(x_ref.shape[0], 256, body)

    o_ref[...] = jnp.dot(h_scr[...], w_ref[...], preferred_element_type=F32)


def _in_proj(xa, mod3, w_bf, n_batch, seq):
    t, d = xa.shape
    n = w_bf.shape[1]
    tn = 768
    tm = next(m for m in (1024, 512, 256) if t % m == 0 and seq % m == 0)

    def mrow(i):
        return jnp.minimum((i * tm) // seq, n_batch)

    return pl.pallas_call(
        _inproj_kernel,
        grid=(t // tm, n // tn),
        in_specs=[
            pl.BlockSpec((tm, d), lambda i, j: (i, 0)),
            pl.BlockSpec((None, 1, d), lambda i, j: (mrow(i), 0, 0)),
            pl.BlockSpec((None, 1, d), lambda i, j: (mrow(i), 0, 1)),
            pl.BlockSpec((d, tn), lambda i, j: (0, j)),
        ],
        out_specs=pl.BlockSpec((tm, tn), lambda i, j: (i, j)),
        out_shape=jax.ShapeDtypeStruct((t, n), F32),
        scratch_shapes=[pltpu.VMEM((tm, d), BF16)],
        compiler_params=_cparams(("arbitrary", "arbitrary")),
        name="in_proj",
    )(xa, mod3, mod3, w_bf)


def _pool_kernel(z_ref, w_ref, ps_ref, g_ref, o_ref, buf):
    seq = z_ref.shape[0]
    rc = min(256, seq)
    zeros = jnp.zeros((POOL_HALO, MIXER_W), F32)
    buf[0:POOL_HALO, :] = zeros
    buf[seq + POOL_HALO:seq + 2 * POOL_HALO, :] = zeros

    def fill(r0):
        buf[pl.ds(r0 + POOL_HALO, rc), :] = z_ref[pl.ds(r0, rc), :]

    _row_chunks(seq, rc, fill)

    def body(r0):
        win = buf[pl.ds(r0, rc + 2 * POOL_HALO), :]
        t = r0 + lax.broadcasted_iota(jnp.int32, (rc, 1), 0)
        outs = []
        for gi, w in enumerate(POOL_WINDOWS):
            lanes = slice(gi * POOL_GROUP, (gi + 1) * POOL_GROUP)
            acc = None
            for j in range(-w // 2, w // 2):
                piece = win[POOL_HALO + j:POOL_HALO + j + rc, lanes]
                acc = piece if acc is None else acc + piece
            cnt = (jnp.minimum(t + w // 2, seq) - jnp.maximum(t - w // 2, 0)).astype(F32)
            dlt = acc / cnt - win[POOL_HALO:POOL_HALO + rc, lanes]
            outs.append(jnp.dot(dlt.astype(BF16), w_ref[gi], preferred_element_type=F32))
        y = jnp.concatenate(outs, axis=-1) * ps_ref[...]
        o_ref[pl.ds(r0, rc), :] = _rms_gain(y, g_ref[...]).astype(BF16)

    _row_chunks(seq, rc, body)


def _pool_mix(z, pool_w_bf, pool_scale, gain, n_seq, seq, row_blk0):
    return pl.pallas_call(
        _pool_kernel,
        grid=(n_seq,),
        in_specs=[
            pl.BlockSpec((seq, MIXER_W), lambda s: (row_blk0 + s, COL_POOL)),
            pl.BlockSpec((4, POOL_GROUP, POOL_GROUP), lambda s: (0, 0, 0)),
            pl.BlockSpec((1, MIXER_W), lambda s: (0, 0)),
            pl.BlockSpec((1, MIXER_W), lambda s: (0, 0)),
        ],
        out_specs=pl.BlockSpec((seq, MIXER_W), lambda s: (s, 0)),
        out_shape=jax.ShapeDtypeStruct((n_seq * seq, MIXER_W), BF16),
        scratch_shapes=[pltpu.VMEM((seq + 2 * POOL_HALO, MIXER_W), F32)],
        compiler_params=_cparams(("arbitrary",)),
        name="pool_mix",
    )(z, pool_w_bf, pool_scale, gain)


def _rope(x, cos, sin_signed):
    lane = lax.broadcasted_iota(jnp.int32, x.shape, 1)
    partner = jnp.where((lane & 31) < 16, pltpu.roll(x, 112, axis=1), pltpu.roll(x, 16, axis=1))
    return x * cos + partner * sin_signed


def _attend(qs, sink_col, parts):
    dn = (((1,), (1,)), ((), ()))
    scores = []
    for kk, _, mask in parts:
        s = lax.dot_general(qs, kk, dn, preferred_element_type=F32)
        if mask is not None:
            s = s + mask
        scores.append(s)
    m = sink_col
    for s in scores:
        m = jnp.maximum(m, jnp.max(s, axis=-1, keepdims=True))
    den = jnp.exp(sink_col - m)
    out = None
    for s, (_, vv, _) in zip(scores, parts):
        p = jnp.exp(s - m)
        den = den + jnp.sum(p, axis=-1, keepdims=True)
        o = jnp.dot(p.astype(BF16), vv, preferred_element_type=F32)
        out = o if out is None else out + o
    return out / den


def _heads_attend(q_all, sink_ref, tq, make_parts):
    cols = [None] * N_HEADS
    for kh in range(N_KV_HEADS):
        heads = [kh * KV_GROUP + g for g in range(KV_GROUP)]
        qs = jnp.concatenate([q_all[:, h * HEAD_DIM:(h + 1) * HEAD_DIM] for h in heads], axis=0)
        sink_col = jnp.concatenate([jnp.full((tq, 1), sink_ref[h], F32) for h in heads], axis=0)
        o = _attend(qs.astype(BF16), sink_col, make_parts(kh))
        for g, h in enumerate(heads):
            cols[h] = o[g * tq:(g + 1) * tq, :]
    return jnp.concatenate(cols, axis=-1)


def _attn_lat_kernel(sink_ref, q_ref, k_ref, v_ref, kc_ref, vc_ref, cos_ref, sin_ref, g_ref, o_ref):
    tq = q_ref.shape[0]
    seq = k_ref.shape[0]
    nband = tq + 2 * WINDOW
    i = pl.program_id(1)
    q0 = pl.multiple_of(i * tq, tq)
    k0 = pl.multiple_of(jnp.clip(q0 - WINDOW, 0, seq - nband), WINDOW)

    cq = cos_ref[pl.ds(q0, tq), :]
    sq = sin_ref[pl.ds(q0, tq), :]
    q = q_ref[...]
    q_all = jnp.concatenate(
        [_rope(q[:, c * 128:(c + 1) * 128], cq, sq) for c in range(MIXER_W // 128)], axis=-1)
    q_all = q_all * (HEAD_DIM ** -0.5)

    kb = _rope(k_ref[pl.ds(k0, nband), :], cos_ref[pl.ds(k0, nband), :],
               sin_ref[pl.ds(k0, nband), :]).astype(BF16)
    vb = v_ref[pl.ds(k0, nband), :].astype(BF16)
    kc = kc_ref[...].astype(BF16)
    vc = vc_ref[...].astype(BF16)

    qpos = q0 + lax.broadcasted_iota(jnp.int32, (tq, nband), 0)
    kpos = k0 + lax.broadcasted_iota(jnp.int32, (tq, nband), 1)
    band = jnp.where(jnp.abs(qpos - kpos) <= WINDOW, 0.0, -1e30)
    mask = jnp.concatenate([band] * KV_GROUP, axis=0)

    def make_parts(kh):
        hs = slice(kh * HEAD_DIM, (kh + 1) * HEAD_DIM)
        return [(kb[:, hs], vb[:, hs], mask), (kc[:, hs], vc[:, hs], None)]

    y = _heads_attend(q_all, sink_ref, tq, make_parts)
    o_ref[...] = _rms_gain(y, g_ref[...]).astype(BF16)


def _attn_ctx_kernel(sink_ref, q_ref, kc_ref, vc_ref, g_ref, o_ref):
    tq = q_ref.shape[0]
    q_all = q_ref[...] * (HEAD_DIM ** -0.5)
    kc = kc_ref[...].astype(BF16)
    vc = vc_ref[...].astype(BF16)

    def make_parts(kh):
        hs = slice(kh * HEAD_DIM, (kh + 1) * HEAD_DIM)
        return [(kc[:, hs], vc[:, hs], None)]

    y = _heads_attend(q_all, sink_ref, tq, make_parts)
    o_ref[...] = _rms_gain(y, g_ref[...]).astype(BF16)


def _attn_lat(z, sink, cos_t, sin_t, gain, n_batch, seq, ctx_len):
    tq = 256
    nq = seq // tq
    ctx_blk0 = (n_batch * seq) // ctx_len
    smem = pl.BlockSpec(memory_space=pltpu.SMEM)
    return pl.pallas_call(
        _attn_lat_kernel,
        grid=(n_batch, nq),
        in_specs=[
            smem,
            pl.BlockSpec((tq, MIXER_W), lambda b, i: (b * nq + i, COL_Q)),
            pl.BlockSpec((seq, KV_W), lambda b, i: (b, COL_K)),
            pl.BlockSpec((seq, KV_W), lambda b, i: (b, COL_V)),
            pl.BlockSpec((ctx_len, KV_W), lambda b, i: (ctx_blk0 + b, COL_K)),
            pl.BlockSpec((ctx_len, KV_W), lambda b, i: (ctx_blk0 + b, COL_V)),
            pl.BlockSpec((seq, 128), lambda b, i: (0, 0)),
            pl.BlockSpec((seq, 128), lambda b, i: (0, 0)),
            pl.BlockSpec((1, MIXER_W), lambda b, i: (0, 0)),
        ],
        out_specs=pl.BlockSpec((tq, MIXER_W), lambda b, i: (b * nq + i, 0)),
        out_shape=jax.ShapeDtypeStruct((n_batch * seq, MIXER_W), BF16),
        compiler_params=_cparams(("arbitrary", "arbitrary")),
        name="attn_lat",
    )(sink, z, z, z, z, z, cos_t, sin_t, gain)


def _attn_ctx(z, sink, gain, n_batch, seq, ctx_len):
    ctx_blk0 = (n_batch * seq) // ctx_len
    smem = pl.BlockSpec(memory_space=pltpu.SMEM)
    return pl.pallas_call(
        _attn_ctx_kernel,
        grid=(n_batch,),
        in_specs=[
            smem,
            pl.BlockSpec((ctx_len, MIXER_W), lambda b: (ctx_blk0 + b, COL_Q)),
            pl.BlockSpec((ctx_len, KV_W), lambda b: (ctx_blk0 + b, COL_K)),
            pl.BlockSpec((ctx_len, KV_W), lambda b: (ctx_blk0 + b, COL_V)),
            pl.BlockSpec((1, MIXER_W), lambda b: (0, 0)),
        ],
        out_specs=pl.BlockSpec((ctx_len, MIXER_W), lambda b: (b, 0)),
        out_shape=jax.ShapeDtypeStruct((n_batch * ctx_len, MIXER_W), BF16),
        compiler_params=_cparams(("arbitrary",)),
        name="attn_ctx",
    )(sink, z, z, z, gain)


def _rope_tables(seq, grid_w):
    nf = HEAD_DIM // 4
    inv = ROPE_THETA ** (-jnp.arange(nf, dtype=F32) / nf)
    t = jnp.arange(seq)
    row = (t // grid_w).astype(F32)[:, None] * inv[None, :]
    col = (t % grid_w).astype(F32)[:, None] * inv[None, :]
    cos_h = jnp.concatenate([jnp.cos(row), jnp.cos(row), jnp.cos(col), jnp.cos(col)], axis=-1)
    sin_h = jnp.concatenate([-jnp.sin(row), jnp.sin(row), -jnp.sin(col), jnp.sin(col)], axis=-1)
    return jnp.tile(cos_h, (1, 2)), jnp.tile(sin_h, (1, 2))


def _hy_conv_kernel(*refs):
    z_refs, zp_refs, zn_refs = refs[0:3], refs[3:6], refs[6:9]
    w_ref, b_ref, g_ref, u_ref = refs[9:]
    i = pl.program_id(1)
    th = z_refs[0].shape[0] // 2
    has_prev = (i > 0).astype(F32)
    has_next = (i < pl.num_programs(1) - 1).astype(F32)
    row = lax.broadcasted_iota(jnp.int32, (th, 128), 0)
    for part in range(3):
        ze = z_refs[part][pl.ds(0, th, stride=2), :]
        zo = z_refs[part][pl.ds(1, th, stride=2), :]
        prev_row = zp_refs[part][POOL_HALO - 1:POOL_HALO, :] * has_prev
        next_row = zn_refs[part][0:1, :] * has_next
        zo_m = jnp.where(row == 0, prev_row, pltpu.roll(zo, 1, axis=0))
        ze_p = jnp.where(row == th - 1, next_row, pltpu.roll(ze, th - 1, axis=0))
        w0, w1, w2, b = w_ref[part, 0:1, :], w_ref[part, 1:2, :], w_ref[part, 2:3, :], b_ref[part]
        ye = b + zo_m * w0 + ze * w1 + zo * w2
        yo = b + ze * w0 + zo * w1 + ze_p * w2
        g_ref[part, 0] = ye
        g_ref[part, 1] = yo
        if part == 2:
            u_ref[0] = ye.astype(BF16)
            u_ref[1] = yo.astype(BF16)


def _hy_conv(z, conv_w, conv_b, n_seq, seq, row0):
    tl = min(2048, seq)
    nt = seq // tl
    blk0 = row0 // tl
    hb0 = row0 // POOL_HALO
    hpt = tl // POOL_HALO
    last_halo = (row0 + n_seq * seq) // POOL_HALO - 1
    cpp = MIXER_W // 128
    ncol = n_seq * MIXER_W

    def main_spec(part):
        return pl.BlockSpec((tl, 128), lambda s, i, c: (blk0 + s * nt + i, part * cpp + c))

    def prev_spec(part):
        return pl.BlockSpec((POOL_HALO, 128),
                            lambda s, i, c: (jnp.maximum(hb0 + (s * nt + i) * hpt - 1, 0), part * cpp + c))

    def next_spec(part):
        return pl.BlockSpec((POOL_HALO, 128),
                            lambda s, i, c: (jnp.minimum(hb0 + (s * nt + i + 1) * hpt, last_halo), part * cpp + c))

    w3 = conv_w.reshape(3, 3, MIXER_W).transpose(1, 0, 2)
    b3 = conv_b.reshape(3, 1, MIXER_W)
    return pl.pallas_call(
        _hy_conv_kernel,
        grid=(n_seq, nt, cpp),
        in_specs=[main_spec(p) for p in range(3)] + [prev_spec(p) for p in range(3)]
        + [next_spec(p) for p in range(3)]
        + [pl.BlockSpec((3, 3, 128), lambda s, i, c: (0, 0, c)),
           pl.BlockSpec((3, 1, 128), lambda s, i, c: (0, 0, c))],
        out_specs=[
            pl.BlockSpec((3, 2, tl // 2, 128), lambda s, i, c: (0, 0, i, s * cpp + c)),
            pl.BlockSpec((2, tl // 2, 128), lambda s, i, c: (0, i, s * cpp + c)),
        ],
        out_shape=[jax.ShapeDtypeStruct((3, 2, seq // 2, ncol), F32),
                   jax.ShapeDtypeStruct((2, seq // 2, ncol), BF16)],
        compiler_params=_cparams(("arbitrary", "arbitrary", "arbitrary")),
        name="hy_conv",
    )(*([z] * 9), w3, b3)


def _hy_filter_kernel(feat_ref, dec_ref, w1_ref, b1_ref, f1_ref, w2_ref, b2_ref, f2_ref, w3_ref, o_ref):
    tl = feat_ref.shape[0]
    h = jnp.dot(feat_ref[...], w1_ref[...], precision=HI, preferred_element_type=F32) + b1_ref[...]
    h = jnp.sin(f1_ref[...] * h)
    h = jnp.dot(h, w2_ref[...], precision=HI, preferred_element_type=F32) + b2_ref[...]
    h = jnp.sin(f2_ref[...] * h)
    filt = jnp.dot(h, w3_ref[...], precision=HI, preferred_element_type=F32)
    dec = dec_ref[...]
    t = 2 * (pl.program_id(1) * tl + lax.broadcasted_iota(jnp.int32, (tl, 1), 0)) + pl.program_id(0)
    not_first = (t > 0).astype(F32)
    for o in range(2):
        hf = filt[:, (2 * o) * MIXER_W:(2 * o + 1) * MIXER_W] * dec
        hb = filt[:, (2 * o + 1) * MIXER_W:(2 * o + 2) * MIXER_W] * dec * not_first
        o_ref[:, o * MIXER_W:(o + 1) * MIXER_W] = (hf + hb).astype(BF16)
        o_ref[:, (2 + o) * MIXER_W:(3 + o) * MIXER_W] = (hb - hf).astype(BF16)


def _hy_filters(feat, decay, w1p, b1, f1, w2, b2, f2, w3):
    half = feat.shape[1]
    tl = min(512, half)
    hid = w2.shape[0]
    full = lambda shape: pl.BlockSpec(shape, lambda p, i: tuple(0 for _ in shape))
    return pl.pallas_call(
        _hy_filter_kernel,
        grid=(2, half // tl),
        in_specs=[
            pl.BlockSpec((None, tl, HY_EMB_PAD), lambda p, i: (p, i, 0)),
            pl.BlockSpec((None, tl, MIXER_W), lambda p, i: (p, i, 0)),
            full((HY_EMB_PAD, hid)), full((1, hid)), full((1, hid)),
            full((hid, hid)), full((1, hid)), full((1, hid)),
            full((hid, 4 * MIXER_W)),
        ],
        out_specs=pl.BlockSpec((None, tl, 4 * MIXER_W), lambda p, i: (p, i, 0)),
        out_shape=jax.ShapeDtypeStruct((2, half, 4 * MIXER_W), BF16),
        compiler_params=_cparams(("arbitrary", "arbitrary")),
        name="hy_filters",
    )(feat, decay, w1p, b1, f1, w2, b2, f2, w3)


def _hy_kspec_kernel(te_ref, to_ref, ab_ref, o_ref, *, inv_len):
    hb = te_ref.shape[0] // 2
    half = ab_ref.shape[2] // 2
    ec = jnp.dot(te_ref[0:hb, :], ab_ref[0, :, 0:half], preferred_element_type=F32)
    oc = jnp.dot(to_ref[0:hb, :], ab_ref[1, :, 0:half], preferred_element_type=F32)
    es = jnp.dot(te_ref[hb:, :], ab_ref[0, :, half:], preferred_element_type=F32)
    os_ = jnp.dot(to_ref[hb:, :], ab_ref[1, :, half:], preferred_element_type=F32)
    o_ref[0, 0:hb, :] = (ec + oc) * inv_len
    o_ref[0, hb:, :] = (es + os_) * inv_len
    o_ref[1, 0:hb, :] = (ec - oc) * inv_len
    o_ref[1, hb:, :] = (os_ - es) * inv_len


def _hy_kspec(t_e, t_o, ab, tm):
    seq, half = t_e.shape
    return pl.pallas_call(
        functools.partial(_hy_kspec_kernel, inv_len=1.0 / seq),
        grid=(seq // tm,),
        in_specs=[
            pl.BlockSpec((tm, half), lambda i: (i, 0)),
            pl.BlockSpec((tm, half), lambda i: (i, 0)),
            pl.BlockSpec((2, half, 4 * MIXER_W), lambda i: (0, 0, 0)),
        ],
        out_specs=pl.BlockSpec((2, tm, 2 * MIXER_W), lambda i: (0, i, 0)),
        out_shape=jax.ShapeDtypeStruct((2, seq, 2 * MIXER_W), F32),
        compiler_params=_cparams(("arbitrary",)),
        name="hy_kspec",
    )(t_e, t_o, ab)


def _hy_fwd_kernel(te_ref, to_ref, u_ref, k_ref, o_ref):
    hb = te_ref.shape[0] // 2
    reps = u_ref.shape[2] // MIXER_W
    ep = jnp.dot(te_ref[...], u_ref[0], preferred_element_type=F32)
    op = jnp.dot(to_ref[...], u_ref[1], preferred_element_type=F32)
    ec, es, oc, os_ = ep[0:hb, :], ep[hb:, :], op[0:hb, :], op[hb:, :]
    tile = lambda v: jnp.concatenate([v] * reps, axis=-1)
    p, q, pm, qm = ec + oc, es + os_, ec - oc, os_ - es
    kr, ki = tile(k_ref[0, 0:hb, :]), tile(k_ref[0, hb:, :])
    krm, kim = tile(k_ref[1, 0:hb, :]), tile(k_ref[1, hb:, :])
    yr, yn = p * kr + q * ki, q * kr - p * ki
    yrm, ynm = pm * krm + qm * kim, qm * krm - pm * kim
    o_ref[0, 0:hb, :] = (yr + yrm).astype(BF16)
    o_ref[0, hb:, :] = (yn - ynm).astype(BF16)
    o_ref[1, 0:hb, :] = (yr - yrm).astype(BF16)
    o_ref[1, hb:, :] = (yn + ynm).astype(BF16)


def _hy_fwd(t_e, t_o, u, kspec, order, tm, tn):
    seq, half = t_e.shape
    ncol = u.shape[2]
    return pl.pallas_call(
        _hy_fwd_kernel,
        grid=(ncol // tn, seq // tm),
        in_specs=[
            pl.BlockSpec((tm, half), lambda j, i: (i, 0)),
            pl.BlockSpec((tm, half), lambda j, i: (i, 0)),
            pl.BlockSpec((2, half, tn), lambda j, i: (0, 0, j)),
            pl.BlockSpec((2, tm, MIXER_W), lambda j, i: (0, i, order)),
        ],
        out_specs=pl.BlockSpec((2, tm, tn), lambda j, i: (0, i, j)),
        out_shape=jax.ShapeDtypeStruct((2, seq, ncol), BF16),
        compiler_params=_cparams(("arbitrary", "arbitrary")),
        name="hy_fwd",
    )(t_e, t_o, u, kspec)


def _hy_inv_kernel(ie_ref, io_ref, y_ref, u_ref, gate_ref, skip_ref, *rest, final):
    reps = u_ref.shape[2] // MIXER_W
    skip = jnp.concatenate([skip_ref[...]] * reps, axis=-1)
    for par, inv_ref in enumerate((ie_ref, io_ref)):
        conv = jnp.dot(inv_ref[...], y_ref[par], preferred_element_type=F32)
        out = gate_ref[par] * (conv + u_ref[par].astype(F32) * skip)
        if final:
            g_ref, o_ref = rest
            for r in range(reps):
                o_ref[r, par] = _rms_gain(out[:, r * MIXER_W:(r + 1) * MIXER_W], g_ref[...]).astype(BF16)
        else:
            (o_ref,) = rest
            o_ref[par] = out.astype(BF16)


def _hy_inv(inv_e, inv_o, yspec, u, conv, part, skip, gain, final, tm, tn):
    half, seq = inv_e.shape
    ncol = u.shape[2]
    reps = tn // MIXER_W
    in_specs = [
        pl.BlockSpec((tm, seq), lambda j, i: (i, 0)),
        pl.BlockSpec((tm, seq), lambda j, i: (i, 0)),
        pl.BlockSpec((2, seq, tn), lambda j, i: (0, 0, j)),
        pl.BlockSpec((2, tm, tn), lambda j, i: (0, i, j)),
        pl.BlockSpec((None, 2, tm, tn), lambda j, i: (part, 0, i, j)),
        pl.BlockSpec((1, MIXER_W), lambda j, i: (0, 0)),
    ]
    args = [inv_e, inv_o, yspec, u, conv, skip]
    if final:
        in_specs.append(pl.BlockSpec((1, MIXER_W), lambda j, i: (0, 0)))
        args.append(gain)
        out_spec = pl.BlockSpec((reps, 2, tm, MIXER_W), lambda j, i: (j, 0, i, 0))
        out_shape = jax.ShapeDtypeStruct((ncol // MIXER_W, 2, half, MIXER_W), BF16)
    else:
        out_spec = pl.BlockSpec((2, tm, tn), lambda j, i: (0, i, j))
        out_shape = jax.ShapeDtypeStruct((2, half, ncol), BF16)
    return pl.pallas_call(
        functools.partial(_hy_inv_kernel, final=final),
        grid=(ncol // tn, half // tm),
        in_specs=in_specs,
        out_specs=out_spec,
        out_shape=out_shape,
        compiler_params=_cparams(("arbitrary", "arbitrary")),
        name="hy_inv",
    )(*args)


def _hy_tables(seq, tm):
    half = seq // 2
    hb = tm // 2
    nb = half // hb
    split = 64 if half % 64 == 0 else 1
    unit = 2.0 * math.pi / (4 * seq)

    def cos_sin(mult, n):
        ang = ((mult[:, None] * n[None, :]) % (4 * seq)).astype(F32) * unit
        return jnp.cos(ang), jnp.sin(ang)

    tabs = []
    for par in range(2):
        n = 2 * jnp.arange(half, dtype=jnp.int32) + par
        ch, sh = cos_sin(2 * split * jnp.arange(half // split, dtype=jnp.int32), n)
        cl, sl = cos_sin(2 * jnp.arange(split, dtype=jnp.int32) + 1, n)
        cos_t = (ch[:, None, :] * cl[None, :, :] - sh[:, None, :] * sl[None, :, :]).reshape(nb, hb, half)
        sin_t = (sh[:, None, :] * cl[None, :, :] + ch[:, None, :] * sl[None, :, :]).reshape(nb, hb, half)
        tabs.append(jnp.stack([cos_t, sin_t], axis=1).reshape(seq, half).astype(BF16))
    return tabs[0], tabs[1]


def _hy_features(seq):
    t = jnp.linspace(0.0, 1.0, seq, dtype=F32)[:, None]
    bands = (HY_EMB - 1) // 2
    freqs = jnp.linspace(1e-4, bands - 1, bands, dtype=F32)[None, :]
    ang = (2.0 * math.pi / seq) * jnp.arange(seq, dtype=F32)[:, None] * freqs
    feat = jnp.concatenate([t, jnp.cos(ang), -jnp.sin(ang)], -1)
    feat = jnp.pad(feat, ((0, 0), (0, HY_EMB_PAD - HY_EMB)))
    deltas = jnp.abs(jnp.linspace(math.log(HY_TARGET) / HY_SLOW, math.log(HY_TARGET) / HY_FAST,
                                  MIXER_W, dtype=F32))
    decay = jnp.exp(-t * deltas[None, :])
    split = lambda a: jnp.stack([a[0::2], a[1::2]], axis=0)
    return split(feat), split(decay)


def _hyena_mix(z, lp, gain, n_seq, seq, row0):
    half = seq // 2
    tm_f = min(512, seq)
    tm_i = min(256, half)
    ncol = n_seq * MIXER_W
    tn = min(1024, ncol)
    t_e, t_o = _hy_tables(seq, tm_f)
    inv_e, inv_o = t_e.T, t_o.T
    feat, decay = _hy_features(seq)
    ab = _hy_filters(feat, decay, lp["hy_w1p"], lp["hy_b1"], lp["hy_f1"], lp["hy_w2"], lp["hy_b2"],
                     lp["hy_f2"], lp["hy_w3"])
    kspec = _hy_kspec(t_e, t_o, ab, tm_f)
    conv, u = _hy_conv(z, lp["hy_conv_w"], lp["hy_conv_b"], n_seq, seq, row0)
    y0 = _hy_fwd(t_e, t_o, u, kspec, 0, tm_f, tn)
    u1 = _hy_inv(inv_e, inv_o, y0, u, conv, 0, lp["hy_skip"][0:1], None, False, tm_i, tn)
    y1 = _hy_fwd(t_e, t_o, u1, kspec, 1, tm_f, tn)
    out = _hy_inv(inv_e, inv_o, y1, u1, conv, 1, lp["hy_skip"][1:2], gain, True, tm_i, tn)
    return jnp.transpose(out, (0, 2, 1, 3)).reshape(n_seq * seq, MIXER_W)


def _gelu(x):
    c = math.sqrt(2.0 / math.pi)
    return 0.5 * x * (1.0 + jnp.tanh(c * (x + 0.044715 * (x * x * x))))


def _gmlp_kernel(u_ref, v_ref, lg_ref, lb_ref, ws_ref, bs_ref, g_ref, o_ref):
    tl = u_ref.shape[0]
    gw = MIXER_W // GMLP_GROUPS
    lane_grp = lax.broadcasted_iota(jnp.int32, (GMLP_CHUNK, MIXER_W), 1) // gw
    for c in range(tl // GMLP_CHUNK):
        rows = slice(c * GMLP_CHUNK, (c + 1) * GMLP_CHUNK)
        v = _ln(_gelu(v_ref[rows, :])) * lg_ref[...] + lb_ref[...]
        vb = v.astype(BF16)
        stacked = jnp.concatenate(
            [jnp.where(lane_grp == g, vb, jnp.zeros_like(vb)) for g in range(GMLP_GROUPS)], axis=0)
        s = jnp.dot(ws_ref[...], stacked, preferred_element_type=F32) + bs_ref[...]
        y = _gelu(u_ref[rows, :]) * s
        o_ref[rows, :] = _rms_gain(y, g_ref[...]).astype(BF16)


def _gmlp_mix(z, ln_g, ln_b, ws_cat, bs_mat, gain, n_rows):
    tl = 512
    full = lambda shape: pl.BlockSpec(shape, lambda i: tuple(0 for _ in shape))
    return pl.pallas_call(
        _gmlp_kernel,
        grid=(n_rows // tl,),
        in_specs=[
            pl.BlockSpec((tl, MIXER_W), lambda i: (i, COL_GU)),
            pl.BlockSpec((tl, MIXER_W), lambda i: (i, COL_GV)),
            full((1, MIXER_W)), full((1, MIXER_W)),
            full((GMLP_CHUNK, GMLP_GROUPS * GMLP_CHUNK)),
            full((GMLP_CHUNK, MIXER_W)),
            full((1, MIXER_W)),
        ],
        out_specs=pl.BlockSpec((tl, MIXER_W), lambda i: (i, 0)),
        out_shape=jax.ShapeDtypeStruct((n_rows, MIXER_W), BF16),
        compiler_params=_cparams(("arbitrary",)),
        name="gmlp_mix",
    )(z, z, ln_g, ln_b, ws_cat, bs_mat, gain)


def _merge_kernel(yp_ref, ya_ref, yh_ref, yg_ref, w_ref, x_ref, g1_ref, lg_ref, lb_ref, sh_ref, sc_ref,
                  wrh_ref, wrl_ref, rb_ref, xo_ref, h_ref, idx_ref, gate_ref, y_a, y_b, *, alpha):
    s = pl.program_id(0)

    @pl.when(s == 0)
    def _():
        y_b[...] = jnp.zeros_like(y_b)

    def step(y_new, y_old):
        ymix = jnp.concatenate([yp_ref[...], ya_ref[...], yh_ref[...], yg_ref[...]], axis=-1)
        y_new[...] = jnp.dot(ymix, w_ref[...], preferred_element_type=F32)

        x1 = _ln(alpha * x_ref[...] + g1_ref[...] * y_old[...]) * lg_ref[...] + lb_ref[...]
        xo_ref[...] = x1
        h = _ln(x1) * (1.0 + sc_ref[...]) + sh_ref[...]
        h_ref[...] = h
        h_hi = h.astype(BF16)
        h_lo = (h - h_hi.astype(F32)).astype(BF16)
        logits = (jnp.dot(h_hi, wrh_ref[...], preferred_element_type=F32)
                  + (jnp.dot(h_lo, wrh_ref[...], preferred_element_type=F32)
                     + jnp.dot(h_hi, wrl_ref[...], preferred_element_type=F32)))
        idx, gate = _route_rows(logits, rb_ref[...])
        idx_ref[...] = idx
        gate_ref[...] = gate

    even = lax.rem(s, 2) == 0
    pl.when(even)(functools.partial(step, y_a, y_b))
    pl.when(jnp.logical_not(even))(functools.partial(step, y_b, y_a))


def _route_rows(logits, bias):
    neg = -1e30
    lane = lax.broadcasted_iota(jnp.int32, logits.shape, 1)
    lane_f = lane.astype(F32)
    valid = lane < N_EXPERTS
    per = N_EXPERTS // N_EXPERT_GROUPS
    s = jax.nn.sigmoid(logits)
    sel = jnp.where(valid, s + bias, neg)
    sh = [sel] + [pltpu.roll(sel, LOGIT_PAD - j, axis=1) for j in range(1, per)]
    pair = None
    for a in range(per):
        for b in range(a + 1, per):
            t = sh[a] + sh[b]
            pair = t if pair is None else jnp.maximum(pair, t)
    grp = jnp.where(valid & ((lane & (per - 1)) == 0), pair, neg)

    def first_max(v):
        m = jnp.max(v, axis=-1, keepdims=True)
        return jnp.min(jnp.where(v == m, lane_f, float(LOGIT_PAD)), axis=-1, keepdims=True)

    best = first_max(grp).astype(jnp.int32)
    shift = per.bit_length() - 1
    cand = jnp.where(valid & ((lane >> shift) == (best >> shift)), sel, neg)
    i1 = first_max(cand)
    i2 = first_max(jnp.where(lane_f == i1, neg, cand))
    w1 = jnp.sum(jnp.where(lane_f == i1, s, 0.0), axis=-1, keepdims=True)
    w2 = jnp.sum(jnp.where(lane_f == i2, s, 0.0), axis=-1, keepdims=True)
    tot = w1 + w2
    idx = jnp.where(lane == 0, i1, jnp.where(lane == 1, i2, 0.0)).astype(jnp.int32)
    gate = jnp.where(lane == 0, w1 / tot, jnp.where(lane == 1, w2 / tot, 0.0))
    return idx, gate


def _merge(ys, w_out_bf, xa, mod3, ln_g, ln_b, wr_hi, wr_lo, rb_pad, n_rows, n_batch, seq, alpha):
    d = xa.shape[1]
    tm = 512
    nt = n_rows // tm

    def cur(i):
        return jnp.minimum(i, nt - 1)

    def prv(i):
        return jnp.maximum(i - 1, 0)

    def mrow(i):
        return jnp.minimum((prv(i) * tm) // seq, n_batch)

    ymix = pl.BlockSpec((tm, MIXER_W), lambda i: (cur(i), 0))
    rowvec = pl.BlockSpec((1, d), lambda i: (0, 0))
    return pl.pallas_call(
        functools.partial(_merge_kernel, alpha=alpha),
        grid=(nt + 1,),
        in_specs=[
            ymix, ymix, ymix, ymix,
            pl.BlockSpec((d, d), lambda i: (0, 0)),
            pl.BlockSpec((tm, d), lambda i: (prv(i), 0)),
            pl.BlockSpec((None, 1, d), lambda i: (mrow(i), 0, 2)),
            rowvec, rowvec,
            pl.BlockSpec((None, 1, d), lambda i: (mrow(i), 0, 3)),
            pl.BlockSpec((None, 1, d), lambda i: (mrow(i), 0, 4)),
            pl.BlockSpec((d, LOGIT_PAD), lambda i: (0, 0)),
            pl.BlockSpec((d, LOGIT_PAD), lambda i: (0, 0)),
            pl.BlockSpec((1, LOGIT_PAD), lambda i: (0, 0)),
        ],
        out_specs=[
            pl.BlockSpec((tm, d), lambda i: (prv(i), 0)),
            pl.BlockSpec((tm, d), lambda i: (prv(i), 0)),
            pl.BlockSpec((tm, LOGIT_PAD), lambda i: (prv(i), 0)),
            pl.BlockSpec((tm, LOGIT_PAD), lambda i: (prv(i), 0)),
        ],
        out_shape=[
            jax.ShapeDtypeStruct((n_rows, d), F32),
            jax.ShapeDtypeStruct((n_rows, d), F32),
            jax.ShapeDtypeStruct((n_rows, LOGIT_PAD), jnp.int32),
            jax.ShapeDtypeStruct((n_rows, LOGIT_PAD), F32),
        ],
        scratch_shapes=[pltpu.VMEM((tm, d), F32), pltpu.VMEM((tm, d), F32)],
        compiler_params=_cparams(("arbitrary",)),
        name="merge",
    )(*ys, w_out_bf, xa, mod3, ln_g, ln_b, mod3, mod3, wr_hi, wr_lo, rb_pad)


def _moe_kernel(be_ref, nu_ref, dest_ref, h_hbm, wg_ref, wu_ref, wd_ref, o_ref, xbuf, sem, tok_ref,
                wg_s, wu_s, wd_s):
    i = pl.program_id(0)
    n_used = nu_ref[0]
    ring = xbuf.shape[0]
    slot = lax.rem(i, ring)

    @pl.when(i == 0)
    def _():
        def clear(s, carry):
            tok_ref[s] = 0
            return carry

        def place(a, carry):
            tok_ref[dest_ref[a]] = lax.shift_right_logical(a, TOP_K.bit_length() - 1)
            return carry

        lax.fori_loop(0, tok_ref.shape[0], clear, 0, unroll=8)
        lax.fori_loop(0, dest_ref.shape[0], place, 0, unroll=8)

    def row_copy(blk, r, sl):
        tok = tok_ref[blk * MOE_BLOCK + r]
        return pltpu.make_async_copy(h_hbm.at[pl.ds(tok, 1)], xbuf.at[sl, pl.ds(r, 1)], sem.at[sl])

    def gather_start(blk, sl):
        def body(r, carry):
            row_copy(blk, r, sl).start()
            return carry

        lax.fori_loop(0, MOE_BLOCK, body, 0, unroll=8)

    def gather_wait(sl):
        pltpu.make_async_copy(h_hbm.at[pl.ds(0, MOE_BLOCK)], xbuf.at[sl], sem.at[sl]).wait()

    @pl.when(i == 0)
    def _():
        gather_start(0, 0)

        @pl.when(n_used > 1)
        def _():
            gather_start(1, 1)

    active = i < n_used
    new_expert = jnp.logical_or(i == 0, be_ref[i] != be_ref[jnp.maximum(i - 1, 0)])

    @pl.when(jnp.logical_and(active, new_expert))
    def _():
        for src, dst in ((wg_ref, wg_s), (wu_ref, wu_s), (wd_ref, wd_s)):
            def cast(r0, src=src, dst=dst):
                dst[pl.ds(r0, 256), :] = src[pl.ds(r0, 256), :].astype(BF16)

            _row_chunks(src.shape[0], 256, cast)

    def ffn_block(prefetch_next):
        gather_wait(slot)
        x = xbuf[slot].astype(BF16)
        if prefetch_next:
            nxt = lax.rem(i + 2, ring)
            for r in range(MOE_BLOCK):
                row_copy(i + 2, r, nxt).start()
        g = jnp.dot(x, wg_s[...], preferred_element_type=F32)
        u = jnp.dot(x, wu_s[...], preferred_element_type=F32)
        a = (g * jax.nn.sigmoid(g) * u).astype(BF16)
        o_ref[...] = jnp.dot(a, wd_s[...], preferred_element_type=F32).astype(o_ref.dtype)

    has_next = i + 2 < n_used
    pl.when(jnp.logical_and(active, has_next))(functools.partial(ffn_block, True))
    pl.when(jnp.logical_and(active, jnp.logical_not(has_next)))(functools.partial(ffn_block, False))

    @pl.when(jnp.logical_not(active))
    def _():
        o_ref[...] = jnp.zeros_like(o_ref)


def _moe_experts(h, dest, n_slots, block_e, n_used, wg, wu, wd, layer):
    d = h.shape[1]
    de = wg.shape[3]
    n_blocks = n_slots // MOE_BLOCK
    grid_spec = pltpu.PrefetchScalarGridSpec(
        num_scalar_prefetch=3,
        grid=(n_blocks,),
        in_specs=[
            pl.BlockSpec(memory_space=pl.ANY),
            pl.BlockSpec((None, None, d, de), lambda i, be, nu, tok: (layer, be[i], 0, 0)),
            pl.BlockSpec((None, None, d, de), lambda i, be, nu, tok: (layer, be[i], 0, 0)),
            pl.BlockSpec((None, None, de, d), lambda i, be, nu, tok: (layer, be[i], 0, 0)),
        ],
        out_specs=pl.BlockSpec((MOE_BLOCK, d), lambda i, be, nu, tok: (i, 0)),
        scratch_shapes=[
            pltpu.VMEM((3, MOE_BLOCK, d), F32),
            pltpu.SemaphoreType.DMA((3,)),
            pltpu.SMEM((n_slots,), jnp.int32),
            pltpu.VMEM((d, de), BF16), pltpu.VMEM((d, de), BF16), pltpu.VMEM((de, d), BF16),
        ],
    )
    return pl.pallas_call(
        _moe_kernel,
        grid_spec=grid_spec,
        out_shape=jax.ShapeDtypeStruct((n_slots, d), BF16),
        compiler_params=_cparams(("arbitrary",)),
        name="moe_experts",
    )(block_e, n_used, dest.reshape(-1), h, wg, wu, wd)


def _assignment_ranks(flat_e):
    a = flat_e.shape[0]
    blk = 128
    nb = a // blk
    onehot = (flat_e[:, None] == jnp.arange(N_EXPERTS)[None, :]).astype(BF16).reshape(nb, blk, N_EXPERTS)
    tri = (jnp.arange(blk)[:, None] >= jnp.arange(blk)[None, :]).astype(BF16)
    intra = jnp.einsum("ij,bjk->bik", tri, onehot, preferred_element_type=F32)
    bsum = intra[:, -1, :]
    before = (jnp.arange(nb)[:, None] > jnp.arange(nb)[None, :]).astype(BF16)
    offs = jnp.dot(before, bsum.astype(BF16), preferred_element_type=F32)
    csum = intra + offs[:, None, :]
    rank = jnp.sum(csum * onehot.astype(F32), axis=-1).reshape(a) - 1.0
    counts = offs[-1] + bsum[-1]
    return rank.astype(jnp.int32), counts.astype(jnp.int32)


def _dispatch_plan(idx):
    t = idx.shape[0]
    a = t * TOP_K
    flat_e = idx.reshape(a)
    rank, counts = _assignment_ranks(flat_e)
    padded = (counts + MOE_BLOCK - 1) // MOE_BLOCK * MOE_BLOCK
    pad_end = jnp.cumsum(padded)
    pad_start = pad_end - padded
    dest = pad_start[flat_e] + rank
    n_blocks = (a + N_EXPERTS * (MOE_BLOCK - 1) + MOE_BLOCK - 1) // MOE_BLOCK
    n_slots = n_blocks * MOE_BLOCK
    blk_start = jnp.arange(n_blocks, dtype=jnp.int32) * MOE_BLOCK
    block_e = jnp.minimum(
        jnp.sum((pad_end[None, :] <= blk_start[:, None]).astype(jnp.int32), axis=1), N_EXPERTS - 1)
    n_used = (pad_end[-1] // MOE_BLOCK).astype(jnp.int32).reshape(1)
    return n_slots, block_e.astype(jnp.int32), n_used, dest.reshape(t, TOP_K)


def _final_kernel(x_ref, ya_ref, yb_ref, gate_ref, g2_ref, lg_ref, lb_ref, o_ref, *, alpha):
    f = ya_ref[...].astype(F32) * gate_ref[:, 0:1] + yb_ref[...].astype(F32) * gate_ref[:, 1:2]
    o_ref[...] = _ln(alpha * x_ref[...] + g2_ref[...] * f) * lg_ref[...] + lb_ref[...]


def _final_norm(x1, ya, yb, gate, mod3, ln_g, ln_b, n_batch, seq, alpha):
    n_rows, d = x1.shape
    tm = 512

    def mrow(i):
        return jnp.minimum((i * tm) // seq, n_batch)

    tile = pl.BlockSpec((tm, d), lambda i: (i, 0))
    rowvec = pl.BlockSpec((1, d), lambda i: (0, 0))
    return pl.pallas_call(
        functools.partial(_final_kernel, alpha=alpha),
        grid=(n_rows // tm,),
        in_specs=[tile, tile, tile, pl.BlockSpec((tm, LOGIT_PAD), lambda i: (i, 0)),
                  pl.BlockSpec((None, 1, d), lambda i: (mrow(i), 0, 5)), rowvec, rowvec],
        out_specs=tile,
        out_shape=jax.ShapeDtypeStruct((n_rows, d), F32),
        compiler_params=_cparams(("arbitrary",)),
        name="final_norm",
    )(x1, ya, yb, gate, mod3, ln_g, ln_b)


def _permute_in_cols(w):
    pool, q, k, v, hy, gm = jnp.split(w, (512, 1024, 1152, 1280, 2816), axis=-1)
    return jnp.concatenate([hy, pool, q, gm, k, v], axis=-1)


def kernel(x, c, ctx, c_ctx, w_ada, b_ada, w_in, w_out, mix_norm_g, pool_w, pool_scale, attn_sink,
           hy_conv_w, hy_conv_b, hy_w1, hy_b1, hy_f1, hy_w2, hy_b2, hy_f2, hy_w3, hy_skip,
           gm_ln_g, gm_ln_b, gm_ws, gm_bs, ln1_g, ln1_b, ln2_g, ln2_b, w_router, router_bias,
           w_gate, w_up, w_down):
    n_batch, seq, d = x.shape
    ctx_len = ctx.shape[1]
    depth = w_in.shape[0]
    grid_w = 64
    alpha = (2 * depth) ** 0.25
    n_lat = n_batch * seq
    n_ctx = n_batch * ctx_len

    xa = jnp.concatenate([x.reshape(n_lat, d), ctx.reshape(n_ctx, d)], axis=0)
    cvec = jnp.concatenate([c, c_ctx[None, :], jnp.zeros((8 - n_batch - 1, d), F32)], axis=0)
    mods = _ada_mods(cvec, w_ada, b_ada)
    cos_t, sin_t = _rope_tables(seq, grid_w)
    wr_pad = jnp.pad(w_router, ((0, 0), (0, LOGIT_PAD - N_EXPERTS)))
    wr_hi = wr_pad.astype(BF16)
    wr_lo = (wr_pad - wr_hi.astype(F32)).astype(BF16)
    rb_pad = jnp.pad(router_bias.astype(F32), (0, LOGIT_PAD - N_EXPERTS)).reshape(1, LOGIT_PAD)
    row = lambda v: v.reshape(1, -1)

    for layer in range(depth):
        last = layer == depth - 1
        mod3 = mods[layer].reshape(8, 1, 6 * d)
        gains = mix_norm_g[layer].reshape(4, 1, MIXER_W)
        lp = {
            "hy_conv_w": hy_conv_w[layer], "hy_conv_b": row(hy_conv_b[layer]),
            "hy_w1p": jnp.pad(hy_w1[layer], ((0, HY_EMB_PAD - HY_EMB), (0, 0))),
            "hy_b1": row(hy_b1[layer]), "hy_f1": row(hy_f1[layer]),
            "hy_w2": hy_w2[layer], "hy_b2": row(hy_b2[layer]), "hy_f2": row(hy_f2[layer]),
            "hy_w3": hy_w3[layer], "hy_skip": hy_skip[layer],
        }
        z = _in_proj(xa, mod3, _permute_in_cols(w_in[layer]).astype(BF16), n_batch, seq)

        pool_bf = pool_w[layer].astype(BF16)
        ps = row(pool_scale[layer])
        ws_cat = jnp.transpose(gm_ws[layer], (1, 0, 2)).reshape(GMLP_CHUNK, GMLP_GROUPS * GMLP_CHUNK)
        bs_mat = jnp.repeat(gm_bs[layer].T, MIXER_W // GMLP_GROUPS, axis=1)
        n_rows = n_lat if last else n_lat + n_ctx

        y_pool = _pool_mix(z, pool_bf, ps, gains[0], n_batch, seq, 0)
        y_attn = _attn_lat(z, attn_sink[layer], cos_t, sin_t, gains[1], n_batch, seq, ctx_len)
        y_hy = _hyena_mix(z, lp, gains[2], n_batch, seq, 0)
        y_gm = _gmlp_mix(z, row(gm_ln_g[layer]), row(gm_ln_b[layer]), ws_cat.astype(BF16), bs_mat,
                         gains[3], n_rows)
        if not last:
            y_pool = jnp.concatenate(
                [y_pool, _pool_mix(z, pool_bf, ps, gains[0], n_batch, ctx_len, n_lat // ctx_len)], axis=0)
            y_attn = jnp.concatenate(
                [y_attn, _attn_ctx(z, attn_sink[layer], gains[1], n_batch, seq, ctx_len)], axis=0)
            y_hy = jnp.concatenate([y_hy, _hyena_mix(z, lp, gains[2], n_batch, ctx_len, n_lat)], axis=0)

        x1, h2, idx, gate = _merge((y_pool, y_attn, y_hy, y_gm), w_out[layer].astype(BF16), xa, mod3,
                                   row(ln1_g[layer]), row(ln1_b[layer]), wr_hi, wr_lo, rb_pad,
                                   n_rows, n_batch, seq, alpha)

        n_slots, block_e, n_used, dest = _dispatch_plan(idx[:, :TOP_K])
        ys = _moe_experts(h2, dest, n_slots, block_e, n_used, w_gate, w_up, w_down, layer)
        xa = _final_norm(x1, ys[dest[:, 0]], ys[dest[:, 1]], gate, mod3, row(ln2_g[layer]), row(ln2_b[layer]),
                         n_batch, seq, alpha)
    return xa.reshape(n_batch, seq, d)
```

```python
import functools
import math

import jax
import jax.numpy as jnp
from jax import lax
from jax.experimental import pallas as pl
from jax.experimental.pallas import tpu as pltpu

F32 = jnp.float32
BF16 = jnp.bfloat16
HI = lax.Precision.HIGHEST

LN_EPS = 1e-6
MIXER_W = 512
HEAD_DIM = 64
N_HEADS = 8
N_KV_HEADS = 2
KV_GROUP = 4
KV_W = 128
WINDOW = 128
ROPE_THETA = 10000.0
POOL_WINDOWS = (2, 4, 8, 16)
POOL_GROUP = 128
POOL_HALO = 8
GMLP_CHUNK = 128
GMLP_GROUPS = 8
HY_EMB = 33
HY_EMB_PAD = 128
HY_TARGET = 1e-2
HY_FAST = 0.3
HY_SLOW = 1.5
N_EXPERTS = 32
N_EXPERT_GROUPS = 8
TOP_K = 2
D_EXPERT = 768
MOE_BLOCK = 256
LOGIT_PAD = 128

COL_HY = 0
COL_POOL = 3
COL_Q = 4
COL_GU = 5
COL_GV = 6
COL_K = 28
COL_V = 29
IN_W = 3840

VMEM_LIMIT = 56 * 1024 * 1024
MOE_VMEM_LIMIT = 59 * 1024 * 1024


def _cparams(sem, vmem_limit=VMEM_LIMIT):
    return pltpu.CompilerParams(dimension_semantics=sem, vmem_limit_bytes=vmem_limit)


def _ln(x):
    mu = jnp.mean(x, axis=-1, keepdims=True)
    xc = x - mu
    var = jnp.mean(xc * xc, axis=-1, keepdims=True)
    return xc * lax.rsqrt(var + LN_EPS)


def _rms_gain(y, g):
    return y * lax.rsqrt(jnp.mean(y * y, axis=-1, keepdims=True) + LN_EPS) * g


def _row_chunks(n_rows, chunk, body):
    n = n_rows // chunk
    if n == 1:
        body(0)
        return

    def step(i, carry):
        body(pl.multiple_of(i * chunk, chunk))
        return carry

    lax.fori_loop(0, n, step, 0)


def _ada_kernel(c_ref, w_ref, b_ref, o_ref):
    c = c_ref[...]
    s = c * jax.nn.sigmoid(c)
    o_ref[...] = jnp.dot(s, w_ref[...], precision=HI, preferred_element_type=F32) + b_ref[...]


def _ada_mods(cvec, w_ada, b_ada):
    depth, d, n = w_ada.shape
    tn = 1024
    return pl.pallas_call(
        _ada_kernel,
        grid=(depth, n // tn),
        in_specs=[
            pl.BlockSpec((8, d), lambda l, j: (0, 0)),
            pl.BlockSpec((None, d, tn), lambda l, j: (l, 0, j)),
            pl.BlockSpec((None, 1, tn), lambda l, j: (l, 0, j)),
        ],
        out_specs=pl.BlockSpec((None, 8, tn), lambda l, j: (l, 0, j)),
        out_shape=jax.ShapeDtypeStruct((depth, 8, n), F32),
        compiler_params=_cparams(("arbitrary", "arbitrary")),
        name="ada_mods",
    )(cvec, w_ada, b_ada.reshape(depth, 1, n))


def _inproj_kernel(x_ref, sh_ref, sc_ref, w_ref, o_ref, h_scr):
    @pl.when(pl.program_id(1) == 0)
    def _():
        sh = sh_ref[...]
        sc1 = 1.0 + sc_ref[...]

        def body(r0):
            x = x_ref[pl.ds(r0, 256), :]
            h_scr[pl.ds(r0, 256), :] = (_ln(x) * sc1 + sh).astype(BF16)

        _row_chunks(x_ref.shape[0], 256, body)

    o_ref[...] = jnp.dot(h_scr[...], w_ref[...], preferred_element_type=F32)


def _in_proj(xa, mod3, w_bf, n_batch, seq):
    t, d = xa.shape
    n = w_bf.shape[1]
    tn = 768
    tm = next(m for m in (1024, 512, 256) if t % m == 0 and seq % m == 0)

    def mrow(i):
        return jnp.minimum((i * tm) // seq, n_batch)

    return pl.pallas_call(
        _inproj_kernel,
        grid=(t // tm, n // tn),
        in_specs=[
            pl.BlockSpec((tm, d), lambda i, j: (i, 0)),
            pl.BlockSpec((None, 1, d), lambda i, j: (mrow(i), 0, 0)),
            pl.BlockSpec((None, 1, d), lambda i, j: (mrow(i), 0, 1)),
            pl.BlockSpec((d, tn), lambda i, j: (0, j)),
        ],
        out_specs=pl.BlockSpec((tm, tn), lambda i, j: (i, j)),
        out_shape=jax.ShapeDtypeStruct((t, n), F32),
        scratch_shapes=[pltpu.VMEM((tm, d), BF16)],
        compiler_params=_cparams(("arbitrary", "arbitrary")),
        name="in_proj",
    )(xa, mod3, mod3, w_bf)


def _pool_kernel(z_ref, w_ref, ps_ref, g_ref, o_ref, buf):
    seq = z_ref.shape[0]
    rc = min(256, seq)
    zeros = jnp.zeros((POOL_HALO, MIXER_W), F32)
    buf[0:POOL_HALO, :] = zeros
    buf[seq + POOL_HALO:seq + 2 * POOL_HALO, :] = zeros

    def fill(r0):
        buf[pl.ds(r0 + POOL_HALO, rc), :] = z_ref[pl.ds(r0, rc), :]

    _row_chunks(seq, rc, fill)

    def body(r0):
        win = buf[pl.ds(r0, rc + 2 * POOL_HALO), :]
        t = r0 + lax.broadcasted_iota(jnp.int32, (rc, 1), 0)
        outs = []
        for gi, w in enumerate(POOL_WINDOWS):
            lanes = slice(gi * POOL_GROUP, (gi + 1) * POOL_GROUP)
            acc = None
            for j in range(-w // 2, w // 2):
                piece = win[POOL_HALO + j:POOL_HALO + j + rc, lanes]
                acc = piece if acc is None else acc + piece
            cnt = (jnp.minimum(t + w // 2, seq) - jnp.maximum(t - w // 2, 0)).astype(F32)
            dlt = acc / cnt - win[POOL_HALO:POOL_HALO + rc, lanes]
            outs.append(jnp.dot(dlt.astype(BF16), w_ref[gi], preferred_element_type=F32))
        y = jnp.concatenate(outs, axis=-1) * ps_ref[...]
        o_ref[pl.ds(r0, rc), :] = _rms_gain(y, g_ref[...]).astype(BF16)

    _row_chunks(seq, rc, body)


def _pool_mix(z, pool_w_bf, pool_scale, gain, n_seq, seq, row_blk0):
    return pl.pallas_call(
        _pool_kernel,
        grid=(n_seq,),
        in_specs=[
            pl.BlockSpec((seq, MIXER_W), lambda s: (row_blk0 + s, COL_POOL)),
            pl.BlockSpec((4, POOL_GROUP, POOL_GROUP), lambda s: (0, 0, 0)),
            pl.BlockSpec((1, MIXER_W), lambda s: (0, 0)),
            pl.BlockSpec((1, MIXER_W), lambda s: (0, 0)),
        ],
        out_specs=pl.BlockSpec((seq, MIXER_W), lambda s: (s, 0)),
        out_shape=jax.ShapeDtypeStruct((n_seq * seq, MIXER_W), BF16),
        scratch_shapes=[pltpu.VMEM((seq + 2 * POOL_HALO, MIXER_W), F32)],
        compiler_params=_cparams(("arbitrary",)),
        name="pool_mix",
    )(z, pool_w_bf, pool_scale, gain)


def _rope(x, cos, sin_signed):
    lane = lax.broadcasted_iota(jnp.int32, x.shape, 1)
    partner = jnp.where((lane & 31) < 16, pltpu.roll(x, 112, axis=1), pltpu.roll(x, 16, axis=1))
    return x * cos + partner * sin_signed


def _attend(qs, sink_col, parts):
    dn = (((1,), (1,)), ((), ()))
    scores = []
    for kk, _, mask in parts:
        s = lax.dot_general(qs, kk, dn, preferred_element_type=F32)
        if mask is not None:
            s = s + mask
        scores.append(s)
    m = sink_col
    for s in scores:
        m = jnp.maximum(m, jnp.max(s, axis=-1, keepdims=True))
    den = jnp.exp(sink_col - m)
    out = None
    for s, (_, vv, _) in zip(scores, parts):
        p = jnp.exp(s - m)
        den = den + jnp.sum(p, axis=-1, keepdims=True)
        o = jnp.dot(p.astype(BF16), vv, preferred_element_type=F32)
        out = o if out is None else out + o
    return out / den


def _heads_attend(q_all, sink_ref, tq, make_parts):
    cols = [None] * N_HEADS
    for kh in range(N_KV_HEADS):
        heads = [kh * KV_GROUP + g for g in range(KV_GROUP)]
        qs = jnp.concatenate([q_all[:, h * HEAD_DIM:(h + 1) * HEAD_DIM] for h in heads], axis=0)
        sink_col = jnp.concatenate([jnp.full((tq, 1), sink_ref[h], F32) for h in heads], axis=0)
        o = _attend(qs.astype(BF16), sink_col, make_parts(kh))
        for g, h in enumerate(heads):
            cols[h] = o[g * tq:(g + 1) * tq, :]
    return jnp.concatenate(cols, axis=-1)


def _attn_lat_kernel(sink_ref, q_ref, k_ref, v_ref, kc_ref, vc_ref, cos_ref, sin_ref, g_ref, o_ref):
    tq = q_ref.shape[0]
    seq = k_ref.shape[0]
    nband = tq + 2 * WINDOW
    i = pl.program_id(1)
    q0 = pl.multiple_of(i * tq, tq)
    k0 = pl.multiple_of(jnp.clip(q0 - WINDOW, 0, seq - nband), WINDOW)

    cq = cos_ref[pl.ds(q0, tq), :]
    sq = sin_ref[pl.ds(q0, tq), :]
    q = q_ref[...]
    q_all = jnp.concatenate(
        [_rope(q[:, c * 128:(c + 1) * 128], cq, sq) for c in range(MIXER_W // 128)], axis=-1)
    q_all = q_all * (HEAD_DIM ** -0.5)

    kb = _rope(k_ref[pl.ds(k0, nband), :], cos_ref[pl.ds(k0, nband), :],
               sin_ref[pl.ds(k0, nband), :]).astype(BF16)
    vb = v_ref[pl.ds(k0, nband), :].astype(BF16)
    kc = kc_ref[...].astype(BF16)
    vc = vc_ref[...].astype(BF16)

    qpos = q0 + lax.broadcasted_iota(jnp.int32, (tq, nband), 0)
    kpos = k0 + lax.broadcasted_iota(jnp.int32, (tq, nband), 1)
    band = jnp.where(jnp.abs(qpos - kpos) <= WINDOW, 0.0, -1e30)
    mask = jnp.concatenate([band] * KV_GROUP, axis=0)

    def make_parts(kh):
        hs = slice(kh * HEAD_DIM, (kh + 1) * HEAD_DIM)
        return [(kb[:, hs], vb[:, hs], mask), (kc[:, hs], vc[:, hs], None)]

    y = _heads_attend(q_all, sink_ref, tq, make_parts)
    o_ref[...] = _rms_gain(y, g_ref[...]).astype(BF16)


def _attn_ctx_kernel(sink_ref, q_ref, kc_ref, vc_ref, g_ref, o_ref):
    tq = q_ref.shape[0]
    q_all = q_ref[...] * (HEAD_DIM ** -0.5)
    kc = kc_ref[...].astype(BF16)
    vc = vc_ref[...].astype(BF16)

    def make_parts(kh):
        hs = slice(kh * HEAD_DIM, (kh + 1) * HEAD_DIM)
        return [(kc[:, hs], vc[:, hs], None)]

    y = _heads_attend(q_all, sink_ref, tq, make_parts)
    o_ref[...] = _rms_gain(y, g_ref[...]).astype(BF16)


def _attn_lat(z, sink, cos_t, sin_t, gain, n_batch, seq, ctx_len):
    tq = 256
    nq = seq // tq
    ctx_blk0 = (n_batch * seq) // ctx_len
    smem = pl.BlockSpec(memory_space=pltpu.SMEM)
    return pl.pallas_call(
        _attn_lat_kernel,
        grid=(n_batch, nq),
        in_specs=[
            smem,
            pl.BlockSpec((tq, MIXER_W), lambda b, i: (b * nq + i, COL_Q)),
            pl.BlockSpec((seq, KV_W), lambda b, i: (b, COL_K)),
            pl.BlockSpec((seq, KV_W), lambda b, i: (b, COL_V)),
            pl.BlockSpec((ctx_len, KV_W), lambda b, i: (ctx_blk0 + b, COL_K)),
            pl.BlockSpec((ctx_len, KV_W), lambda b, i: (ctx_blk0 + b, COL_V)),
            pl.BlockSpec((seq, 128), lambda b, i: (0, 0)),
            pl.BlockSpec((seq, 128), lambda b, i: (0, 0)),
            pl.BlockSpec((1, MIXER_W), lambda b, i: (0, 0)),
        ],
        out_specs=pl.BlockSpec((tq, MIXER_W), lambda b, i: (b * nq + i, 0)),
        out_shape=jax.ShapeDtypeStruct((n_batch * seq, MIXER_W), BF16),
        compiler_params=_cparams(("arbitrary", "arbitrary")),
        name="attn_lat",
    )(sink, z, z, z, z, z, cos_t, sin_t, gain)


def _attn_ctx(z, sink, gain, n_batch, seq, ctx_len):
    ctx_blk0 = (n_batch * seq) // ctx_len
    smem = pl.BlockSpec(memory_space=pltpu.SMEM)
    return pl.pallas_call(
        _attn_ctx_kernel,
        grid=(n_batch,),
        in_specs=[
            smem,
            pl.BlockSpec((ctx_len, MIXER_W), lambda b: (ctx_blk0 + b, COL_Q)),
            pl.BlockSpec((ctx_len, KV_W), lambda b: (ctx_blk0 + b, COL_K)),
            pl.BlockSpec((ctx_len, KV_W), lambda b: (ctx_blk0 + b, COL_V)),
            pl.BlockSpec((1, MIXER_W), lambda b: (0, 0)),
        ],
        out_specs=pl.BlockSpec((ctx_len, MIXER_W), lambda b: (b, 0)),
        out_shape=jax.ShapeDtypeStruct((n_batch * ctx_len, MIXER_W), BF16),
        compiler_params=_cparams(("arbitrary",)),
        name="attn_ctx",
    )(sink, z, z, z, gain)


def _rope_tables(seq, grid_w):
    nf = HEAD_DIM // 4
    inv = ROPE_THETA ** (-jnp.arange(nf, dtype=F32) / nf)
    t = jnp.arange(seq)
    row = (t // grid_w).astype(F32)[:, None] * inv[None, :]
    col = (t % grid_w).astype(F32)[:, None] * inv[None, :]
    cos_h = jnp.concatenate([jnp.cos(row), jnp.cos(row), jnp.cos(col), jnp.cos(col)], axis=-1)
    sin_h = jnp.concatenate([-jnp.sin(row), jnp.sin(row), -jnp.sin(col), jnp.sin(col)], axis=-1)
    return jnp.tile(cos_h, (1, 2)), jnp.tile(sin_h, (1, 2))


def _hy_conv_kernel(*refs):
    z_refs, zp_refs, zn_refs = refs[0:3], refs[3:6], refs[6:9]
    w_ref, b_ref, g_ref, u_ref = refs[9:]
    i = pl.program_id(1)
    th = z_refs[0].shape[0] // 2
    has_prev = (i > 0).astype(F32)
    has_next = (i < pl.num_programs(1) - 1).astype(F32)
    row = lax.broadcasted_iota(jnp.int32, (th, 128), 0)
    for part in range(3):
        ze = z_refs[part][pl.ds(0, th, stride=2), :]
        zo = z_refs[part][pl.ds(1, th, stride=2), :]
        prev_row = zp_refs[part][POOL_HALO - 1:POOL_HALO, :] * has_prev
        next_row = zn_refs[part][0:1, :] * has_next
        zo_m = jnp.where(row == 0, prev_row, pltpu.roll(zo, 1, axis=0))
        ze_p = jnp.where(row == th - 1, next_row, pltpu.roll(ze, th - 1, axis=0))
        w0, w1, w2, b = w_ref[part, 0:1, :], w_ref[part, 1:2, :], w_ref[part, 2:3, :], b_ref[part]
        ye = b + zo_m * w0 + ze * w1 + zo * w2
        yo = b + ze * w0 + zo * w1 + ze_p * w2
        g_ref[part, 0] = ye
        g_ref[part, 1] = yo
        if part == 2:
            u_ref[0] = ye.astype(BF16)
            u_ref[1] = yo.astype(BF16)


def _hy_conv(z, conv_w, conv_b, n_seq, seq, row0):
    tl = min(2048, seq)
    nt = seq // tl
    blk0 = row0 // tl
    hb0 = row0 // POOL_HALO
    hpt = tl // POOL_HALO
    last_halo = (row0 + n_seq * seq) // POOL_HALO - 1
    cpp = MIXER_W // 128
    ncol = n_seq * MIXER_W

    def main_spec(part):
        return pl.BlockSpec((tl, 128), lambda s, i, c: (blk0 + s * nt + i, part * cpp + c))

    def prev_spec(part):
        return pl.BlockSpec((POOL_HALO, 128),
                            lambda s, i, c: (jnp.maximum(hb0 + (s * nt + i) * hpt - 1, 0), part * cpp + c))

    def next_spec(part):
        return pl.BlockSpec((POOL_HALO, 128),
                            lambda s, i, c: (jnp.minimum(hb0 + (s * nt + i + 1) * hpt, last_halo), part * cpp + c))

    w3 = conv_w.reshape(3, 3, MIXER_W).transpose(1, 0, 2)
    b3 = conv_b.reshape(3, 1, MIXER_W)
    return pl.pallas_call(
        _hy_conv_kernel,
        grid=(n_seq, nt, cpp),
        in_specs=[main_spec(p) for p in range(3)] + [prev_spec(p) for p in range(3)]
        + [next_spec(p) for p in range(3)]
        + [pl.BlockSpec((3, 3, 128), lambda s, i, c: (0, 0, c)),
           pl.BlockSpec((3, 1, 128), lambda s, i, c: (0, 0, c))],
        out_specs=[
            pl.BlockSpec((3, 2, tl // 2, 128), lambda s, i, c: (0, 0, i, s * cpp + c)),
            pl.BlockSpec((2, tl // 2, 128), lambda s, i, c: (0, i, s * cpp + c)),
        ],
        out_shape=[jax.ShapeDtypeStruct((3, 2, seq // 2, ncol), F32),
                   jax.ShapeDtypeStruct((2, seq // 2, ncol), BF16)],
        compiler_params=_cparams(("arbitrary", "arbitrary", "arbitrary")),
        name="hy_conv",
    )(*([z] * 9), w3, b3)


def _hy_filter_kernel(feat_ref, dec_ref, w1_ref, b1_ref, f1_ref, w2_ref, b2_ref, f2_ref, w3_ref, o_ref):
    tl = feat_ref.shape[0]
    h = jnp.dot(feat_ref[...], w1_ref[...], precision=HI, preferred_element_type=F32) + b1_ref[...]
    h = jnp.sin(f1_ref[...] * h)
    h = jnp.dot(h, w2_ref[...], precision=HI, preferred_element_type=F32) + b2_ref[...]
    h = jnp.sin(f2_ref[...] * h)
    filt = jnp.dot(h, w3_ref[...], precision=HI, preferred_element_type=F32)
    dec = dec_ref[...]
    t = 2 * (pl.program_id(1) * tl + lax.broadcasted_iota(jnp.int32, (tl, 1), 0)) + pl.program_id(0)
    not_first = (t > 0).astype(F32)
    for o in range(2):
        hf = filt[:, (2 * o) * MIXER_W:(2 * o + 1) * MIXER_W] * dec
        hb = filt[:, (2 * o + 1) * MIXER_W:(2 * o + 2) * MIXER_W] * dec * not_first
        o_ref[:, o * MIXER_W:(o + 1) * MIXER_W] = (hf + hb).astype(BF16)
        o_ref[:, (2 + o) * MIXER_W:(3 + o) * MIXER_W] = (hb - hf).astype(BF16)


def _hy_filters(feat, decay, w1p, b1, f1, w2, b2, f2, w3):
    half = feat.shape[1]
    tl = min(512, half)
    hid = w2.shape[0]
    full = lambda shape: pl.BlockSpec(shape, lambda p, i: tuple(0 for _ in shape))
    return pl.pallas_call(
        _hy_filter_kernel,
        grid=(2, half // tl),
        in_specs=[
            pl.BlockSpec((None, tl, HY_EMB_PAD), lambda p, i: (p, i, 0)),
            pl.BlockSpec((None, tl, MIXER_W), lambda p, i: (p, i, 0)),
            full((HY_EMB_PAD, hid)), full((1, hid)), full((1, hid)),
            full((hid, hid)), full((1, hid)), full((1, hid)),
            full((hid, 4 * MIXER_W)),
        ],
        out_specs=pl.BlockSpec((None, tl, 4 * MIXER_W), lambda p, i: (p, i, 0)),
        out_shape=jax.ShapeDtypeStruct((2, half, 4 * MIXER_W), BF16),
        compiler_params=_cparams(("arbitrary", "arbitrary")),
        name="hy_filters",
    )(feat, decay, w1p, b1, f1, w2, b2, f2, w3)


def _hy_kspec_kernel(te_ref, to_ref, ab_ref, o_ref, *, inv_len):
    hb = te_ref.shape[0] // 2
    half = ab_ref.shape[2] // 2
    ec = jnp.dot(te_ref[0:hb, :], ab_ref[0, :, 0:half], preferred_element_type=F32)
    oc = jnp.dot(to_ref[0:hb, :], ab_ref[1, :, 0:half], preferred_element_type=F32)
    es = jnp.dot(te_ref[hb:, :], ab_ref[0, :, half:], preferred_element_type=F32)
    os_ = jnp.dot(to_ref[hb:, :], ab_ref[1, :, half:], preferred_element_type=F32)
    o_ref[0, 0:hb, :] = (ec + oc) * inv_len
    o_ref[0, hb:, :] = (es + os_) * inv_len
    o_ref[1, 0:hb, :] = (ec - oc) * inv_len
    o_ref[1, hb:, :] = (os_ - es) * inv_len


def _hy_kspec(t_e, t_o, ab, tm):
    seq, half = t_e.shape
    return pl.pallas_call(
        functools.partial(_hy_kspec_kernel, inv_len=1.0 / seq),
        grid=(seq // tm,),
        in_specs=[
            pl.BlockSpec((tm, half), lambda i: (i, 0)),
            pl.BlockSpec((tm, half), lambda i: (i, 0)),
            pl.BlockSpec((2, half, 4 * MIXER_W), lambda i: (0, 0, 0)),
        ],
        out_specs=pl.BlockSpec((2, tm, 2 * MIXER_W), lambda i: (0, i, 0)),
        out_shape=jax.ShapeDtypeStruct((2, seq, 2 * MIXER_W), F32),
        compiler_params=_cparams(("arbitrary",)),
        name="hy_kspec",
    )(t_e, t_o, ab)


def _hy_fwd_kernel(te_ref, to_ref, u_ref, k_ref, o_ref):
    hb = te_ref.shape[0] // 2
    reps = u_ref.shape[2] // MIXER_W
    ep = jnp.dot(te_ref[...], u_ref[0], preferred_element_type=F32)
    op = jnp.dot(to_ref[...], u_ref[1], preferred_element_type=F32)
    ec, es, oc, os_ = ep[0:hb, :], ep[hb:, :], op[0:hb, :], op[hb:, :]
    tile = lambda v: jnp.concatenate([v] * reps, axis=-1)
    p, q, pm, qm = ec + oc, es + os_, ec - oc, os_ - es
    kr, ki = tile(k_ref[0, 0:hb, :]), tile(k_ref[0, hb:, :])
    krm, kim = tile(k_ref[1, 0:hb, :]), tile(k_ref[1, hb:, :])
    yr, yn = p * kr + q * ki, q * kr - p * ki
    yrm, ynm = pm * krm + qm * kim, qm * krm - pm * kim
    o_ref[0, 0:hb, :] = (yr + yrm).astype(BF16)
    o_ref[0, hb:, :] = (yn - ynm).astype(BF16)
    o_ref[1, 0:hb, :] = (yr - yrm).astype(BF16)
    o_ref[1, hb:, :] = (yn + ynm).astype(BF16)


def _hy_fwd(t_e, t_o, u, kspec, order, tm, tn):
    seq, half = t_e.shape
    ncol = u.shape[2]
    return pl.pallas_call(
        _hy_fwd_kernel,
        grid=(ncol // tn, seq // tm),
        in_specs=[
            pl.BlockSpec((tm, half), lambda j, i: (i, 0)),
            pl.BlockSpec((tm, half), lambda j, i: (i, 0)),
            pl.BlockSpec((2, half, tn), lambda j, i: (0, 0, j)),
            pl.BlockSpec((2, tm, MIXER_W), lambda j, i: (0, i, order)),
        ],
        out_specs=pl.BlockSpec((2, tm, tn), lambda j, i: (0, i, j)),
        out_shape=jax.ShapeDtypeStruct((2, seq, ncol), BF16),
        compiler_params=_cparams(("arbitrary", "arbitrary")),
        name="hy_fwd",
    )(t_e, t_o, u, kspec)


def _hy_inv_kernel(ie_ref, io_ref, y_ref, u_ref, gate_ref, skip_ref, *rest, final):
    reps = u_ref.shape[2] // MIXER_W
    skip = jnp.concatenate([skip_ref[...]] * reps, axis=-1)
    for par, inv_ref in enumerate((ie_ref, io_ref)):
        conv = jnp.dot(inv_ref[...], y_ref[par], preferred_element_type=F32)
        out = gate_ref[par] * (conv + u_ref[par].astype(F32) * skip)
        if final:
            g_ref, o_ref = rest
            for r in range(reps):
                o_ref[r, par] = _rms_gain(out[:, r * MIXER_W:(r + 1) * MIXER_W], g_ref[...]).astype(BF16)
        else:
            (o_ref,) = rest
            o_ref[par] = out.astype(BF16)


def _hy_inv(inv_e, inv_o, yspec, u, conv, part, skip, gain, final, tm, tn):
    half, seq = inv_e.shape
    ncol = u.shape[2]
    reps = tn // MIXER_W
    in_specs = [
        pl.BlockSpec((tm, seq), lambda j, i: (i, 0)),
        pl.BlockSpec((tm, seq), lambda j, i: (i, 0)),
        pl.BlockSpec((2, seq, tn), lambda j, i: (0, 0, j)),
        pl.BlockSpec((2, tm, tn), lambda j, i: (0, i, j)),
        pl.BlockSpec((None, 2, tm, tn), lambda j, i: (part, 0, i, j)),
        pl.BlockSpec((1, MIXER_W), lambda j, i: (0, 0)),
    ]
    args = [inv_e, inv_o, yspec, u, conv, skip]
    if final:
        in_specs.append(pl.BlockSpec((1, MIXER_W), lambda j, i: (0, 0)))
        args.append(gain)
        out_spec = pl.BlockSpec((reps, 2, tm, MIXER_W), lambda j, i: (j, 0, i, 0))
        out_shape = jax.ShapeDtypeStruct((ncol // MIXER_W, 2, half, MIXER_W), BF16)
    else:
        out_spec = pl.BlockSpec((2, tm, tn), lambda j, i: (0, i, j))
        out_shape = jax.ShapeDtypeStruct((2, half, ncol), BF16)
    return pl.pallas_call(
        functools.partial(_hy_inv_kernel, final=final),
        grid=(ncol // tn, half // tm),
        in_specs=in_specs,
        out_specs=out_spec,
        out_shape=out_shape,
        compiler_params=_cparams(("arbitrary", "arbitrary")),
        name="hy_inv",
    )(*args)


def _hy_tables(seq, tm):
    half = seq // 2
    hb = tm // 2
    nb = half // hb
    split = 64 if half % 64 == 0 else 1
    unit = 2.0 * math.pi / (4 * seq)

    def cos_sin(mult, n):
        ang = ((mult[:, None] * n[None, :]) % (4 * seq)).astype(F32) * unit
        return jnp.cos(ang), jnp.sin(ang)

    tabs = []
    for par in range(2):
        n = 2 * jnp.arange(half, dtype=jnp.int32) + par
        ch, sh = cos_sin(2 * split * jnp.arange(half // split, dtype=jnp.int32), n)
        cl, sl = cos_sin(2 * jnp.arange(split, dtype=jnp.int32) + 1, n)
        cos_t = (ch[:, None, :] * cl[None, :, :] - sh[:, None, :] * sl[None, :, :]).reshape(nb, hb, half)
        sin_t = (sh[:, None, :] * cl[None, :, :] + ch[:, None, :] * sl[None, :, :]).reshape(nb, hb, half)
        tabs.append(jnp.stack([cos_t, sin_t], axis=1).reshape(seq, half).astype(BF16))
    return tabs[0], tabs[1]


def _hy_features(seq):
    t = jnp.linspace(0.0, 1.0, seq, dtype=F32)[:, None]
    bands = (HY_EMB - 1) // 2
    freqs = jnp.linspace(1e-4, bands - 1, bands, dtype=F32)[None, :]
    ang = (2.0 * math.pi / seq) * jnp.arange(seq, dtype=F32)[:, None] * freqs
    feat = jnp.concatenate([t, jnp.cos(ang), -jnp.sin(ang)], -1)
    feat = jnp.pad(feat, ((0, 0), (0, HY_EMB_PAD - HY_EMB)))
    deltas = jnp.abs(jnp.linspace(math.log(HY_TARGET) / HY_SLOW, math.log(HY_TARGET) / HY_FAST,
                                  MIXER_W, dtype=F32))
    decay = jnp.exp(-t * deltas[None, :])
    split = lambda a: jnp.stack([a[0::2], a[1::2]], axis=0)
    return split(feat), split(decay)


def _hyena_mix(z, lp, gain, n_seq, seq, row0):
    half = seq // 2
    tm_f = min(512, seq)
    tm_i = min(256, half)
    ncol = n_seq * MIXER_W
    tn = min(1024, ncol)
    t_e, t_o = _hy_tables(seq, tm_f)
    inv_e, inv_o = t_e.T, t_o.T
    feat, decay = _hy_features(seq)
    ab = _hy_filters(feat, decay, lp["hy_w1p"], lp["hy_b1"], lp["hy_f1"], lp["hy_w2"], lp["hy_b2"],
                     lp["hy_f2"], lp["hy_w3"])
    kspec = _hy_kspec(t_e, t_o, ab, tm_f)
    conv, u = _hy_conv(z, lp["hy_conv_w"], lp["hy_conv_b"], n_seq, seq, row0)
    y0 = _hy_fwd(t_e, t_o, u, kspec, 0, tm_f, tn)
    u1 = _hy_inv(inv_e, inv_o, y0, u, conv, 0, lp["hy_skip"][0:1], None, False, tm_i, tn)
    y1 = _hy_fwd(t_e, t_o, u1, kspec, 1, tm_f, tn)
    out = _hy_inv(inv_e, inv_o, y1, u1, conv, 1, lp["hy_skip"][1:2], gain, True, tm_i, tn)
    return jnp.transpose(out, (0, 2, 1, 3)).reshape(n_seq * seq, MIXER_W)


def _gelu(x):
    c = math.sqrt(2.0 / math.pi)
    return 0.5 * x * (1.0 + jnp.tanh(c * (x + 0.044715 * (x * x * x))))


def _gmlp_kernel(u_ref, v_ref, lg_ref, lb_ref, ws_ref, bs_ref, g_ref, o_ref):
    tl = u_ref.shape[0]
    gw = MIXER_W // GMLP_GROUPS
    lane_grp = lax.broadcasted_iota(jnp.int32, (GMLP_CHUNK, MIXER_W), 1) // gw
    for c in range(tl // GMLP_CHUNK):
        rows = slice(c * GMLP_CHUNK, (c + 1) * GMLP_CHUNK)
        v = _ln(_gelu(v_ref[rows, :])) * lg_ref[...] + lb_ref[...]
        vb = v.astype(BF16)
        stacked = jnp.concatenate(
            [jnp.where(lane_grp == g, vb, jnp.zeros_like(vb)) for g in range(GMLP_GROUPS)], axis=0)
        s = jnp.dot(ws_ref[...], stacked, preferred_element_type=F32) + bs_ref[...]
        y = _gelu(u_ref[rows, :]) * s
        o_ref[rows, :] = _rms_gain(y, g_ref[...]).astype(BF16)


def _gmlp_mix(z, ln_g, ln_b, ws_cat, bs_mat, gain, n_rows):
    tl = 512
    full = lambda shape: pl.BlockSpec(shape, lambda i: tuple(0 for _ in shape))
    return pl.pallas_call(
        _gmlp_kernel,
        grid=(n_rows // tl,),
        in_specs=[
            pl.BlockSpec((tl, MIXER_W), lambda i: (i, COL_GU)),
            pl.BlockSpec((tl, MIXER_W), lambda i: (i, COL_GV)),
            full((1, MIXER_W)), full((1, MIXER_W)),
            full((GMLP_CHUNK, GMLP_GROUPS * GMLP_CHUNK)),
            full((GMLP_CHUNK, MIXER_W)),
            full((1, MIXER_W)),
        ],
        out_specs=pl.BlockSpec((tl, MIXER_W), lambda i: (i, 0)),
        out_shape=jax.ShapeDtypeStruct((n_rows, MIXER_W), BF16),
        compiler_params=_cparams(("arbitrary",)),
        name="gmlp_mix",
    )(z, z, ln_g, ln_b, ws_cat, bs_mat, gain)


def _merge_kernel(yp_ref, ya_ref, yh_ref, yg_ref, w_ref, x_ref, g1_ref, lg_ref, lb_ref, sh_ref, sc_ref,
                  wrh_ref, wrl_ref, rb_ref, xo_ref, h_ref, idx_ref, gate_ref, y_a, y_b, *, alpha):
    s = pl.program_id(0)

    @pl.when(s == 0)
    def _():
        y_b[...] = jnp.zeros_like(y_b)

    def step(y_new, y_old):
        ymix = jnp.concatenate([yp_ref[...], ya_ref[...], yh_ref[...], yg_ref[...]], axis=-1)
        y_new[...] = jnp.dot(ymix, w_ref[...], preferred_element_type=F32)

        x1 = _ln(alpha * x_ref[...] + g1_ref[...] * y_old[...]) * lg_ref[...] + lb_ref[...]
        xo_ref[...] = x1
        h = _ln(x1) * (1.0 + sc_ref[...]) + sh_ref[...]
        h_ref[...] = h
        h_hi = h.astype(BF16)
        h_lo = (h - h_hi.astype(F32)).astype(BF16)
        logits = (jnp.dot(h_hi, wrh_ref[...], preferred_element_type=F32)
                  + (jnp.dot(h_lo, wrh_ref[...], preferred_element_type=F32)
                     + jnp.dot(h_hi, wrl_ref[...], preferred_element_type=F32)))
        idx, gate = _route_rows(logits, rb_ref[...])
        idx_ref[...] = idx
        gate_ref[...] = gate

    even = lax.rem(s, 2) == 0
    pl.when(even)(functools.partial(step, y_a, y_b))
    pl.when(jnp.logical_not(even))(functools.partial(step, y_b, y_a))


def _route_rows(logits, bias):
    neg = -1e30
    lane = lax.broadcasted_iota(jnp.int32, logits.shape, 1)
    lane_f = lane.astype(F32)
    valid = lane < N_EXPERTS
    per = N_EXPERTS // N_EXPERT_GROUPS
    s = jax.nn.sigmoid(logits)
    sel = jnp.where(valid, s + bias, neg)
    sh = [sel] + [pltpu.roll(sel, LOGIT_PAD - j, axis=1) for j in range(1, per)]
    pair = None
    for a in range(per):
        for b in range(a + 1, per):
            t = sh[a] + sh[b]
            pair = t if pair is None else jnp.maximum(pair, t)
    grp = jnp.where(valid & ((lane & (per - 1)) == 0), pair, neg)

    def first_max(v):
        m = jnp.max(v, axis=-1, keepdims=True)
        return jnp.min(jnp.where(v == m, lane_f, float(LOGIT_PAD)), axis=-1, keepdims=True)

    best = first_max(grp).astype(jnp.int32)
    shift = per.bit_length() - 1
    cand = jnp.where(valid & ((lane >> shift) == (best >> shift)), sel, neg)
    i1 = first_max(cand)
    i2 = first_max(jnp.where(lane_f == i1, neg, cand))
    w1 = jnp.sum(jnp.where(lane_f == i1, s, 0.0), axis=-1, keepdims=True)
    w2 = jnp.sum(jnp.where(lane_f == i2, s, 0.0), axis=-1, keepdims=True)
    tot = w1 + w2
    idx = jnp.where(lane == 0, i1, jnp.where(lane == 1, i2, 0.0)).astype(jnp.int32)
    gate = jnp.where(lane == 0, w1 / tot, jnp.where(lane == 1, w2 / tot, 0.0))
    return idx, gate


def _merge(ys, w_out_bf, xa, mod3, ln_g, ln_b, wr_hi, wr_lo, rb_pad, n_rows, n_batch, seq, alpha):
    d = xa.shape[1]
    tm = 512
    nt = n_rows // tm

    def cur(i):
        return jnp.minimum(i, nt - 1)

    def prv(i):
        return jnp.maximum(i - 1, 0)

    def mrow(i):
        return jnp.minimum((prv(i) * tm) // seq, n_batch)

    ymix = pl.BlockSpec((tm, MIXER_W), lambda i: (cur(i), 0))
    rowvec = pl.BlockSpec((1, d), lambda i: (0, 0))
    return pl.pallas_call(
        functools.partial(_merge_kernel, alpha=alpha),
        grid=(nt + 1,),
        in_specs=[
            ymix, ymix, ymix, ymix,
            pl.BlockSpec((d, d), lambda i: (0, 0)),
            pl.BlockSpec((tm, d), lambda i: (prv(i), 0)),
            pl.BlockSpec((None, 1, d), lambda i: (mrow(i), 0, 2)),
            rowvec, rowvec,
            pl.BlockSpec((None, 1, d), lambda i: (mrow(i), 0, 3)),
            pl.BlockSpec((None, 1, d), lambda i: (mrow(i), 0, 4)),
            pl.BlockSpec((d, LOGIT_PAD), lambda i: (0, 0)),
            pl.BlockSpec((d, LOGIT_PAD), lambda i: (0, 0)),
            pl.BlockSpec((1, LOGIT_PAD), lambda i: (0, 0)),
        ],
        out_specs=[
            pl.BlockSpec((tm, d), lambda i: (prv(i), 0)),
            pl.BlockSpec((tm, d), lambda i: (prv(i), 0)),
            pl.BlockSpec((tm, LOGIT_PAD), lambda i: (prv(i), 0)),
            pl.BlockSpec((tm, LOGIT_PAD), lambda i: (prv(i), 0)),
        ],
        out_shape=[
            jax.ShapeDtypeStruct((n_rows, d), F32),
            jax.ShapeDtypeStruct((n_rows, d), F32),
            jax.ShapeDtypeStruct((n_rows, LOGIT_PAD), jnp.int32),
            jax.ShapeDtypeStruct((n_rows, LOGIT_PAD), F32),
        ],
        scratch_shapes=[pltpu.VMEM((tm, d), F32), pltpu.VMEM((tm, d), F32)],
        compiler_params=_cparams(("arbitrary",)),
        name="merge",
    )(*ys, w_out_bf, xa, mod3, ln_g, ln_b, mod3, mod3, wr_hi, wr_lo, rb_pad)


def _moe_kernel(be_ref, nu_ref, dest_ref, ws_ref, nxt_ref, h_hbm, wg_hbm, wu_hbm, wd_hbm, o_ref,
                xbuf, sem, tok_ref, stage_g, stage_u, stage_d, wsem, wg_s, wu_s, wd_s, *, layer):
    i = pl.program_id(0)
    n_used = nu_ref[0]
    ring = xbuf.shape[0]
    slot = lax.rem(i, ring)

    @pl.when(i == 0)
    def _():
        def clear(s, carry):
            tok_ref[s] = 0
            return carry

        def place(a, carry):
            tok_ref[dest_ref[a]] = lax.shift_right_logical(a, TOP_K.bit_length() - 1)
            return carry

        lax.fori_loop(0, tok_ref.shape[0], clear, 0, unroll=8)
        lax.fori_loop(0, dest_ref.shape[0], place, 0, unroll=8)

    def row_copy(blk, r, sl):
        tok = tok_ref[blk * MOE_BLOCK + r]
        return pltpu.make_async_copy(h_hbm.at[pl.ds(tok, 1)], xbuf.at[sl, pl.ds(r, 1)], sem.at[sl])

    def gather_start(blk, sl):
        def body(r, carry):
            row_copy(blk, r, sl).start()
            return carry

        lax.fori_loop(0, MOE_BLOCK, body, 0, unroll=8)

    def gather_wait(sl):
        pltpu.make_async_copy(h_hbm.at[pl.ds(0, MOE_BLOCK)], xbuf.at[sl], sem.at[sl]).wait()

    def weight_copies(e, ws):
        return (pltpu.make_async_copy(wg_hbm.at[layer, e], stage_g.at[ws], wsem.at[ws]),
                pltpu.make_async_copy(wu_hbm.at[layer, e], stage_u.at[ws], wsem.at[ws]),
                pltpu.make_async_copy(wd_hbm.at[layer, e], stage_d.at[ws], wsem.at[ws]))

    @pl.when(i == 0)
    def _():
        for cp in weight_copies(be_ref[0], ws_ref[0]):
            cp.start(priority=1)
        gather_start(0, 0)

        @pl.when(n_used > 1)
        def _():
            gather_start(1, 1)

    active = i < n_used
    new_expert = jnp.logical_or(i == 0, be_ref[i] != be_ref[jnp.maximum(i - 1, 0)])

    @pl.when(jnp.logical_and(active, new_expert))
    def _():
        ws = ws_ref[i]
        for cp in weight_copies(be_ref[i], ws):
            cp.wait()
        nxt = nxt_ref[i]

        @pl.when(nxt >= 0)
        def _():
            for cp in weight_copies(nxt, 1 - ws):
                cp.start(priority=1)

        for src, dst in ((stage_g, wg_s), (stage_u, wu_s), (stage_d, wd_s)):
            def cast(r0, src=src, dst=dst):
                dst[pl.ds(r0, 256), :] = src[ws, pl.ds(r0, 256), :].astype(BF16)

            _row_chunks(dst.shape[0], 256, cast)

    def ffn_block(prefetch_next):
        gather_wait(slot)
        x = xbuf[slot].astype(BF16)
        if prefetch_next:
            nxt = lax.rem(i + 2, ring)
            for r in range(MOE_BLOCK):
                row_copy(i + 2, r, nxt).start()
        g = jnp.dot(x, wg_s[...], preferred_element_type=F32)
        u = jnp.dot(x, wu_s[...], preferred_element_type=F32)
        a = (g * jax.nn.sigmoid(g) * u).astype(BF16)
        o_ref[...] = jnp.dot(a, wd_s[...], preferred_element_type=F32).astype(o_ref.dtype)

    has_next = i + 2 < n_used
    pl.when(jnp.logical_and(active, has_next))(functools.partial(ffn_block, True))
    pl.when(jnp.logical_and(active, jnp.logical_not(has_next)))(functools.partial(ffn_block, False))

    @pl.when(jnp.logical_not(active))
    def _():
        o_ref[...] = jnp.zeros_like(o_ref)


def _moe_experts(h, dest, plan, wg, wu, wd, layer):
    d = h.shape[1]
    n_slots = plan["n_slots"]
    de = wg.shape[3]
    n_blocks = n_slots // MOE_BLOCK
    hbm = pl.BlockSpec(memory_space=pl.ANY)
    grid_spec = pltpu.PrefetchScalarGridSpec(
        num_scalar_prefetch=5,
        grid=(n_blocks,),
        in_specs=[hbm, hbm, hbm, hbm],
        out_specs=pl.BlockSpec((MOE_BLOCK, d), lambda i, *_: (i, 0)),
        scratch_shapes=[
            pltpu.VMEM((3, MOE_BLOCK, d), F32),
            pltpu.SemaphoreType.DMA((3,)),
            pltpu.SMEM((n_slots,), jnp.int32),
            pltpu.VMEM((2, d, de), F32), pltpu.VMEM((2, d, de), F32), pltpu.VMEM((2, de, d), F32),
            pltpu.SemaphoreType.DMA((2,)),
            pltpu.VMEM((d, de), BF16), pltpu.VMEM((d, de), BF16), pltpu.VMEM((de, d), BF16),
        ],
    )
    return pl.pallas_call(
        functools.partial(_moe_kernel, layer=layer),
        grid_spec=grid_spec,
        out_shape=jax.ShapeDtypeStruct((n_slots, d), BF16),
        compiler_params=_cparams(("arbitrary",), MOE_VMEM_LIMIT),
        name="moe_experts",
    )(plan["block_e"], plan["n_used"], dest.reshape(-1), plan["wslot"], plan["next_e"], h, wg, wu, wd)


def _assignment_ranks(flat_e):
    a = flat_e.shape[0]
    blk = 128
    nb = a // blk
    onehot = (flat_e[:, None] == jnp.arange(N_EXPERTS)[None, :]).astype(BF16).reshape(nb, blk, N_EXPERTS)
    tri = (jnp.arange(blk)[:, None] >= jnp.arange(blk)[None, :]).astype(BF16)
    intra = jnp.einsum("ij,bjk->bik", tri, onehot, preferred_element_type=F32)
    bsum = intra[:, -1, :]
    before = (jnp.arange(nb)[:, None] > jnp.arange(nb)[None, :]).astype(BF16)
    offs = jnp.dot(before, bsum.astype(BF16), preferred_element_type=F32)
    csum = intra + offs[:, None, :]
    rank = jnp.sum(csum * onehot.astype(F32), axis=-1).reshape(a) - 1.0
    counts = offs[-1] + bsum[-1]
    return rank.astype(jnp.int32), counts.astype(jnp.int32)


def _dispatch_plan(idx):
    t = idx.shape[0]
    a = t * TOP_K
    flat_e = idx.reshape(a)
    rank, counts = _assignment_ranks(flat_e)
    padded = (counts + MOE_BLOCK - 1) // MOE_BLOCK * MOE_BLOCK
    pad_end = jnp.cumsum(padded)
    pad_start = pad_end - padded
    dest = pad_start[flat_e] + rank
    n_blocks = (a + N_EXPERTS * (MOE_BLOCK - 1) + MOE_BLOCK - 1) // MOE_BLOCK
    n_slots = n_blocks * MOE_BLOCK
    blk_start = jnp.arange(n_blocks, dtype=jnp.int32) * MOE_BLOCK
    block_e = jnp.minimum(
        jnp.sum((pad_end[None, :] <= blk_start[:, None]).astype(jnp.int32), axis=1), N_EXPERTS - 1)
    n_used = (pad_end[-1] // MOE_BLOCK).astype(jnp.int32).reshape(1)
    nonempty = counts > 0
    eid = jnp.arange(N_EXPERTS, dtype=jnp.int32)
    run_of_e = jnp.cumsum(nonempty.astype(jnp.int32)) - 1
    later = jnp.logical_and(eid[None, :] > eid[:, None], nonempty[None, :])
    nxt_of_e = jnp.min(jnp.where(later, eid[None, :], N_EXPERTS), axis=1)
    nxt_of_e = jnp.where(nxt_of_e == N_EXPERTS, -1, nxt_of_e)
    plan = {
        "n_slots": n_slots, "block_e": block_e.astype(jnp.int32), "n_used": n_used,
        "wslot": (run_of_e[block_e] % 2).astype(jnp.int32), "next_e": nxt_of_e[block_e].astype(jnp.int32),
    }
    return plan, dest.reshape(t, TOP_K)


def _final_kernel(x_ref, ya_ref, yb_ref, gate_ref, g2_ref, lg_ref, lb_ref, o_ref, *, alpha):
    f = ya_ref[...].astype(F32) * gate_ref[:, 0:1] + yb_ref[...].astype(F32) * gate_ref[:, 1:2]
    o_ref[...] = _ln(alpha * x_ref[...] + g2_ref[...] * f) * lg_ref[...] + lb_ref[...]


def _final_norm(x1, ya, yb, gate, mod3, ln_g, ln_b, n_batch, seq, alpha):
    n_rows, d = x1.shape
    tm = 512

    def mrow(i):
        return jnp.minimum((i * tm) // seq, n_batch)

    tile = pl.BlockSpec((tm, d), lambda i: (i, 0))
    rowvec = pl.BlockSpec((1, d), lambda i: (0, 0))
    return pl.pallas_call(
        functools.partial(_final_kernel, alpha=alpha),
        grid=(n_rows // tm,),
        in_specs=[tile, tile, tile, pl.BlockSpec((tm, LOGIT_PAD), lambda i: (i, 0)),
                  pl.BlockSpec((None, 1, d), lambda i: (mrow(i), 0, 5)), rowvec, rowvec],
        out_specs=tile,
        out_shape=jax.ShapeDtypeStruct((n_rows, d), F32),
        compiler_params=_cparams(("arbitrary",)),
        name="final_norm",
    )(x1, ya, yb, gate, mod3, ln_g, ln_b)


def _permute_in_cols(w):
    pool, q, k, v, hy, gm = jnp.split(w, (512, 1024, 1152, 1280, 2816), axis=-1)
    return jnp.concatenate([hy, pool, q, gm, k, v], axis=-1)


def kernel(x, c, ctx, c_ctx, w_ada, b_ada, w_in, w_out, mix_norm_g, pool_w, pool_scale, attn_sink,
           hy_conv_w, hy_conv_b, hy_w1, hy_b1, hy_f1, hy_w2, hy_b2, hy_f2, hy_w3, hy_skip,
           gm_ln_g, gm_ln_b, gm_ws, gm_bs, ln1_g, ln1_b, ln2_g, ln2_b, w_router, router_bias,
           w_gate, w_up, w_down):
    n_batch, seq, d = x.shape
    ctx_len = ctx.shape[1]
    depth = w_in.shape[0]
    grid_w = 64
    alpha = (2 * depth) ** 0.25
    n_lat = n_batch * seq
    n_ctx = n_batch * ctx_len

    xa = jnp.concatenate([x.reshape(n_lat, d), ctx.reshape(n_ctx, d)], axis=0)
    cvec = jnp.concatenate([c, c_ctx[None, :], jnp.zeros((8 - n_batch - 1, d), F32)], axis=0)
    mods = _ada_mods(cvec, w_ada, b_ada)
    cos_t, sin_t = _rope_tables(seq, grid_w)
    wr_pad = jnp.pad(w_router, ((0, 0), (0, LOGIT_PAD - N_EXPERTS)))
    wr_hi = wr_pad.astype(BF16)
    wr_lo = (wr_pad - wr_hi.astype(F32)).astype(BF16)
    rb_pad = jnp.pad(router_bias.astype(F32), (0, LOGIT_PAD - N_EXPERTS)).reshape(1, LOGIT_PAD)
    row = lambda v: v.reshape(1, -1)

    for layer in range(depth):
        last = layer == depth - 1
        mod3 = mods[layer].reshape(8, 1, 6 * d)
        gains = mix_norm_g[layer].reshape(4, 1, MIXER_W)
        lp = {
            "hy_conv_w": hy_conv_w[layer], "hy_conv_b": row(hy_conv_b[layer]),
            "hy_w1p": jnp.pad(hy_w1[layer], ((0, HY_EMB_PAD - HY_EMB), (0, 0))),
            "hy_b1": row(hy_b1[layer]), "hy_f1": row(hy_f1[layer]),
            "hy_w2": hy_w2[layer], "hy_b2": row(hy_b2[layer]), "hy_f2": row(hy_f2[layer]),
            "hy_w3": hy_w3[layer], "hy_skip": hy_skip[layer],
        }
        z = _in_proj(xa, mod3, _permute_in_cols(w_in[layer]).astype(BF16), n_batch, seq)

        pool_bf = pool_w[layer].astype(BF16)
        ps = row(pool_scale[layer])
        ws_cat = jnp.transpose(gm_ws[layer], (1, 0, 2)).reshape(GMLP_CHUNK, GMLP_GROUPS * GMLP_CHUNK)
        bs_mat = jnp.repeat(gm_bs[layer].T, MIXER_W // GMLP_GROUPS, axis=1)
        n_rows = n_lat if last else n_lat + n_ctx

        y_pool = _pool_mix(z, pool_bf, ps, gains[0], n_batch, seq, 0)
        y_attn = _attn_lat(z, attn_sink[layer], cos_t, sin_t, gains[1], n_batch, seq, ctx_len)
        y_hy = _hyena_mix(z, lp, gains[2], n_batch, seq, 0)
        y_gm = _gmlp_mix(z, row(gm_ln_g[layer]), row(gm_ln_b[layer]), ws_cat.astype(BF16), bs_mat,
                         gains[3], n_rows)
        if not last:
            y_pool = jnp.concatenate(
                [y_pool, _pool_mix(z, pool_bf, ps, gains[0], n_batch, ctx_len, n_lat // ctx_len)], axis=0)
            y_attn = jnp.concatenate(
                [y_attn, _attn_ctx(z, attn_sink[layer], gains[1], n_batch, seq, ctx_len)], axis=0)
            y_hy = jnp.concatenate([y_hy, _hyena_mix(z, lp, gains[2], n_batch, ctx_len, n_lat)], axis=0)

        x1, h2, idx, gate = _merge((y_pool, y_attn, y_hy, y_gm), w_out[layer].astype(BF16), xa, mod3,
                                   row(ln1_g[layer]), row(ln1_b[layer]), wr_hi, wr_lo, rb_pad,
                                   n_rows, n_batch, seq, alpha)

        plan, dest = _dispatch_plan(idx[:, :TOP_K])
        ys = _moe_experts(h2, dest, plan, w_gate, w_up, w_down, layer)
        xa = _final_norm(x1, ys[dest[:, 0]], ys[dest[:, 1]], gate, mod3, row(ln2_g[layer]), row(ln2_b[layer]),
                         n_batch, seq, alpha)
    return xa.reshape(n_batch, seq, d)
```

```python
import functools
import math

import jax
import jax.numpy as jnp
from jax import lax
from jax.experimental import pallas as pl
from jax.experimental.pallas import tpu as pltpu

F32 = jnp.float32
BF16 = jnp.bfloat16
HI = lax.Precision.HIGHEST

LN_EPS = 1e-6
MIXER_W = 512
HEAD_DIM = 64
N_HEADS = 8
N_KV_HEADS = 2
KV_GROUP = 4
KV_W = 128
WINDOW = 128
ROPE_THETA = 10000.0
POOL_WINDOWS = (2, 4, 8, 16)
POOL_GROUP = 128
POOL_HALO = 8
GMLP_CHUNK = 128
GMLP_GROUPS = 8
HY_EMB = 33
HY_EMB_PAD = 128
HY_TARGET = 1e-2
HY_FAST = 0.3
HY_SLOW = 1.5
N_EXPERTS = 32
N_EXPERT_GROUPS = 8
TOP_K = 2
D_EXPERT = 768
MOE_BLOCK = 256
LOGIT_PAD = 128

COL_HY = 0
COL_POOL = 3
COL_Q = 4
COL_GU = 5
COL_GV = 6
COL_K = 28
COL_V = 29
IN_W = 3840

VMEM_LIMIT = 56 * 1024 * 1024
MOE_VMEM_LIMIT = 59 * 1024 * 1024


def _cparams(sem, vmem_limit=VMEM_LIMIT):
    return pltpu.CompilerParams(dimension_semantics=sem, vmem_limit_bytes=vmem_limit)


def _ln(x):
    mu = jnp.mean(x, axis=-1, keepdims=True)
    xc = x - mu
    var = jnp.mean(xc * xc, axis=-1, keepdims=True)
    return xc * lax.rsqrt(var + LN_EPS)


def _rms_gain(y, g):
    return y * lax.rsqrt(jnp.mean(y * y, axis=-1, keepdims=True) + LN_EPS) * g


def _row_chunks(n_rows, chunk, body):
    n = n_rows // chunk
    if n == 1:
        body(0)
        return

    def step(i, carry):
        body(pl.multiple_of(i * chunk, chunk))
        return carry

    lax.fori_loop(0, n, step, 0)


def _ada_kernel(c_ref, w_ref, b_ref, o_ref):
    c = c_ref[...]
    s = c * jax.nn.sigmoid(c)
    o_ref[...] = jnp.dot(s, w_ref[...], precision=HI, preferred_element_type=F32) + b_ref[...]


def _ada_mods(cvec, w_ada, b_ada):
    depth, d, n = w_ada.shape
    tn = 1024
    return pl.pallas_call(
        _ada_kernel,
        grid=(depth, n // tn),
        in_specs=[
            pl.BlockSpec((8, d), lambda l, j: (0, 0)),
            pl.BlockSpec((None, d, tn), lambda l, j: (l, 0, j)),
            pl.BlockSpec((None, 1, tn), lambda l, j: (l, 0, j)),
        ],
        out_specs=pl.BlockSpec((None, 8, tn), lambda l, j: (l, 0, j)),
        out_shape=jax.ShapeDtypeStruct((depth, 8, n), F32),
        compiler_params=_cparams(("arbitrary", "arbitrary")),
        name="ada_mods",
    )(cvec, w_ada, b_ada.reshape(depth, 1, n))


def _inproj_kernel(x_ref, sh_ref, sc_ref, w_ref, o_ref, h_scr):
    @pl.when(pl.program_id(1) == 0)
    def _():
        sh = sh_ref[...]
        sc1 = 1.0 + sc_ref[...]

        def body(r0):
            x = x_ref[pl.ds(r0, 256), :]
            h_scr[pl.ds(r0, 256), :] = (_ln(x) * sc1 + sh).astype(BF16)

        _row_chunks(x_ref.shape[0], 256, body)

    o_ref[...] = jnp.dot(h_scr[...], w_ref[...], preferred_element_type=F32)


def _in_proj(xa, mod3, w_bf, n_batch, seq):
    t, d = xa.shape
    n = w_bf.shape[1]
    tn = 768
    tm = next(m for m in (1024, 512, 256) if t % m == 0 and seq % m == 0)

    def mrow(i):
        return jnp.minimum((i * tm) // seq, n_batch)

    return pl.pallas_call(
        _inproj_kernel,
        grid=(t // tm, n // tn),
        in_specs=[
            pl.BlockSpec((tm, d), lambda i, j: (i, 0)),
            pl.BlockSpec((None, 1, d), lambda i, j: (mrow(i), 0, 0)),
            pl.BlockSpec((None, 1, d), lambda i, j: (mrow(i), 0, 1)),
            pl.BlockSpec((d, tn), lambda i, j: (0, j)),
        ],
        out_specs=pl.BlockSpec((tm, tn), lambda i, j: (i, j)),
        out_shape=jax.ShapeDtypeStruct((t, n), F32),
        scratch_shapes=[pltpu.VMEM((tm, d), BF16)],
        compiler_params=_cparams(("arbitrary", "arbitrary")),
        name="in_proj",
    )(xa, mod3, mod3, w_bf)


def _pool_kernel(z_ref, w_ref, ps_ref, g_ref, o_ref, buf):
    seq = z_ref.shape[0]
    rc = min(256, seq)
    zeros = jnp.zeros((POOL_HALO, MIXER_W), F32)
    buf[0:POOL_HALO, :] = zeros
    buf[seq + POOL_HALO:seq + 2 * POOL_HALO, :] = zeros

    def fill(r0):
        buf[pl.ds(r0 + POOL_HALO, rc), :] = z_ref[pl.ds(r0, rc), :]

    _row_chunks(seq, rc, fill)

    def body(r0):
        win = buf[pl.ds(r0, rc + 2 * POOL_HALO), :]
        t = r0 + lax.broadcasted_iota(jnp.int32, (rc, 1), 0)
        outs = []
        for gi, w in enumerate(POOL_WINDOWS):
            lanes = slice(gi * POOL_GROUP, (gi + 1) * POOL_GROUP)
            acc = None
            for j in range(-w // 2, w // 2):
                piece = win[POOL_HALO + j:POOL_HALO + j + rc, lanes]
                acc = piece if acc is None else acc + piece
            cnt = (jnp.minimum(t + w // 2, seq) - jnp.maximum(t - w // 2, 0)).astype(F32)
            dlt = acc / cnt - win[POOL_HALO:POOL_HALO + rc, lanes]
            outs.append(jnp.dot(dlt.astype(BF16), w_ref[gi], preferred_element_type=F32))
        y = jnp.concatenate(outs, axis=-1) * ps_ref[...]
        o_ref[pl.ds(r0, rc), :] = _rms_gain(y, g_ref[...]).astype(BF16)

    _row_chunks(seq, rc, body)


def _pool_mix(z, pool_w_bf, pool_scale, gain, n_seq, seq, row_blk0):
    return pl.pallas_call(
        _pool_kernel,
        grid=(n_seq,),
        in_specs=[
            pl.BlockSpec((seq, MIXER_W), lambda s: (row_blk0 + s, COL_POOL)),
            pl.BlockSpec((4, POOL_GROUP, POOL_GROUP), lambda s: (0, 0, 0)),
            pl.BlockSpec((1, MIXER_W), lambda s: (0, 0)),
            pl.BlockSpec((1, MIXER_W), lambda s: (0, 0)),
        ],
        out_specs=pl.BlockSpec((seq, MIXER_W), lambda s: (s, 0)),
        out_shape=jax.ShapeDtypeStruct((n_seq * seq, MIXER_W), BF16),
        scratch_shapes=[pltpu.VMEM((seq + 2 * POOL_HALO, MIXER_W), F32)],
        compiler_params=_cparams(("arbitrary",)),
        name="pool_mix",
    )(z, pool_w_bf, pool_scale, gain)


def _rope(x, cos, sin_signed):
    lane = lax.broadcasted_iota(jnp.int32, x.shape, 1)
    partner = jnp.where((lane & 31) < 16, pltpu.roll(x, 112, axis=1), pltpu.roll(x, 16, axis=1))
    return x * cos + partner * sin_signed


def _attend(qs, sink_col, parts):
    dn = (((1,), (1,)), ((), ()))
    scores = []
    for kk, _, mask in parts:
        s = lax.dot_general(qs, kk, dn, preferred_element_type=F32)
        if mask is not None:
            s = s + mask
        scores.append(s)
    m = sink_col
    for s in scores:
        m = jnp.maximum(m, jnp.max(s, axis=-1, keepdims=True))
    den = jnp.exp(sink_col - m)
    out = None
    for s, (_, vv, _) in zip(scores, parts):
        p = jnp.exp(s - m)
        den = den + jnp.sum(p, axis=-1, keepdims=True)
        o = jnp.dot(p.astype(BF16), vv, preferred_element_type=F32)
        out = o if out is None else out + o
    return out / den


def _heads_attend(q_all, sink_ref, tq, make_parts):
    cols = [None] * N_HEADS
    for kh in range(N_KV_HEADS):
        heads = [kh * KV_GROUP + g for g in range(KV_GROUP)]
        qs = jnp.concatenate([q_all[:, h * HEAD_DIM:(h + 1) * HEAD_DIM] for h in heads], axis=0)
        sink_col = jnp.concatenate([jnp.full((tq, 1), sink_ref[h], F32) for h in heads], axis=0)
        o = _attend(qs.astype(BF16), sink_col, make_parts(kh))
        for g, h in enumerate(heads):
            cols[h] = o[g * tq:(g + 1) * tq, :]
    return jnp.concatenate(cols, axis=-1)


def _attn_lat_kernel(sink_ref, q_ref, k_ref, v_ref, kc_ref, vc_ref, cos_ref, sin_ref, g_ref, o_ref):
    tq = q_ref.shape[0]
    seq = k_ref.shape[0]
    nband = tq + 2 * WINDOW
    i = pl.program_id(1)
    q0 = pl.multiple_of(i * tq, tq)
    k0 = pl.multiple_of(jnp.clip(q0 - WINDOW, 0, seq - nband), WINDOW)

    cq = cos_ref[pl.ds(q0, tq), :]
    sq = sin_ref[pl.ds(q0, tq), :]
    q = q_ref[...]
    q_all = jnp.concatenate(
        [_rope(q[:, c * 128:(c + 1) * 128], cq, sq) for c in range(MIXER_W // 128)], axis=-1)
    q_all = q_all * (HEAD_DIM ** -0.5)

    kb = _rope(k_ref[pl.ds(k0, nband), :], cos_ref[pl.ds(k0, nband), :],
               sin_ref[pl.ds(k0, nband), :]).astype(BF16)
    vb = v_ref[pl.ds(k0, nband), :].astype(BF16)
    kc = kc_ref[...].astype(BF16)
    vc = vc_ref[...].astype(BF16)

    qpos = q0 + lax.broadcasted_iota(jnp.int32, (tq, nband), 0)
    kpos = k0 + lax.broadcasted_iota(jnp.int32, (tq, nband), 1)
    band = jnp.where(jnp.abs(qpos - kpos) <= WINDOW, 0.0, -1e30)
    mask = jnp.concatenate([band] * KV_GROUP, axis=0)

    def make_parts(kh):
        hs = slice(kh * HEAD_DIM, (kh + 1) * HEAD_DIM)
        return [(kb[:, hs], vb[:, hs], mask), (kc[:, hs], vc[:, hs], None)]

    y = _heads_attend(q_all, sink_ref, tq, make_parts)
    o_ref[...] = _rms_gain(y, g_ref[...]).astype(BF16)


def _attn_ctx_kernel(sink_ref, q_ref, kc_ref, vc_ref, g_ref, o_ref):
    tq = q_ref.shape[0]
    q_all = q_ref[...] * (HEAD_DIM ** -0.5)
    kc = kc_ref[...].astype(BF16)
    vc = vc_ref[...].astype(BF16)

    def make_parts(kh):
        hs = slice(kh * HEAD_DIM, (kh + 1) * HEAD_DIM)
        return [(kc[:, hs], vc[:, hs], None)]

    y = _heads_attend(q_all, sink_ref, tq, make_parts)
    o_ref[...] = _rms_gain(y, g_ref[...]).astype(BF16)


def _attn_lat(z, sink, cos_t, sin_t, gain, n_batch, seq, ctx_len):
    tq = 256
    nq = seq // tq
    ctx_blk0 = (n_batch * seq) // ctx_len
    smem = pl.BlockSpec(memory_space=pltpu.SMEM)
    return pl.pallas_call(
        _attn_lat_kernel,
        grid=(n_batch, nq),
        in_specs=[
            smem,
            pl.BlockSpec((tq, MIXER_W), lambda b, i: (b * nq + i, COL_Q)),
            pl.BlockSpec((seq, KV_W), lambda b, i: (b, COL_K)),
            pl.BlockSpec((seq, KV_W), lambda b, i: (b, COL_V)),
            pl.BlockSpec((ctx_len, KV_W), lambda b, i: (ctx_blk0 + b, COL_K)),
            pl.BlockSpec((ctx_len, KV_W), lambda b, i: (ctx_blk0 + b, COL_V)),
            pl.BlockSpec((seq, 128), lambda b, i: (0, 0)),
            pl.BlockSpec((seq, 128), lambda b, i: (0, 0)),
            pl.BlockSpec((1, MIXER_W), lambda b, i: (0, 0)),
        ],
        out_specs=pl.BlockSpec((tq, MIXER_W), lambda b, i: (b * nq + i, 0)),
        out_shape=jax.ShapeDtypeStruct((n_batch * seq, MIXER_W), BF16),
        compiler_params=_cparams(("arbitrary", "arbitrary")),
        name="attn_lat",
    )(sink, z, z, z, z, z, cos_t, sin_t, gain)


def _attn_ctx(z, sink, gain, n_batch, seq, ctx_len):
    ctx_blk0 = (n_batch * seq) // ctx_len
    smem = pl.BlockSpec(memory_space=pltpu.SMEM)
    return pl.pallas_call(
        _attn_ctx_kernel,
        grid=(n_batch,),
        in_specs=[
            smem,
            pl.BlockSpec((ctx_len, MIXER_W), lambda b: (ctx_blk0 + b, COL_Q)),
            pl.BlockSpec((ctx_len, KV_W), lambda b: (ctx_blk0 + b, COL_K)),
            pl.BlockSpec((ctx_len, KV_W), lambda b: (ctx_blk0 + b, COL_V)),
            pl.BlockSpec((1, MIXER_W), lambda b: (0, 0)),
        ],
        out_specs=pl.BlockSpec((ctx_len, MIXER_W), lambda b: (b, 0)),
        out_shape=jax.ShapeDtypeStruct((n_batch * ctx_len, MIXER_W), BF16),
        compiler_params=_cparams(("arbitrary",)),
        name="attn_ctx",
    )(sink, z, z, z, gain)


def _rope_tables(seq, grid_w):
    nf = HEAD_DIM // 4
    inv = ROPE_THETA ** (-jnp.arange(nf, dtype=F32) / nf)
    t = jnp.arange(seq)
    row = (t // grid_w).astype(F32)[:, None] * inv[None, :]
    col = (t % grid_w).astype(F32)[:, None] * inv[None, :]
    cos_h = jnp.concatenate([jnp.cos(row), jnp.cos(row), jnp.cos(col), jnp.cos(col)], axis=-1)
    sin_h = jnp.concatenate([-jnp.sin(row), jnp.sin(row), -jnp.sin(col), jnp.sin(col)], axis=-1)
    return jnp.tile(cos_h, (1, 2)), jnp.tile(sin_h, (1, 2))


def _hy_conv_kernel(*refs):
    z_refs, zp_refs, zn_refs = refs[0:3], refs[3:6], refs[6:9]
    w_ref, b_ref, g_ref, u_ref = refs[9:]
    i = pl.program_id(1)
    th = z_refs[0].shape[0] // 2
    has_prev = (i > 0).astype(F32)
    has_next = (i < pl.num_programs(1) - 1).astype(F32)
    row = lax.broadcasted_iota(jnp.int32, (th, 128), 0)
    for part in range(3):
        ze = z_refs[part][pl.ds(0, th, stride=2), :]
        zo = z_refs[part][pl.ds(1, th, stride=2), :]
        prev_row = zp_refs[part][POOL_HALO - 1:POOL_HALO, :] * has_prev
        next_row = zn_refs[part][0:1, :] * has_next
        zo_m = jnp.where(row == 0, prev_row, pltpu.roll(zo, 1, axis=0))
        ze_p = jnp.where(row == th - 1, next_row, pltpu.roll(ze, th - 1, axis=0))
        w0, w1, w2, b = w_ref[part, 0:1, :], w_ref[part, 1:2, :], w_ref[part, 2:3, :], b_ref[part]
        ye = b + zo_m * w0 + ze * w1 + zo * w2
        yo = b + ze * w0 + zo * w1 + ze_p * w2
        g_ref[part, 0] = ye
        g_ref[part, 1] = yo
        if part == 2:
            u_ref[0] = ye.astype(BF16)
            u_ref[1] = yo.astype(BF16)


def _hy_conv(z, conv_w, conv_b, n_seq, seq, row0):
    tl = min(2048, seq)
    nt = seq // tl
    blk0 = row0 // tl
    hb0 = row0 // POOL_HALO
    hpt = tl // POOL_HALO
    last_halo = (row0 + n_seq * seq) // POOL_HALO - 1
    cpp = MIXER_W // 128
    ncol = n_seq * MIXER_W

    def main_spec(part):
        return pl.BlockSpec((tl, 128), lambda s, i, c: (blk0 + s * nt + i, part * cpp + c))

    def prev_spec(part):
        return pl.BlockSpec((POOL_HALO, 128),
                            lambda s, i, c: (jnp.maximum(hb0 + (s * nt + i) * hpt - 1, 0), part * cpp + c))

    def next_spec(part):
        return pl.BlockSpec((POOL_HALO, 128),
                            lambda s, i, c: (jnp.minimum(hb0 + (s * nt + i + 1) * hpt, last_halo), part * cpp + c))

    w3 = conv_w.reshape(3, 3, MIXER_W).transpose(1, 0, 2)
    b3 = conv_b.reshape(3, 1, MIXER_W)
    return pl.pallas_call(
        _hy_conv_kernel,
        grid=(n_seq, nt, cpp),
        in_specs=[main_spec(p) for p in range(3)] + [prev_spec(p) for p in range(3)]
        + [next_spec(p) for p in range(3)]
        + [pl.BlockSpec((3, 3, 128), lambda s, i, c: (0, 0, c)),
           pl.BlockSpec((3, 1, 128), lambda s, i, c: (0, 0, c))],
        out_specs=[
            pl.BlockSpec((3, 2, tl // 2, 128), lambda s, i, c: (0, 0, i, s * cpp + c)),
            pl.BlockSpec((2, tl // 2, 128), lambda s, i, c: (0, i, s * cpp + c)),
        ],
        out_shape=[jax.ShapeDtypeStruct((3, 2, seq // 2, ncol), F32),
                   jax.ShapeDtypeStruct((2, seq // 2, ncol), BF16)],
        compiler_params=_cparams(("arbitrary", "arbitrary", "arbitrary")),
        name="hy_conv",
    )(*([z] * 9), w3, b3)


def _hy_filter_kernel(feat_ref, dec_ref, w1_ref, b1_ref, f1_ref, w2_ref, b2_ref, f2_ref, w3_ref, o_ref):
    tl = feat_ref.shape[0]
    h = jnp.dot(feat_ref[...], w1_ref[...], precision=HI, preferred_element_type=F32) + b1_ref[...]
    h = jnp.sin(f1_ref[...] * h)
    h = jnp.dot(h, w2_ref[...], precision=HI, preferred_element_type=F32) + b2_ref[...]
    h = jnp.sin(f2_ref[...] * h)
    filt = jnp.dot(h, w3_ref[...], precision=HI, preferred_element_type=F32)
    dec = dec_ref[...]
    t = 2 * (pl.program_id(1) * tl + lax.broadcasted_iota(jnp.int32, (tl, 1), 0)) + pl.program_id(0)
    not_first = (t > 0).astype(F32)
    for o in range(2):
        hf = filt[:, (2 * o) * MIXER_W:(2 * o + 1) * MIXER_W] * dec
        hb = filt[:, (2 * o + 1) * MIXER_W:(2 * o + 2) * MIXER_W] * dec * not_first
        o_ref[:, o * MIXER_W:(o + 1) * MIXER_W] = (hf + hb).astype(BF16)
        o_ref[:, (2 + o) * MIXER_W:(3 + o) * MIXER_W] = (hb - hf).astype(BF16)


def _hy_filters(feat, decay, w1p, b1, f1, w2, b2, f2, w3):
    half = feat.shape[1]
    tl = min(512, half)
    hid = w2.shape[0]
    full = lambda shape: pl.BlockSpec(shape, lambda p, i: tuple(0 for _ in shape))
    return pl.pallas_call(
        _hy_filter_kernel,
        grid=(2, half // tl),
        in_specs=[
            pl.BlockSpec((None, tl, HY_EMB_PAD), lambda p, i: (p, i, 0)),
            pl.BlockSpec((None, tl, MIXER_W), lambda p, i: (p, i, 0)),
            full((HY_EMB_PAD, hid)), full((1, hid)), full((1, hid)),
            full((hid, hid)), full((1, hid)), full((1, hid)),
            full((hid, 4 * MIXER_W)),
        ],
        out_specs=pl.BlockSpec((None, tl, 4 * MIXER_W), lambda p, i: (p, i, 0)),
        out_shape=jax.ShapeDtypeStruct((2, half, 4 * MIXER_W), BF16),
        compiler_params=_cparams(("arbitrary", "arbitrary")),
        name="hy_filters",
    )(feat, decay, w1p, b1, f1, w2, b2, f2, w3)


def _hy_kspec_kernel(te_ref, to_ref, ab_ref, o_ref, *, inv_len):
    hb = te_ref.shape[0] // 2
    half = ab_ref.shape[2] // 2
    ec = jnp.dot(te_ref[0:hb, :], ab_ref[0, :, 0:half], preferred_element_type=F32)
    oc = jnp.dot(to_ref[0:hb, :], ab_ref[1, :, 0:half], preferred_element_type=F32)
    es = jnp.dot(te_ref[hb:, :], ab_ref[0, :, half:], preferred_element_type=F32)
    os_ = jnp.dot(to_ref[hb:, :], ab_ref[1, :, half:], preferred_element_type=F32)
    o_ref[0, 0:hb, :] = (ec + oc) * inv_len
    o_ref[0, hb:, :] = (es + os_) * inv_len
    o_ref[1, 0:hb, :] = (ec - oc) * inv_len
    o_ref[1, hb:, :] = (os_ - es) * inv_len


def _hy_kspec(fwd, ab, tm):
    _, seq, half = fwd.shape
    return pl.pallas_call(
        functools.partial(_hy_kspec_kernel, inv_len=1.0 / seq),
        grid=(seq // tm,),
        in_specs=[
            pl.BlockSpec((None, tm, half), lambda i: (0, i, 0)),
            pl.BlockSpec((None, tm, half), lambda i: (1, i, 0)),
            pl.BlockSpec((2, half, 4 * MIXER_W), lambda i: (0, 0, 0)),
        ],
        out_specs=pl.BlockSpec((2, tm, 2 * MIXER_W), lambda i: (0, i, 0)),
        out_shape=jax.ShapeDtypeStruct((2, seq, 2 * MIXER_W), F32),
        compiler_params=_cparams(("arbitrary",)),
        name="hy_kspec",
    )(fwd, fwd, ab)


def _hy_fwd_kernel(te_ref, to_ref, u_ref, k_ref, o_ref):
    hb = te_ref.shape[0] // 2
    reps = u_ref.shape[2] // MIXER_W
    ep = jnp.dot(te_ref[...], u_ref[0], preferred_element_type=F32)
    op = jnp.dot(to_ref[...], u_ref[1], preferred_element_type=F32)
    ec, es, oc, os_ = ep[0:hb, :], ep[hb:, :], op[0:hb, :], op[hb:, :]
    tile = lambda v: jnp.concatenate([v] * reps, axis=-1)
    p, q, pm, qm = ec + oc, es + os_, ec - oc, os_ - es
    kr, ki = tile(k_ref[0, 0:hb, :]), tile(k_ref[0, hb:, :])
    krm, kim = tile(k_ref[1, 0:hb, :]), tile(k_ref[1, hb:, :])
    yr, yn = p * kr + q * ki, q * kr - p * ki
    yrm, ynm = pm * krm + qm * kim, qm * krm - pm * kim
    o_ref[0, 0:hb, :] = (yr + yrm).astype(BF16)
    o_ref[0, hb:, :] = (yn - ynm).astype(BF16)
    o_ref[1, 0:hb, :] = (yr - yrm).astype(BF16)
    o_ref[1, hb:, :] = (yn + ynm).astype(BF16)


def _hy_fwd(fwd, u, kspec, order, tm, tn):
    _, seq, half = fwd.shape
    ncol = u.shape[2]
    return pl.pallas_call(
        _hy_fwd_kernel,
        grid=(ncol // tn, seq // tm),
        in_specs=[
            pl.BlockSpec((None, tm, half), lambda j, i: (0, i, 0)),
            pl.BlockSpec((None, tm, half), lambda j, i: (1, i, 0)),
            pl.BlockSpec((2, half, tn), lambda j, i: (0, 0, j)),
            pl.BlockSpec((2, tm, MIXER_W), lambda j, i: (0, i, order)),
        ],
        out_specs=pl.BlockSpec((2, tm, tn), lambda j, i: (0, i, j)),
        out_shape=jax.ShapeDtypeStruct((2, seq, ncol), BF16),
        compiler_params=_cparams(("arbitrary", "arbitrary")),
        name="hy_fwd",
    )(fwd, fwd, u, kspec)


def _hy_inv_kernel(ie_ref, io_ref, y_ref, u_ref, gate_ref, skip_ref, *rest, final):
    reps = u_ref.shape[2] // MIXER_W
    skip = jnp.concatenate([skip_ref[...]] * reps, axis=-1)
    for par, inv_ref in enumerate((ie_ref, io_ref)):
        conv = jnp.dot(inv_ref[...], y_ref[par], preferred_element_type=F32)
        out = gate_ref[par] * (conv + u_ref[par].astype(F32) * skip)
        if final:
            g_ref, o_ref = rest
            for r in range(reps):
                o_ref[r, par] = _rms_gain(out[:, r * MIXER_W:(r + 1) * MIXER_W], g_ref[...]).astype(BF16)
        else:
            (o_ref,) = rest
            o_ref[par] = out.astype(BF16)


def _hy_inv(inv, yspec, u, conv, part, skip, gain, final, tm, tn):
    _, half, seq = inv.shape
    ncol = u.shape[2]
    reps = tn // MIXER_W
    in_specs = [
        pl.BlockSpec((None, tm, seq), lambda j, i: (0, i, 0)),
        pl.BlockSpec((None, tm, seq), lambda j, i: (1, i, 0)),
        pl.BlockSpec((2, seq, tn), lambda j, i: (0, 0, j)),
        pl.BlockSpec((2, tm, tn), lambda j, i: (0, i, j)),
        pl.BlockSpec((None, 2, tm, tn), lambda j, i: (part, 0, i, j)),
        pl.BlockSpec((1, MIXER_W), lambda j, i: (0, 0)),
    ]
    args = [inv, inv, yspec, u, conv, skip]
    if final:
        in_specs.append(pl.BlockSpec((1, MIXER_W), lambda j, i: (0, 0)))
        args.append(gain)
        out_spec = pl.BlockSpec((reps, 2, tm, MIXER_W), lambda j, i: (j, 0, i, 0))
        out_shape = jax.ShapeDtypeStruct((ncol // MIXER_W, 2, half, MIXER_W), BF16)
    else:
        out_spec = pl.BlockSpec((2, tm, tn), lambda j, i: (0, i, j))
        out_shape = jax.ShapeDtypeStruct((2, half, ncol), BF16)
    return pl.pallas_call(
        functools.partial(_hy_inv_kernel, final=final),
        grid=(ncol // tn, half // tm),
        in_specs=in_specs,
        out_specs=out_spec,
        out_shape=out_shape,
        compiler_params=_cparams(("arbitrary", "arbitrary")),
        name="hy_inv",
    )(*args)


def _hy_table_kernel(hc_ref, hs_ref, lc_ref, ls_ref, o_ref, *, interleave):
    n_hi, split = hc_ref.shape[0], lc_ref.shape[0]
    lc, ls = lc_ref[...], ls_ref[...]
    for g in range(n_hi):
        hc, hs = hc_ref[g:g + 1, :], hs_ref[g:g + 1, :]
        cos_t = (hc * lc - hs * ls).astype(BF16)
        sin_t = (hs * lc + hc * ls).astype(BF16)
        rows = slice(g * split, (g + 1) * split)
        if interleave == 0:
            o_ref[rows, :] = cos_t
            o_ref[n_hi * split + g * split:n_hi * split + (g + 1) * split, :] = sin_t
        else:
            w = interleave
            for c in range(cos_t.shape[1] // w):
                o_ref[rows, 2 * c * w:(2 * c + 1) * w] = cos_t[:, c * w:(c + 1) * w]
                o_ref[rows, (2 * c + 1) * w:(2 * c + 2) * w] = sin_t[:, c * w:(c + 1) * w]


def _hy_tables(seq, tm, tm_inv):
    half = seq // 2
    hb = tm // 2
    split = 32
    unit = 2.0 * math.pi / (4 * seq)
    ar = lambda n: jnp.arange(n, dtype=jnp.int32)

    def cos_sin(a, b):
        ang = ((a[..., :, None] * b[..., None, :]) % (4 * seq)).astype(F32) * unit
        return jnp.cos(ang), jnp.sin(ang)

    par = ar(2)[:, None]
    n_all = 2 * ar(half)[None, :] + par
    k_odd = 2 * ar(half) + 1
    f_hc, f_hs = cos_sin(jnp.broadcast_to(2 * split * ar(half // split), (2, half // split)), n_all)
    f_lc, f_ls = cos_sin(jnp.broadcast_to(2 * ar(split) + 1, (2, split)), n_all)
    i_hc, i_hs = cos_sin(2 * split * ar(half // split), k_odd)
    i_lc, i_ls = cos_sin(2 * ar(split)[None, :] + par, jnp.broadcast_to(k_odd, (2, half)))

    n_hi_f = hb // split
    fwd = pl.pallas_call(
        functools.partial(_hy_table_kernel, interleave=0),
        grid=(2, half // hb),
        in_specs=[
            pl.BlockSpec((None, n_hi_f, half), lambda p, i: (p, i, 0)),
            pl.BlockSpec((None, n_hi_f, half), lambda p, i: (p, i, 0)),
            pl.BlockSpec((None, split, half), lambda p, i: (p, 0, 0)),
            pl.BlockSpec((None, split, half), lambda p, i: (p, 0, 0)),
        ],
        out_specs=pl.BlockSpec((None, tm, half), lambda p, i: (p, i, 0)),
        out_shape=jax.ShapeDtypeStruct((2, seq, half), BF16),
        compiler_params=_cparams(("arbitrary", "arbitrary")),
        name="hy_table_fwd",
    )(f_hc, f_hs, f_lc, f_ls)
    n_hi_i = tm_inv // split
    inv = pl.pallas_call(
        functools.partial(_hy_table_kernel, interleave=hb),
        grid=(2, half // tm_inv),
        in_specs=[
            pl.BlockSpec((n_hi_i, half), lambda p, i: (i, 0)),
            pl.BlockSpec((n_hi_i, half), lambda p, i: (i, 0)),
            pl.BlockSpec((None, split, half), lambda p, i: (p, 0, 0)),
            pl.BlockSpec((None, split, half), lambda p, i: (p, 0, 0)),
        ],
        out_specs=pl.BlockSpec((None, tm_inv, seq), lambda p, i: (p, i, 0)),
        out_shape=jax.ShapeDtypeStruct((2, half, seq), BF16),
        compiler_params=_cparams(("arbitrary", "arbitrary")),
        name="hy_table_inv",
    )(i_hc, i_hs, i_lc, i_ls)
    return fwd, inv


def _hy_features(seq):
    t = jnp.linspace(0.0, 1.0, seq, dtype=F32)[:, None]
    bands = (HY_EMB - 1) // 2
    freqs = jnp.linspace(1e-4, bands - 1, bands, dtype=F32)[None, :]
    ang = (2.0 * math.pi / seq) * jnp.arange(seq, dtype=F32)[:, None] * freqs
    feat = jnp.concatenate([t, jnp.cos(ang), -jnp.sin(ang)], -1)
    feat = jnp.pad(feat, ((0, 0), (0, HY_EMB_PAD - HY_EMB)))
    deltas = jnp.abs(jnp.linspace(math.log(HY_TARGET) / HY_SLOW, math.log(HY_TARGET) / HY_FAST,
                                  MIXER_W, dtype=F32))
    decay = jnp.exp(-t * deltas[None, :])
    split = lambda a: jnp.stack([a[0::2], a[1::2]], axis=0)
    return split(feat), split(decay)


def _hy_tiles(seq):
    return min(512, seq), min(256, seq // 2)


def _hyena_mix(z, lp, gain, n_seq, seq, row0, tables):
    tm_f, tm_i = _hy_tiles(seq)
    ncol = n_seq * MIXER_W
    tn = min(1024, ncol)
    fwd, inv = tables
    feat, decay = _hy_features(seq)
    ab = _hy_filters(feat, decay, lp["hy_w1p"], lp["hy_b1"], lp["hy_f1"], lp["hy_w2"], lp["hy_b2"],
                     lp["hy_f2"], lp["hy_w3"])
    kspec = _hy_kspec(fwd, ab, tm_f)
    conv, u = _hy_conv(z, lp["hy_conv_w"], lp["hy_conv_b"], n_seq, seq, row0)
    y0 = _hy_fwd(fwd, u, kspec, 0, tm_f, tn)
    u1 = _hy_inv(inv, y0, u, conv, 0, lp["hy_skip"][0:1], None, False, tm_i, tn)
    y1 = _hy_fwd(fwd, u1, kspec, 1, tm_f, tn)
    out = _hy_inv(inv, y1, u1, conv, 1, lp["hy_skip"][1:2], gain, True, tm_i, tn)
    return jnp.transpose(out, (0, 2, 1, 3)).reshape(n_seq * seq, MIXER_W)


def _gelu(x):
    c = math.sqrt(2.0 / math.pi)
    return 0.5 * x * (1.0 + jnp.tanh(c * (x + 0.044715 * (x * x * x))))


def _gmlp_kernel(u_ref, v_ref, lg_ref, lb_ref, ws_ref, bs_ref, g_ref, o_ref):
    tl = u_ref.shape[0]
    gw = MIXER_W // GMLP_GROUPS
    lane_grp = lax.broadcasted_iota(jnp.int32, (GMLP_CHUNK, MIXER_W), 1) // gw
    for c in range(tl // GMLP_CHUNK):
        rows = slice(c * GMLP_CHUNK, (c + 1) * GMLP_CHUNK)
        v = _ln(_gelu(v_ref[rows, :])) * lg_ref[...] + lb_ref[...]
        vb = v.astype(BF16)
        stacked = jnp.concatenate(
            [jnp.where(lane_grp == g, vb, jnp.zeros_like(vb)) for g in range(GMLP_GROUPS)], axis=0)
        s = jnp.dot(ws_ref[...], stacked, preferred_element_type=F32) + bs_ref[...]
        y = _gelu(u_ref[rows, :]) * s
        o_ref[rows, :] = _rms_gain(y, g_ref[...]).astype(BF16)


def _gmlp_mix(z, ln_g, ln_b, ws_cat, bs_mat, gain, n_rows):
    tl = 512
    full = lambda shape: pl.BlockSpec(shape, lambda i: tuple(0 for _ in shape))
    return pl.pallas_call(
        _gmlp_kernel,
        grid=(n_rows // tl,),
        in_specs=[
            pl.BlockSpec((tl, MIXER_W), lambda i: (i, COL_GU)),
            pl.BlockSpec((tl, MIXER_W), lambda i: (i, COL_GV)),
            full((1, MIXER_W)), full((1, MIXER_W)),
            full((GMLP_CHUNK, GMLP_GROUPS * GMLP_CHUNK)),
            full((GMLP_CHUNK, MIXER_W)),
            full((1, MIXER_W)),
        ],
        out_specs=pl.BlockSpec((tl, MIXER_W), lambda i: (i, 0)),
        out_shape=jax.ShapeDtypeStruct((n_rows, MIXER_W), BF16),
        compiler_params=_cparams(("arbitrary",)),
        name="gmlp_mix",
    )(z, z, ln_g, ln_b, ws_cat, bs_mat, gain)


def _merge_kernel(yp_ref, ya_ref, yh_ref, yg_ref, w_ref, x_ref, g1_ref, lg_ref, lb_ref, sh_ref, sc_ref,
                  wrh_ref, wrl_ref, rb_ref, xo_ref, h_ref, idx_ref, gate_ref, y_a, y_b, *, alpha):
    s = pl.program_id(0)

    @pl.when(s == 0)
    def _():
        y_b[...] = jnp.zeros_like(y_b)

    def step(y_new, y_old):
        ymix = jnp.concatenate([yp_ref[...], ya_ref[...], yh_ref[...], yg_ref[...]], axis=-1)
        y_new[...] = jnp.dot(ymix, w_ref[...], preferred_element_type=F32)

        x1 = _ln(alpha * x_ref[...] + g1_ref[...] * y_old[...]) * lg_ref[...] + lb_ref[...]
        xo_ref[...] = x1
        h = _ln(x1) * (1.0 + sc_ref[...]) + sh_ref[...]
        h_ref[...] = h
        h_hi = h.astype(BF16)
        h_lo = (h - h_hi.astype(F32)).astype(BF16)
        logits = (jnp.dot(h_hi, wrh_ref[...], preferred_element_type=F32)
                  + (jnp.dot(h_lo, wrh_ref[...], preferred_element_type=F32)
                     + jnp.dot(h_hi, wrl_ref[...], preferred_element_type=F32)))
        idx, gate = _route_rows(logits, rb_ref[...])
        idx_ref[...] = idx
        gate_ref[...] = gate

    even = lax.rem(s, 2) == 0
    pl.when(even)(functools.partial(step, y_a, y_b))
    pl.when(jnp.logical_not(even))(functools.partial(step, y_b, y_a))


def _route_rows(logits, bias):
    neg = -1e30
    lane = lax.broadcasted_iota(jnp.int32, logits.shape, 1)
    lane_f = lane.astype(F32)
    valid = lane < N_EXPERTS
    per = N_EXPERTS // N_EXPERT_GROUPS
    s = jax.nn.sigmoid(logits)
    sel = jnp.where(valid, s + bias, neg)
    sh = [sel] + [pltpu.roll(sel, LOGIT_PAD - j, axis=1) for j in range(1, per)]
    pair = None
    for a in range(per):
        for b in range(a + 1, per):
            t = sh[a] + sh[b]
            pair = t if pair is None else jnp.maximum(pair, t)
    grp = jnp.where(valid & ((lane & (per - 1)) == 0), pair, neg)

    def first_max(v):
        m = jnp.max(v, axis=-1, keepdims=True)
        return jnp.min(jnp.where(v == m, lane_f, float(LOGIT_PAD)), axis=-1, keepdims=True)

    best = first_max(grp).astype(jnp.int32)
    shift = per.bit_length() - 1
    cand = jnp.where(valid & ((lane >> shift) == (best >> shift)), sel, neg)
    i1 = first_max(cand)
    i2 = first_max(jnp.where(lane_f == i1, neg, cand))
    w1 = jnp.sum(jnp.where(lane_f == i1, s, 0.0), axis=-1, keepdims=True)
    w2 = jnp.sum(jnp.where(lane_f == i2, s, 0.0), axis=-1, keepdims=True)
    tot = w1 + w2
    idx = jnp.where(lane == 0, i1, jnp.where(lane == 1, i2, 0.0)).astype(jnp.int32)
    gate = jnp.where(lane == 0, w1 / tot, jnp.where(lane == 1, w2 / tot, 0.0))
    return idx, gate


def _merge(ys, w_out_bf, xa, mod3, ln_g, ln_b, wr_hi, wr_lo, rb_pad, n_rows, n_batch, seq, alpha):
    d = xa.shape[1]
    tm = 512
    nt = n_rows // tm

    def cur(i):
        return jnp.minimum(i, nt - 1)

    def prv(i):
        return jnp.maximum(i - 1, 0)

    def mrow(i):
        return jnp.minimum((prv(i) * tm) // seq, n_batch)

    ymix = pl.BlockSpec((tm, MIXER_W), lambda i: (cur(i), 0))
    rowvec = pl.BlockSpec((1, d), lambda i: (0, 0))
    return pl.pallas_call(
        functools.partial(_merge_kernel, alpha=alpha),
        grid=(nt + 1,),
        in_specs=[
            ymix, ymix, ymix, ymix,
            pl.BlockSpec((d, d), lambda i: (0, 0)),
            pl.BlockSpec((tm, d), lambda i: (prv(i), 0)),
            pl.BlockSpec((None, 1, d), lambda i: (mrow(i), 0, 2)),
            rowvec, rowvec,
            pl.BlockSpec((None, 1, d), lambda i: (mrow(i), 0, 3)),
            pl.BlockSpec((None, 1, d), lambda i: (mrow(i), 0, 4)),
            pl.BlockSpec((d, LOGIT_PAD), lambda i: (0, 0)),
            pl.BlockSpec((d, LOGIT_PAD), lambda i: (0, 0)),
            pl.BlockSpec((1, LOGIT_PAD), lambda i: (0, 0)),
        ],
        out_specs=[
            pl.BlockSpec((tm, d), lambda i: (prv(i), 0)),
            pl.BlockSpec((tm, d), lambda i: (prv(i), 0)),
            pl.BlockSpec((tm, LOGIT_PAD), lambda i: (prv(i), 0)),
            pl.BlockSpec((tm, LOGIT_PAD), lambda i: (prv(i), 0)),
        ],
        out_shape=[
            jax.ShapeDtypeStruct((n_rows, d), F32),
            jax.ShapeDtypeStruct((n_rows, d), F32),
            jax.ShapeDtypeStruct((n_rows, LOGIT_PAD), jnp.int32),
            jax.ShapeDtypeStruct((n_rows, LOGIT_PAD), F32),
        ],
        scratch_shapes=[pltpu.VMEM((tm, d), F32), pltpu.VMEM((tm, d), F32)],
        compiler_params=_cparams(("arbitrary",)),
        name="merge",
    )(*ys, w_out_bf, xa, mod3, ln_g, ln_b, mod3, mod3, wr_hi, wr_lo, rb_pad)


def _moe_kernel(be_ref, nu_ref, dest_ref, ws_ref, nxt_ref, h_hbm, wg_hbm, wu_hbm, wd_hbm, o_ref,
                xbuf, sem, tok_ref, stage_g, stage_u, stage_d, wsem, wg_s, wu_s, wd_s, *, layer):
    i = pl.program_id(0)
    n_used = nu_ref[0]
    ring = xbuf.shape[0]
    slot = lax.rem(i, ring)

    @pl.when(i == 0)
    def _():
        def clear(s, carry):
            tok_ref[s] = 0
            return carry

        def place(a, carry):
            tok_ref[dest_ref[a]] = lax.shift_right_logical(a, TOP_K.bit_length() - 1)
            return carry

        lax.fori_loop(0, tok_ref.shape[0], clear, 0, unroll=8)
        lax.fori_loop(0, dest_ref.shape[0], place, 0, unroll=8)

    def row_copy(blk, r, sl):
        tok = tok_ref[blk * MOE_BLOCK + r]
        return pltpu.make_async_copy(h_hbm.at[pl.ds(tok, 1)], xbuf.at[sl, pl.ds(r, 1)], sem.at[sl])

    def gather_start(blk, sl):
        def body(r, carry):
            row_copy(blk, r, sl).start()
            return carry

        lax.fori_loop(0, MOE_BLOCK, body, 0, unroll=8)

    def gather_wait(sl):
        pltpu.make_async_copy(h_hbm.at[pl.ds(0, MOE_BLOCK)], xbuf.at[sl], sem.at[sl]).wait()

    def weight_copies(e, ws):
        return (pltpu.make_async_copy(wg_hbm.at[layer, e], stage_g.at[ws], wsem.at[ws]),
                pltpu.make_async_copy(wu_hbm.at[layer, e], stage_u.at[ws], wsem.at[ws]),
                pltpu.make_async_copy(wd_hbm.at[layer, e], stage_d.at[ws], wsem.at[ws]))

    @pl.when(i == 0)
    def _():
        for cp in weight_copies(be_ref[0], ws_ref[0]):
            cp.start(priority=1)
        gather_start(0, 0)

        @pl.when(n_used > 1)
        def _():
            gather_start(1, 1)

    active = i < n_used
    new_expert = jnp.logical_or(i == 0, be_ref[i] != be_ref[jnp.maximum(i - 1, 0)])

    @pl.when(jnp.logical_and(active, new_expert))
    def _():
        ws = ws_ref[i]
        for cp in weight_copies(be_ref[i], ws):
            cp.wait()
        nxt = nxt_ref[i]

        @pl.when(nxt >= 0)
        def _():
            for cp in weight_copies(nxt, 1 - ws):
                cp.start(priority=1)

        for src, dst in ((stage_g, wg_s), (stage_u, wu_s), (stage_d, wd_s)):
            def cast(r0, src=src, dst=dst):
                dst[pl.ds(r0, 256), :] = src[ws, pl.ds(r0, 256), :].astype(BF16)

            _row_chunks(dst.shape[0], 256, cast)

    def ffn_block(prefetch_next):
        gather_wait(slot)
        x = xbuf[slot].astype(BF16)
        if prefetch_next:
            nxt = lax.rem(i + 2, ring)
            for r in range(MOE_BLOCK):
                row_copy(i + 2, r, nxt).start()
        g = jnp.dot(x, wg_s[...], preferred_element_type=F32)
        u = jnp.dot(x, wu_s[...], preferred_element_type=F32)
        a = (g * jax.nn.sigmoid(g) * u).astype(BF16)
        o_ref[...] = jnp.dot(a, wd_s[...], preferred_element_type=F32).astype(o_ref.dtype)

    has_next = i + 2 < n_used
    pl.when(jnp.logical_and(active, has_next))(functools.partial(ffn_block, True))
    pl.when(jnp.logical_and(active, jnp.logical_not(has_next)))(functools.partial(ffn_block, False))

    @pl.when(jnp.logical_not(active))
    def _():
        o_ref[...] = jnp.zeros_like(o_ref)


def _moe_experts(h, dest, plan, wg, wu, wd, layer):
    d = h.shape[1]
    n_slots = plan["n_slots"]
    de = wg.shape[3]
    n_blocks = n_slots // MOE_BLOCK
    hbm = pl.BlockSpec(memory_space=pl.ANY)
    grid_spec = pltpu.PrefetchScalarGridSpec(
        num_scalar_prefetch=5,
        grid=(n_blocks,),
        in_specs=[hbm, hbm, hbm, hbm],
        out_specs=pl.BlockSpec((MOE_BLOCK, d), lambda i, *_: (i, 0)),
        scratch_shapes=[
            pltpu.VMEM((3, MOE_BLOCK, d), F32),
            pltpu.SemaphoreType.DMA((3,)),
            pltpu.SMEM((n_slots,), jnp.int32),
            pltpu.VMEM((2, d, de), F32), pltpu.VMEM((2, d, de), F32), pltpu.VMEM((2, de, d), F32),
            pltpu.SemaphoreType.DMA((2,)),
            pltpu.VMEM((d, de), BF16), pltpu.VMEM((d, de), BF16), pltpu.VMEM((de, d), BF16),
        ],
    )
    return pl.pallas_call(
        functools.partial(_moe_kernel, layer=layer),
        grid_spec=grid_spec,
        out_shape=jax.ShapeDtypeStruct((n_slots, d), BF16),
        compiler_params=_cparams(("arbitrary",), MOE_VMEM_LIMIT),
        name="moe_experts",
    )(plan["block_e"], plan["n_used"], dest.reshape(-1), plan["wslot"], plan["next_e"], h, wg, wu, wd)


def _assignment_ranks(flat_e):
    a = flat_e.shape[0]
    blk = 128
    nb = a // blk
    onehot = (flat_e[:, None] == jnp.arange(N_EXPERTS)[None, :]).astype(BF16).reshape(nb, blk, N_EXPERTS)
    tri = (jnp.arange(blk)[:, None] >= jnp.arange(blk)[None, :]).astype(BF16)
    intra = jnp.einsum("ij,bjk->bik", tri, onehot, preferred_element_type=F32)
    bsum = intra[:, -1, :]
    before = (jnp.arange(nb)[:, None] > jnp.arange(nb)[None, :]).astype(BF16)
    offs = jnp.dot(before, bsum.astype(BF16), preferred_element_type=F32)
    csum = intra + offs[:, None, :]
    rank = jnp.sum(csum * onehot.astype(F32), axis=-1).reshape(a) - 1.0
    counts = offs[-1] + bsum[-1]
    return rank.astype(jnp.int32), counts.astype(jnp.int32)


def _dispatch_plan(idx):
    t = idx.shape[0]
    a = t * TOP_K
    flat_e = idx.reshape(a)
    rank, counts = _assignment_ranks(flat_e)
    padded = (counts + MOE_BLOCK - 1) // MOE_BLOCK * MOE_BLOCK
    pad_end = jnp.cumsum(padded)
    pad_start = pad_end - padded
    dest = pad_start[flat_e] + rank
    n_blocks = (a + N_EXPERTS * (MOE_BLOCK - 1) + MOE_BLOCK - 1) // MOE_BLOCK
    n_slots = n_blocks * MOE_BLOCK
    blk_start = jnp.arange(n_blocks, dtype=jnp.int32) * MOE_BLOCK
    block_e = jnp.minimum(
        jnp.sum((pad_end[None, :] <= blk_start[:, None]).astype(jnp.int32), axis=1), N_EXPERTS - 1)
    n_used = (pad_end[-1] // MOE_BLOCK).astype(jnp.int32).reshape(1)
    nonempty = counts > 0
    eid = jnp.arange(N_EXPERTS, dtype=jnp.int32)
    run_of_e = jnp.cumsum(nonempty.astype(jnp.int32)) - 1
    later = jnp.logical_and(eid[None, :] > eid[:, None], nonempty[None, :])
    nxt_of_e = jnp.min(jnp.where(later, eid[None, :], N_EXPERTS), axis=1)
    nxt_of_e = jnp.where(nxt_of_e == N_EXPERTS, -1, nxt_of_e)
    plan = {
        "n_slots": n_slots, "block_e": block_e.astype(jnp.int32), "n_used": n_used,
        "wslot": (run_of_e[block_e] % 2).astype(jnp.int32), "next_e": nxt_of_e[block_e].astype(jnp.int32),
    }
    return plan, dest.reshape(t, TOP_K)


def _final_kernel(x_ref, ya_ref, yb_ref, gate_ref, g2_ref, lg_ref, lb_ref, o_ref, *, alpha):
    f = ya_ref[...].astype(F32) * gate_ref[:, 0:1] + yb_ref[...].astype(F32) * gate_ref[:, 1:2]
    o_ref[...] = _ln(alpha * x_ref[...] + g2_ref[...] * f) * lg_ref[...] + lb_ref[...]


def _final_norm(x1, ya, yb, gate, mod3, ln_g, ln_b, n_batch, seq, alpha):
    n_rows, d = x1.shape
    tm = 512

    def mrow(i):
        return jnp.minimum((i * tm) // seq, n_batch)

    tile = pl.BlockSpec((tm, d), lambda i: (i, 0))
    rowvec = pl.BlockSpec((1, d), lambda i: (0, 0))
    return pl.pallas_call(
        functools.partial(_final_kernel, alpha=alpha),
        grid=(n_rows // tm,),
        in_specs=[tile, tile, tile, pl.BlockSpec((tm, LOGIT_PAD), lambda i: (i, 0)),
                  pl.BlockSpec((None, 1, d), lambda i: (mrow(i), 0, 5)), rowvec, rowvec],
        out_specs=tile,
        out_shape=jax.ShapeDtypeStruct((n_rows, d), F32),
        compiler_params=_cparams(("arbitrary",)),
        name="final_norm",
    )(x1, ya, yb, gate, mod3, ln_g, ln_b)


def _permute_in_cols(w):
    pool, q, k, v, hy, gm = jnp.split(w, (512, 1024, 1152, 1280, 2816), axis=-1)
    return jnp.concatenate([hy, pool, q, gm, k, v], axis=-1)


def kernel(x, c, ctx, c_ctx, w_ada, b_ada, w_in, w_out, mix_norm_g, pool_w, pool_scale, attn_sink,
           hy_conv_w, hy_conv_b, hy_w1, hy_b1, hy_f1, hy_w2, hy_b2, hy_f2, hy_w3, hy_skip,
           gm_ln_g, gm_ln_b, gm_ws, gm_bs, ln1_g, ln1_b, ln2_g, ln2_b, w_router, router_bias,
           w_gate, w_up, w_down):
    n_batch, seq, d = x.shape
    ctx_len = ctx.shape[1]
    depth = w_in.shape[0]
    grid_w = 64
    alpha = (2 * depth) ** 0.25
    n_lat = n_batch * seq
    n_ctx = n_batch * ctx_len

    xa = jnp.concatenate([x.reshape(n_lat, d), ctx.reshape(n_ctx, d)], axis=0)
    cvec = jnp.concatenate([c, c_ctx[None, :], jnp.zeros((8 - n_batch - 1, d), F32)], axis=0)
    mods = _ada_mods(cvec, w_ada, b_ada)
    cos_t, sin_t = _rope_tables(seq, grid_w)
    tabs_lat = _hy_tables(seq, *_hy_tiles(seq))
    tabs_ctx = _hy_tables(ctx_len, *_hy_tiles(ctx_len))
    wr_pad = jnp.pad(w_router, ((0, 0), (0, LOGIT_PAD - N_EXPERTS)))
    wr_hi = wr_pad.astype(BF16)
    wr_lo = (wr_pad - wr_hi.astype(F32)).astype(BF16)
    rb_pad = jnp.pad(router_bias.astype(F32), (0, LOGIT_PAD - N_EXPERTS)).reshape(1, LOGIT_PAD)
    row = lambda v: v.reshape(1, -1)

    for layer in range(depth):
        last = layer == depth - 1
        mod3 = mods[layer].reshape(8, 1, 6 * d)
        gains = mix_norm_g[layer].reshape(4, 1, MIXER_W)
        lp = {
            "hy_conv_w": hy_conv_w[layer], "hy_conv_b": row(hy_conv_b[layer]),
            "hy_w1p": jnp.pad(hy_w1[layer], ((0, HY_EMB_PAD - HY_EMB), (0, 0))),
            "hy_b1": row(hy_b1[layer]), "hy_f1": row(hy_f1[layer]),
            "hy_w2": hy_w2[layer], "hy_b2": row(hy_b2[layer]), "hy_f2": row(hy_f2[layer]),
            "hy_w3": hy_w3[layer], "hy_skip": hy_skip[layer],
        }
        z = _in_proj(xa, mod3, _permute_in_cols(w_in[layer]).astype(BF16), n_batch, seq)

        pool_bf = pool_w[layer].astype(BF16)
        ps = row(pool_scale[layer])
        ws_cat = jnp.transpose(gm_ws[layer], (1, 0, 2)).reshape(GMLP_CHUNK, GMLP_GROUPS * GMLP_CHUNK)
        bs_mat = jnp.repeat(gm_bs[layer].T, MIXER_W // GMLP_GROUPS, axis=1)
        n_rows = n_lat if last else n_lat + n_ctx

        y_pool = _pool_mix(z, pool_bf, ps, gains[0], n_batch, seq, 0)
        y_attn = _attn_lat(z, attn_sink[layer], cos_t, sin_t, gains[1], n_batch, seq, ctx_len)
        y_hy = _hyena_mix(z, lp, gains[2], n_batch, seq, 0, tabs_lat)
        y_gm = _gmlp_mix(z, row(gm_ln_g[layer]), row(gm_ln_b[layer]), ws_cat.astype(BF16), bs_mat,
                         gains[3], n_rows)
        if not last:
            y_pool = jnp.concatenate(
                [y_pool, _pool_mix(z, pool_bf, ps, gains[0], n_batch, ctx_len, n_lat // ctx_len)], axis=0)
            y_attn = jnp.concatenate(
                [y_attn, _attn_ctx(z, attn_sink[layer], gains[1], n_batch, seq, ctx_len)], axis=0)
            y_hy = jnp.concatenate(
                [y_hy, _hyena_mix(z, lp, gains[2], n_batch, ctx_len, n_lat, tabs_ctx)], axis=0)

        x1, h2, idx, gate = _merge((y_pool, y_attn, y_hy, y_gm), w_out[layer].astype(BF16), xa, mod3,
                                   row(ln1_g[layer]), row(ln1_b[layer]), wr_hi, wr_lo, rb_pad,
                                   n_rows, n_batch, seq, alpha)

        plan, dest = _dispatch_plan(idx[:, :TOP_K])
        ys = _moe_experts(h2, dest, plan, w_gate, w_up, w_down, layer)
        xa = _final_norm(x1, ys[dest[:, 0]], ys[dest[:, 1]], gate, mod3, row(ln2_g[layer]), row(ln2_b[layer]),
                         n_batch, seq, alpha)
    return xa.reshape(n_batch, seq, d)
```

```python
import functools
import math

import jax
import jax.numpy as jnp
from jax import lax
from jax.experimental import pallas as pl
from jax.experimental.pallas import tpu as pltpu

F32 = jnp.float32
BF16 = jnp.bfloat16
HI = lax.Precision.HIGHEST

LN_EPS = 1e-6
MIXER_W = 512
HEAD_DIM = 64
N_HEADS = 8
N_KV_HEADS = 2
KV_GROUP = 4
KV_W = 128
WINDOW = 128
ROPE_THETA = 10000.0
POOL_WINDOWS = (2, 4, 8, 16)
POOL_GROUP = 128
POOL_HALO = 8
GMLP_CHUNK = 128
GMLP_GROUPS = 8
HY_EMB = 33
HY_EMB_PAD = 128
HY_TARGET = 1e-2
HY_FAST = 0.3
HY_SLOW = 1.5
N_EXPERTS = 32
N_EXPERT_GROUPS = 8
TOP_K = 2
D_EXPERT = 768
MOE_BLOCK = 256
LOGIT_PAD = 128

COL_HY = 0
COL_POOL = 3
COL_Q = 4
COL_GU = 5
COL_GV = 6
COL_K = 28
COL_V = 29
IN_W = 3840

VMEM_LIMIT = 56 * 1024 * 1024
MOE_VMEM_LIMIT = 59 * 1024 * 1024


def _cparams(sem, vmem_limit=VMEM_LIMIT):
    return pltpu.CompilerParams(dimension_semantics=sem, vmem_limit_bytes=vmem_limit)


def _ln(x):
    mu = jnp.mean(x, axis=-1, keepdims=True)
    xc = x - mu
    var = jnp.mean(xc * xc, axis=-1, keepdims=True)
    return xc * lax.rsqrt(var + LN_EPS)


def _rms_gain(y, g):
    return y * lax.rsqrt(jnp.mean(y * y, axis=-1, keepdims=True) + LN_EPS) * g


def _row_chunks(n_rows, chunk, body):
    n = n_rows // chunk
    if n == 1:
        body(0)
        return

    def step(i, carry):
        body(pl.multiple_of(i * chunk, chunk))
        return carry

    lax.fori_loop(0, n, step, 0)


def _ada_kernel(c_ref, w_ref, b_ref, o_ref):
    c = c_ref[...]
    s = c * jax.nn.sigmoid(c)
    o_ref[...] = jnp.dot(s, w_ref[...], precision=HI, preferred_element_type=F32) + b_ref[...]


def _ada_mods(cvec, w_ada, b_ada):
    depth, d, n = w_ada.shape
    tn = 1024
    return pl.pallas_call(
        _ada_kernel,
        grid=(depth, n // tn),
        in_specs=[
            pl.BlockSpec((8, d), lambda l, j: (0, 0)),
            pl.BlockSpec((None, d, tn), lambda l, j: (l, 0, j)),
            pl.BlockSpec((None, 1, tn), lambda l, j: (l, 0, j)),
        ],
        out_specs=pl.BlockSpec((None, 8, tn), lambda l, j: (l, 0, j)),
        out_shape=jax.ShapeDtypeStruct((depth, 8, n), F32),
        compiler_params=_cparams(("arbitrary", "arbitrary")),
        name="ada_mods",
    )(cvec, w_ada, b_ada.reshape(depth, 1, n))


def _inproj_kernel(x_ref, sh_ref, sc_ref, w_ref, o_ref, h_scr):
    @pl.when(pl.program_id(1) == 0)
    def _():
        sh = sh_ref[...]
        sc1 = 1.0 + sc_ref[...]

        def body(r0):
            x = x_ref[pl.ds(r0, 256), :]
            h_scr[pl.ds(r0, 256), :] = (_ln(x) * sc1 + sh).astype(BF16)

        _row_chunks(x_ref.shape[0], 256, body)

    o_ref[...] = jnp.dot(h_scr[...], w_ref[...], preferred_element_type=F32)


def _in_proj(xa, mod3, w_bf, n_batch, seq):
    t, d = xa.shape
    n = w_bf.shape[1]
    tn = 768
    tm = next(m for m in (1024, 512, 256) if t % m == 0 and seq % m == 0)

    def mrow(i):
        return jnp.minimum((i * tm) // seq, n_batch)

    return pl.pallas_call(
        _inproj_kernel,
        grid=(t // tm, n // tn),
        in_specs=[
            pl.BlockSpec((tm, d), lambda i, j: (i, 0)),
            pl.BlockSpec((None, 1, d), lambda i, j: (mrow(i), 0, 0)),
            pl.BlockSpec((None, 1, d), lambda i, j: (mrow(i), 0, 1)),
            pl.BlockSpec((d, tn), lambda i, j: (0, j)),
        ],
        out_specs=pl.BlockSpec((tm, tn), lambda i, j: (i, j)),
        out_shape=jax.ShapeDtypeStruct((t, n), F32),
        scratch_shapes=[pltpu.VMEM((tm, d), BF16)],
        compiler_params=_cparams(("arbitrary", "arbitrary")),
        name="in_proj",
    )(xa, mod3, mod3, w_bf)


def _pool_kernel(z_ref, w_ref, ps_ref, g_ref, o_ref, buf):
    seq = z_ref.shape[0]
    rc = min(256, seq)
    zeros = jnp.zeros((POOL_HALO, MIXER_W), F32)
    buf[0:POOL_HALO, :] = zeros
    buf[seq + POOL_HALO:seq + 2 * POOL_HALO, :] = zeros

    def fill(r0):
        buf[pl.ds(r0 + POOL_HALO, rc), :] = z_ref[pl.ds(r0, rc), :]

    _row_chunks(seq, rc, fill)

    def body(r0):
        win = buf[pl.ds(r0, rc + 2 * POOL_HALO), :]
        t = r0 + lax.broadcasted_iota(jnp.int32, (rc, 1), 0)
        outs = []
        for gi, w in enumerate(POOL_WINDOWS):
            lanes = slice(gi * POOL_GROUP, (gi + 1) * POOL_GROUP)
            acc = None
            for j in range(-w // 2, w // 2):
                piece = win[POOL_HALO + j:POOL_HALO + j + rc, lanes]
                acc = piece if acc is None else acc + piece
            cnt = (jnp.minimum(t + w // 2, seq) - jnp.maximum(t - w // 2, 0)).astype(F32)
            dlt = acc / cnt - win[POOL_HALO:POOL_HALO + rc, lanes]
            outs.append(jnp.dot(dlt.astype(BF16), w_ref[gi], preferred_element_type=F32))
        y = jnp.concatenate(outs, axis=-1) * ps_ref[...]
        o_ref[pl.ds(r0, rc), :] = _rms_gain(y, g_ref[...]).astype(BF16)

    _row_chunks(seq, rc, body)


def _pool_mix(z, pool_w_bf, pool_scale, gain, n_seq, seq, row_blk0):
    return pl.pallas_call(
        _pool_kernel,
        grid=(n_seq,),
        in_specs=[
            pl.BlockSpec((seq, MIXER_W), lambda s: (row_blk0 + s, COL_POOL)),
            pl.BlockSpec((4, POOL_GROUP, POOL_GROUP), lambda s: (0, 0, 0)),
            pl.BlockSpec((1, MIXER_W), lambda s: (0, 0)),
            pl.BlockSpec((1, MIXER_W), lambda s: (0, 0)),
        ],
        out_specs=pl.BlockSpec((seq, MIXER_W), lambda s: (s, 0)),
        out_shape=jax.ShapeDtypeStruct((n_seq * seq, MIXER_W), BF16),
        scratch_shapes=[pltpu.VMEM((seq + 2 * POOL_HALO, MIXER_W), F32)],
        compiler_params=_cparams(("arbitrary",)),
        name="pool_mix",
    )(z, pool_w_bf, pool_scale, gain)


def _rope(x, cos, sin_signed):
    lane = lax.broadcasted_iota(jnp.int32, x.shape, 1)
    partner = jnp.where((lane & 31) < 16, pltpu.roll(x, 112, axis=1), pltpu.roll(x, 16, axis=1))
    return x * cos + partner * sin_signed


def _attend(qs, sink_col, parts):
    dn = (((1,), (1,)), ((), ()))
    scores = []
    for kk, _, mask in parts:
        s = lax.dot_general(qs, kk, dn, preferred_element_type=F32)
        if mask is not None:
            s = s + mask
        scores.append(s)
    m = sink_col
    for s in scores:
        m = jnp.maximum(m, jnp.max(s, axis=-1, keepdims=True))
    den = jnp.exp(sink_col - m)
    out = None
    for s, (_, vv, _) in zip(scores, parts):
        p = jnp.exp(s - m)
        den = den + jnp.sum(p, axis=-1, keepdims=True)
        o = jnp.dot(p.astype(BF16), vv, preferred_element_type=F32)
        out = o if out is None else out + o
    return out / den


def _heads_attend(q_all, sink_ref, tq, make_parts):
    cols = [None] * N_HEADS
    for kh in range(N_KV_HEADS):
        heads = [kh * KV_GROUP + g for g in range(KV_GROUP)]
        qs = jnp.concatenate([q_all[:, h * HEAD_DIM:(h + 1) * HEAD_DIM] for h in heads], axis=0)
        sink_col = jnp.concatenate([jnp.full((tq, 1), sink_ref[h], F32) for h in heads], axis=0)
        o = _attend(qs.astype(BF16), sink_col, make_parts(kh))
        for g, h in enumerate(heads):
            cols[h] = o[g * tq:(g + 1) * tq, :]
    return jnp.concatenate(cols, axis=-1)


def _attn_lat_kernel(sink_ref, q_ref, k_ref, v_ref, kc_ref, vc_ref, cos_ref, sin_ref, g_ref, o_ref):
    tq = q_ref.shape[0]
    seq = k_ref.shape[0]
    nband = tq + 2 * WINDOW
    i = pl.program_id(1)
    q0 = pl.multiple_of(i * tq, tq)
    k0 = pl.multiple_of(jnp.clip(q0 - WINDOW, 0, seq - nband), WINDOW)

    cq = cos_ref[pl.ds(q0, tq), :]
    sq = sin_ref[pl.ds(q0, tq), :]
    q = q_ref[...]
    q_all = jnp.concatenate(
        [_rope(q[:, c * 128:(c + 1) * 128], cq, sq) for c in range(MIXER_W // 128)], axis=-1)
    q_all = q_all * (HEAD_DIM ** -0.5)

    kb = _rope(k_ref[pl.ds(k0, nband), :], cos_ref[pl.ds(k0, nband), :],
               sin_ref[pl.ds(k0, nband), :]).astype(BF16)
    vb = v_ref[pl.ds(k0, nband), :].astype(BF16)
    kc = kc_ref[...].astype(BF16)
    vc = vc_ref[...].astype(BF16)

    qpos = q0 + lax.broadcasted_iota(jnp.int32, (tq, nband), 0)
    kpos = k0 + lax.broadcasted_iota(jnp.int32, (tq, nband), 1)
    band = jnp.where(jnp.abs(qpos - kpos) <= WINDOW, 0.0, -1e30)
    mask = jnp.concatenate([band] * KV_GROUP, axis=0)

    def make_parts(kh):
        hs = slice(kh * HEAD_DIM, (kh + 1) * HEAD_DIM)
        return [(kb[:, hs], vb[:, hs], mask), (kc[:, hs], vc[:, hs], None)]

    y = _heads_attend(q_all, sink_ref, tq, make_parts)
    o_ref[...] = _rms_gain(y, g_ref[...]).astype(BF16)


def _attn_ctx_kernel(sink_ref, q_ref, kc_ref, vc_ref, g_ref, o_ref):
    tq = q_ref.shape[0]
    q_all = q_ref[...] * (HEAD_DIM ** -0.5)
    kc = kc_ref[...].astype(BF16)
    vc = vc_ref[...].astype(BF16)

    def make_parts(kh):
        hs = slice(kh * HEAD_DIM, (kh + 1) * HEAD_DIM)
        return [(kc[:, hs], vc[:, hs], None)]

    y = _heads_attend(q_all, sink_ref, tq, make_parts)
    o_ref[...] = _rms_gain(y, g_ref[...]).astype(BF16)


def _attn_lat(z, sink, cos_t, sin_t, gain, n_batch, seq, ctx_len):
    tq = 256
    nq = seq // tq
    ctx_blk0 = (n_batch * seq) // ctx_len
    smem = pl.BlockSpec(memory_space=pltpu.SMEM)
    return pl.pallas_call(
        _attn_lat_kernel,
        grid=(n_batch, nq),
        in_specs=[
            smem,
            pl.BlockSpec((tq, MIXER_W), lambda b, i: (b * nq + i, COL_Q)),
            pl.BlockSpec((seq, KV_W), lambda b, i: (b, COL_K)),
            pl.BlockSpec((seq, KV_W), lambda b, i: (b, COL_V)),
            pl.BlockSpec((ctx_len, KV_W), lambda b, i: (ctx_blk0 + b, COL_K)),
            pl.BlockSpec((ctx_len, KV_W), lambda b, i: (ctx_blk0 + b, COL_V)),
            pl.BlockSpec((seq, 128), lambda b, i: (0, 0)),
            pl.BlockSpec((seq, 128), lambda b, i: (0, 0)),
            pl.BlockSpec((1, MIXER_W), lambda b, i: (0, 0)),
        ],
        out_specs=pl.BlockSpec((tq, MIXER_W), lambda b, i: (b * nq + i, 0)),
        out_shape=jax.ShapeDtypeStruct((n_batch * seq, MIXER_W), BF16),
        compiler_params=_cparams(("arbitrary", "arbitrary")),
        name="attn_lat",
    )(sink, z, z, z, z, z, cos_t, sin_t, gain)


def _attn_ctx(z, sink, gain, n_batch, seq, ctx_len):
    ctx_blk0 = (n_batch * seq) // ctx_len
    smem = pl.BlockSpec(memory_space=pltpu.SMEM)
    return pl.pallas_call(
        _attn_ctx_kernel,
        grid=(n_batch,),
        in_specs=[
            smem,
            pl.BlockSpec((ctx_len, MIXER_W), lambda b: (ctx_blk0 + b, COL_Q)),
            pl.BlockSpec((ctx_len, KV_W), lambda b: (ctx_blk0 + b, COL_K)),
            pl.BlockSpec((ctx_len, KV_W), lambda b: (ctx_blk0 + b, COL_V)),
            pl.BlockSpec((1, MIXER_W), lambda b: (0, 0)),
        ],
        out_specs=pl.BlockSpec((ctx_len, MIXER_W), lambda b: (b, 0)),
        out_shape=jax.ShapeDtypeStruct((n_batch * ctx_len, MIXER_W), BF16),
        compiler_params=_cparams(("arbitrary",)),
        name="attn_ctx",
    )(sink, z, z, z, gain)


def _rope_tables(seq, grid_w):
    nf = HEAD_DIM // 4
    inv = ROPE_THETA ** (-jnp.arange(nf, dtype=F32) / nf)
    t = jnp.arange(seq)
    row = (t // grid_w).astype(F32)[:, None] * inv[None, :]
    col = (t % grid_w).astype(F32)[:, None] * inv[None, :]
    cos_h = jnp.concatenate([jnp.cos(row), jnp.cos(row), jnp.cos(col), jnp.cos(col)], axis=-1)
    sin_h = jnp.concatenate([-jnp.sin(row), jnp.sin(row), -jnp.sin(col), jnp.sin(col)], axis=-1)
    return jnp.tile(cos_h, (1, 2)), jnp.tile(sin_h, (1, 2))


def _hy_conv_kernel(*refs):
    z_refs, zp_refs, zn_refs = refs[0:3], refs[3:6], refs[6:9]
    w_ref, b_ref, g_ref, u_ref = refs[9:]
    i = pl.program_id(1)
    th = z_refs[0].shape[0] // 2
    has_prev = (i > 0).astype(F32)
    has_next = (i < pl.num_programs(1) - 1).astype(F32)
    row = lax.broadcasted_iota(jnp.int32, (th, 128), 0)
    for part in range(3):
        ze = z_refs[part][pl.ds(0, th, stride=2), :]
        zo = z_refs[part][pl.ds(1, th, stride=2), :]
        prev_row = zp_refs[part][POOL_HALO - 1:POOL_HALO, :] * has_prev
        next_row = zn_refs[part][0:1, :] * has_next
        zo_m = jnp.where(row == 0, prev_row, pltpu.roll(zo, 1, axis=0))
        ze_p = jnp.where(row == th - 1, next_row, pltpu.roll(ze, th - 1, axis=0))
        w0, w1, w2, b = w_ref[part, 0:1, :], w_ref[part, 1:2, :], w_ref[part, 2:3, :], b_ref[part]
        ye = b + zo_m * w0 + ze * w1 + zo * w2
        yo = b + ze * w0 + zo * w1 + ze_p * w2
        g_ref[part, 0] = ye
        g_ref[part, 1] = yo
        if part == 2:
            u_ref[0] = ye.astype(BF16)
            u_ref[1] = yo.astype(BF16)


def _hy_conv(z, conv_w, conv_b, n_seq, seq, row0):
    tl = min(2048, seq)
    nt = seq // tl
    blk0 = row0 // tl
    hb0 = row0 // POOL_HALO
    hpt = tl // POOL_HALO
    last_halo = (row0 + n_seq * seq) // POOL_HALO - 1
    cpp = MIXER_W // 128
    ncol = n_seq * MIXER_W

    def main_spec(part):
        return pl.BlockSpec((tl, 128), lambda s, i, c: (blk0 + s * nt + i, part * cpp + c))

    def prev_spec(part):
        return pl.BlockSpec((POOL_HALO, 128),
                            lambda s, i, c: (jnp.maximum(hb0 + (s * nt + i) * hpt - 1, 0), part * cpp + c))

    def next_spec(part):
        return pl.BlockSpec((POOL_HALO, 128),
                            lambda s, i, c: (jnp.minimum(hb0 + (s * nt + i + 1) * hpt, last_halo), part * cpp + c))

    w3 = conv_w.reshape(3, 3, MIXER_W).transpose(1, 0, 2)
    b3 = conv_b.reshape(3, 1, MIXER_W)
    return pl.pallas_call(
        _hy_conv_kernel,
        grid=(n_seq, nt, cpp),
        in_specs=[main_spec(p) for p in range(3)] + [prev_spec(p) for p in range(3)]
        + [next_spec(p) for p in range(3)]
        + [pl.BlockSpec((3, 3, 128), lambda s, i, c: (0, 0, c)),
           pl.BlockSpec((3, 1, 128), lambda s, i, c: (0, 0, c))],
        out_specs=[
            pl.BlockSpec((3, 2, tl // 2, 128), lambda s, i, c: (0, 0, i, s * cpp + c)),
            pl.BlockSpec((2, tl // 2, 128), lambda s, i, c: (0, i, s * cpp + c)),
        ],
        out_shape=[jax.ShapeDtypeStruct((3, 2, seq // 2, ncol), F32),
                   jax.ShapeDtypeStruct((2, seq // 2, ncol), BF16)],
        compiler_params=_cparams(("arbitrary", "arbitrary", "arbitrary")),
        name="hy_conv",
    )(*([z] * 9), w3, b3)


def _hy_filter_kernel(feat_ref, dec_ref, w1_ref, b1_ref, f1_ref, w2_ref, b2_ref, f2_ref, w3_ref, o_ref):
    tl = feat_ref.shape[0]
    h = jnp.dot(feat_ref[...], w1_ref[...], precision=HI, preferred_element_type=F32) + b1_ref[...]
    h = jnp.sin(f1_ref[...] * h)
    h = jnp.dot(h, w2_ref[...], precision=HI, preferred_element_type=F32) + b2_ref[...]
    h = jnp.sin(f2_ref[...] * h)
    filt = jnp.dot(h, w3_ref[...], precision=HI, preferred_element_type=F32)
    dec = dec_ref[...]
    t = 2 * (pl.program_id(1) * tl + lax.broadcasted_iota(jnp.int32, (tl, 1), 0)) + pl.program_id(0)
    not_first = (t > 0).astype(F32)
    for o in range(2):
        hf = filt[:, (2 * o) * MIXER_W:(2 * o + 1) * MIXER_W] * dec
        hb = filt[:, (2 * o + 1) * MIXER_W:(2 * o + 2) * MIXER_W] * dec * not_first
        o_ref[:, o * MIXER_W:(o + 1) * MIXER_W] = (hf + hb).astype(BF16)
        o_ref[:, (2 + o) * MIXER_W:(3 + o) * MIXER_W] = (hb - hf).astype(BF16)


def _hy_filters(feat, decay, w1p, b1, f1, w2, b2, f2, w3):
    half = feat.shape[1]
    tl = min(512, half)
    hid = w2.shape[0]
    full = lambda shape: pl.BlockSpec(shape, lambda p, i: tuple(0 for _ in shape))
    return pl.pallas_call(
        _hy_filter_kernel,
        grid=(2, half // tl),
        in_specs=[
            pl.BlockSpec((None, tl, HY_EMB_PAD), lambda p, i: (p, i, 0)),
            pl.BlockSpec((None, tl, MIXER_W), lambda p, i: (p, i, 0)),
            full((HY_EMB_PAD, hid)), full((1, hid)), full((1, hid)),
            full((hid, hid)), full((1, hid)), full((1, hid)),
            full((hid, 4 * MIXER_W)),
        ],
        out_specs=pl.BlockSpec((None, tl, 4 * MIXER_W), lambda p, i: (p, i, 0)),
        out_shape=jax.ShapeDtypeStruct((2, half, 4 * MIXER_W), BF16),
        compiler_params=_cparams(("arbitrary", "arbitrary")),
        name="hy_filters",
    )(feat, decay, w1p, b1, f1, w2, b2, f2, w3)


def _hy_kspec_kernel(te_ref, to_ref, ab_ref, o_ref, *, inv_len):
    hb = te_ref.shape[0] // 2
    half = ab_ref.shape[2] // 2
    ec = jnp.dot(te_ref[0:hb, :], ab_ref[0, :, 0:half], preferred_element_type=F32)
    oc = jnp.dot(to_ref[0:hb, :], ab_ref[1, :, 0:half], preferred_element_type=F32)
    es = jnp.dot(te_ref[hb:, :], ab_ref[0, :, half:], preferred_element_type=F32)
    os_ = jnp.dot(to_ref[hb:, :], ab_ref[1, :, half:], preferred_element_type=F32)
    o_ref[0, 0:hb, :] = (ec + oc) * inv_len
    o_ref[0, hb:, :] = (es + os_) * inv_len
    o_ref[1, 0:hb, :] = (ec - oc) * inv_len
    o_ref[1, hb:, :] = (os_ - es) * inv_len


def _hy_kspec(fwd, ab, tm):
    _, seq, half = fwd.shape
    return pl.pallas_call(
        functools.partial(_hy_kspec_kernel, inv_len=1.0 / seq),
        grid=(seq // tm,),
        in_specs=[
            pl.BlockSpec((None, tm, half), lambda i: (0, i, 0)),
            pl.BlockSpec((None, tm, half), lambda i: (1, i, 0)),
            pl.BlockSpec((2, half, 4 * MIXER_W), lambda i: (0, 0, 0)),
        ],
        out_specs=pl.BlockSpec((2, tm, 2 * MIXER_W), lambda i: (0, i, 0)),
        out_shape=jax.ShapeDtypeStruct((2, seq, 2 * MIXER_W), F32),
        compiler_params=_cparams(("arbitrary",)),
        name="hy_kspec",
    )(fwd, fwd, ab)


def _hy_fwd_kernel(te_ref, to_ref, u_ref, k_ref, o_ref):
    hb = te_ref.shape[0] // 2
    reps = u_ref.shape[2] // MIXER_W
    ep = jnp.dot(te_ref[...], u_ref[0], preferred_element_type=F32)
    op = jnp.dot(to_ref[...], u_ref[1], preferred_element_type=F32)
    ec, es, oc, os_ = ep[0:hb, :], ep[hb:, :], op[0:hb, :], op[hb:, :]
    tile = lambda v: jnp.concatenate([v] * reps, axis=-1)
    p, q, pm, qm = ec + oc, es + os_, ec - oc, os_ - es
    kr, ki = tile(k_ref[0, 0:hb, :]), tile(k_ref[0, hb:, :])
    krm, kim = tile(k_ref[1, 0:hb, :]), tile(k_ref[1, hb:, :])
    yr, yn = p * kr + q * ki, q * kr - p * ki
    yrm, ynm = pm * krm + qm * kim, qm * krm - pm * kim
    o_ref[0, 0:hb, :] = (yr + yrm).astype(BF16)
    o_ref[0, hb:, :] = (yn - ynm).astype(BF16)
    o_ref[1, 0:hb, :] = (yr - yrm).astype(BF16)
    o_ref[1, hb:, :] = (yn + ynm).astype(BF16)


def _hy_fwd(fwd, u, kspec, order, tm, tn):
    _, seq, half = fwd.shape
    ncol = u.shape[2]
    return pl.pallas_call(
        _hy_fwd_kernel,
        grid=(ncol // tn, seq // tm),
        in_specs=[
            pl.BlockSpec((None, tm, half), lambda j, i: (0, i, 0)),
            pl.BlockSpec((None, tm, half), lambda j, i: (1, i, 0)),
            pl.BlockSpec((2, half, tn), lambda j, i: (0, 0, j)),
            pl.BlockSpec((2, tm, MIXER_W), lambda j, i: (0, i, order)),
        ],
        out_specs=pl.BlockSpec((2, tm, tn), lambda j, i: (0, i, j)),
        out_shape=jax.ShapeDtypeStruct((2, seq, ncol), BF16),
        compiler_params=_cparams(("arbitrary", "arbitrary")),
        name="hy_fwd",
    )(fwd, fwd, u, kspec)


def _hy_inv_kernel(ie_ref, io_ref, y_ref, u_ref, gate_ref, skip_ref, *rest, final):
    reps = u_ref.shape[2] // MIXER_W
    skip = jnp.concatenate([skip_ref[...]] * reps, axis=-1)
    for par, inv_ref in enumerate((ie_ref, io_ref)):
        conv = jnp.dot(inv_ref[...], y_ref[par], preferred_element_type=F32)
        out = gate_ref[par] * (conv + u_ref[par].astype(F32) * skip)
        if final:
            g_ref, o_ref = rest
            for r in range(reps):
                o_ref[r, par] = _rms_gain(out[:, r * MIXER_W:(r + 1) * MIXER_W], g_ref[...]).astype(BF16)
        else:
            (o_ref,) = rest
            o_ref[par] = out.astype(BF16)


def _hy_inv(inv, yspec, u, conv, part, skip, gain, final, tm, tn):
    _, half, seq = inv.shape
    ncol = u.shape[2]
    reps = tn // MIXER_W
    in_specs = [
        pl.BlockSpec((None, tm, seq), lambda j, i: (0, i, 0)),
        pl.BlockSpec((None, tm, seq), lambda j, i: (1, i, 0)),
        pl.BlockSpec((2, seq, tn), lambda j, i: (0, 0, j)),
        pl.BlockSpec((2, tm, tn), lambda j, i: (0, i, j)),
        pl.BlockSpec((None, 2, tm, tn), lambda j, i: (part, 0, i, j)),
        pl.BlockSpec((1, MIXER_W), lambda j, i: (0, 0)),
    ]
    args = [inv, inv, yspec, u, conv, skip]
    if final:
        in_specs.append(pl.BlockSpec((1, MIXER_W), lambda j, i: (0, 0)))
        args.append(gain)
        out_spec = pl.BlockSpec((reps, 2, tm, MIXER_W), lambda j, i: (j, 0, i, 0))
        out_shape = jax.ShapeDtypeStruct((ncol // MIXER_W, 2, half, MIXER_W), BF16)
    else:
        out_spec = pl.BlockSpec((2, tm, tn), lambda j, i: (0, i, j))
        out_shape = jax.ShapeDtypeStruct((2, half, ncol), BF16)
    return pl.pallas_call(
        functools.partial(_hy_inv_kernel, final=final),
        grid=(ncol // tn, half // tm),
        in_specs=in_specs,
        out_specs=out_spec,
        out_shape=out_shape,
        compiler_params=_cparams(("arbitrary", "arbitrary")),
        name="hy_inv",
    )(*args)


def _hy_table_kernel(hc_ref, hs_ref, lc_ref, ls_ref, o_ref, *, interleave):
    n_hi, split = hc_ref.shape[0], lc_ref.shape[0]
    lc, ls = lc_ref[...], ls_ref[...]
    for g in range(n_hi):
        hc, hs = hc_ref[g:g + 1, :], hs_ref[g:g + 1, :]
        cos_t = (hc * lc - hs * ls).astype(BF16)
        sin_t = (hs * lc + hc * ls).astype(BF16)
        rows = slice(g * split, (g + 1) * split)
        if interleave == 0:
            o_ref[rows, :] = cos_t
            o_ref[n_hi * split + g * split:n_hi * split + (g + 1) * split, :] = sin_t
        else:
            w = interleave
            for c in range(cos_t.shape[1] // w):
                o_ref[rows, 2 * c * w:(2 * c + 1) * w] = cos_t[:, c * w:(c + 1) * w]
                o_ref[rows, (2 * c + 1) * w:(2 * c + 2) * w] = sin_t[:, c * w:(c + 1) * w]


def _hy_tables(seq, tm, tm_inv):
    half = seq // 2
    hb = tm // 2
    split = 32
    unit = 2.0 * math.pi / (4 * seq)
    ar = lambda n: jnp.arange(n, dtype=jnp.int32)

    def cos_sin(a, b):
        ang = ((a[..., :, None] * b[..., None, :]) % (4 * seq)).astype(F32) * unit
        return jnp.cos(ang), jnp.sin(ang)

    par = ar(2)[:, None]
    n_all = 2 * ar(half)[None, :] + par
    k_odd = 2 * ar(half) + 1
    f_hc, f_hs = cos_sin(jnp.broadcast_to(2 * split * ar(half // split), (2, half // split)), n_all)
    f_lc, f_ls = cos_sin(jnp.broadcast_to(2 * ar(split) + 1, (2, split)), n_all)
    i_hc, i_hs = cos_sin(2 * split * ar(half // split), k_odd)
    i_lc, i_ls = cos_sin(2 * ar(split)[None, :] + par, jnp.broadcast_to(k_odd, (2, half)))

    n_hi_f = hb // split
    fwd = pl.pallas_call(
        functools.partial(_hy_table_kernel, interleave=0),
        grid=(2, half // hb),
        in_specs=[
            pl.BlockSpec((None, n_hi_f, half), lambda p, i: (p, i, 0)),
            pl.BlockSpec((None, n_hi_f, half), lambda p, i: (p, i, 0)),
            pl.BlockSpec((None, split, half), lambda p, i: (p, 0, 0)),
            pl.BlockSpec((None, split, half), lambda p, i: (p, 0, 0)),
        ],
        out_specs=pl.BlockSpec((None, tm, half), lambda p, i: (p, i, 0)),
        out_shape=jax.ShapeDtypeStruct((2, seq, half), BF16),
        compiler_params=_cparams(("arbitrary", "arbitrary")),
        name="hy_table_fwd",
    )(f_hc, f_hs, f_lc, f_ls)
    n_hi_i = tm_inv // split
    inv = pl.pallas_call(
        functools.partial(_hy_table_kernel, interleave=hb),
        grid=(2, half // tm_inv),
        in_specs=[
            pl.BlockSpec((n_hi_i, half), lambda p, i: (i, 0)),
            pl.BlockSpec((n_hi_i, half), lambda p, i: (i, 0)),
            pl.BlockSpec((None, split, half), lambda p, i: (p, 0, 0)),
            pl.BlockSpec((None, split, half), lambda p, i: (p, 0, 0)),
        ],
        out_specs=pl.BlockSpec((None, tm_inv, seq), lambda p, i: (p, i, 0)),
        out_shape=jax.ShapeDtypeStruct((2, half, seq), BF16),
        compiler_params=_cparams(("arbitrary", "arbitrary")),
        name="hy_table_inv",
    )(i_hc, i_hs, i_lc, i_ls)
    return fwd, inv


def _hy_features(seq):
    t = jnp.linspace(0.0, 1.0, seq, dtype=F32)[:, None]
    bands = (HY_EMB - 1) // 2
    freqs = jnp.linspace(1e-4, bands - 1, bands, dtype=F32)[None, :]
    ang = (2.0 * math.pi / seq) * jnp.arange(seq, dtype=F32)[:, None] * freqs
    feat = jnp.concatenate([t, jnp.cos(ang), -jnp.sin(ang)], -1)
    feat = jnp.pad(feat, ((0, 0), (0, HY_EMB_PAD - HY_EMB)))
    deltas = jnp.abs(jnp.linspace(math.log(HY_TARGET) / HY_SLOW, math.log(HY_TARGET) / HY_FAST,
                                  MIXER_W, dtype=F32))
    decay = jnp.exp(-t * deltas[None, :])
    split = lambda a: jnp.stack([a[0::2], a[1::2]], axis=0)
    return split(feat), split(decay)


def _hy_tiles(seq):
    return min(512, seq), min(256, seq // 2)


def _hyena_mix(z, lp, gain, n_seq, seq, row0, tables):
    tm_f, tm_i = _hy_tiles(seq)
    ncol = n_seq * MIXER_W
    tn = min(1024, ncol)
    fwd, inv = tables
    feat, decay = _hy_features(seq)
    ab = _hy_filters(feat, decay, lp["hy_w1p"], lp["hy_b1"], lp["hy_f1"], lp["hy_w2"], lp["hy_b2"],
                     lp["hy_f2"], lp["hy_w3"])
    kspec = _hy_kspec(fwd, ab, tm_f)
    conv, u = _hy_conv(z, lp["hy_conv_w"], lp["hy_conv_b"], n_seq, seq, row0)
    y0 = _hy_fwd(fwd, u, kspec, 0, tm_f, tn)
    u1 = _hy_inv(inv, y0, u, conv, 0, lp["hy_skip"][0:1], None, False, tm_i, tn)
    y1 = _hy_fwd(fwd, u1, kspec, 1, tm_f, tn)
    out = _hy_inv(inv, y1, u1, conv, 1, lp["hy_skip"][1:2], gain, True, tm_i, tn)
    return jnp.transpose(out, (0, 2, 1, 3)).reshape(n_seq * seq, MIXER_W)


def _gelu(x):
    c = math.sqrt(2.0 / math.pi)
    return 0.5 * x * (1.0 + jnp.tanh(c * (x + 0.044715 * (x * x * x))))


def _gmlp_kernel(u_ref, v_ref, lg_ref, lb_ref, ws_ref, bs_ref, g_ref, o_ref):
    tl = u_ref.shape[0]
    gw = MIXER_W // GMLP_GROUPS
    lane_grp = lax.broadcasted_iota(jnp.int32, (GMLP_CHUNK, MIXER_W), 1) // gw
    for c in range(tl // GMLP_CHUNK):
        rows = slice(c * GMLP_CHUNK, (c + 1) * GMLP_CHUNK)
        v = _ln(_gelu(v_ref[rows, :])) * lg_ref[...] + lb_ref[...]
        vb = v.astype(BF16)
        stacked = jnp.concatenate(
            [jnp.where(lane_grp == g, vb, jnp.zeros_like(vb)) for g in range(GMLP_GROUPS)], axis=0)
        s = jnp.dot(ws_ref[...], stacked, preferred_element_type=F32) + bs_ref[...]
        y = _gelu(u_ref[rows, :]) * s
        o_ref[rows, :] = _rms_gain(y, g_ref[...]).astype(BF16)


def _gmlp_mix(z, ln_g, ln_b, ws_cat, bs_mat, gain, n_rows):
    tl = 512
    full = lambda shape: pl.BlockSpec(shape, lambda i: tuple(0 for _ in shape))
    return pl.pallas_call(
        _gmlp_kernel,
        grid=(n_rows // tl,),
        in_specs=[
            pl.BlockSpec((tl, MIXER_W), lambda i: (i, COL_GU)),
            pl.BlockSpec((tl, MIXER_W), lambda i: (i, COL_GV)),
            full((1, MIXER_W)), full((1, MIXER_W)),
            full((GMLP_CHUNK, GMLP_GROUPS * GMLP_CHUNK)),
            full((GMLP_CHUNK, MIXER_W)),
            full((1, MIXER_W)),
        ],
        out_specs=pl.BlockSpec((tl, MIXER_W), lambda i: (i, 0)),
        out_shape=jax.ShapeDtypeStruct((n_rows, MIXER_W), BF16),
        compiler_params=_cparams(("arbitrary",)),
        name="gmlp_mix",
    )(z, z, ln_g, ln_b, ws_cat, bs_mat, gain)


def _merge_kernel(*refs, alpha, n_lat_tiles, n_tiles):
    has_ctx = n_tiles > n_lat_tiles
    lat_refs, refs = refs[:3], refs[3:]
    if has_ctx:
        ctx_refs, refs = refs[:3], refs[3:]
    (yg_ref, w_ref, x_ref, g1_ref, lg_ref, lb_ref, sh_ref, sc_ref, wrh_ref, wrl_ref, rb_ref,
     xo_ref, h_ref, idx_ref, gate_ref, y_a, y_b) = refs
    s = pl.program_id(0)

    @pl.when(s == 0)
    def _():
        y_b[...] = jnp.zeros_like(y_b)

    def step(y_new, y_old):
        parts = [r[...] for r in lat_refs]
        if has_ctx:
            is_ctx = jnp.minimum(s, n_tiles - 1) >= n_lat_tiles
            parts = [jnp.where(is_ctx, c[...], p) for c, p in zip(ctx_refs, parts)]
        ymix = jnp.concatenate(parts + [yg_ref[...]], axis=-1)
        y_new[...] = jnp.dot(ymix, w_ref[...], preferred_element_type=F32)

        x1 = _ln(alpha * x_ref[...] + g1_ref[...] * y_old[...]) * lg_ref[...] + lb_ref[...]
        xo_ref[...] = x1
        h = _ln(x1) * (1.0 + sc_ref[...]) + sh_ref[...]
        h_ref[...] = h
        h_hi = h.astype(BF16)
        h_lo = (h - h_hi.astype(F32)).astype(BF16)
        logits = (jnp.dot(h_hi, wrh_ref[...], preferred_element_type=F32)
                  + (jnp.dot(h_lo, wrh_ref[...], preferred_element_type=F32)
                     + jnp.dot(h_hi, wrl_ref[...], preferred_element_type=F32)))
        idx, gate = _route_rows(logits, rb_ref[...])
        idx_ref[...] = idx
        gate_ref[...] = gate

    even = lax.rem(s, 2) == 0
    pl.when(even)(functools.partial(step, y_a, y_b))
    pl.when(jnp.logical_not(even))(functools.partial(step, y_b, y_a))


def _route_rows(logits, bias):
    neg = -1e30
    lane = lax.broadcasted_iota(jnp.int32, logits.shape, 1)
    lane_f = lane.astype(F32)
    valid = lane < N_EXPERTS
    per = N_EXPERTS // N_EXPERT_GROUPS
    s = jax.nn.sigmoid(logits)
    sel = jnp.where(valid, s + bias, neg)
    sh = [sel] + [pltpu.roll(sel, LOGIT_PAD - j, axis=1) for j in range(1, per)]
    pair = None
    for a in range(per):
        for b in range(a + 1, per):
            t = sh[a] + sh[b]
            pair = t if pair is None else jnp.maximum(pair, t)
    grp = jnp.where(valid & ((lane & (per - 1)) == 0), pair, neg)

    def first_max(v):
        m = jnp.max(v, axis=-1, keepdims=True)
        return jnp.min(jnp.where(v == m, lane_f, float(LOGIT_PAD)), axis=-1, keepdims=True)

    best = first_max(grp).astype(jnp.int32)
    shift = per.bit_length() - 1
    cand = jnp.where(valid & ((lane >> shift) == (best >> shift)), sel, neg)
    i1 = first_max(cand)
    i2 = first_max(jnp.where(lane_f == i1, neg, cand))
    w1 = jnp.sum(jnp.where(lane_f == i1, s, 0.0), axis=-1, keepdims=True)
    w2 = jnp.sum(jnp.where(lane_f == i2, s, 0.0), axis=-1, keepdims=True)
    tot = w1 + w2
    idx = jnp.where(lane == 0, i1, jnp.where(lane == 1, i2, 0.0)).astype(jnp.int32)
    gate = jnp.where(lane == 0, w1 / tot, jnp.where(lane == 1, w2 / tot, 0.0))
    return idx, gate


def _merge(ys_lat, ys_ctx, y_gm, w_out_bf, xa, mod3, ln_g, ln_b, wr_hi, wr_lo, rb_pad, n_rows, n_batch, seq,
           alpha):
    d = xa.shape[1]
    tm = 512
    nt = n_rows // tm
    n_lat_tiles = ys_lat[0].shape[0] // tm

    def cur(i):
        return jnp.minimum(i, nt - 1)

    def prv(i):
        return jnp.maximum(i - 1, 0)

    def mrow(i):
        return jnp.minimum((prv(i) * tm) // seq, n_batch)

    ymix = pl.BlockSpec((tm, MIXER_W), lambda i: (cur(i), 0))
    ylat = pl.BlockSpec((tm, MIXER_W), lambda i: (jnp.minimum(cur(i), n_lat_tiles - 1), 0))
    mixer_specs, mixer_args = [ylat] * 3, list(ys_lat)
    if ys_ctx is not None:
        n_ctx_tiles = nt - n_lat_tiles
        yctx = pl.BlockSpec((tm, MIXER_W), lambda i: (jnp.clip(cur(i) - n_lat_tiles, 0, n_ctx_tiles - 1), 0))
        mixer_specs, mixer_args = mixer_specs + [yctx] * 3, mixer_args + list(ys_ctx)
    rowvec = pl.BlockSpec((1, d), lambda i: (0, 0))
    return pl.pallas_call(
        functools.partial(_merge_kernel, alpha=alpha, n_lat_tiles=n_lat_tiles, n_tiles=nt),
        grid=(nt + 1,),
        in_specs=mixer_specs + [
            ymix,
            pl.BlockSpec((d, d), lambda i: (0, 0)),
            pl.BlockSpec((tm, d), lambda i: (prv(i), 0)),
            pl.BlockSpec((None, 1, d), lambda i: (mrow(i), 0, 2)),
            rowvec, rowvec,
            pl.BlockSpec((None, 1, d), lambda i: (mrow(i), 0, 3)),
            pl.BlockSpec((None, 1, d), lambda i: (mrow(i), 0, 4)),
            pl.BlockSpec((d, LOGIT_PAD), lambda i: (0, 0)),
            pl.BlockSpec((d, LOGIT_PAD), lambda i: (0, 0)),
            pl.BlockSpec((1, LOGIT_PAD), lambda i: (0, 0)),
        ],
        out_specs=[
            pl.BlockSpec((tm, d), lambda i: (prv(i), 0)),
            pl.BlockSpec((tm, d), lambda i: (prv(i), 0)),
            pl.BlockSpec((tm, LOGIT_PAD), lambda i: (prv(i), 0)),
            pl.BlockSpec((tm, LOGIT_PAD), lambda i: (prv(i), 0)),
        ],
        out_shape=[
            jax.ShapeDtypeStruct((n_rows, d), F32),
            jax.ShapeDtypeStruct((n_rows, d), F32),
            jax.ShapeDtypeStruct((n_rows, LOGIT_PAD), jnp.int32),
            jax.ShapeDtypeStruct((n_rows, LOGIT_PAD), F32),
        ],
        scratch_shapes=[pltpu.VMEM((tm, d), F32), pltpu.VMEM((tm, d), F32)],
        compiler_params=_cparams(("arbitrary",)),
        name="merge",
    )(*mixer_args, y_gm, w_out_bf, xa, mod3, ln_g, ln_b, mod3, mod3, wr_hi, wr_lo, rb_pad)


def _moe_kernel(be_ref, nu_ref, dest_ref, ws_ref, nxt_ref, h_hbm, wg_hbm, wu_hbm, wd_hbm, o_ref,
                xbuf, sem, tok_ref, stage_g, stage_u, stage_d, wsem, wg_s, wu_s, wd_s, *, layer):
    i = pl.program_id(0)
    n_used = nu_ref[0]
    ring = xbuf.shape[0]
    slot = lax.rem(i, ring)

    @pl.when(i == 0)
    def _():
        def clear(s, carry):
            tok_ref[s] = 0
            return carry

        def place(a, carry):
            tok_ref[dest_ref[a]] = lax.shift_right_logical(a, TOP_K.bit_length() - 1)
            return carry

        lax.fori_loop(0, tok_ref.shape[0], clear, 0, unroll=8)
        lax.fori_loop(0, dest_ref.shape[0], place, 0, unroll=8)

    def row_copy(blk, r, sl):
        tok = tok_ref[blk * MOE_BLOCK + r]
        return pltpu.make_async_copy(h_hbm.at[pl.ds(tok, 1)], xbuf.at[sl, pl.ds(r, 1)], sem.at[sl])

    def gather_start(blk, sl):
        def body(r, carry):
            row_copy(blk, r, sl).start()
            return carry

        lax.fori_loop(0, MOE_BLOCK, body, 0, unroll=8)

    def gather_wait(sl):
        pltpu.make_async_copy(h_hbm.at[pl.ds(0, MOE_BLOCK)], xbuf.at[sl], sem.at[sl]).wait()

    def weight_copies(e, ws):
        return (pltpu.make_async_copy(wg_hbm.at[layer, e], stage_g.at[ws], wsem.at[ws]),
                pltpu.make_async_copy(wu_hbm.at[layer, e], stage_u.at[ws], wsem.at[ws]),
                pltpu.make_async_copy(wd_hbm.at[layer, e], stage_d.at[ws], wsem.at[ws]))

    @pl.when(i == 0)
    def _():
        for cp in weight_copies(be_ref[0], ws_ref[0]):
            cp.start(priority=1)
        gather_start(0, 0)

        @pl.when(n_used > 1)
        def _():
            gather_start(1, 1)

    active = i < n_used
    new_expert = jnp.logical_or(i == 0, be_ref[i] != be_ref[jnp.maximum(i - 1, 0)])

    @pl.when(jnp.logical_and(active, new_expert))
    def _():
        ws = ws_ref[i]
        for cp in weight_copies(be_ref[i], ws):
            cp.wait()
        nxt = nxt_ref[i]

        @pl.when(nxt >= 0)
        def _():
            for cp in weight_copies(nxt, 1 - ws):
                cp.start(priority=1)

        for src, dst in ((stage_g, wg_s), (stage_u, wu_s), (stage_d, wd_s)):
            def cast(r0, src=src, dst=dst):
                dst[pl.ds(r0, 256), :] = src[ws, pl.ds(r0, 256), :].astype(BF16)

            _row_chunks(dst.shape[0], 256, cast)

    def ffn_block(prefetch_next):
        gather_wait(slot)
        x = xbuf[slot].astype(BF16)
        if prefetch_next:
            nxt = lax.rem(i + 2, ring)
            for r in range(MOE_BLOCK):
                row_copy(i + 2, r, nxt).start()
        g = jnp.dot(x, wg_s[...], preferred_element_type=F32)
        u = jnp.dot(x, wu_s[...], preferred_element_type=F32)
        a = (g * jax.nn.sigmoid(g) * u).astype(BF16)
        o_ref[...] = jnp.dot(a, wd_s[...], preferred_element_type=F32).astype(o_ref.dtype)

    has_next = i + 2 < n_used
    pl.when(jnp.logical_and(active, has_next))(functools.partial(ffn_block, True))
    pl.when(jnp.logical_and(active, jnp.logical_not(has_next)))(functools.partial(ffn_block, False))

    @pl.when(jnp.logical_not(active))
    def _():
        o_ref[...] = jnp.zeros_like(o_ref)


def _moe_experts(h, dest, plan, wg, wu, wd, layer):
    d = h.shape[1]
    n_slots = plan["n_slots"]
    de = wg.shape[3]
    n_blocks = n_slots // MOE_BLOCK
    hbm = pl.BlockSpec(memory_space=pl.ANY)
    grid_spec = pltpu.PrefetchScalarGridSpec(
        num_scalar_prefetch=5,
        grid=(n_blocks,),
        in_specs=[hbm, hbm, hbm, hbm],
        out_specs=pl.BlockSpec((MOE_BLOCK, d), lambda i, *_: (i, 0)),
        scratch_shapes=[
            pltpu.VMEM((3, MOE_BLOCK, d), F32),
            pltpu.SemaphoreType.DMA((3,)),
            pltpu.SMEM((n_slots,), jnp.int32),
            pltpu.VMEM((2, d, de), F32), pltpu.VMEM((2, d, de), F32), pltpu.VMEM((2, de, d), F32),
            pltpu.SemaphoreType.DMA((2,)),
            pltpu.VMEM((d, de), BF16), pltpu.VMEM((d, de), BF16), pltpu.VMEM((de, d), BF16),
        ],
    )
    return pl.pallas_call(
        functools.partial(_moe_kernel, layer=layer),
        grid_spec=grid_spec,
        out_shape=jax.ShapeDtypeStruct((n_slots, d), BF16),
        compiler_params=_cparams(("arbitrary",), MOE_VMEM_LIMIT),
        name="moe_experts",
    )(plan["block_e"], plan["n_used"], dest.reshape(-1), plan["wslot"], plan["next_e"], h, wg, wu, wd)


def _assignment_ranks(flat_e):
    a = flat_e.shape[0]
    blk = 128
    nb = a // blk
    onehot = (flat_e[:, None] == jnp.arange(N_EXPERTS)[None, :]).astype(BF16).reshape(nb, blk, N_EXPERTS)
    tri = (jnp.arange(blk)[:, None] >= jnp.arange(blk)[None, :]).astype(BF16)
    intra = jnp.einsum("ij,bjk->bik", tri, onehot, preferred_element_type=F32)
    bsum = intra[:, -1, :]
    before = (jnp.arange(nb)[:, None] > jnp.arange(nb)[None, :]).astype(BF16)
    offs = jnp.dot(before, bsum.astype(BF16), preferred_element_type=F32)
    csum = intra + offs[:, None, :]
    rank = jnp.sum(csum * onehot.astype(F32), axis=-1).reshape(a) - 1.0
    counts = offs[-1] + bsum[-1]
    return rank.astype(jnp.int32), counts.astype(jnp.int32)


def _dispatch_plan(idx):
    t = idx.shape[0]
    a = t * TOP_K
    flat_e = idx.reshape(a)
    rank, counts = _assignment_ranks(flat_e)
    padded = (counts + MOE_BLOCK - 1) // MOE_BLOCK * MOE_BLOCK
    pad_end = jnp.cumsum(padded)
    pad_start = pad_end - padded
    dest = pad_start[flat_e] + rank
    n_blocks = (a + N_EXPERTS * (MOE_BLOCK - 1) + MOE_BLOCK - 1) // MOE_BLOCK
    n_slots = n_blocks * MOE_BLOCK
    blk_start = jnp.arange(n_blocks, dtype=jnp.int32) * MOE_BLOCK
    block_e = jnp.minimum(
        jnp.sum((pad_end[None, :] <= blk_start[:, None]).astype(jnp.int32), axis=1), N_EXPERTS - 1)
    n_used = (pad_end[-1] // MOE_BLOCK).astype(jnp.int32).reshape(1)
    nonempty = counts > 0
    eid = jnp.arange(N_EXPERTS, dtype=jnp.int32)
    run_of_e = jnp.cumsum(nonempty.astype(jnp.int32)) - 1
    later = jnp.logical_and(eid[None, :] > eid[:, None], nonempty[None, :])
    nxt_of_e = jnp.min(jnp.where(later, eid[None, :], N_EXPERTS), axis=1)
    nxt_of_e = jnp.where(nxt_of_e == N_EXPERTS, -1, nxt_of_e)
    plan = {
        "n_slots": n_slots, "block_e": block_e.astype(jnp.int32), "n_used": n_used,
        "wslot": (run_of_e[block_e] % 2).astype(jnp.int32), "next_e": nxt_of_e[block_e].astype(jnp.int32),
    }
    return plan, dest.reshape(t, TOP_K)


def _final_kernel(x_ref, ya_ref, yb_ref, gate_ref, g2_ref, lg_ref, lb_ref, o_ref, *, alpha):
    f = ya_ref[...].astype(F32) * gate_ref[:, 0:1] + yb_ref[...].astype(F32) * gate_ref[:, 1:2]
    o_ref[...] = _ln(alpha * x_ref[...] + g2_ref[...] * f) * lg_ref[...] + lb_ref[...]


def _final_norm(x1, ya, yb, gate, mod3, ln_g, ln_b, n_batch, seq, alpha):
    n_rows, d = x1.shape
    tm = 512

    def mrow(i):
        return jnp.minimum((i * tm) // seq, n_batch)

    tile = pl.BlockSpec((tm, d), lambda i: (i, 0))
    rowvec = pl.BlockSpec((1, d), lambda i: (0, 0))
    return pl.pallas_call(
        functools.partial(_final_kernel, alpha=alpha),
        grid=(n_rows // tm,),
        in_specs=[tile, tile, tile, pl.BlockSpec((tm, LOGIT_PAD), lambda i: (i, 0)),
                  pl.BlockSpec((None, 1, d), lambda i: (mrow(i), 0, 5)), rowvec, rowvec],
        out_specs=tile,
        out_shape=jax.ShapeDtypeStruct((n_rows, d), F32),
        compiler_params=_cparams(("arbitrary",)),
        name="final_norm",
    )(x1, ya, yb, gate, mod3, ln_g, ln_b)


def _permute_in_cols(w):
    pool, q, k, v, hy, gm = jnp.split(w, (512, 1024, 1152, 1280, 2816), axis=-1)
    return jnp.concatenate([hy, pool, q, gm, k, v], axis=-1)


def kernel(x, c, ctx, c_ctx, w_ada, b_ada, w_in, w_out, mix_norm_g, pool_w, pool_scale, attn_sink,
           hy_conv_w, hy_conv_b, hy_w1, hy_b1, hy_f1, hy_w2, hy_b2, hy_f2, hy_w3, hy_skip,
           gm_ln_g, gm_ln_b, gm_ws, gm_bs, ln1_g, ln1_b, ln2_g, ln2_b, w_router, router_bias,
           w_gate, w_up, w_down):
    n_batch, seq, d = x.shape
    ctx_len = ctx.shape[1]
    depth = w_in.shape[0]
    grid_w = 64
    alpha = (2 * depth) ** 0.25
    n_lat = n_batch * seq
    n_ctx = n_batch * ctx_len

    xa = jnp.concatenate([x.reshape(n_lat, d), ctx.reshape(n_ctx, d)], axis=0)
    cvec = jnp.concatenate([c, c_ctx[None, :], jnp.zeros((8 - n_batch - 1, d), F32)], axis=0)
    mods = _ada_mods(cvec, w_ada, b_ada)
    cos_t, sin_t = _rope_tables(seq, grid_w)
    tabs_lat = _hy_tables(seq, *_hy_tiles(seq))
    tabs_ctx = _hy_tables(ctx_len, *_hy_tiles(ctx_len))
    wr_pad = jnp.pad(w_router, ((0, 0), (0, LOGIT_PAD - N_EXPERTS)))
    wr_hi = wr_pad.astype(BF16)
    wr_lo = (wr_pad - wr_hi.astype(F32)).astype(BF16)
    rb_pad = jnp.pad(router_bias.astype(F32), (0, LOGIT_PAD - N_EXPERTS)).reshape(1, LOGIT_PAD)
    row = lambda v: v.reshape(1, -1)

    for layer in range(depth):
        last = layer == depth - 1
        mod3 = mods[layer].reshape(8, 1, 6 * d)
        gains = mix_norm_g[layer].reshape(4, 1, MIXER_W)
        lp = {
            "hy_conv_w": hy_conv_w[layer], "hy_conv_b": row(hy_conv_b[layer]),
            "hy_w1p": jnp.pad(hy_w1[layer], ((0, HY_EMB_PAD - HY_EMB), (0, 0))),
            "hy_b1": row(hy_b1[layer]), "hy_f1": row(hy_f1[layer]),
            "hy_w2": hy_w2[layer], "hy_b2": row(hy_b2[layer]), "hy_f2": row(hy_f2[layer]),
            "hy_w3": hy_w3[layer], "hy_skip": hy_skip[layer],
        }
        z = _in_proj(xa, mod3, _permute_in_cols(w_in[layer]).astype(BF16), n_batch, seq)

        pool_bf = pool_w[layer].astype(BF16)
        ps = row(pool_scale[layer])
        ws_cat = jnp.transpose(gm_ws[layer], (1, 0, 2)).reshape(GMLP_CHUNK, GMLP_GROUPS * GMLP_CHUNK)
        bs_mat = jnp.repeat(gm_bs[layer].T, MIXER_W // GMLP_GROUPS, axis=1)
        n_rows = n_lat if last else n_lat + n_ctx

        y_pool = _pool_mix(z, pool_bf, ps, gains[0], n_batch, seq, 0)
        y_attn = _attn_lat(z, attn_sink[layer], cos_t, sin_t, gains[1], n_batch, seq, ctx_len)
        y_hy = _hyena_mix(z, lp, gains[2], n_batch, seq, 0, tabs_lat)
        y_gm = _gmlp_mix(z, row(gm_ln_g[layer]), row(gm_ln_b[layer]), ws_cat.astype(BF16), bs_mat,
                         gains[3], n_rows)
        ys_ctx = None
        if not last:
            ys_ctx = (_pool_mix(z, pool_bf, ps, gains[0], n_batch, ctx_len, n_lat // ctx_len),
                      _attn_ctx(z, attn_sink[layer], gains[1], n_batch, seq, ctx_len),
                      _hyena_mix(z, lp, gains[2], n_batch, ctx_len, n_lat, tabs_ctx))

        x1, h2, idx, gate = _merge((y_pool, y_attn, y_hy), ys_ctx, y_gm, w_out[layer].astype(BF16), xa, mod3,
                                   row(ln1_g[layer]), row(ln1_b[layer]), wr_hi, wr_lo, rb_pad,
                                   n_rows, n_batch, seq, alpha)

        plan, dest = _dispatch_plan(idx[:, :TOP_K])
        ys = _moe_experts(h2, dest, plan, w_gate, w_up, w_down, layer)
        xa = _final_norm(x1, ys[dest[:, 0]], ys[dest[:, 1]], gate, mod3, row(ln2_g[layer]), row(ln2_b[layer]),
                         n_batch, seq, alpha)
    return xa.reshape(n_batch, seq, d)
```

```python
import functools
import math

import jax
import jax.numpy as jnp
from jax import lax
from jax.experimental import pallas as pl
from jax.experimental.pallas import tpu as pltpu

F32 = jnp.float32
BF16 = jnp.bfloat16
HI = lax.Precision.HIGHEST

LN_EPS = 1e-6
MIXER_W = 512
HEAD_DIM = 64
N_HEADS = 8
N_KV_HEADS = 2
KV_GROUP = 4
KV_W = 128
WINDOW = 128
ROPE_THETA = 10000.0
POOL_WINDOWS = (2, 4, 8, 16)
POOL_GROUP = 128
POOL_HALO = 8
GMLP_CHUNK = 128
GMLP_GROUPS = 8
HY_EMB = 33
HY_EMB_PAD = 128
HY_TARGET = 1e-2
HY_FAST = 0.3
HY_SLOW = 1.5
N_EXPERTS = 32
N_EXPERT_GROUPS = 8
TOP_K = 2
D_EXPERT = 768
MOE_BLOCK = 256
LOGIT_PAD = 128

COL_HY = 0
COL_POOL = 3
COL_Q = 4
COL_GU = 5
COL_GV = 6
COL_K = 28
COL_V = 29
IN_W = 3840

VMEM_LIMIT = 56 * 1024 * 1024
MOE_VMEM_LIMIT = 59 * 1024 * 1024


def _cparams(sem, vmem_limit=VMEM_LIMIT):
    return pltpu.CompilerParams(dimension_semantics=sem, vmem_limit_bytes=vmem_limit)


def _ln(x):
    mu = jnp.mean(x, axis=-1, keepdims=True)
    xc = x - mu
    var = jnp.mean(xc * xc, axis=-1, keepdims=True)
    return xc * lax.rsqrt(var + LN_EPS)


def _rms_gain(y, g):
    return y * lax.rsqrt(jnp.mean(y * y, axis=-1, keepdims=True) + LN_EPS) * g


def _row_chunks(n_rows, chunk, body):
    n = n_rows // chunk
    if n == 1:
        body(0)
        return

    def step(i, carry):
        body(pl.multiple_of(i * chunk, chunk))
        return carry

    lax.fori_loop(0, n, step, 0)


def _ada_kernel(c_ref, w_ref, b_ref, o_ref):
    c = c_ref[...]
    s = c * jax.nn.sigmoid(c)
    o_ref[...] = jnp.dot(s, w_ref[...], precision=HI, preferred_element_type=F32) + b_ref[...]


def _ada_mods(cvec, w_ada, b_ada):
    depth, d, n = w_ada.shape
    tn = 1024
    return pl.pallas_call(
        _ada_kernel,
        grid=(depth, n // tn),
        in_specs=[
            pl.BlockSpec((8, d), lambda l, j: (0, 0)),
            pl.BlockSpec((None, d, tn), lambda l, j: (l, 0, j)),
            pl.BlockSpec((None, 1, tn), lambda l, j: (l, 0, j)),
        ],
        out_specs=pl.BlockSpec((None, 8, tn), lambda l, j: (l, 0, j)),
        out_shape=jax.ShapeDtypeStruct((depth, 8, n), F32),
        compiler_params=_cparams(("arbitrary", "arbitrary")),
        name="ada_mods",
    )(cvec, w_ada, b_ada.reshape(depth, 1, n))


def _inproj_kernel(x_ref, sh_ref, sc_ref, w_ref, o_ref, h_scr):
    @pl.when(pl.program_id(1) == 0)
    def _():
        sh = sh_ref[...]
        sc1 = 1.0 + sc_ref[...]

        def body(r0):
            x = x_ref[pl.ds(r0, 256), :]
            h_scr[pl.ds(r0, 256), :] = (_ln(x) * sc1 + sh).astype(BF16)

        _row_chunks(x_ref.shape[0], 256, body)

    o_ref[...] = jnp.dot(h_scr[...], w_ref[...], preferred_element_type=F32)


def _in_proj(xa, mod3, w_bf, n_batch, seq):
    t, d = xa.shape
    n = w_bf.shape[1]
    tn = 1280
    tm = next(m for m in (1024, 512, 256) if t % m == 0 and seq % m == 0)

    def mrow(i):
        return jnp.minimum((i * tm) // seq, n_batch)

    return pl.pallas_call(
        _inproj_kernel,
        grid=(t // tm, n // tn),
        in_specs=[
            pl.BlockSpec((tm, d), lambda i, j: (i, 0)),
            pl.BlockSpec((None, 1, d), lambda i, j: (mrow(i), 0, 0)),
            pl.BlockSpec((None, 1, d), lambda i, j: (mrow(i), 0, 1)),
            pl.BlockSpec((d, tn), lambda i, j: (0, j)),
        ],
        out_specs=pl.BlockSpec((tm, tn), lambda i, j: (i, j)),
        out_shape=jax.ShapeDtypeStruct((t, n), F32),
        scratch_shapes=[pltpu.VMEM((tm, d), BF16)],
        compiler_params=_cparams(("arbitrary", "arbitrary")),
        name="in_proj",
    )(xa, mod3, mod3, w_bf)


def _pool_kernel(z_ref, w_ref, ps_ref, g_ref, o_ref, buf):
    seq = z_ref.shape[0]
    rc = min(256, seq)
    zeros = jnp.zeros((POOL_HALO, MIXER_W), F32)
    buf[0:POOL_HALO, :] = zeros
    buf[seq + POOL_HALO:seq + 2 * POOL_HALO, :] = zeros

    def fill(r0):
        buf[pl.ds(r0 + POOL_HALO, rc), :] = z_ref[pl.ds(r0, rc), :]

    _row_chunks(seq, rc, fill)

    def body(r0):
        win = buf[pl.ds(r0, rc + 2 * POOL_HALO), :]
        t = r0 + lax.broadcasted_iota(jnp.int32, (rc, 1), 0)
        outs = []
        for gi, w in enumerate(POOL_WINDOWS):
            lanes = slice(gi * POOL_GROUP, (gi + 1) * POOL_GROUP)
            acc = None
            for j in range(-w // 2, w // 2):
                piece = win[POOL_HALO + j:POOL_HALO + j + rc, lanes]
                acc = piece if acc is None else acc + piece
            cnt = (jnp.minimum(t + w // 2, seq) - jnp.maximum(t - w // 2, 0)).astype(F32)
            dlt = acc / cnt - win[POOL_HALO:POOL_HALO + rc, lanes]
            outs.append(jnp.dot(dlt.astype(BF16), w_ref[gi], preferred_element_type=F32))
        y = jnp.concatenate(outs, axis=-1) * ps_ref[...]
        o_ref[pl.ds(r0, rc), :] = _rms_gain(y, g_ref[...]).astype(BF16)

    _row_chunks(seq, rc, body)


def _pool_mix(z, pool_w_bf, pool_scale, gain, n_seq, seq, row_blk0):
    return pl.pallas_call(
        _pool_kernel,
        grid=(n_seq,),
        in_specs=[
            pl.BlockSpec((seq, MIXER_W), lambda s: (row_blk0 + s, COL_POOL)),
            pl.BlockSpec((4, POOL_GROUP, POOL_GROUP), lambda s: (0, 0, 0)),
            pl.BlockSpec((1, MIXER_W), lambda s: (0, 0)),
            pl.BlockSpec((1, MIXER_W), lambda s: (0, 0)),
        ],
        out_specs=pl.BlockSpec((seq, MIXER_W), lambda s: (s, 0)),
        out_shape=jax.ShapeDtypeStruct((n_seq * seq, MIXER_W), BF16),
        scratch_shapes=[pltpu.VMEM((seq + 2 * POOL_HALO, MIXER_W), F32)],
        compiler_params=_cparams(("arbitrary",)),
        name="pool_mix",
    )(z, pool_w_bf, pool_scale, gain)


def _rope(x, cos, sin_signed):
    lane = lax.broadcasted_iota(jnp.int32, x.shape, 1)
    partner = jnp.where((lane & 31) < 16, pltpu.roll(x, 112, axis=1), pltpu.roll(x, 16, axis=1))
    return x * cos + partner * sin_signed


def _attend(qs, sink_col, parts):
    dn = (((1,), (1,)), ((), ()))
    scores = []
    for kk, _, mask in parts:
        s = lax.dot_general(qs, kk, dn, preferred_element_type=F32)
        if mask is not None:
            s = s + mask
        scores.append(s)
    m = sink_col
    for s in scores:
        m = jnp.maximum(m, jnp.max(s, axis=-1, keepdims=True))
    den = jnp.exp(sink_col - m)
    out = None
    for s, (_, vv, _) in zip(scores, parts):
        p = jnp.exp(s - m)
        den = den + jnp.sum(p, axis=-1, keepdims=True)
        o = jnp.dot(p.astype(BF16), vv, preferred_element_type=F32)
        out = o if out is None else out + o
    return out / den


def _heads_attend(q_all, sink_ref, tq, make_parts):
    cols = [None] * N_HEADS
    for kh in range(N_KV_HEADS):
        heads = [kh * KV_GROUP + g for g in range(KV_GROUP)]
        qs = jnp.concatenate([q_all[:, h * HEAD_DIM:(h + 1) * HEAD_DIM] for h in heads], axis=0)
        sink_col = jnp.concatenate([jnp.full((tq, 1), sink_ref[h], F32) for h in heads], axis=0)
        o = _attend(qs.astype(BF16), sink_col, make_parts(kh))
        for g, h in enumerate(heads):
            cols[h] = o[g * tq:(g + 1) * tq, :]
    return jnp.concatenate(cols, axis=-1)


def _attn_lat_kernel(sink_ref, q_ref, k_ref, v_ref, kc_ref, vc_ref, cos_ref, sin_ref, g_ref, o_ref):
    tq = q_ref.shape[0]
    seq = k_ref.shape[0]
    nband = tq + 2 * WINDOW
    i = pl.program_id(1)
    q0 = pl.multiple_of(i * tq, tq)
    k0 = pl.multiple_of(jnp.clip(q0 - WINDOW, 0, seq - nband), WINDOW)

    cq = cos_ref[pl.ds(q0, tq), :]
    sq = sin_ref[pl.ds(q0, tq), :]
    q = q_ref[...]
    q_all = jnp.concatenate(
        [_rope(q[:, c * 128:(c + 1) * 128], cq, sq) for c in range(MIXER_W // 128)], axis=-1)
    q_all = q_all * (HEAD_DIM ** -0.5)

    kb = _rope(k_ref[pl.ds(k0, nband), :], cos_ref[pl.ds(k0, nband), :],
               sin_ref[pl.ds(k0, nband), :]).astype(BF16)
    vb = v_ref[pl.ds(k0, nband), :].astype(BF16)
    kc = kc_ref[...].astype(BF16)
    vc = vc_ref[...].astype(BF16)

    qpos = q0 + lax.broadcasted_iota(jnp.int32, (tq, nband), 0)
    kpos = k0 + lax.broadcasted_iota(jnp.int32, (tq, nband), 1)
    band = jnp.where(jnp.abs(qpos - kpos) <= WINDOW, 0.0, -1e30)
    mask = jnp.concatenate([band] * KV_GROUP, axis=0)

    def make_parts(kh):
        hs = slice(kh * HEAD_DIM, (kh + 1) * HEAD_DIM)
        return [(kb[:, hs], vb[:, hs], mask), (kc[:, hs], vc[:, hs], None)]

    y = _heads_attend(q_all, sink_ref, tq, make_parts)
    o_ref[...] = _rms_gain(y, g_ref[...]).astype(BF16)


def _attn_ctx_kernel(sink_ref, q_ref, kc_ref, vc_ref, g_ref, o_ref):
    tq = q_ref.shape[0]
    q_all = q_ref[...] * (HEAD_DIM ** -0.5)
    kc = kc_ref[...].astype(BF16)
    vc = vc_ref[...].astype(BF16)

    def make_parts(kh):
        hs = slice(kh * HEAD_DIM, (kh + 1) * HEAD_DIM)
        return [(kc[:, hs], vc[:, hs], None)]

    y = _heads_attend(q_all, sink_ref, tq, make_parts)
    o_ref[...] = _rms_gain(y, g_ref[...]).astype(BF16)


def _attn_lat(z, sink, cos_t, sin_t, gain, n_batch, seq, ctx_len):
    tq = 256
    nq = seq // tq
    ctx_blk0 = (n_batch * seq) // ctx_len
    smem = pl.BlockSpec(memory_space=pltpu.SMEM)
    return pl.pallas_call(
        _attn_lat_kernel,
        grid=(n_batch, nq),
        in_specs=[
            smem,
            pl.BlockSpec((tq, MIXER_W), lambda b, i: (b * nq + i, COL_Q)),
            pl.BlockSpec((seq, KV_W), lambda b, i: (b, COL_K)),
            pl.BlockSpec((seq, KV_W), lambda b, i: (b, COL_V)),
            pl.BlockSpec((ctx_len, KV_W), lambda b, i: (ctx_blk0 + b, COL_K)),
            pl.BlockSpec((ctx_len, KV_W), lambda b, i: (ctx_blk0 + b, COL_V)),
            pl.BlockSpec((seq, 128), lambda b, i: (0, 0)),
            pl.BlockSpec((seq, 128), lambda b, i: (0, 0)),
            pl.BlockSpec((1, MIXER_W), lambda b, i: (0, 0)),
        ],
        out_specs=pl.BlockSpec((tq, MIXER_W), lambda b, i: (b * nq + i, 0)),
        out_shape=jax.ShapeDtypeStruct((n_batch * seq, MIXER_W), BF16),
        compiler_params=_cparams(("arbitrary", "arbitrary")),
        name="attn_lat",
    )(sink, z, z, z, z, z, cos_t, sin_t, gain)


def _attn_ctx(z, sink, gain, n_batch, seq, ctx_len):
    ctx_blk0 = (n_batch * seq) // ctx_len
    smem = pl.BlockSpec(memory_space=pltpu.SMEM)
    return pl.pallas_call(
        _attn_ctx_kernel,
        grid=(n_batch,),
        in_specs=[
            smem,
            pl.BlockSpec((ctx_len, MIXER_W), lambda b: (ctx_blk0 + b, COL_Q)),
            pl.BlockSpec((ctx_len, KV_W), lambda b: (ctx_blk0 + b, COL_K)),
            pl.BlockSpec((ctx_len, KV_W), lambda b: (ctx_blk0 + b, COL_V)),
            pl.BlockSpec((1, MIXER_W), lambda b: (0, 0)),
        ],
        out_specs=pl.BlockSpec((ctx_len, MIXER_W), lambda b: (b, 0)),
        out_shape=jax.ShapeDtypeStruct((n_batch * ctx_len, MIXER_W), BF16),
        compiler_params=_cparams(("arbitrary",)),
        name="attn_ctx",
    )(sink, z, z, z, gain)


def _rope_tables(seq, grid_w):
    nf = HEAD_DIM // 4
    inv = ROPE_THETA ** (-jnp.arange(nf, dtype=F32) / nf)
    t = jnp.arange(seq)
    row = (t // grid_w).astype(F32)[:, None] * inv[None, :]
    col = (t % grid_w).astype(F32)[:, None] * inv[None, :]
    cos_h = jnp.concatenate([jnp.cos(row), jnp.cos(row), jnp.cos(col), jnp.cos(col)], axis=-1)
    sin_h = jnp.concatenate([-jnp.sin(row), jnp.sin(row), -jnp.sin(col), jnp.sin(col)], axis=-1)
    return jnp.tile(cos_h, (1, 2)), jnp.tile(sin_h, (1, 2))


def _hy_conv_kernel(*refs):
    z_refs, zp_refs, zn_refs = refs[0:3], refs[3:6], refs[6:9]
    w_ref, b_ref, g_ref, u_ref = refs[9:]
    i = pl.program_id(1)
    th = z_refs[0].shape[0] // 2
    has_prev = (i > 0).astype(F32)
    has_next = (i < pl.num_programs(1) - 1).astype(F32)
    row = lax.broadcasted_iota(jnp.int32, (th, 128), 0)
    for part in range(3):
        ze = z_refs[part][pl.ds(0, th, stride=2), :]
        zo = z_refs[part][pl.ds(1, th, stride=2), :]
        prev_row = zp_refs[part][POOL_HALO - 1:POOL_HALO, :] * has_prev
        next_row = zn_refs[part][0:1, :] * has_next
        zo_m = jnp.where(row == 0, prev_row, pltpu.roll(zo, 1, axis=0))
        ze_p = jnp.where(row == th - 1, next_row, pltpu.roll(ze, th - 1, axis=0))
        w0, w1, w2, b = w_ref[part, 0:1, :], w_ref[part, 1:2, :], w_ref[part, 2:3, :], b_ref[part]
        ye = b + zo_m * w0 + ze * w1 + zo * w2
        yo = b + ze * w0 + zo * w1 + ze_p * w2
        g_ref[part, 0] = ye
        g_ref[part, 1] = yo
        if part == 2:
            u_ref[0] = ye.astype(BF16)
            u_ref[1] = yo.astype(BF16)


def _hy_conv(z, conv_w, conv_b, n_seq, seq, row0):
    tl = min(2048, seq)
    nt = seq // tl
    blk0 = row0 // tl
    hb0 = row0 // POOL_HALO
    hpt = tl // POOL_HALO
    last_halo = (row0 + n_seq * seq) // POOL_HALO - 1
    cpp = MIXER_W // 128
    ncol = n_seq * MIXER_W

    def main_spec(part):
        return pl.BlockSpec((tl, 128), lambda s, i, c: (blk0 + s * nt + i, part * cpp + c))

    def prev_spec(part):
        return pl.BlockSpec((POOL_HALO, 128),
                            lambda s, i, c: (jnp.maximum(hb0 + (s * nt + i) * hpt - 1, 0), part * cpp + c))

    def next_spec(part):
        return pl.BlockSpec((POOL_HALO, 128),
                            lambda s, i, c: (jnp.minimum(hb0 + (s * nt + i + 1) * hpt, last_halo), part * cpp + c))

    w3 = conv_w.reshape(3, 3, MIXER_W).transpose(1, 0, 2)
    b3 = conv_b.reshape(3, 1, MIXER_W)
    return pl.pallas_call(
        _hy_conv_kernel,
        grid=(n_seq, nt, cpp),
        in_specs=[main_spec(p) for p in range(3)] + [prev_spec(p) for p in range(3)]
        + [next_spec(p) for p in range(3)]
        + [pl.BlockSpec((3, 3, 128), lambda s, i, c: (0, 0, c)),
           pl.BlockSpec((3, 1, 128), lambda s, i, c: (0, 0, c))],
        out_specs=[
            pl.BlockSpec((3, 2, tl // 2, 128), lambda s, i, c: (0, 0, i, s * cpp + c)),
            pl.BlockSpec((2, tl // 2, 128), lambda s, i, c: (0, i, s * cpp + c)),
        ],
        out_shape=[jax.ShapeDtypeStruct((3, 2, seq // 2, ncol), F32),
                   jax.ShapeDtypeStruct((2, seq // 2, ncol), BF16)],
        compiler_params=_cparams(("arbitrary", "arbitrary", "arbitrary")),
        name="hy_conv",
    )(*([z] * 9), w3, b3)


def _hy_filter_kernel(feat_ref, dec_ref, w1_ref, b1_ref, f1_ref, w2_ref, b2_ref, f2_ref, w3_ref, o_ref):
    tl = feat_ref.shape[0]
    h = jnp.dot(feat_ref[...], w1_ref[...], precision=HI, preferred_element_type=F32) + b1_ref[...]
    h = jnp.sin(f1_ref[...] * h)
    h = jnp.dot(h, w2_ref[...], precision=HI, preferred_element_type=F32) + b2_ref[...]
    h = jnp.sin(f2_ref[...] * h)
    filt = jnp.dot(h, w3_ref[...], precision=HI, preferred_element_type=F32)
    dec = dec_ref[...]
    t = 2 * (pl.program_id(1) * tl + lax.broadcasted_iota(jnp.int32, (tl, 1), 0)) + pl.program_id(0)
    not_first = (t > 0).astype(F32)
    for o in range(2):
        hf = filt[:, (2 * o) * MIXER_W:(2 * o + 1) * MIXER_W] * dec
        hb = filt[:, (2 * o + 1) * MIXER_W:(2 * o + 2) * MIXER_W] * dec * not_first
        o_ref[:, o * MIXER_W:(o + 1) * MIXER_W] = (hf + hb).astype(BF16)
        o_ref[:, (2 + o) * MIXER_W:(3 + o) * MIXER_W] = (hb - hf).astype(BF16)


def _hy_filters(feat, decay, w1p, b1, f1, w2, b2, f2, w3):
    half = feat.shape[1]
    tl = min(512, half)
    hid = w2.shape[0]
    full = lambda shape: pl.BlockSpec(shape, lambda p, i: tuple(0 for _ in shape))
    return pl.pallas_call(
        _hy_filter_kernel,
        grid=(2, half // tl),
        in_specs=[
            pl.BlockSpec((None, tl, HY_EMB_PAD), lambda p, i: (p, i, 0)),
            pl.BlockSpec((None, tl, MIXER_W), lambda p, i: (p, i, 0)),
            full((HY_EMB_PAD, hid)), full((1, hid)), full((1, hid)),
            full((hid, hid)), full((1, hid)), full((1, hid)),
            full((hid, 4 * MIXER_W)),
        ],
        out_specs=pl.BlockSpec((None, tl, 4 * MIXER_W), lambda p, i: (p, i, 0)),
        out_shape=jax.ShapeDtypeStruct((2, half, 4 * MIXER_W), BF16),
        compiler_params=_cparams(("arbitrary", "arbitrary")),
        name="hy_filters",
    )(feat, decay, w1p, b1, f1, w2, b2, f2, w3)


def _hy_kspec_kernel(te_ref, to_ref, ab_ref, o_ref, *, inv_len):
    hb = te_ref.shape[0] // 2
    half = ab_ref.shape[2] // 2
    ec = jnp.dot(te_ref[0:hb, :], ab_ref[0, :, 0:half], preferred_element_type=F32)
    oc = jnp.dot(to_ref[0:hb, :], ab_ref[1, :, 0:half], preferred_element_type=F32)
    es = jnp.dot(te_ref[hb:, :], ab_ref[0, :, half:], preferred_element_type=F32)
    os_ = jnp.dot(to_ref[hb:, :], ab_ref[1, :, half:], preferred_element_type=F32)
    o_ref[0, 0:hb, :] = (ec + oc) * inv_len
    o_ref[0, hb:, :] = (es + os_) * inv_len
    o_ref[1, 0:hb, :] = (ec - oc) * inv_len
    o_ref[1, hb:, :] = (os_ - es) * inv_len


def _hy_kspec(fwd, ab, tm):
    _, seq, half = fwd.shape
    return pl.pallas_call(
        functools.partial(_hy_kspec_kernel, inv_len=1.0 / seq),
        grid=(seq // tm,),
        in_specs=[
            pl.BlockSpec((None, tm, half), lambda i: (0, i, 0)),
            pl.BlockSpec((None, tm, half), lambda i: (1, i, 0)),
            pl.BlockSpec((2, half, 4 * MIXER_W), lambda i: (0, 0, 0)),
        ],
        out_specs=pl.BlockSpec((2, tm, 2 * MIXER_W), lambda i: (0, i, 0)),
        out_shape=jax.ShapeDtypeStruct((2, seq, 2 * MIXER_W), F32),
        compiler_params=_cparams(("arbitrary",)),
        name="hy_kspec",
    )(fwd, fwd, ab)


def _hy_fwd_kernel(te_ref, to_ref, u_ref, k_ref, o_ref):
    hb = te_ref.shape[0] // 2
    reps = u_ref.shape[2] // MIXER_W
    ep = jnp.dot(te_ref[...], u_ref[0], preferred_element_type=F32)
    op = jnp.dot(to_ref[...], u_ref[1], preferred_element_type=F32)
    ec, es, oc, os_ = ep[0:hb, :], ep[hb:, :], op[0:hb, :], op[hb:, :]
    tile = lambda v: jnp.concatenate([v] * reps, axis=-1)
    p, q, pm, qm = ec + oc, es + os_, ec - oc, os_ - es
    kr, ki = tile(k_ref[0, 0:hb, :]), tile(k_ref[0, hb:, :])
    krm, kim = tile(k_ref[1, 0:hb, :]), tile(k_ref[1, hb:, :])
    yr, yn = p * kr + q * ki, q * kr - p * ki
    yrm, ynm = pm * krm + qm * kim, qm * krm - pm * kim
    o_ref[0, 0:hb, :] = (yr + yrm).astype(BF16)
    o_ref[0, hb:, :] = (yn - ynm).astype(BF16)
    o_ref[1, 0:hb, :] = (yr - yrm).astype(BF16)
    o_ref[1, hb:, :] = (yn + ynm).astype(BF16)


def _hy_fwd(fwd, u, kspec, order, tm, tn):
    _, seq, half = fwd.shape
    ncol = u.shape[2]
    return pl.pallas_call(
        _hy_fwd_kernel,
        grid=(ncol // tn, seq // tm),
        in_specs=[
            pl.BlockSpec((None, tm, half), lambda j, i: (0, i, 0)),
            pl.BlockSpec((None, tm, half), lambda j, i: (1, i, 0)),
            pl.BlockSpec((2, half, tn), lambda j, i: (0, 0, j)),
            pl.BlockSpec((2, tm, MIXER_W), lambda j, i: (0, i, order)),
        ],
        out_specs=pl.BlockSpec((2, tm, tn), lambda j, i: (0, i, j)),
        out_shape=jax.ShapeDtypeStruct((2, seq, ncol), BF16),
        compiler_params=_cparams(("arbitrary", "arbitrary")),
        name="hy_fwd",
    )(fwd, fwd, u, kspec)


def _hy_inv_kernel(ie_ref, io_ref, y_ref, u_ref, gate_ref, skip_ref, *rest, final):
    reps = u_ref.shape[2] // MIXER_W
    skip = jnp.concatenate([skip_ref[...]] * reps, axis=-1)
    for par, inv_ref in enumerate((ie_ref, io_ref)):
        conv = jnp.dot(inv_ref[...], y_ref[par], preferred_element_type=F32)
        out = gate_ref[par] * (conv + u_ref[par].astype(F32) * skip)
        if final:
            g_ref, o_ref = rest
            for r in range(reps):
                o_ref[r, par] = _rms_gain(out[:, r * MIXER_W:(r + 1) * MIXER_W], g_ref[...]).astype(BF16)
        else:
            (o_ref,) = rest
            o_ref[par] = out.astype(BF16)


def _hy_inv(inv, yspec, u, conv, part, skip, gain, final, tm, tn):
    _, half, seq = inv.shape
    ncol = u.shape[2]
    reps = tn // MIXER_W
    in_specs = [
        pl.BlockSpec((None, tm, seq), lambda j, i: (0, i, 0)),
        pl.BlockSpec((None, tm, seq), lambda j, i: (1, i, 0)),
        pl.BlockSpec((2, seq, tn), lambda j, i: (0, 0, j)),
        pl.BlockSpec((2, tm, tn), lambda j, i: (0, i, j)),
        pl.BlockSpec((None, 2, tm, tn), lambda j, i: (part, 0, i, j)),
        pl.BlockSpec((1, MIXER_W), lambda j, i: (0, 0)),
    ]
    args = [inv, inv, yspec, u, conv, skip]
    if final:
        in_specs.append(pl.BlockSpec((1, MIXER_W), lambda j, i: (0, 0)))
        args.append(gain)
        out_spec = pl.BlockSpec((reps, 2, tm, MIXER_W), lambda j, i: (j, 0, i, 0))
        out_shape = jax.ShapeDtypeStruct((ncol // MIXER_W, 2, half, MIXER_W), BF16)
    else:
        out_spec = pl.BlockSpec((2, tm, tn), lambda j, i: (0, i, j))
        out_shape = jax.ShapeDtypeStruct((2, half, ncol), BF16)
    return pl.pallas_call(
        functools.partial(_hy_inv_kernel, final=final),
        grid=(ncol // tn, half // tm),
        in_specs=in_specs,
        out_specs=out_spec,
        out_shape=out_shape,
        compiler_params=_cparams(("arbitrary", "arbitrary")),
        name="hy_inv",
    )(*args)


def _hy_table_kernel(hc_ref, hs_ref, lc_ref, ls_ref, o_ref, *, interleave):
    n_hi, split = hc_ref.shape[0], lc_ref.shape[0]
    lc, ls = lc_ref[...], ls_ref[...]
    for g in range(n_hi):
        hc, hs = hc_ref[g:g + 1, :], hs_ref[g:g + 1, :]
        cos_t = (hc * lc - hs * ls).astype(BF16)
        sin_t = (hs * lc + hc * ls).astype(BF16)
        rows = slice(g * split, (g + 1) * split)
        if interleave == 0:
            o_ref[rows, :] = cos_t
            o_ref[n_hi * split + g * split:n_hi * split + (g + 1) * split, :] = sin_t
        else:
            w = interleave
            for c in range(cos_t.shape[1] // w):
                o_ref[rows, 2 * c * w:(2 * c + 1) * w] = cos_t[:, c * w:(c + 1) * w]
                o_ref[rows, (2 * c + 1) * w:(2 * c + 2) * w] = sin_t[:, c * w:(c + 1) * w]


def _hy_tables(seq, tm, tm_inv):
    half = seq // 2
    hb = tm // 2
    split = 32
    unit = 2.0 * math.pi / (4 * seq)
    ar = lambda n: jnp.arange(n, dtype=jnp.int32)

    def cos_sin(a, b):
        ang = ((a[..., :, None] * b[..., None, :]) % (4 * seq)).astype(F32) * unit
        return jnp.cos(ang), jnp.sin(ang)

    par = ar(2)[:, None]
    n_all = 2 * ar(half)[None, :] + par
    k_odd = 2 * ar(half) + 1
    f_hc, f_hs = cos_sin(jnp.broadcast_to(2 * split * ar(half // split), (2, half // split)), n_all)
    f_lc, f_ls = cos_sin(jnp.broadcast_to(2 * ar(split) + 1, (2, split)), n_all)
    i_hc, i_hs = cos_sin(2 * split * ar(half // split), k_odd)
    i_lc, i_ls = cos_sin(2 * ar(split)[None, :] + par, jnp.broadcast_to(k_odd, (2, half)))

    n_hi_f = hb // split
    fwd = pl.pallas_call(
        functools.partial(_hy_table_kernel, interleave=0),
        grid=(2, half // hb),
        in_specs=[
            pl.BlockSpec((None, n_hi_f, half), lambda p, i: (p, i, 0)),
            pl.BlockSpec((None, n_hi_f, half), lambda p, i: (p, i, 0)),
            pl.BlockSpec((None, split, half), lambda p, i: (p, 0, 0)),
            pl.BlockSpec((None, split, half), lambda p, i: (p, 0, 0)),
        ],
        out_specs=pl.BlockSpec((None, tm, half), lambda p, i: (p, i, 0)),
        out_shape=jax.ShapeDtypeStruct((2, seq, half), BF16),
        compiler_params=_cparams(("arbitrary", "arbitrary")),
        name="hy_table_fwd",
    )(f_hc, f_hs, f_lc, f_ls)
    n_hi_i = tm_inv // split
    inv = pl.pallas_call(
        functools.partial(_hy_table_kernel, interleave=hb),
        grid=(2, half // tm_inv),
        in_specs=[
            pl.BlockSpec((n_hi_i, half), lambda p, i: (i, 0)),
            pl.BlockSpec((n_hi_i, half), lambda p, i: (i, 0)),
            pl.BlockSpec((None, split, half), lambda p, i: (p, 0, 0)),
            pl.BlockSpec((None, split, half), lambda p, i: (p, 0, 0)),
        ],
        out_specs=pl.BlockSpec((None, tm_inv, seq), lambda p, i: (p, i, 0)),
        out_shape=jax.ShapeDtypeStruct((2, half, seq), BF16),
        compiler_params=_cparams(("arbitrary", "arbitrary")),
        name="hy_table_inv",
    )(i_hc, i_hs, i_lc, i_ls)
    return fwd, inv


def _hy_features(seq):
    t = jnp.linspace(0.0, 1.0, seq, dtype=F32)[:, None]
    bands = (HY_EMB - 1) // 2
    freqs = jnp.linspace(1e-4, bands - 1, bands, dtype=F32)[None, :]
    ang = (2.0 * math.pi / seq) * jnp.arange(seq, dtype=F32)[:, None] * freqs
    feat = jnp.concatenate([t, jnp.cos(ang), -jnp.sin(ang)], -1)
    feat = jnp.pad(feat, ((0, 0), (0, HY_EMB_PAD - HY_EMB)))
    deltas = jnp.abs(jnp.linspace(math.log(HY_TARGET) / HY_SLOW, math.log(HY_TARGET) / HY_FAST,
                                  MIXER_W, dtype=F32))
    decay = jnp.exp(-t * deltas[None, :])
    split = lambda a: jnp.stack([a[0::2], a[1::2]], axis=0)
    return split(feat), split(decay)


def _hy_tiles(seq):
    return min(512, seq), min(256, seq // 2)


def _hyena_mix(z, lp, gain, n_seq, seq, row0, tables):
    tm_f, tm_i = _hy_tiles(seq)
    ncol = n_seq * MIXER_W
    tn = min(1024, ncol)
    fwd, inv = tables
    feat, decay = _hy_features(seq)
    ab = _hy_filters(feat, decay, lp["hy_w1p"], lp["hy_b1"], lp["hy_f1"], lp["hy_w2"], lp["hy_b2"],
                     lp["hy_f2"], lp["hy_w3"])
    kspec = _hy_kspec(fwd, ab, tm_f)
    conv, u = _hy_conv(z, lp["hy_conv_w"], lp["hy_conv_b"], n_seq, seq, row0)
    y0 = _hy_fwd(fwd, u, kspec, 0, tm_f, tn)
    u1 = _hy_inv(inv, y0, u, conv, 0, lp["hy_skip"][0:1], None, False, tm_i, tn)
    y1 = _hy_fwd(fwd, u1, kspec, 1, tm_f, tn)
    out = _hy_inv(inv, y1, u1, conv, 1, lp["hy_skip"][1:2], gain, True, tm_i, tn)
    return jnp.transpose(out, (0, 2, 1, 3)).reshape(n_seq * seq, MIXER_W)


def _gelu(x):
    c = math.sqrt(2.0 / math.pi)
    return 0.5 * x * (1.0 + jnp.tanh(c * (x + 0.044715 * (x * x * x))))


def _gmlp_kernel(u_ref, v_ref, lg_ref, lb_ref, ws_ref, bs_ref, g_ref, o_ref):
    tl = u_ref.shape[0]
    gw = MIXER_W // GMLP_GROUPS
    lane_grp = lax.broadcasted_iota(jnp.int32, (GMLP_CHUNK, MIXER_W), 1) // gw
    for c in range(tl // GMLP_CHUNK):
        rows = slice(c * GMLP_CHUNK, (c + 1) * GMLP_CHUNK)
        v = _ln(_gelu(v_ref[rows, :])) * lg_ref[...] + lb_ref[...]
        vb = v.astype(BF16)
        stacked = jnp.concatenate(
            [jnp.where(lane_grp == g, vb, jnp.zeros_like(vb)) for g in range(GMLP_GROUPS)], axis=0)
        s = jnp.dot(ws_ref[...], stacked, preferred_element_type=F32) + bs_ref[...]
        y = _gelu(u_ref[rows, :]) * s
        o_ref[rows, :] = _rms_gain(y, g_ref[...]).astype(BF16)


def _gmlp_mix(z, ln_g, ln_b, ws_cat, bs_mat, gain, n_rows):
    tl = 512
    full = lambda shape: pl.BlockSpec(shape, lambda i: tuple(0 for _ in shape))
    return pl.pallas_call(
        _gmlp_kernel,
        grid=(n_rows // tl,),
        in_specs=[
            pl.BlockSpec((tl, MIXER_W), lambda i: (i, COL_GU)),
            pl.BlockSpec((tl, MIXER_W), lambda i: (i, COL_GV)),
            full((1, MIXER_W)), full((1, MIXER_W)),
            full((GMLP_CHUNK, GMLP_GROUPS * GMLP_CHUNK)),
            full((GMLP_CHUNK, MIXER_W)),
            full((1, MIXER_W)),
        ],
        out_specs=pl.BlockSpec((tl, MIXER_W), lambda i: (i, 0)),
        out_shape=jax.ShapeDtypeStruct((n_rows, MIXER_W), BF16),
        compiler_params=_cparams(("arbitrary",)),
        name="gmlp_mix",
    )(z, z, ln_g, ln_b, ws_cat, bs_mat, gain)


def _merge_kernel(*refs, alpha, n_lat_tiles, n_tiles):
    has_ctx = n_tiles > n_lat_tiles
    lat_refs, refs = refs[:3], refs[3:]
    if has_ctx:
        ctx_refs, refs = refs[:3], refs[3:]
    (yg_ref, w_ref, x_ref, g1_ref, lg_ref, lb_ref, sh_ref, sc_ref, wrc_ref, rb_ref,
     xo_ref, h_ref, idx_ref, gate_ref, y_a, y_b) = refs
    s = pl.program_id(0)

    @pl.when(s == 0)
    def _():
        y_b[...] = jnp.zeros_like(y_b)

    def step(y_new, y_old):
        parts = [r[...] for r in lat_refs]
        if has_ctx:
            is_ctx = jnp.minimum(s, n_tiles - 1) >= n_lat_tiles
            parts = [jnp.where(is_ctx, c[...], p) for c, p in zip(ctx_refs, parts)]
        ymix = jnp.concatenate(parts + [yg_ref[...]], axis=-1)
        y_new[...] = jnp.dot(ymix, w_ref[...], preferred_element_type=F32)

        x1 = _ln(alpha * x_ref[...] + g1_ref[...] * y_old[...]) * lg_ref[...] + lb_ref[...]
        xo_ref[...] = x1
        h = _ln(x1) * (1.0 + sc_ref[...]) + sh_ref[...]
        h_ref[...] = h
        h_hi = h.astype(BF16)
        h_lo = (h - h_hi.astype(F32)).astype(BF16)
        both = jnp.dot(h_hi, wrc_ref[...], preferred_element_type=F32)
        logits = both[:, :LOGIT_PAD] + (jnp.dot(h_lo, wrc_ref[:, :LOGIT_PAD], preferred_element_type=F32)
                                        + both[:, LOGIT_PAD:])
        idx, gate = _route_rows(logits, rb_ref[...])
        idx_ref[...] = idx
        gate_ref[...] = gate

    even = lax.rem(s, 2) == 0
    pl.when(even)(functools.partial(step, y_a, y_b))
    pl.when(jnp.logical_not(even))(functools.partial(step, y_b, y_a))


def _route_rows(logits, bias):
    neg = -1e30
    lane = lax.broadcasted_iota(jnp.int32, logits.shape, 1)
    lane_f = lane.astype(F32)
    valid = lane < N_EXPERTS
    per = N_EXPERTS // N_EXPERT_GROUPS
    s = jax.nn.sigmoid(logits)
    sel = jnp.where(valid, s + bias, neg)
    sh = [sel] + [pltpu.roll(sel, LOGIT_PAD - j, axis=1) for j in range(1, per)]
    pair = None
    for a in range(per):
        for b in range(a + 1, per):
            t = sh[a] + sh[b]
            pair = t if pair is None else jnp.maximum(pair, t)
    grp = jnp.where(valid & ((lane & (per - 1)) == 0), pair, neg)

    def first_max(v):
        m = jnp.max(v, axis=-1, keepdims=True)
        return jnp.min(jnp.where(v == m, lane_f, float(LOGIT_PAD)), axis=-1, keepdims=True)

    best = first_max(grp).astype(jnp.int32)
    shift = per.bit_length() - 1
    cand = jnp.where(valid & ((lane >> shift) == (best >> shift)), sel, neg)
    i1 = first_max(cand)
    i2 = first_max(jnp.where(lane_f == i1, neg, cand))
    w1 = jnp.sum(jnp.where(lane_f == i1, s, 0.0), axis=-1, keepdims=True)
    w2 = jnp.sum(jnp.where(lane_f == i2, s, 0.0), axis=-1, keepdims=True)
    tot = w1 + w2
    idx = jnp.where(lane == 0, i1, jnp.where(lane == 1, i2, 0.0)).astype(jnp.int32)
    gate = jnp.where(lane == 0, w1 / tot, jnp.where(lane == 1, w2 / tot, 0.0))
    return idx, gate


def _merge(ys_lat, ys_ctx, y_gm, w_out_bf, xa, mod3, ln_g, ln_b, wr_cat, rb_pad, n_rows, n_batch, seq, alpha):
    d = xa.shape[1]
    tm = 512
    nt = n_rows // tm
    n_lat_tiles = ys_lat[0].shape[0] // tm

    def cur(i):
        return jnp.minimum(i, nt - 1)

    def prv(i):
        return jnp.maximum(i - 1, 0)

    def mrow(i):
        return jnp.minimum((prv(i) * tm) // seq, n_batch)

    ymix = pl.BlockSpec((tm, MIXER_W), lambda i: (cur(i), 0))
    ylat = pl.BlockSpec((tm, MIXER_W), lambda i: (jnp.minimum(cur(i), n_lat_tiles - 1), 0))
    mixer_specs, mixer_args = [ylat] * 3, list(ys_lat)
    if ys_ctx is not None:
        n_ctx_tiles = nt - n_lat_tiles
        yctx = pl.BlockSpec((tm, MIXER_W), lambda i: (jnp.clip(cur(i) - n_lat_tiles, 0, n_ctx_tiles - 1), 0))
        mixer_specs, mixer_args = mixer_specs + [yctx] * 3, mixer_args + list(ys_ctx)
    rowvec = pl.BlockSpec((1, d), lambda i: (0, 0))
    return pl.pallas_call(
        functools.partial(_merge_kernel, alpha=alpha, n_lat_tiles=n_lat_tiles, n_tiles=nt),
        grid=(nt + 1,),
        in_specs=mixer_specs + [
            ymix,
            pl.BlockSpec((d, d), lambda i: (0, 0)),
            pl.BlockSpec((tm, d), lambda i: (prv(i), 0)),
            pl.BlockSpec((None, 1, d), lambda i: (mrow(i), 0, 2)),
            rowvec, rowvec,
            pl.BlockSpec((None, 1, d), lambda i: (mrow(i), 0, 3)),
            pl.BlockSpec((None, 1, d), lambda i: (mrow(i), 0, 4)),
            pl.BlockSpec((d, 2 * LOGIT_PAD), lambda i: (0, 0)),
            pl.BlockSpec((1, LOGIT_PAD), lambda i: (0, 0)),
        ],
        out_specs=[
            pl.BlockSpec((tm, d), lambda i: (prv(i), 0)),
            pl.BlockSpec((tm, d), lambda i: (prv(i), 0)),
            pl.BlockSpec((tm, LOGIT_PAD), lambda i: (prv(i), 0)),
            pl.BlockSpec((tm, LOGIT_PAD), lambda i: (prv(i), 0)),
        ],
        out_shape=[
            jax.ShapeDtypeStruct((n_rows, d), F32),
            jax.ShapeDtypeStruct((n_rows, d), F32),
            jax.ShapeDtypeStruct((n_rows, LOGIT_PAD), jnp.int32),
            jax.ShapeDtypeStruct((n_rows, LOGIT_PAD), F32),
        ],
        scratch_shapes=[pltpu.VMEM((tm, d), F32), pltpu.VMEM((tm, d), F32)],
        compiler_params=_cparams(("arbitrary",)),
        name="merge",
    )(*mixer_args, y_gm, w_out_bf, xa, mod3, ln_g, ln_b, mod3, mod3, wr_cat, rb_pad)


def _moe_kernel(be_ref, nu_ref, dest_ref, ws_ref, nxt_ref, pad_ref, h_hbm, wg_hbm, wu_hbm, wd_hbm, o_ref,
                xbuf, sem, tok_ref, stage_g, stage_u, stage_d, wsem, wg_s, wu_s, wd_s, *, layer):
    i = pl.program_id(0)
    n_used = nu_ref[0]
    ring = xbuf.shape[0]
    slot = lax.rem(i, ring)

    @pl.when(i == 0)
    def _():
        def clear(s, carry):
            tok_ref[s] = 0
            return carry

        def place(a, carry):
            tok_ref[dest_ref[a]] = lax.shift_right_logical(a, TOP_K.bit_length() - 1)
            return carry

        for e in range(N_EXPERTS):
            lax.fori_loop(pad_ref[e], pad_ref[N_EXPERTS + e], clear, 0)
        lax.fori_loop(0, dest_ref.shape[0], place, 0, unroll=8)

    def row_copy(blk, r, sl):
        tok = tok_ref[blk * MOE_BLOCK + r]
        return pltpu.make_async_copy(h_hbm.at[pl.ds(tok, 1)], xbuf.at[sl, pl.ds(r, 1)], sem.at[sl])

    def gather_start(blk, sl):
        def body(r, carry):
            row_copy(blk, r, sl).start()
            return carry

        lax.fori_loop(0, MOE_BLOCK, body, 0, unroll=8)

    def gather_wait(sl):
        pltpu.make_async_copy(h_hbm.at[pl.ds(0, MOE_BLOCK)], xbuf.at[sl], sem.at[sl]).wait()

    def weight_copies(e, ws):
        return (pltpu.make_async_copy(wg_hbm.at[layer, e], stage_g.at[ws], wsem.at[ws]),
                pltpu.make_async_copy(wu_hbm.at[layer, e], stage_u.at[ws], wsem.at[ws]),
                pltpu.make_async_copy(wd_hbm.at[layer, e], stage_d.at[ws], wsem.at[ws]))

    @pl.when(i == 0)
    def _():
        for cp in weight_copies(be_ref[0], ws_ref[0]):
            cp.start(priority=1)
        gather_start(0, 0)

        @pl.when(n_used > 1)
        def _():
            gather_start(1, 1)

    active = i < n_used
    new_expert = jnp.logical_or(i == 0, be_ref[i] != be_ref[jnp.maximum(i - 1, 0)])

    @pl.when(jnp.logical_and(active, new_expert))
    def _():
        ws = ws_ref[i]
        for cp in weight_copies(be_ref[i], ws):
            cp.wait()
        nxt = nxt_ref[i]

        @pl.when(nxt >= 0)
        def _():
            for cp in weight_copies(nxt, 1 - ws):
                cp.start(priority=1)

        for src, dst in ((stage_g, wg_s), (stage_u, wu_s), (stage_d, wd_s)):
            def cast(r0, src=src, dst=dst):
                dst[pl.ds(r0, 256), :] = src[ws, pl.ds(r0, 256), :].astype(BF16)

            _row_chunks(dst.shape[0], 256, cast)

    def ffn_block(prefetch_next):
        gather_wait(slot)
        x = xbuf[slot].astype(BF16)
        if prefetch_next:
            nxt = lax.rem(i + 2, ring)
            for r in range(MOE_BLOCK):
                row_copy(i + 2, r, nxt).start()
        g = jnp.dot(x, wg_s[...], preferred_element_type=F32)
        u = jnp.dot(x, wu_s[...], preferred_element_type=F32)
        a = (g * jax.nn.sigmoid(g) * u).astype(BF16)
        o_ref[...] = jnp.dot(a, wd_s[...], preferred_element_type=F32).astype(o_ref.dtype)

    has_next = i + 2 < n_used
    pl.when(jnp.logical_and(active, has_next))(functools.partial(ffn_block, True))
    pl.when(jnp.logical_and(active, jnp.logical_not(has_next)))(functools.partial(ffn_block, False))

    @pl.when(jnp.logical_not(active))
    def _():
        o_ref[...] = jnp.zeros_like(o_ref)


def _moe_experts(h, dest, plan, wg, wu, wd, layer):
    d = h.shape[1]
    n_slots = plan["n_slots"]
    de = wg.shape[3]
    n_blocks = n_slots // MOE_BLOCK
    hbm = pl.BlockSpec(memory_space=pl.ANY)
    grid_spec = pltpu.PrefetchScalarGridSpec(
        num_scalar_prefetch=6,
        grid=(n_blocks,),
        in_specs=[hbm, hbm, hbm, hbm],
        out_specs=pl.BlockSpec((MOE_BLOCK, d), lambda i, *_: (i, 0)),
        scratch_shapes=[
            pltpu.VMEM((3, MOE_BLOCK, d), F32),
            pltpu.SemaphoreType.DMA((3,)),
            pltpu.SMEM((n_slots,), jnp.int32),
            pltpu.VMEM((2, d, de), F32), pltpu.VMEM((2, d, de), F32), pltpu.VMEM((2, de, d), F32),
            pltpu.SemaphoreType.DMA((2,)),
            pltpu.VMEM((d, de), BF16), pltpu.VMEM((d, de), BF16), pltpu.VMEM((de, d), BF16),
        ],
    )
    return pl.pallas_call(
        functools.partial(_moe_kernel, layer=layer),
        grid_spec=grid_spec,
        out_shape=jax.ShapeDtypeStruct((n_slots, d), BF16),
        compiler_params=_cparams(("arbitrary",), MOE_VMEM_LIMIT),
        name="moe_experts",
    )(plan["block_e"], plan["n_used"], dest.reshape(-1), plan["wslot"], plan["next_e"], plan["pad_range"],
      h, wg, wu, wd)


def _assignment_ranks(flat_e):
    a = flat_e.shape[0]
    blk = 128
    nb = a // blk
    onehot = (flat_e[:, None] == jnp.arange(N_EXPERTS)[None, :]).astype(BF16).reshape(nb, blk, N_EXPERTS)
    tri = (jnp.arange(blk)[:, None] >= jnp.arange(blk)[None, :]).astype(BF16)
    intra = jnp.einsum("ij,bjk->bik", tri, onehot, preferred_element_type=F32)
    bsum = intra[:, -1, :]
    before = (jnp.arange(nb)[:, None] > jnp.arange(nb)[None, :]).astype(BF16)
    offs = jnp.dot(before, bsum.astype(BF16), preferred_element_type=F32)
    csum = intra + offs[:, None, :]
    rank = jnp.sum(csum * onehot.astype(F32), axis=-1).reshape(a) - 1.0
    counts = offs[-1] + bsum[-1]
    return rank.astype(jnp.int32), counts.astype(jnp.int32)


def _dispatch_plan(idx):
    t = idx.shape[0]
    a = t * TOP_K
    flat_e = idx.reshape(a)
    rank, counts = _assignment_ranks(flat_e)
    padded = (counts + MOE_BLOCK - 1) // MOE_BLOCK * MOE_BLOCK
    pad_end = jnp.cumsum(padded)
    pad_start = pad_end - padded
    dest = pad_start[flat_e] + rank
    n_blocks = (a + N_EXPERTS * (MOE_BLOCK - 1) + MOE_BLOCK - 1) // MOE_BLOCK
    n_slots = n_blocks * MOE_BLOCK
    blk_start = jnp.arange(n_blocks, dtype=jnp.int32) * MOE_BLOCK
    block_e = jnp.minimum(
        jnp.sum((pad_end[None, :] <= blk_start[:, None]).astype(jnp.int32), axis=1), N_EXPERTS - 1)
    n_used = (pad_end[-1] // MOE_BLOCK).astype(jnp.int32).reshape(1)
    nonempty = counts > 0
    eid = jnp.arange(N_EXPERTS, dtype=jnp.int32)
    run_of_e = jnp.cumsum(nonempty.astype(jnp.int32)) - 1
    later = jnp.logical_and(eid[None, :] > eid[:, None], nonempty[None, :])
    nxt_of_e = jnp.min(jnp.where(later, eid[None, :], N_EXPERTS), axis=1)
    nxt_of_e = jnp.where(nxt_of_e == N_EXPERTS, -1, nxt_of_e)
    plan = {
        "n_slots": n_slots, "block_e": block_e.astype(jnp.int32), "n_used": n_used,
        "wslot": (run_of_e[block_e] % 2).astype(jnp.int32), "next_e": nxt_of_e[block_e].astype(jnp.int32),
        "pad_range": jnp.concatenate([pad_start + counts, pad_end]).astype(jnp.int32),
    }
    return plan, dest.reshape(t, TOP_K)


def _final_kernel(x_ref, ya_ref, yb_ref, gate_ref, g2_ref, lg_ref, lb_ref, o_ref, *, alpha):
    f = ya_ref[...].astype(F32) * gate_ref[:, 0:1] + yb_ref[...].astype(F32) * gate_ref[:, 1:2]
    o_ref[...] = _ln(alpha * x_ref[...] + g2_ref[...] * f) * lg_ref[...] + lb_ref[...]


def _final_norm(x1, ya, yb, gate, mod3, ln_g, ln_b, n_batch, seq, alpha):
    n_rows, d = x1.shape
    tm = 512

    def mrow(i):
        return jnp.minimum((i * tm) // seq, n_batch)

    tile = pl.BlockSpec((tm, d), lambda i: (i, 0))
    rowvec = pl.BlockSpec((1, d), lambda i: (0, 0))
    return pl.pallas_call(
        functools.partial(_final_kernel, alpha=alpha),
        grid=(n_rows // tm,),
        in_specs=[tile, tile, tile, pl.BlockSpec((tm, LOGIT_PAD), lambda i: (i, 0)),
                  pl.BlockSpec((None, 1, d), lambda i: (mrow(i), 0, 5)), rowvec, rowvec],
        out_specs=tile,
        out_shape=jax.ShapeDtypeStruct((n_rows, d), F32),
        compiler_params=_cparams(("arbitrary",)),
        name="final_norm",
    )(x1, ya, yb, gate, mod3, ln_g, ln_b)


def _permute_in_cols(w):
    pool, q, k, v, hy, gm = jnp.split(w, (512, 1024, 1152, 1280, 2816), axis=-1)
    return jnp.concatenate([hy, pool, q, gm, k, v], axis=-1)


def kernel(x, c, ctx, c_ctx, w_ada, b_ada, w_in, w_out, mix_norm_g, pool_w, pool_scale, attn_sink,
           hy_conv_w, hy_conv_b, hy_w1, hy_b1, hy_f1, hy_w2, hy_b2, hy_f2, hy_w3, hy_skip,
           gm_ln_g, gm_ln_b, gm_ws, gm_bs, ln1_g, ln1_b, ln2_g, ln2_b, w_router, router_bias,
           w_gate, w_up, w_down):
    n_batch, seq, d = x.shape
    ctx_len = ctx.shape[1]
    depth = w_in.shape[0]
    grid_w = 64
    alpha = (2 * depth) ** 0.25
    n_lat = n_batch * seq
    n_ctx = n_batch * ctx_len

    xa = jnp.concatenate([x.reshape(n_lat, d), ctx.reshape(n_ctx, d)], axis=0)
    cvec = jnp.concatenate([c, c_ctx[None, :], jnp.zeros((8 - n_batch - 1, d), F32)], axis=0)
    mods = _ada_mods(cvec, w_ada, b_ada)
    cos_t, sin_t = _rope_tables(seq, grid_w)
    tabs_lat = _hy_tables(seq, *_hy_tiles(seq))
    tabs_ctx = _hy_tables(ctx_len, *_hy_tiles(ctx_len))
    wr_pad = jnp.pad(w_router, ((0, 0), (0, LOGIT_PAD - N_EXPERTS)))
    wr_hi = wr_pad.astype(BF16)
    wr_lo = (wr_pad - wr_hi.astype(F32)).astype(BF16)
    wr_cat = jnp.concatenate([wr_hi, wr_lo], axis=1)
    rb_pad = jnp.pad(router_bias.astype(F32), (0, LOGIT_PAD - N_EXPERTS)).reshape(1, LOGIT_PAD)
    row = lambda v: v.reshape(1, -1)

    for layer in range(depth):
        last = layer == depth - 1
        mod3 = mods[layer].reshape(8, 1, 6 * d)
        gains = mix_norm_g[layer].reshape(4, 1, MIXER_W)
        lp = {
            "hy_conv_w": hy_conv_w[layer], "hy_conv_b": row(hy_conv_b[layer]),
            "hy_w1p": jnp.pad(hy_w1[layer], ((0, HY_EMB_PAD - HY_EMB), (0, 0))),
            "hy_b1": row(hy_b1[layer]), "hy_f1": row(hy_f1[layer]),
            "hy_w2": hy_w2[layer], "hy_b2": row(hy_b2[layer]), "hy_f2": row(hy_f2[layer]),
            "hy_w3": hy_w3[layer], "hy_skip": hy_skip[layer],
        }
        z = _in_proj(xa, mod3, _permute_in_cols(w_in[layer]).astype(BF16), n_batch, seq)

        pool_bf = pool_w[layer].astype(BF16)
        ps = row(pool_scale[layer])
        ws_cat = jnp.transpose(gm_ws[layer], (1, 0, 2)).reshape(GMLP_CHUNK, GMLP_GROUPS * GMLP_CHUNK)
        bs_mat = jnp.repeat(gm_bs[layer].T, MIXER_W // GMLP_GROUPS, axis=1)
        n_rows = n_lat if last else n_lat + n_ctx

        y_pool = _pool_mix(z, pool_bf, ps, gains[0], n_batch, seq, 0)
        y_attn = _attn_lat(z, attn_sink[layer], cos_t, sin_t, gains[1], n_batch, seq, ctx_len)
        y_hy = _hyena_mix(z, lp, gains[2], n_batch, seq, 0, tabs_lat)
        y_gm = _gmlp_mix(z, row(gm_ln_g[layer]), row(gm_ln_b[layer]), ws_cat.astype(BF16), bs_mat,
                         gains[3], n_rows)
        ys_ctx = None
        if not last:
            ys_ctx = (_pool_mix(z, pool_bf, ps, gains[0], n_batch, ctx_len, n_lat // ctx_len),
                      _attn_ctx(z, attn_sink[layer], gains[1], n_batch, seq, ctx_len),
                      _hyena_mix(z, lp, gains[2], n_batch, ctx_len, n_lat, tabs_ctx))

        x1, h2, idx, gate = _merge((y_pool, y_attn, y_hy), ys_ctx, y_gm, w_out[layer].astype(BF16), xa, mod3,
                                   row(ln1_g[layer]), row(ln1_b[layer]), wr_cat, rb_pad,
                                   n_rows, n_batch, seq, alpha)

        plan, dest = _dispatch_plan(idx[:, :TOP_K])
        ys = _moe_experts(h2, dest, plan, w_gate, w_up, w_down, layer)
        xa = _final_norm(x1, ys[dest[:, 0]], ys[dest[:, 1]], gate, mod3, row(ln2_g[layer]), row(ln2_b[layer]),
                         n_batch, seq, alpha)
    return xa.reshape(n_batch, seq, d)
```

```python
import functools
import math

import jax
import jax.numpy as jnp
from jax import lax
from jax.experimental import pallas as pl
from jax.experimental.pallas import tpu as pltpu

F32 = jnp.float32
BF16 = jnp.bfloat16
HI = lax.Precision.HIGHEST

LN_EPS = 1e-6
MIXER_W = 512
HEAD_DIM = 64
N_HEADS = 8
N_KV_HEADS = 2
KV_GROUP = 4
KV_W = 128
WINDOW = 128
ROPE_THETA = 10000.0
POOL_WINDOWS = (2, 4, 8, 16)
POOL_GROUP = 128
POOL_HALO = 8
GMLP_CHUNK = 128
GMLP_GROUPS = 8
HY_EMB = 33
HY_EMB_PAD = 128
HY_TARGET = 1e-2
HY_FAST = 0.3
HY_SLOW = 1.5
N_EXPERTS = 32
N_EXPERT_GROUPS = 8
TOP_K = 2
D_EXPERT = 768
MOE_BLOCK = 256
LOGIT_PAD = 128

COL_HY = 0
COL_POOL = 3
COL_Q = 4
COL_GU = 5
COL_GV = 6
COL_K = 28
COL_V = 29
IN_W = 3840

VMEM_LIMIT = 56 * 1024 * 1024
MOE_VMEM_LIMIT = 59 * 1024 * 1024


def _cparams(sem, vmem_limit=VMEM_LIMIT):
    return pltpu.CompilerParams(dimension_semantics=sem, vmem_limit_bytes=vmem_limit)


def _ln(x):
    mu = jnp.mean(x, axis=-1, keepdims=True)
    xc = x - mu
    var = jnp.mean(xc * xc, axis=-1, keepdims=True)
    return xc * lax.rsqrt(var + LN_EPS)


def _rms_gain(y, g):
    return y * lax.rsqrt(jnp.mean(y * y, axis=-1, keepdims=True) + LN_EPS) * g


def _row_chunks(n_rows, chunk, body):
    n = n_rows // chunk
    if n == 1:
        body(0)
        return

    def step(i, carry):
        body(pl.multiple_of(i * chunk, chunk))
        return carry

    lax.fori_loop(0, n, step, 0)


def _ada_kernel(c_ref, w_ref, b_ref, o_ref):
    c = c_ref[...]
    s = c * jax.nn.sigmoid(c)
    o_ref[...] = jnp.dot(s, w_ref[...], precision=HI, preferred_element_type=F32) + b_ref[...]


def _ada_mods(cvec, w_ada, b_ada):
    depth, d, n = w_ada.shape
    tn = 1024
    return pl.pallas_call(
        _ada_kernel,
        grid=(depth, n // tn),
        in_specs=[
            pl.BlockSpec((8, d), lambda l, j: (0, 0)),
            pl.BlockSpec((None, d, tn), lambda l, j: (l, 0, j)),
            pl.BlockSpec((None, 1, tn), lambda l, j: (l, 0, j)),
        ],
        out_specs=pl.BlockSpec((None, 8, tn), lambda l, j: (l, 0, j)),
        out_shape=jax.ShapeDtypeStruct((depth, 8, n), F32),
        compiler_params=_cparams(("arbitrary", "arbitrary")),
        name="ada_mods",
    )(cvec, w_ada, b_ada.reshape(depth, 1, n))


def _inproj_kernel(x_ref, sh_ref, sc_ref, w_ref, o_ref, h_scr):
    @pl.when(pl.program_id(1) == 0)
    def _():
        sh = sh_ref[...]
        sc1 = 1.0 + sc_ref[...]

        def body(r0):
            x = x_ref[pl.ds(r0, 256), :]
            h_scr[pl.ds(r0, 256), :] = (_ln(x) * sc1 + sh).astype(BF16)

        _row_chunks(x_ref.shape[0], 256, body)

    o_ref[...] = jnp.dot(h_scr[...], w_ref[...], preferred_element_type=F32)


def _in_proj(xa, mod3, w_bf, n_batch, seq):
    t, d = xa.shape
    n = w_bf.shape[1]
    tn = 1920
    tm = next(m for m in (1024, 512, 256) if t % m == 0 and seq % m == 0)

    def mrow(i):
        return jnp.minimum((i * tm) // seq, n_batch)

    return pl.pallas_call(
        _inproj_kernel,
        grid=(t // tm, n // tn),
        in_specs=[
            pl.BlockSpec((tm, d), lambda i, j: (i, 0)),
            pl.BlockSpec((None, 1, d), lambda i, j: (mrow(i), 0, 0)),
            pl.BlockSpec((None, 1, d), lambda i, j: (mrow(i), 0, 1)),
            pl.BlockSpec((d, tn), lambda i, j: (0, j)),
        ],
        out_specs=pl.BlockSpec((tm, tn), lambda i, j: (i, j)),
        out_shape=jax.ShapeDtypeStruct((t, n), F32),
        scratch_shapes=[pltpu.VMEM((tm, d), BF16)],
        compiler_params=_cparams(("arbitrary", "arbitrary")),
        name="in_proj",
    )(xa, mod3, mod3, w_bf)


def _pool_kernel(z_ref, w_ref, ps_ref, g_ref, o_ref, buf):
    seq = z_ref.shape[0]
    rc = min(256, seq)
    zeros = jnp.zeros((POOL_HALO, MIXER_W), F32)
    buf[0:POOL_HALO, :] = zeros
    buf[seq + POOL_HALO:seq + 2 * POOL_HALO, :] = zeros

    def fill(r0):
        buf[pl.ds(r0 + POOL_HALO, rc), :] = z_ref[pl.ds(r0, rc), :]

    _row_chunks(seq, rc, fill)

    def body(r0):
        win = buf[pl.ds(r0, rc + 2 * POOL_HALO), :]
        t = r0 + lax.broadcasted_iota(jnp.int32, (rc, 1), 0)
        outs = []
        for gi, w in enumerate(POOL_WINDOWS):
            lanes = slice(gi * POOL_GROUP, (gi + 1) * POOL_GROUP)
            acc = None
            for j in range(-w // 2, w // 2):
                piece = win[POOL_HALO + j:POOL_HALO + j + rc, lanes]
                acc = piece if acc is None else acc + piece
            cnt = (jnp.minimum(t + w // 2, seq) - jnp.maximum(t - w // 2, 0)).astype(F32)
            dlt = acc / cnt - win[POOL_HALO:POOL_HALO + rc, lanes]
            outs.append(jnp.dot(dlt.astype(BF16), w_ref[gi], preferred_element_type=F32))
        y = jnp.concatenate(outs, axis=-1) * ps_ref[...]
        o_ref[pl.ds(r0, rc), :] = _rms_gain(y, g_ref[...]).astype(BF16)

    _row_chunks(seq, rc, body)


def _pool_mix(z, pool_w_bf, pool_scale, gain, n_seq, seq, row_blk0):
    return pl.pallas_call(
        _pool_kernel,
        grid=(n_seq,),
        in_specs=[
            pl.BlockSpec((seq, MIXER_W), lambda s: (row_blk0 + s, COL_POOL)),
            pl.BlockSpec((4, POOL_GROUP, POOL_GROUP), lambda s: (0, 0, 0)),
            pl.BlockSpec((1, MIXER_W), lambda s: (0, 0)),
            pl.BlockSpec((1, MIXER_W), lambda s: (0, 0)),
        ],
        out_specs=pl.BlockSpec((seq, MIXER_W), lambda s: (s, 0)),
        out_shape=jax.ShapeDtypeStruct((n_seq * seq, MIXER_W), BF16),
        scratch_shapes=[pltpu.VMEM((seq + 2 * POOL_HALO, MIXER_W), F32)],
        compiler_params=_cparams(("arbitrary",)),
        name="pool_mix",
    )(z, pool_w_bf, pool_scale, gain)


def _rope(x, cos, sin_signed):
    lane = lax.broadcasted_iota(jnp.int32, x.shape, 1)
    partner = jnp.where((lane & 31) < 16, pltpu.roll(x, 112, axis=1), pltpu.roll(x, 16, axis=1))
    return x * cos + partner * sin_signed


def _attend(qs, sink_col, parts):
    dn = (((1,), (1,)), ((), ()))
    scores = []
    for kk, _, mask in parts:
        s = lax.dot_general(qs, kk, dn, preferred_element_type=F32)
        if mask is not None:
            s = s + mask
        scores.append(s)
    m = sink_col
    for s in scores:
        m = jnp.maximum(m, jnp.max(s, axis=-1, keepdims=True))
    den = jnp.exp(sink_col - m)
    out = None
    for s, (_, vv, _) in zip(scores, parts):
        p = jnp.exp(s - m)
        den = den + jnp.sum(p, axis=-1, keepdims=True)
        o = jnp.dot(p.astype(BF16), vv, preferred_element_type=F32)
        out = o if out is None else out + o
    return out / den


def _heads_attend(q_all, sink_ref, tq, make_parts):
    cols = [None] * N_HEADS
    for kh in range(N_KV_HEADS):
        heads = [kh * KV_GROUP + g for g in range(KV_GROUP)]
        qs = jnp.concatenate([q_all[:, h * HEAD_DIM:(h + 1) * HEAD_DIM] for h in heads], axis=0)
        sink_col = jnp.concatenate([jnp.full((tq, 1), sink_ref[h], F32) for h in heads], axis=0)
        o = _attend(qs.astype(BF16), sink_col, make_parts(kh))
        for g, h in enumerate(heads):
            cols[h] = o[g * tq:(g + 1) * tq, :]
    return jnp.concatenate(cols, axis=-1)


def _attn_lat_kernel(sink_ref, q_ref, k_ref, v_ref, kc_ref, vc_ref, cos_ref, sin_ref, g_ref, o_ref):
    tq = q_ref.shape[0]
    seq = k_ref.shape[0]
    nband = tq + 2 * WINDOW
    i = pl.program_id(1)
    q0 = pl.multiple_of(i * tq, tq)
    k0 = pl.multiple_of(jnp.clip(q0 - WINDOW, 0, seq - nband), WINDOW)

    cq = cos_ref[pl.ds(q0, tq), :]
    sq = sin_ref[pl.ds(q0, tq), :]
    q = q_ref[...]
    q_all = jnp.concatenate(
        [_rope(q[:, c * 128:(c + 1) * 128], cq, sq) for c in range(MIXER_W // 128)], axis=-1)
    q_all = q_all * (HEAD_DIM ** -0.5)

    kb = _rope(k_ref[pl.ds(k0, nband), :], cos_ref[pl.ds(k0, nband), :],
               sin_ref[pl.ds(k0, nband), :]).astype(BF16)
    vb = v_ref[pl.ds(k0, nband), :].astype(BF16)
    kc = kc_ref[...].astype(BF16)
    vc = vc_ref[...].astype(BF16)

    qpos = q0 + lax.broadcasted_iota(jnp.int32, (tq, nband), 0)
    kpos = k0 + lax.broadcasted_iota(jnp.int32, (tq, nband), 1)
    band = jnp.where(jnp.abs(qpos - kpos) <= WINDOW, 0.0, -1e30)
    mask = jnp.concatenate([band] * KV_GROUP, axis=0)

    def make_parts(kh):
        hs = slice(kh * HEAD_DIM, (kh + 1) * HEAD_DIM)
        return [(kb[:, hs], vb[:, hs], mask), (kc[:, hs], vc[:, hs], None)]

    y = _heads_attend(q_all, sink_ref, tq, make_parts)
    o_ref[...] = _rms_gain(y, g_ref[...]).astype(BF16)


def _attn_ctx_kernel(sink_ref, q_ref, kc_ref, vc_ref, g_ref, o_ref):
    tq = q_ref.shape[0]
    q_all = q_ref[...] * (HEAD_DIM ** -0.5)
    kc = kc_ref[...].astype(BF16)
    vc = vc_ref[...].astype(BF16)

    def make_parts(kh):
        hs = slice(kh * HEAD_DIM, (kh + 1) * HEAD_DIM)
        return [(kc[:, hs], vc[:, hs], None)]

    y = _heads_attend(q_all, sink_ref, tq, make_parts)
    o_ref[...] = _rms_gain(y, g_ref[...]).astype(BF16)


def _attn_lat(z, sink, cos_t, sin_t, gain, n_batch, seq, ctx_len):
    tq = 256
    nq = seq // tq
    ctx_blk0 = (n_batch * seq) // ctx_len
    smem = pl.BlockSpec(memory_space=pltpu.SMEM)
    return pl.pallas_call(
        _attn_lat_kernel,
        grid=(n_batch, nq),
        in_specs=[
            smem,
            pl.BlockSpec((tq, MIXER_W), lambda b, i: (b * nq + i, COL_Q)),
            pl.BlockSpec((seq, KV_W), lambda b, i: (b, COL_K)),
            pl.BlockSpec((seq, KV_W), lambda b, i: (b, COL_V)),
            pl.BlockSpec((ctx_len, KV_W), lambda b, i: (ctx_blk0 + b, COL_K)),
            pl.BlockSpec((ctx_len, KV_W), lambda b, i: (ctx_blk0 + b, COL_V)),
            pl.BlockSpec((seq, 128), lambda b, i: (0, 0)),
            pl.BlockSpec((seq, 128), lambda b, i: (0, 0)),
            pl.BlockSpec((1, MIXER_W), lambda b, i: (0, 0)),
        ],
        out_specs=pl.BlockSpec((tq, MIXER_W), lambda b, i: (b * nq + i, 0)),
        out_shape=jax.ShapeDtypeStruct((n_batch * seq, MIXER_W), BF16),
        compiler_params=_cparams(("arbitrary", "arbitrary")),
        name="attn_lat",
    )(sink, z, z, z, z, z, cos_t, sin_t, gain)


def _attn_ctx(z, sink, gain, n_batch, seq, ctx_len):
    ctx_blk0 = (n_batch * seq) // ctx_len
    smem = pl.BlockSpec(memory_space=pltpu.SMEM)
    return pl.pallas_call(
        _attn_ctx_kernel,
        grid=(n_batch,),
        in_specs=[
            smem,
            pl.BlockSpec((ctx_len, MIXER_W), lambda b: (ctx_blk0 + b, COL_Q)),
            pl.BlockSpec((ctx_len, KV_W), lambda b: (ctx_blk0 + b, COL_K)),
            pl.BlockSpec((ctx_len, KV_W), lambda b: (ctx_blk0 + b, COL_V)),
            pl.BlockSpec((1, MIXER_W), lambda b: (0, 0)),
        ],
        out_specs=pl.BlockSpec((ctx_len, MIXER_W), lambda b: (b, 0)),
        out_shape=jax.ShapeDtypeStruct((n_batch * ctx_len, MIXER_W), BF16),
        compiler_params=_cparams(("arbitrary",)),
        name="attn_ctx",
    )(sink, z, z, z, gain)


def _rope_tables(seq, grid_w):
    nf = HEAD_DIM // 4
    inv = ROPE_THETA ** (-jnp.arange(nf, dtype=F32) / nf)
    t = jnp.arange(seq)
    row = (t // grid_w).astype(F32)[:, None] * inv[None, :]
    col = (t % grid_w).astype(F32)[:, None] * inv[None, :]
    cos_h = jnp.concatenate([jnp.cos(row), jnp.cos(row), jnp.cos(col), jnp.cos(col)], axis=-1)
    sin_h = jnp.concatenate([-jnp.sin(row), jnp.sin(row), -jnp.sin(col), jnp.sin(col)], axis=-1)
    return jnp.tile(cos_h, (1, 2)), jnp.tile(sin_h, (1, 2))


def _hy_conv_kernel(*refs):
    z_refs, zp_refs, zn_refs = refs[0:3], refs[3:6], refs[6:9]
    w_ref, b_ref, g_ref, u_ref = refs[9:]
    i = pl.program_id(1)
    th = z_refs[0].shape[0] // 2
    has_prev = (i > 0).astype(F32)
    has_next = (i < pl.num_programs(1) - 1).astype(F32)
    row = lax.broadcasted_iota(jnp.int32, (th, 128), 0)
    for part in range(3):
        ze = z_refs[part][pl.ds(0, th, stride=2), :]
        zo = z_refs[part][pl.ds(1, th, stride=2), :]
        prev_row = zp_refs[part][POOL_HALO - 1:POOL_HALO, :] * has_prev
        next_row = zn_refs[part][0:1, :] * has_next
        zo_m = jnp.where(row == 0, prev_row, pltpu.roll(zo, 1, axis=0))
        ze_p = jnp.where(row == th - 1, next_row, pltpu.roll(ze, th - 1, axis=0))
        w0, w1, w2, b = w_ref[part, 0:1, :], w_ref[part, 1:2, :], w_ref[part, 2:3, :], b_ref[part]
        ye = b + zo_m * w0 + ze * w1 + zo * w2
        yo = b + ze * w0 + zo * w1 + ze_p * w2
        g_ref[part, 0] = ye
        g_ref[part, 1] = yo
        if part == 2:
            u_ref[0] = ye.astype(BF16)
            u_ref[1] = yo.astype(BF16)


def _hy_conv(z, conv_w, conv_b, n_seq, seq, row0):
    tl = min(2048, seq)
    nt = seq // tl
    blk0 = row0 // tl
    hb0 = row0 // POOL_HALO
    hpt = tl // POOL_HALO
    last_halo = (row0 + n_seq * seq) // POOL_HALO - 1
    cpp = MIXER_W // 128
    ncol = n_seq * MIXER_W

    def main_spec(part):
        return pl.BlockSpec((tl, 128), lambda s, i, c: (blk0 + s * nt + i, part * cpp + c))

    def prev_spec(part):
        return pl.BlockSpec((POOL_HALO, 128),
                            lambda s, i, c: (jnp.maximum(hb0 + (s * nt + i) * hpt - 1, 0), part * cpp + c))

    def next_spec(part):
        return pl.BlockSpec((POOL_HALO, 128),
                            lambda s, i, c: (jnp.minimum(hb0 + (s * nt + i + 1) * hpt, last_halo), part * cpp + c))

    w3 = conv_w.reshape(3, 3, MIXER_W).transpose(1, 0, 2)
    b3 = conv_b.reshape(3, 1, MIXER_W)
    return pl.pallas_call(
        _hy_conv_kernel,
        grid=(n_seq, nt, cpp),
        in_specs=[main_spec(p) for p in range(3)] + [prev_spec(p) for p in range(3)]
        + [next_spec(p) for p in range(3)]
        + [pl.BlockSpec((3, 3, 128), lambda s, i, c: (0, 0, c)),
           pl.BlockSpec((3, 1, 128), lambda s, i, c: (0, 0, c))],
        out_specs=[
            pl.BlockSpec((3, 2, tl // 2, 128), lambda s, i, c: (0, 0, i, s * cpp + c)),
            pl.BlockSpec((2, tl // 2, 128), lambda s, i, c: (0, i, s * cpp + c)),
        ],
        out_shape=[jax.ShapeDtypeStruct((3, 2, seq // 2, ncol), F32),
                   jax.ShapeDtypeStruct((2, seq // 2, ncol), BF16)],
        compiler_params=_cparams(("arbitrary", "arbitrary", "arbitrary")),
        name="hy_conv",
    )(*([z] * 9), w3, b3)


def _hy_filter_kernel(feat_ref, dec_ref, w1_ref, b1_ref, f1_ref, w2_ref, b2_ref, f2_ref, w3_ref, o_ref):
    tl = feat_ref.shape[0]
    h = jnp.dot(feat_ref[...], w1_ref[...], precision=HI, preferred_element_type=F32) + b1_ref[...]
    h = jnp.sin(f1_ref[...] * h)
    h = jnp.dot(h, w2_ref[...], precision=HI, preferred_element_type=F32) + b2_ref[...]
    h = jnp.sin(f2_ref[...] * h)
    filt = jnp.dot(h, w3_ref[...], precision=HI, preferred_element_type=F32)
    dec = dec_ref[...]
    t = 2 * (pl.program_id(1) * tl + lax.broadcasted_iota(jnp.int32, (tl, 1), 0)) + pl.program_id(0)
    not_first = (t > 0).astype(F32)
    for o in range(2):
        hf = filt[:, (2 * o) * MIXER_W:(2 * o + 1) * MIXER_W] * dec
        hb = filt[:, (2 * o + 1) * MIXER_W:(2 * o + 2) * MIXER_W] * dec * not_first
        o_ref[:, o * MIXER_W:(o + 1) * MIXER_W] = (hf + hb).astype(BF16)
        o_ref[:, (2 + o) * MIXER_W:(3 + o) * MIXER_W] = (hb - hf).astype(BF16)


def _hy_filters(feat, decay, w1p, b1, f1, w2, b2, f2, w3):
    half = feat.shape[1]
    tl = min(512, half)
    hid = w2.shape[0]
    full = lambda shape: pl.BlockSpec(shape, lambda p, i: tuple(0 for _ in shape))
    return pl.pallas_call(
        _hy_filter_kernel,
        grid=(2, half // tl),
        in_specs=[
            pl.BlockSpec((None, tl, HY_EMB_PAD), lambda p, i: (p, i, 0)),
            pl.BlockSpec((None, tl, MIXER_W), lambda p, i: (p, i, 0)),
            full((HY_EMB_PAD, hid)), full((1, hid)), full((1, hid)),
            full((hid, hid)), full((1, hid)), full((1, hid)),
            full((hid, 4 * MIXER_W)),
        ],
        out_specs=pl.BlockSpec((None, tl, 4 * MIXER_W), lambda p, i: (p, i, 0)),
        out_shape=jax.ShapeDtypeStruct((2, half, 4 * MIXER_W), BF16),
        compiler_params=_cparams(("arbitrary", "arbitrary")),
        name="hy_filters",
    )(feat, decay, w1p, b1, f1, w2, b2, f2, w3)


def _hy_kspec_kernel(te_ref, to_ref, ab_ref, o_ref, *, inv_len):
    hb = te_ref.shape[0] // 2
    half = ab_ref.shape[2] // 2
    ec = jnp.dot(te_ref[0:hb, :], ab_ref[0, :, 0:half], preferred_element_type=F32)
    oc = jnp.dot(to_ref[0:hb, :], ab_ref[1, :, 0:half], preferred_element_type=F32)
    es = jnp.dot(te_ref[hb:, :], ab_ref[0, :, half:], preferred_element_type=F32)
    os_ = jnp.dot(to_ref[hb:, :], ab_ref[1, :, half:], preferred_element_type=F32)
    o_ref[0, 0:hb, :] = (ec + oc) * inv_len
    o_ref[0, hb:, :] = (es + os_) * inv_len
    o_ref[1, 0:hb, :] = (ec - oc) * inv_len
    o_ref[1, hb:, :] = (os_ - es) * inv_len


def _hy_kspec(fwd, ab, tm):
    _, seq, half = fwd.shape
    return pl.pallas_call(
        functools.partial(_hy_kspec_kernel, inv_len=1.0 / seq),
        grid=(seq // tm,),
        in_specs=[
            pl.BlockSpec((None, tm, half), lambda i: (0, i, 0)),
            pl.BlockSpec((None, tm, half), lambda i: (1, i, 0)),
            pl.BlockSpec((2, half, 4 * MIXER_W), lambda i: (0, 0, 0)),
        ],
        out_specs=pl.BlockSpec((2, tm, 2 * MIXER_W), lambda i: (0, i, 0)),
        out_shape=jax.ShapeDtypeStruct((2, seq, 2 * MIXER_W), F32),
        compiler_params=_cparams(("arbitrary",)),
        name="hy_kspec",
    )(fwd, fwd, ab)


def _hy_fwd_kernel(te_ref, to_ref, u_ref, k_ref, o_ref):
    hb = te_ref.shape[0] // 2
    reps = u_ref.shape[2] // MIXER_W
    ep = jnp.dot(te_ref[...], u_ref[0], preferred_element_type=F32)
    op = jnp.dot(to_ref[...], u_ref[1], preferred_element_type=F32)
    ec, es, oc, os_ = ep[0:hb, :], ep[hb:, :], op[0:hb, :], op[hb:, :]
    tile = lambda v: jnp.concatenate([v] * reps, axis=-1)
    p, q, pm, qm = ec + oc, es + os_, ec - oc, os_ - es
    kr, ki = tile(k_ref[0, 0:hb, :]), tile(k_ref[0, hb:, :])
    krm, kim = tile(k_ref[1, 0:hb, :]), tile(k_ref[1, hb:, :])
    yr, yn = p * kr + q * ki, q * kr - p * ki
    yrm, ynm = pm * krm + qm * kim, qm * krm - pm * kim
    o_ref[0, 0:hb, :] = (yr + yrm).astype(BF16)
    o_ref[0, hb:, :] = (yn - ynm).astype(BF16)
    o_ref[1, 0:hb, :] = (yr - yrm).astype(BF16)
    o_ref[1, hb:, :] = (yn + ynm).astype(BF16)


def _hy_fwd(fwd, u, kspec, order, tm, tn):
    _, seq, half = fwd.shape
    ncol = u.shape[2]
    return pl.pallas_call(
        _hy_fwd_kernel,
        grid=(ncol // tn, seq // tm),
        in_specs=[
            pl.BlockSpec((None, tm, half), lambda j, i: (0, i, 0)),
            pl.BlockSpec((None, tm, half), lambda j, i: (1, i, 0)),
            pl.BlockSpec((2, half, tn), lambda j, i: (0, 0, j)),
            pl.BlockSpec((2, tm, MIXER_W), lambda j, i: (0, i, order)),
        ],
        out_specs=pl.BlockSpec((2, tm, tn), lambda j, i: (0, i, j)),
        out_shape=jax.ShapeDtypeStruct((2, seq, ncol), BF16),
        compiler_params=_cparams(("arbitrary", "arbitrary")),
        name="hy_fwd",
    )(fwd, fwd, u, kspec)


def _hy_inv_kernel(ie_ref, io_ref, y_ref, u_ref, gate_ref, skip_ref, *rest, final):
    reps = u_ref.shape[2] // MIXER_W
    skip = jnp.concatenate([skip_ref[...]] * reps, axis=-1)
    for par, inv_ref in enumerate((ie_ref, io_ref)):
        conv = jnp.dot(inv_ref[...], y_ref[par], preferred_element_type=F32)
        out = gate_ref[par] * (conv + u_ref[par].astype(F32) * skip)
        if final:
            g_ref, o_ref = rest
            for r in range(reps):
                o_ref[r, par] = _rms_gain(out[:, r * MIXER_W:(r + 1) * MIXER_W], g_ref[...]).astype(BF16)
        else:
            (o_ref,) = rest
            o_ref[par] = out.astype(BF16)


def _hy_inv(inv, yspec, u, conv, part, skip, gain, final, tm, tn):
    _, half, seq = inv.shape
    ncol = u.shape[2]
    reps = tn // MIXER_W
    in_specs = [
        pl.BlockSpec((None, tm, seq), lambda j, i: (0, i, 0)),
        pl.BlockSpec((None, tm, seq), lambda j, i: (1, i, 0)),
        pl.BlockSpec((2, seq, tn), lambda j, i: (0, 0, j)),
        pl.BlockSpec((2, tm, tn), lambda j, i: (0, i, j)),
        pl.BlockSpec((None, 2, tm, tn), lambda j, i: (part, 0, i, j)),
        pl.BlockSpec((1, MIXER_W), lambda j, i: (0, 0)),
    ]
    args = [inv, inv, yspec, u, conv, skip]
    if final:
        in_specs.append(pl.BlockSpec((1, MIXER_W), lambda j, i: (0, 0)))
        args.append(gain)
        out_spec = pl.BlockSpec((reps, 2, tm, MIXER_W), lambda j, i: (j, 0, i, 0))
        out_shape = jax.ShapeDtypeStruct((ncol // MIXER_W, 2, half, MIXER_W), BF16)
    else:
        out_spec = pl.BlockSpec((2, tm, tn), lambda j, i: (0, i, j))
        out_shape = jax.ShapeDtypeStruct((2, half, ncol), BF16)
    return pl.pallas_call(
        functools.partial(_hy_inv_kernel, final=final),
        grid=(ncol // tn, half // tm),
        in_specs=in_specs,
        out_specs=out_spec,
        out_shape=out_shape,
        compiler_params=_cparams(("arbitrary", "arbitrary")),
        name="hy_inv",
    )(*args)


def _hy_table_kernel(hc_ref, hs_ref, lc_ref, ls_ref, o_ref, *, interleave):
    n_hi, split = hc_ref.shape[0], lc_ref.shape[0]
    lc, ls = lc_ref[...], ls_ref[...]
    for g in range(n_hi):
        hc, hs = hc_ref[g:g + 1, :], hs_ref[g:g + 1, :]
        cos_t = (hc * lc - hs * ls).astype(BF16)
        sin_t = (hs * lc + hc * ls).astype(BF16)
        rows = slice(g * split, (g + 1) * split)
        if interleave == 0:
            o_ref[rows, :] = cos_t
            o_ref[n_hi * split + g * split:n_hi * split + (g + 1) * split, :] = sin_t
        else:
            w = interleave
            for c in range(cos_t.shape[1] // w):
                o_ref[rows, 2 * c * w:(2 * c + 1) * w] = cos_t[:, c * w:(c + 1) * w]
                o_ref[rows, (2 * c + 1) * w:(2 * c + 2) * w] = sin_t[:, c * w:(c + 1) * w]


def _hy_tables(seq, tm, tm_inv):
    half = seq // 2
    hb = tm // 2
    split = 32
    unit = 2.0 * math.pi / (4 * seq)
    ar = lambda n: jnp.arange(n, dtype=jnp.int32)

    def cos_sin(a, b):
        ang = ((a[..., :, None] * b[..., None, :]) % (4 * seq)).astype(F32) * unit
        return jnp.cos(ang), jnp.sin(ang)

    par = ar(2)[:, None]
    n_all = 2 * ar(half)[None, :] + par
    k_odd = 2 * ar(half) + 1
    f_hc, f_hs = cos_sin(jnp.broadcast_to(2 * split * ar(half // split), (2, half // split)), n_all)
    f_lc, f_ls = cos_sin(jnp.broadcast_to(2 * ar(split) + 1, (2, split)), n_all)
    i_hc, i_hs = cos_sin(2 * split * ar(half // split), k_odd)
    i_lc, i_ls = cos_sin(2 * ar(split)[None, :] + par, jnp.broadcast_to(k_odd, (2, half)))

    n_hi_f = hb // split
    fwd = pl.pallas_call(
        functools.partial(_hy_table_kernel, interleave=0),
        grid=(2, half // hb),
        in_specs=[
            pl.BlockSpec((None, n_hi_f, half), lambda p, i: (p, i, 0)),
            pl.BlockSpec((None, n_hi_f, half), lambda p, i: (p, i, 0)),
            pl.BlockSpec((None, split, half), lambda p, i: (p, 0, 0)),
            pl.BlockSpec((None, split, half), lambda p, i: (p, 0, 0)),
        ],
        out_specs=pl.BlockSpec((None, tm, half), lambda p, i: (p, i, 0)),
        out_shape=jax.ShapeDtypeStruct((2, seq, half), BF16),
        compiler_params=_cparams(("arbitrary", "arbitrary")),
        name="hy_table_fwd",
    )(f_hc, f_hs, f_lc, f_ls)
    n_hi_i = tm_inv // split
    inv = pl.pallas_call(
        functools.partial(_hy_table_kernel, interleave=hb),
        grid=(2, half // tm_inv),
        in_specs=[
            pl.BlockSpec((n_hi_i, half), lambda p, i: (i, 0)),
            pl.BlockSpec((n_hi_i, half), lambda p, i: (i, 0)),
            pl.BlockSpec((None, split, half), lambda p, i: (p, 0, 0)),
            pl.BlockSpec((None, split, half), lambda p, i: (p, 0, 0)),
        ],
        out_specs=pl.BlockSpec((None, tm_inv, seq), lambda p, i: (p, i, 0)),
        out_shape=jax.ShapeDtypeStruct((2, half, seq), BF16),
        compiler_params=_cparams(("arbitrary", "arbitrary")),
        name="hy_table_inv",
    )(i_hc, i_hs, i_lc, i_ls)
    return fwd, inv


def _hy_features(seq):
    t = jnp.linspace(0.0, 1.0, seq, dtype=F32)[:, None]
    bands = (HY_EMB - 1) // 2
    freqs = jnp.linspace(1e-4, bands - 1, bands, dtype=F32)[None, :]
    ang = (2.0 * math.pi / seq) * jnp.arange(seq, dtype=F32)[:, None] * freqs
    feat = jnp.concatenate([t, jnp.cos(ang), -jnp.sin(ang)], -1)
    feat = jnp.pad(feat, ((0, 0), (0, HY_EMB_PAD - HY_EMB)))
    deltas = jnp.abs(jnp.linspace(math.log(HY_TARGET) / HY_SLOW, math.log(HY_TARGET) / HY_FAST,
                                  MIXER_W, dtype=F32))
    decay = jnp.exp(-t * deltas[None, :])
    split = lambda a: jnp.stack([a[0::2], a[1::2]], axis=0)
    return split(feat), split(decay)


def _hy_tiles(seq):
    return min(512, seq), min(256, seq // 2)


def _hyena_mix(z, lp, gain, n_seq, seq, row0, tables):
    tm_f, tm_i = _hy_tiles(seq)
    ncol = n_seq * MIXER_W
    tn = min(1024, ncol)
    fwd, inv = tables
    feat, decay = _hy_features(seq)
    ab = _hy_filters(feat, decay, lp["hy_w1p"], lp["hy_b1"], lp["hy_f1"], lp["hy_w2"], lp["hy_b2"],
                     lp["hy_f2"], lp["hy_w3"])
    kspec = _hy_kspec(fwd, ab, tm_f)
    conv, u = _hy_conv(z, lp["hy_conv_w"], lp["hy_conv_b"], n_seq, seq, row0)
    y0 = _hy_fwd(fwd, u, kspec, 0, tm_f, tn)
    u1 = _hy_inv(inv, y0, u, conv, 0, lp["hy_skip"][0:1], None, False, tm_i, tn)
    y1 = _hy_fwd(fwd, u1, kspec, 1, tm_f, tn)
    out = _hy_inv(inv, y1, u1, conv, 1, lp["hy_skip"][1:2], gain, True, tm_i, tn)
    return jnp.transpose(out, (0, 2, 1, 3)).reshape(n_seq * seq, MIXER_W)


def _gelu(x):
    c = math.sqrt(2.0 / math.pi)
    return 0.5 * x * (1.0 + jnp.tanh(c * (x + 0.044715 * (x * x * x))))


def _gmlp_kernel(u_ref, v_ref, lg_ref, lb_ref, ws_ref, bs_ref, g_ref, o_ref):
    tl = u_ref.shape[0]
    gw = MIXER_W // GMLP_GROUPS
    lane_grp = lax.broadcasted_iota(jnp.int32, (GMLP_CHUNK, MIXER_W), 1) // gw
    for c in range(tl // GMLP_CHUNK):
        rows = slice(c * GMLP_CHUNK, (c + 1) * GMLP_CHUNK)
        v = _ln(_gelu(v_ref[rows, :])) * lg_ref[...] + lb_ref[...]
        vb = v.astype(BF16)
        stacked = jnp.concatenate(
            [jnp.where(lane_grp == g, vb, jnp.zeros_like(vb)) for g in range(GMLP_GROUPS)], axis=0)
        s = jnp.dot(ws_ref[...], stacked, preferred_element_type=F32) + bs_ref[...]
        y = _gelu(u_ref[rows, :]) * s
        o_ref[rows, :] = _rms_gain(y, g_ref[...]).astype(BF16)


def _gmlp_mix(z, ln_g, ln_b, ws_cat, bs_mat, gain, n_rows):
    tl = 512
    full = lambda shape: pl.BlockSpec(shape, lambda i: tuple(0 for _ in shape))
    return pl.pallas_call(
        _gmlp_kernel,
        grid=(n_rows // tl,),
        in_specs=[
            pl.BlockSpec((tl, MIXER_W), lambda i: (i, COL_GU)),
            pl.BlockSpec((tl, MIXER_W), lambda i: (i, COL_GV)),
            full((1, MIXER_W)), full((1, MIXER_W)),
            full((GMLP_CHUNK, GMLP_GROUPS * GMLP_CHUNK)),
            full((GMLP_CHUNK, MIXER_W)),
            full((1, MIXER_W)),
        ],
        out_specs=pl.BlockSpec((tl, MIXER_W), lambda i: (i, 0)),
        out_shape=jax.ShapeDtypeStruct((n_rows, MIXER_W), BF16),
        compiler_params=_cparams(("arbitrary",)),
        name="gmlp_mix",
    )(z, z, ln_g, ln_b, ws_cat, bs_mat, gain)


def _merge_kernel(*refs, alpha, n_lat_tiles, n_tiles):
    has_ctx = n_tiles > n_lat_tiles
    lat_refs, refs = refs[:3], refs[3:]
    if has_ctx:
        ctx_refs, refs = refs[:3], refs[3:]
    (yg_ref, w_ref, x_ref, g1_ref, lg_ref, lb_ref, sh_ref, sc_ref, wrc_ref, rb_ref,
     xo_ref, h_ref, idx_ref, gate_ref, y_a, y_b) = refs
    s = pl.program_id(0)

    @pl.when(s == 0)
    def _():
        y_b[...] = jnp.zeros_like(y_b)

    def step(y_new, y_old):
        parts = [r[...] for r in lat_refs]
        if has_ctx:
            is_ctx = jnp.minimum(s, n_tiles - 1) >= n_lat_tiles
            parts = [jnp.where(is_ctx, c[...], p) for c, p in zip(ctx_refs, parts)]
        ymix = jnp.concatenate(parts + [yg_ref[...]], axis=-1)
        y_new[...] = jnp.dot(ymix, w_ref[...], preferred_element_type=F32)

        x1 = _ln(alpha * x_ref[...] + g1_ref[...] * y_old[...]) * lg_ref[...] + lb_ref[...]
        xo_ref[...] = x1
        h = _ln(x1) * (1.0 + sc_ref[...]) + sh_ref[...]
        h_ref[...] = h
        h_hi = h.astype(BF16)
        h_lo = (h - h_hi.astype(F32)).astype(BF16)
        both = jnp.dot(h_hi, wrc_ref[...], preferred_element_type=F32)
        logits = both[:, :LOGIT_PAD] + (jnp.dot(h_lo, wrc_ref[:, :LOGIT_PAD], preferred_element_type=F32)
                                        + both[:, LOGIT_PAD:])
        idx, gate = _route_rows(logits, rb_ref[...])
        idx_ref[...] = idx
        gate_ref[...] = gate

    even = lax.rem(s, 2) == 0
    pl.when(even)(functools.partial(step, y_a, y_b))
    pl.when(jnp.logical_not(even))(functools.partial(step, y_b, y_a))


def _route_rows(logits, bias):
    neg = -1e30
    lane = lax.broadcasted_iota(jnp.int32, logits.shape, 1)
    lane_f = lane.astype(F32)
    valid = lane < N_EXPERTS
    per = N_EXPERTS // N_EXPERT_GROUPS
    s = jax.nn.sigmoid(logits)
    sel = jnp.where(valid, s + bias, neg)
    sh = [sel] + [pltpu.roll(sel, LOGIT_PAD - j, axis=1) for j in range(1, per)]
    pair = None
    for a in range(per):
        for b in range(a + 1, per):
            t = sh[a] + sh[b]
            pair = t if pair is None else jnp.maximum(pair, t)
    grp = jnp.where(valid & ((lane & (per - 1)) == 0), pair, neg)

    def first_max(v):
        m = jnp.max(v, axis=-1, keepdims=True)
        return jnp.min(jnp.where(v == m, lane_f, float(LOGIT_PAD)), axis=-1, keepdims=True)

    best = first_max(grp).astype(jnp.int32)
    shift = per.bit_length() - 1
    cand = jnp.where(valid & ((lane >> shift) == (best >> shift)), sel, neg)
    i1 = first_max(cand)
    i2 = first_max(jnp.where(lane_f == i1, neg, cand))
    w1 = jnp.sum(jnp.where(lane_f == i1, s, 0.0), axis=-1, keepdims=True)
    w2 = jnp.sum(jnp.where(lane_f == i2, s, 0.0), axis=-1, keepdims=True)
    tot = w1 + w2
    idx = jnp.where(lane == 0, i1, jnp.where(lane == 1, i2, 0.0)).astype(jnp.int32)
    gate = jnp.where(lane == 0, w1 / tot, jnp.where(lane == 1, w2 / tot, 0.0))
    return idx, gate


def _merge(ys_lat, ys_ctx, y_gm, w_out_bf, xa, mod3, ln_g, ln_b, wr_cat, rb_pad, n_rows, n_batch, seq, alpha):
    d = xa.shape[1]
    tm = 512
    nt = n_rows // tm
    n_lat_tiles = ys_lat[0].shape[0] // tm

    def cur(i):
        return jnp.minimum(i, nt - 1)

    def prv(i):
        return jnp.maximum(i - 1, 0)

    def mrow(i):
        return jnp.minimum((prv(i) * tm) // seq, n_batch)

    ymix = pl.BlockSpec((tm, MIXER_W), lambda i: (cur(i), 0))
    ylat = pl.BlockSpec((tm, MIXER_W), lambda i: (jnp.minimum(cur(i), n_lat_tiles - 1), 0))
    mixer_specs, mixer_args = [ylat] * 3, list(ys_lat)
    if ys_ctx is not None:
        n_ctx_tiles = nt - n_lat_tiles
        yctx = pl.BlockSpec((tm, MIXER_W), lambda i: (jnp.clip(cur(i) - n_lat_tiles, 0, n_ctx_tiles - 1), 0))
        mixer_specs, mixer_args = mixer_specs + [yctx] * 3, mixer_args + list(ys_ctx)
    rowvec = pl.BlockSpec((1, d), lambda i: (0, 0))
    return pl.pallas_call(
        functools.partial(_merge_kernel, alpha=alpha, n_lat_tiles=n_lat_tiles, n_tiles=nt),
        grid=(nt + 1,),
        in_specs=mixer_specs + [
            ymix,
            pl.BlockSpec((d, d), lambda i: (0, 0)),
            pl.BlockSpec((tm, d), lambda i: (prv(i), 0)),
            pl.BlockSpec((None, 1, d), lambda i: (mrow(i), 0, 2)),
            rowvec, rowvec,
            pl.BlockSpec((None, 1, d), lambda i: (mrow(i), 0, 3)),
            pl.BlockSpec((None, 1, d), lambda i: (mrow(i), 0, 4)),
            pl.BlockSpec((d, 2 * LOGIT_PAD), lambda i: (0, 0)),
            pl.BlockSpec((1, LOGIT_PAD), lambda i: (0, 0)),
        ],
        out_specs=[
            pl.BlockSpec((tm, d), lambda i: (prv(i), 0)),
            pl.BlockSpec((tm, d), lambda i: (prv(i), 0)),
            pl.BlockSpec((tm, LOGIT_PAD), lambda i: (prv(i), 0)),
            pl.BlockSpec((tm, LOGIT_PAD), lambda i: (prv(i), 0)),
        ],
        out_shape=[
            jax.ShapeDtypeStruct((n_rows, d), F32),
            jax.ShapeDtypeStruct((n_rows, d), F32),
            jax.ShapeDtypeStruct((n_rows, LOGIT_PAD), jnp.int32),
            jax.ShapeDtypeStruct((n_rows, LOGIT_PAD), F32),
        ],
        scratch_shapes=[pltpu.VMEM((tm, d), F32), pltpu.VMEM((tm, d), F32)],
        compiler_params=_cparams(("arbitrary",)),
        name="merge",
    )(*mixer_args, y_gm, w_out_bf, xa, mod3, ln_g, ln_b, mod3, mod3, wr_cat, rb_pad)


def _moe_kernel(be_ref, nu_ref, dest_ref, ws_ref, nxt_ref, pad_ref, h_hbm, wg_hbm, wu_hbm, wd_hbm, o_ref,
                xbuf, sem, tok_ref, stage_g, stage_u, stage_d, wsem, wg_s, wu_s, wd_s, *, layer):
    i = pl.program_id(0)
    n_used = nu_ref[0]
    ring = xbuf.shape[0]
    slot = lax.rem(i, ring)

    @pl.when(i == 0)
    def _():
        def clear(s, carry):
            tok_ref[s] = 0
            return carry

        def place(a, carry):
            tok_ref[dest_ref[a]] = lax.shift_right_logical(a, TOP_K.bit_length() - 1)
            return carry

        for e in range(N_EXPERTS):
            lax.fori_loop(pad_ref[e], pad_ref[N_EXPERTS + e], clear, 0)
        lax.fori_loop(0, dest_ref.shape[0], place, 0, unroll=8)

    def row_copy(blk, r, sl):
        tok = tok_ref[blk * MOE_BLOCK + r]
        return pltpu.make_async_copy(h_hbm.at[pl.ds(tok, 1)], xbuf.at[sl, pl.ds(r, 1)], sem.at[sl])

    def gather_start(blk, sl):
        def body(r, carry):
            row_copy(blk, r, sl).start()
            return carry

        lax.fori_loop(0, MOE_BLOCK, body, 0, unroll=8)

    def gather_wait(sl):
        pltpu.make_async_copy(h_hbm.at[pl.ds(0, MOE_BLOCK)], xbuf.at[sl], sem.at[sl]).wait()

    def weight_copies(e, ws):
        return (pltpu.make_async_copy(wg_hbm.at[layer, e], stage_g.at[ws], wsem.at[ws]),
                pltpu.make_async_copy(wu_hbm.at[layer, e], stage_u.at[ws], wsem.at[ws]),
                pltpu.make_async_copy(wd_hbm.at[layer, e], stage_d.at[ws], wsem.at[ws]))

    @pl.when(i == 0)
    def _():
        for cp in weight_copies(be_ref[0], ws_ref[0]):
            cp.start(priority=1)
        gather_start(0, 0)

        @pl.when(n_used > 1)
        def _():
            gather_start(1, 1)

    active = i < n_used
    new_expert = jnp.logical_or(i == 0, be_ref[i] != be_ref[jnp.maximum(i - 1, 0)])

    @pl.when(jnp.logical_and(active, new_expert))
    def _():
        ws = ws_ref[i]
        for cp in weight_copies(be_ref[i], ws):
            cp.wait()
        nxt = nxt_ref[i]

        @pl.when(nxt >= 0)
        def _():
            for cp in weight_copies(nxt, 1 - ws):
                cp.start(priority=1)

        for src, dst in ((stage_g, wg_s), (stage_u, wu_s), (stage_d, wd_s)):
            def cast(r0, src=src, dst=dst):
                dst[pl.ds(r0, 256), :] = src[ws, pl.ds(r0, 256), :].astype(BF16)

            _row_chunks(dst.shape[0], 256, cast)

    def ffn_block(prefetch_next):
        gather_wait(slot)
        x = xbuf[slot].astype(BF16)
        if prefetch_next:
            nxt = lax.rem(i + 2, ring)
            for r in range(MOE_BLOCK):
                row_copy(i + 2, r, nxt).start()
        g = jnp.dot(x, wg_s[...], preferred_element_type=F32)
        u = jnp.dot(x, wu_s[...], preferred_element_type=F32)
        a = (g * jax.nn.sigmoid(g) * u).astype(BF16)
        o_ref[...] = jnp.dot(a, wd_s[...], preferred_element_type=F32).astype(o_ref.dtype)

    has_next = i + 2 < n_used
    pl.when(jnp.logical_and(active, has_next))(functools.partial(ffn_block, True))
    pl.when(jnp.logical_and(active, jnp.logical_not(has_next)))(functools.partial(ffn_block, False))

    @pl.when(jnp.logical_not(active))
    def _():
        o_ref[...] = jnp.zeros_like(o_ref)


def _moe_experts(h, dest, plan, wg, wu, wd, layer):
    d = h.shape[1]
    n_slots = plan["n_slots"]
    de = wg.shape[3]
    n_blocks = n_slots // MOE_BLOCK
    hbm = pl.BlockSpec(memory_space=pl.ANY)
    grid_spec = pltpu.PrefetchScalarGridSpec(
        num_scalar_prefetch=6,
        grid=(n_blocks,),
        in_specs=[hbm, hbm, hbm, hbm],
        out_specs=pl.BlockSpec((MOE_BLOCK, d), lambda i, *_: (i, 0)),
        scratch_shapes=[
            pltpu.VMEM((3, MOE_BLOCK, d), F32),
            pltpu.SemaphoreType.DMA((3,)),
            pltpu.SMEM((n_slots,), jnp.int32),
            pltpu.VMEM((2, d, de), F32), pltpu.VMEM((2, d, de), F32), pltpu.VMEM((2, de, d), F32),
            pltpu.SemaphoreType.DMA((2,)),
            pltpu.VMEM((d, de), BF16), pltpu.VMEM((d, de), BF16), pltpu.VMEM((de, d), BF16),
        ],
    )
    return pl.pallas_call(
        functools.partial(_moe_kernel, layer=layer),
        grid_spec=grid_spec,
        out_shape=jax.ShapeDtypeStruct((n_slots, d), BF16),
        compiler_params=_cparams(("arbitrary",), MOE_VMEM_LIMIT),
        name="moe_experts",
    )(plan["block_e"], plan["n_used"], dest.reshape(-1), plan["wslot"], plan["next_e"], plan["pad_range"],
      h, wg, wu, wd)


def _assignment_ranks(flat_e):
    a = flat_e.shape[0]
    blk = 128
    nb = a // blk
    onehot = (flat_e[:, None] == jnp.arange(N_EXPERTS)[None, :]).astype(BF16).reshape(nb, blk, N_EXPERTS)
    tri = (jnp.arange(blk)[:, None] >= jnp.arange(blk)[None, :]).astype(BF16)
    intra = jnp.einsum("ij,bjk->bik", tri, onehot, preferred_element_type=F32)
    bsum = intra[:, -1, :]
    before = (jnp.arange(nb)[:, None] > jnp.arange(nb)[None, :]).astype(BF16)
    offs = jnp.dot(before, bsum.astype(BF16), preferred_element_type=F32)
    csum = intra + offs[:, None, :]
    rank = jnp.sum(csum * onehot.astype(F32), axis=-1).reshape(a) - 1.0
    counts = offs[-1] + bsum[-1]
    return rank.astype(jnp.int32), counts.astype(jnp.int32)


def _dispatch_plan(idx):
    t = idx.shape[0]
    a = t * TOP_K
    flat_e = idx.reshape(a)
    rank, counts = _assignment_ranks(flat_e)
    padded = (counts + MOE_BLOCK - 1) // MOE_BLOCK * MOE_BLOCK
    pad_end = jnp.cumsum(padded)
    pad_start = pad_end - padded
    dest = pad_start[flat_e] + rank
    n_blocks = (a + N_EXPERTS * (MOE_BLOCK - 1) + MOE_BLOCK - 1) // MOE_BLOCK
    n_slots = n_blocks * MOE_BLOCK
    blk_start = jnp.arange(n_blocks, dtype=jnp.int32) * MOE_BLOCK
    block_e = jnp.minimum(
        jnp.sum((pad_end[None, :] <= blk_start[:, None]).astype(jnp.int32), axis=1), N_EXPERTS - 1)
    n_used = (pad_end[-1] // MOE_BLOCK).astype(jnp.int32).reshape(1)
    nonempty = counts > 0
    eid = jnp.arange(N_EXPERTS, dtype=jnp.int32)
    run_of_e = jnp.cumsum(nonempty.astype(jnp.int32)) - 1
    later = jnp.logical_and(eid[None, :] > eid[:, None], nonempty[None, :])
    nxt_of_e = jnp.min(jnp.where(later, eid[None, :], N_EXPERTS), axis=1)
    nxt_of_e = jnp.where(nxt_of_e == N_EXPERTS, -1, nxt_of_e)
    plan = {
        "n_slots": n_slots, "block_e": block_e.astype(jnp.int32), "n_used": n_used,
        "wslot": (run_of_e[block_e] % 2).astype(jnp.int32), "next_e": nxt_of_e[block_e].astype(jnp.int32),
        "pad_range": jnp.concatenate([pad_start + counts, pad_end]).astype(jnp.int32),
    }
    return plan, dest.reshape(t, TOP_K)


def _final_kernel(x_ref, ya_ref, yb_ref, gate_ref, g2_ref, lg_ref, lb_ref, o_ref, *, alpha):
    f = ya_ref[...].astype(F32) * gate_ref[:, 0:1] + yb_ref[...].astype(F32) * gate_ref[:, 1:2]
    o_ref[...] = _ln(alpha * x_ref[...] + g2_ref[...] * f) * lg_ref[...] + lb_ref[...]


def _final_norm(x1, ya, yb, gate, mod3, ln_g, ln_b, n_batch, seq, alpha):
    n_rows, d = x1.shape
    tm = 512

    def mrow(i):
        return jnp.minimum((i * tm) // seq, n_batch)

    tile = pl.BlockSpec((tm, d), lambda i: (i, 0))
    rowvec = pl.BlockSpec((1, d), lambda i: (0, 0))
    return pl.pallas_call(
        functools.partial(_final_kernel, alpha=alpha),
        grid=(n_rows // tm,),
        in_specs=[tile, tile, tile, pl.BlockSpec((tm, LOGIT_PAD), lambda i: (i, 0)),
                  pl.BlockSpec((None, 1, d), lambda i: (mrow(i), 0, 5)), rowvec, rowvec],
        out_specs=tile,
        out_shape=jax.ShapeDtypeStruct((n_rows, d), F32),
        compiler_params=_cparams(("arbitrary",)),
        name="final_norm",
    )(x1, ya, yb, gate, mod3, ln_g, ln_b)


def _permute_in_cols(w):
    pool, q, k, v, hy, gm = jnp.split(w, (512, 1024, 1152, 1280, 2816), axis=-1)
    return jnp.concatenate([hy, pool, q, gm, k, v], axis=-1)


def kernel(x, c, ctx, c_ctx, w_ada, b_ada, w_in, w_out, mix_norm_g, pool_w, pool_scale, attn_sink,
           hy_conv_w, hy_conv_b, hy_w1, hy_b1, hy_f1, hy_w2, hy_b2, hy_f2, hy_w3, hy_skip,
           gm_ln_g, gm_ln_b, gm_ws, gm_bs, ln1_g, ln1_b, ln2_g, ln2_b, w_router, router_bias,
           w_gate, w_up, w_down):
    n_batch, seq, d = x.shape
    ctx_len = ctx.shape[1]
    depth = w_in.shape[0]
    grid_w = 64
    alpha = (2 * depth) ** 0.25
    n_lat = n_batch * seq
    n_ctx = n_batch * ctx_len

    xa = jnp.concatenate([x.reshape(n_lat, d), ctx.reshape(n_ctx, d)], axis=0)
    cvec = jnp.concatenate([c, c_ctx[None, :], jnp.zeros((8 - n_batch - 1, d), F32)], axis=0)
    mods = _ada_mods(cvec, w_ada, b_ada)
    cos_t, sin_t = _rope_tables(seq, grid_w)
    tabs_lat = _hy_tables(seq, *_hy_tiles(seq))
    tabs_ctx = _hy_tables(ctx_len, *_hy_tiles(ctx_len))
    wr_pad = jnp.pad(w_router, ((0, 0), (0, LOGIT_PAD - N_EXPERTS)))
    wr_hi = wr_pad.astype(BF16)
    wr_lo = (wr_pad - wr_hi.astype(F32)).astype(BF16)
    wr_cat = jnp.concatenate([wr_hi, wr_lo], axis=1)
    rb_pad = jnp.pad(router_bias.astype(F32), (0, LOGIT_PAD - N_EXPERTS)).reshape(1, LOGIT_PAD)
    row = lambda v: v.reshape(1, -1)

    for layer in range(depth):
        last = layer == depth - 1
        mod3 = mods[layer].reshape(8, 1, 6 * d)
        gains = mix_norm_g[layer].reshape(4, 1, MIXER_W)
        lp = {
            "hy_conv_w": hy_conv_w[layer], "hy_conv_b": row(hy_conv_b[layer]),
            "hy_w1p": jnp.pad(hy_w1[layer], ((0, HY_EMB_PAD - HY_EMB), (0, 0))),
            "hy_b1": row(hy_b1[layer]), "hy_f1": row(hy_f1[layer]),
            "hy_w2": hy_w2[layer], "hy_b2": row(hy_b2[layer]), "hy_f2": row(hy_f2[layer]),
            "hy_w3": hy_w3[layer], "hy_skip": hy_skip[layer],
        }
        z = _in_proj(xa, mod3, _permute_in_cols(w_in[layer]).astype(BF16), n_batch, seq)

        pool_bf = pool_w[layer].astype(BF16)
        ps = row(pool_scale[layer])
        ws_cat = jnp.transpose(gm_ws[layer], (1, 0, 2)).reshape(GMLP_CHUNK, GMLP_GROUPS * GMLP_CHUNK)
        bs_mat = jnp.repeat(gm_bs[layer].T, MIXER_W // GMLP_GROUPS, axis=1)
        n_rows = n_lat if last else n_lat + n_ctx

        y_pool = _pool_mix(z, pool_bf, ps, gains[0], n_batch, seq, 0)
        y_attn = _attn_lat(z, attn_sink[layer], cos_t, sin_t, gains[1], n_batch, seq, ctx_len)
        y_hy = _hyena_mix(z, lp, gains[2], n_batch, seq, 0, tabs_lat)
        y_gm = _gmlp_mix(z, row(gm_ln_g[layer]), row(gm_ln_b[layer]), ws_cat.astype(BF16), bs_mat,
                         gains[3], n_rows)
        ys_ctx = None
        if not last:
            ys_ctx = (_pool_mix(z, pool_bf, ps, gains[0], n_batch, ctx_len, n_lat // ctx_len),
                      _attn_ctx(z, attn_sink[layer], gains[1], n_batch, seq, ctx_len),
                      _hyena_mix(z, lp, gains[2], n_batch, ctx_len, n_lat, tabs_ctx))

        x1, h2, idx, gate = _merge((y_pool, y_attn, y_hy), ys_ctx, y_gm, w_out[layer].astype(BF16), xa, mod3,
                                   row(ln1_g[layer]), row(ln1_b[layer]), wr_cat, rb_pad,
                                   n_rows, n_batch, seq, alpha)

        plan, dest = _dispatch_plan(idx[:, :TOP_K])
        ys = _moe_experts(h2, dest, plan, w_gate, w_up, w_down, layer)
        xa = _final_norm(x1, ys[dest[:, 0]], ys[dest[:, 1]], gate, mod3, row(ln2_g[layer]), row(ln2_b[layer]),
                         n_batch, seq, alpha)
    return xa.reshape(n_batch, seq, d)
```

```python
import functools
import math

import jax
import jax.numpy as jnp
from jax import lax
from jax.experimental import pallas as pl
from jax.experimental.pallas import tpu as pltpu

F32 = jnp.float32
BF16 = jnp.bfloat16
HI = lax.Precision.HIGHEST

LN_EPS = 1e-6
MIXER_W = 512
HEAD_DIM = 64
N_HEADS = 8
N_KV_HEADS = 2
KV_GROUP = 4
KV_W = 128
WINDOW = 128
ROPE_THETA = 10000.0
POOL_WINDOWS = (2, 4, 8, 16)
POOL_GROUP = 128
POOL_HALO = 8
GMLP_CHUNK = 128
GMLP_GROUPS = 8
HY_EMB = 33
HY_EMB_PAD = 128
HY_TARGET = 1e-2
HY_FAST = 0.3
HY_SLOW = 1.5
N_EXPERTS = 32
N_EXPERT_GROUPS = 8
TOP_K = 2
D_EXPERT = 768
MOE_BLOCK = 256
LOGIT_PAD = 128

COL_HY = 0
COL_POOL = 3
COL_Q = 4
COL_GU = 5
COL_GV = 6
COL_K = 28
COL_V = 29
IN_W = 3840

VMEM_LIMIT = 56 * 1024 * 1024
MOE_VMEM_LIMIT = 59 * 1024 * 1024


def _cparams(sem, vmem_limit=VMEM_LIMIT):
    return pltpu.CompilerParams(dimension_semantics=sem, vmem_limit_bytes=vmem_limit)


def _ln(x):
    mu = jnp.mean(x, axis=-1, keepdims=True)
    xc = x - mu
    var = jnp.mean(xc * xc, axis=-1, keepdims=True)
    return xc * lax.rsqrt(var + LN_EPS)


def _rms_gain(y, g):
    return y * lax.rsqrt(jnp.mean(y * y, axis=-1, keepdims=True) + LN_EPS) * g


def _row_chunks(n_rows, chunk, body):
    n = n_rows // chunk
    if n == 1:
        body(0)
        return

    def step(i, carry):
        body(pl.multiple_of(i * chunk, chunk))
        return carry

    lax.fori_loop(0, n, step, 0)


def _ada_kernel(c_ref, w_ref, b_ref, o_ref):
    c = c_ref[...]
    s = c * jax.nn.sigmoid(c)
    o_ref[...] = jnp.dot(s, w_ref[...], precision=HI, preferred_element_type=F32) + b_ref[...]


def _ada_mods(cvec, w_ada, b_ada):
    depth, d, n = w_ada.shape
    tn = 1024
    return pl.pallas_call(
        _ada_kernel,
        grid=(depth, n // tn),
        in_specs=[
            pl.BlockSpec((8, d), lambda l, j: (0, 0)),
            pl.BlockSpec((None, d, tn), lambda l, j: (l, 0, j)),
            pl.BlockSpec((None, 1, tn), lambda l, j: (l, 0, j)),
        ],
        out_specs=pl.BlockSpec((None, 8, tn), lambda l, j: (l, 0, j)),
        out_shape=jax.ShapeDtypeStruct((depth, 8, n), F32),
        compiler_params=_cparams(("arbitrary", "arbitrary")),
        name="ada_mods",
    )(cvec, w_ada, b_ada.reshape(depth, 1, n))


def _inproj_kernel(x_ref, sh_ref, sc_ref, w_ref, o_ref, h_scr):
    @pl.when(pl.program_id(1) == 0)
    def _():
        sh = sh_ref[...]
        sc1 = 1.0 + sc_ref[...]

        def body(r0):
            x = x_ref[pl.ds(r0, 256), :]
            h_scr[pl.ds(r0, 256), :] = (_ln(x) * sc1 + sh).astype(BF16)

        _row_chunks(x_ref.shape[0], 256, body)

    o_ref[...] = jnp.dot(h_scr[...], w_ref[...], preferred_element_type=F32)


def _in_proj(xa, mod3, w_bf, n_batch, seq):
    t, d = xa.shape
    n = w_bf.shape[1]
    tn = 1280
    tm = next(m for m in (1024, 512, 256) if t % m == 0 and seq % m == 0)

    def mrow(i):
        return jnp.minimum((i * tm) // seq, n_batch)

    return pl.pallas_call(
        _inproj_kernel,
        grid=(t // tm, n // tn),
        in_specs=[
            pl.BlockSpec((tm, d), lambda i, j: (i, 0)),
            pl.BlockSpec((None, 1, d), lambda i, j: (mrow(i), 0, 0)),
            pl.BlockSpec((None, 1, d), lambda i, j: (mrow(i), 0, 1)),
            pl.BlockSpec((d, tn), lambda i, j: (0, j)),
        ],
        out_specs=pl.BlockSpec((tm, tn), lambda i, j: (i, j)),
        out_shape=jax.ShapeDtypeStruct((t, n), F32),
        scratch_shapes=[pltpu.VMEM((tm, d), BF16)],
        compiler_params=_cparams(("arbitrary", "arbitrary")),
        name="in_proj",
    )(xa, mod3, mod3, w_bf)


def _pool_kernel(z_ref, w_ref, ps_ref, g_ref, o_ref, buf):
    seq = z_ref.shape[0]
    rc = min(256, seq)
    zeros = jnp.zeros((POOL_HALO, MIXER_W), F32)
    buf[0:POOL_HALO, :] = zeros
    buf[seq + POOL_HALO:seq + 2 * POOL_HALO, :] = zeros

    def fill(r0):
        buf[pl.ds(r0 + POOL_HALO, rc), :] = z_ref[pl.ds(r0, rc), :]

    _row_chunks(seq, rc, fill)

    def body(r0):
        win = buf[pl.ds(r0, rc + 2 * POOL_HALO), :]
        t = r0 + lax.broadcasted_iota(jnp.int32, (rc, 1), 0)
        outs = []
        for gi, w in enumerate(POOL_WINDOWS):
            lanes = slice(gi * POOL_GROUP, (gi + 1) * POOL_GROUP)
            acc = None
            for j in range(-w // 2, w // 2):
                piece = win[POOL_HALO + j:POOL_HALO + j + rc, lanes]
                acc = piece if acc is None else acc + piece
            cnt = (jnp.minimum(t + w // 2, seq) - jnp.maximum(t - w // 2, 0)).astype(F32)
            dlt = acc / cnt - win[POOL_HALO:POOL_HALO + rc, lanes]
            outs.append(jnp.dot(dlt.astype(BF16), w_ref[gi], preferred_element_type=F32))
        y = jnp.concatenate(outs, axis=-1) * ps_ref[...]
        o_ref[pl.ds(r0, rc), :] = _rms_gain(y, g_ref[...]).astype(BF16)

    _row_chunks(seq, rc, body)


def _pool_mix(z, pool_w_bf, pool_scale, gain, n_seq, seq, row_blk0):
    return pl.pallas_call(
        _pool_kernel,
        grid=(n_seq,),
        in_specs=[
            pl.BlockSpec((seq, MIXER_W), lambda s: (row_blk0 + s, COL_POOL)),
            pl.BlockSpec((4, POOL_GROUP, POOL_GROUP), lambda s: (0, 0, 0)),
            pl.BlockSpec((1, MIXER_W), lambda s: (0, 0)),
            pl.BlockSpec((1, MIXER_W), lambda s: (0, 0)),
        ],
        out_specs=pl.BlockSpec((seq, MIXER_W), lambda s: (s, 0)),
        out_shape=jax.ShapeDtypeStruct((n_seq * seq, MIXER_W), BF16),
        scratch_shapes=[pltpu.VMEM((seq + 2 * POOL_HALO, MIXER_W), F32)],
        compiler_params=_cparams(("arbitrary",)),
        name="pool_mix",
    )(z, pool_w_bf, pool_scale, gain)


def _rope(x, cos, sin_signed):
    lane = lax.broadcasted_iota(jnp.int32, x.shape, 1)
    partner = jnp.where((lane & 31) < 16, pltpu.roll(x, 112, axis=1), pltpu.roll(x, 16, axis=1))
    return x * cos + partner * sin_signed


def _attend(qs, sink_col, parts):
    dn = (((1,), (1,)), ((), ()))
    scores = []
    for kk, _, mask in parts:
        s = lax.dot_general(qs, kk, dn, preferred_element_type=F32)
        if mask is not None:
            s = s + mask
        scores.append(s)
    m = sink_col
    for s in scores:
        m = jnp.maximum(m, jnp.max(s, axis=-1, keepdims=True))
    den = jnp.exp(sink_col - m)
    out = None
    for s, (_, vv, _) in zip(scores, parts):
        p = jnp.exp(s - m)
        den = den + jnp.sum(p, axis=-1, keepdims=True)
        o = jnp.dot(p.astype(BF16), vv, preferred_element_type=F32)
        out = o if out is None else out + o
    return out / den


def _heads_attend(q_all, sink_ref, tq, make_parts):
    cols = [None] * N_HEADS
    for kh in range(N_KV_HEADS):
        heads = [kh * KV_GROUP + g for g in range(KV_GROUP)]
        qs = jnp.concatenate([q_all[:, h * HEAD_DIM:(h + 1) * HEAD_DIM] for h in heads], axis=0)
        sink_col = jnp.concatenate([jnp.full((tq, 1), sink_ref[h], F32) for h in heads], axis=0)
        o = _attend(qs.astype(BF16), sink_col, make_parts(kh))
        for g, h in enumerate(heads):
            cols[h] = o[g * tq:(g + 1) * tq, :]
    return jnp.concatenate(cols, axis=-1)


def _attn_lat_kernel(sink_ref, q_ref, k_ref, v_ref, kc_ref, vc_ref, cos_ref, sin_ref, g_ref, o_ref):
    tq = q_ref.shape[0]
    seq = k_ref.shape[0]
    nband = tq + 2 * WINDOW
    i = pl.program_id(1)
    q0 = pl.multiple_of(i * tq, tq)
    k0 = pl.multiple_of(jnp.clip(q0 - WINDOW, 0, seq - nband), WINDOW)

    cq = cos_ref[pl.ds(q0, tq), :]
    sq = sin_ref[pl.ds(q0, tq), :]
    q = q_ref[...]
    q_all = jnp.concatenate(
        [_rope(q[:, c * 128:(c + 1) * 128], cq, sq) for c in range(MIXER_W // 128)], axis=-1)
    q_all = q_all * (HEAD_DIM ** -0.5)

    kb = _rope(k_ref[pl.ds(k0, nband), :], cos_ref[pl.ds(k0, nband), :],
               sin_ref[pl.ds(k0, nband), :]).astype(BF16)
    vb = v_ref[pl.ds(k0, nband), :].astype(BF16)
    kc = kc_ref[...].astype(BF16)
    vc = vc_ref[...].astype(BF16)

    qpos = q0 + lax.broadcasted_iota(jnp.int32, (tq, nband), 0)
    kpos = k0 + lax.broadcasted_iota(jnp.int32, (tq, nband), 1)
    band = jnp.where(jnp.abs(qpos - kpos) <= WINDOW, 0.0, -1e30)
    mask = jnp.concatenate([band] * KV_GROUP, axis=0)

    def make_parts(kh):
        hs = slice(kh * HEAD_DIM, (kh + 1) * HEAD_DIM)
        return [(kb[:, hs], vb[:, hs], mask), (kc[:, hs], vc[:, hs], None)]

    y = _heads_attend(q_all, sink_ref, tq, make_parts)
    o_ref[...] = _rms_gain(y, g_ref[...]).astype(BF16)


def _attn_ctx_kernel(sink_ref, q_ref, kc_ref, vc_ref, g_ref, o_ref):
    tq = q_ref.shape[0]
    q_all = q_ref[...] * (HEAD_DIM ** -0.5)
    kc = kc_ref[...].astype(BF16)
    vc = vc_ref[...].astype(BF16)

    def make_parts(kh):
        hs = slice(kh * HEAD_DIM, (kh + 1) * HEAD_DIM)
        return [(kc[:, hs], vc[:, hs], None)]

    y = _heads_attend(q_all, sink_ref, tq, make_parts)
    o_ref[...] = _rms_gain(y, g_ref[...]).astype(BF16)


def _attn_lat(z, sink, cos_t, sin_t, gain, n_batch, seq, ctx_len):
    tq = 256
    nq = seq // tq
    ctx_blk0 = (n_batch * seq) // ctx_len
    smem = pl.BlockSpec(memory_space=pltpu.SMEM)
    return pl.pallas_call(
        _attn_lat_kernel,
        grid=(n_batch, nq),
        in_specs=[
            smem,
            pl.BlockSpec((tq, MIXER_W), lambda b, i: (b * nq + i, COL_Q)),
            pl.BlockSpec((seq, KV_W), lambda b, i: (b, COL_K)),
            pl.BlockSpec((seq, KV_W), lambda b, i: (b, COL_V)),
            pl.BlockSpec((ctx_len, KV_W), lambda b, i: (ctx_blk0 + b, COL_K)),
            pl.BlockSpec((ctx_len, KV_W), lambda b, i: (ctx_blk0 + b, COL_V)),
            pl.BlockSpec((seq, 128), lambda b, i: (0, 0)),
            pl.BlockSpec((seq, 128), lambda b, i: (0, 0)),
            pl.BlockSpec((1, MIXER_W), lambda b, i: (0, 0)),
        ],
        out_specs=pl.BlockSpec((tq, MIXER_W), lambda b, i: (b * nq + i, 0)),
        out_shape=jax.ShapeDtypeStruct((n_batch * seq, MIXER_W), BF16),
        compiler_params=_cparams(("arbitrary", "arbitrary")),
        name="attn_lat",
    )(sink, z, z, z, z, z, cos_t, sin_t, gain)


def _attn_ctx(z, sink, gain, n_batch, seq, ctx_len):
    ctx_blk0 = (n_batch * seq) // ctx_len
    smem = pl.BlockSpec(memory_space=pltpu.SMEM)
    return pl.pallas_call(
        _attn_ctx_kernel,
        grid=(n_batch,),
        in_specs=[
            smem,
            pl.BlockSpec((ctx_len, MIXER_W), lambda b: (ctx_blk0 + b, COL_Q)),
            pl.BlockSpec((ctx_len, KV_W), lambda b: (ctx_blk0 + b, COL_K)),
            pl.BlockSpec((ctx_len, KV_W), lambda b: (ctx_blk0 + b, COL_V)),
            pl.BlockSpec((1, MIXER_W), lambda b: (0, 0)),
        ],
        out_specs=pl.BlockSpec((ctx_len, MIXER_W), lambda b: (b, 0)),
        out_shape=jax.ShapeDtypeStruct((n_batch * ctx_len, MIXER_W), BF16),
        compiler_params=_cparams(("arbitrary",)),
        name="attn_ctx",
    )(sink, z, z, z, gain)


def _rope_tables(seq, grid_w):
    nf = HEAD_DIM // 4
    inv = ROPE_THETA ** (-jnp.arange(nf, dtype=F32) / nf)
    t = jnp.arange(seq)
    row = (t // grid_w).astype(F32)[:, None] * inv[None, :]
    col = (t % grid_w).astype(F32)[:, None] * inv[None, :]
    cos_h = jnp.concatenate([jnp.cos(row), jnp.cos(row), jnp.cos(col), jnp.cos(col)], axis=-1)
    sin_h = jnp.concatenate([-jnp.sin(row), jnp.sin(row), -jnp.sin(col), jnp.sin(col)], axis=-1)
    return jnp.tile(cos_h, (1, 2)), jnp.tile(sin_h, (1, 2))


def _hy_conv_kernel(*refs):
    z_refs, zp_refs, zn_refs = refs[0:3], refs[3:6], refs[6:9]
    w_ref, b_ref, g_ref, u_ref = refs[9:]
    i = pl.program_id(1)
    th = z_refs[0].shape[0] // 2
    has_prev = (i > 0).astype(F32)
    has_next = (i < pl.num_programs(1) - 1).astype(F32)
    row = lax.broadcasted_iota(jnp.int32, (th, 128), 0)
    for part in range(3):
        ze = z_refs[part][pl.ds(0, th, stride=2), :]
        zo = z_refs[part][pl.ds(1, th, stride=2), :]
        prev_row = zp_refs[part][POOL_HALO - 1:POOL_HALO, :] * has_prev
        next_row = zn_refs[part][0:1, :] * has_next
        zo_m = jnp.where(row == 0, prev_row, pltpu.roll(zo, 1, axis=0))
        ze_p = jnp.where(row == th - 1, next_row, pltpu.roll(ze, th - 1, axis=0))
        w0, w1, w2, b = w_ref[part, 0:1, :], w_ref[part, 1:2, :], w_ref[part, 2:3, :], b_ref[part]
        ye = b + zo_m * w0 + ze * w1 + zo * w2
        yo = b + ze * w0 + zo * w1 + ze_p * w2
        g_ref[part, 0] = ye
        g_ref[part, 1] = yo
        if part == 2:
            u_ref[0] = ye.astype(BF16)
            u_ref[1] = yo.astype(BF16)


def _hy_conv(z, conv_w, conv_b, n_seq, seq, row0):
    tl = min(2048, seq)
    nt = seq // tl
    blk0 = row0 // tl
    hb0 = row0 // POOL_HALO
    hpt = tl // POOL_HALO
    last_halo = (row0 + n_seq * seq) // POOL_HALO - 1
    cpp = MIXER_W // 128
    ncol = n_seq * MIXER_W

    def main_spec(part):
        return pl.BlockSpec((tl, 128), lambda s, i, c: (blk0 + s * nt + i, part * cpp + c))

    def prev_spec(part):
        return pl.BlockSpec((POOL_HALO, 128),
                            lambda s, i, c: (jnp.maximum(hb0 + (s * nt + i) * hpt - 1, 0), part * cpp + c))

    def next_spec(part):
        return pl.BlockSpec((POOL_HALO, 128),
                            lambda s, i, c: (jnp.minimum(hb0 + (s * nt + i + 1) * hpt, last_halo), part * cpp + c))

    w3 = conv_w.reshape(3, 3, MIXER_W).transpose(1, 0, 2)
    b3 = conv_b.reshape(3, 1, MIXER_W)
    return pl.pallas_call(
        _hy_conv_kernel,
        grid=(n_seq, nt, cpp),
        in_specs=[main_spec(p) for p in range(3)] + [prev_spec(p) for p in range(3)]
        + [next_spec(p) for p in range(3)]
        + [pl.BlockSpec((3, 3, 128), lambda s, i, c: (0, 0, c)),
           pl.BlockSpec((3, 1, 128), lambda s, i, c: (0, 0, c))],
        out_specs=[
            pl.BlockSpec((3, 2, tl // 2, 128), lambda s, i, c: (0, 0, i, s * cpp + c)),
            pl.BlockSpec((2, tl // 2, 128), lambda s, i, c: (0, i, s * cpp + c)),
        ],
        out_shape=[jax.ShapeDtypeStruct((3, 2, seq // 2, ncol), F32),
                   jax.ShapeDtypeStruct((2, seq // 2, ncol), BF16)],
        compiler_params=_cparams(("arbitrary", "arbitrary", "arbitrary")),
        name="hy_conv",
    )(*([z] * 9), w3, b3)


def _hy_filter_kernel(feat_ref, dec_ref, w1_ref, b1_ref, f1_ref, w2_ref, b2_ref, f2_ref, w3_ref, o_ref):
    tl = feat_ref.shape[0]
    h = jnp.dot(feat_ref[...], w1_ref[...], precision=HI, preferred_element_type=F32) + b1_ref[...]
    h = jnp.sin(f1_ref[...] * h)
    h = jnp.dot(h, w2_ref[...], precision=HI, preferred_element_type=F32) + b2_ref[...]
    h = jnp.sin(f2_ref[...] * h)
    filt = jnp.dot(h, w3_ref[...], precision=HI, preferred_element_type=F32)
    dec = dec_ref[...]
    t = 2 * (pl.program_id(1) * tl + lax.broadcasted_iota(jnp.int32, (tl, 1), 0)) + pl.program_id(0)
    not_first = (t > 0).astype(F32)
    for o in range(2):
        hf = filt[:, (2 * o) * MIXER_W:(2 * o + 1) * MIXER_W] * dec
        hb = filt[:, (2 * o + 1) * MIXER_W:(2 * o + 2) * MIXER_W] * dec * not_first
        o_ref[:, o * MIXER_W:(o + 1) * MIXER_W] = (hf + hb).astype(BF16)
        o_ref[:, (2 + o) * MIXER_W:(3 + o) * MIXER_W] = (hb - hf).astype(BF16)


def _hy_filters(feat, decay, w1p, b1, f1, w2, b2, f2, w3):
    half = feat.shape[1]
    tl = min(512, half)
    hid = w2.shape[0]
    full = lambda shape: pl.BlockSpec(shape, lambda p, i: tuple(0 for _ in shape))
    return pl.pallas_call(
        _hy_filter_kernel,
        grid=(2, half // tl),
        in_specs=[
            pl.BlockSpec((None, tl, HY_EMB_PAD), lambda p, i: (p, i, 0)),
            pl.BlockSpec((None, tl, MIXER_W), lambda p, i: (p, i, 0)),
            full((HY_EMB_PAD, hid)), full((1, hid)), full((1, hid)),
            full((hid, hid)), full((1, hid)), full((1, hid)),
            full((hid, 4 * MIXER_W)),
        ],
        out_specs=pl.BlockSpec((None, tl, 4 * MIXER_W), lambda p, i: (p, i, 0)),
        out_shape=jax.ShapeDtypeStruct((2, half, 4 * MIXER_W), BF16),
        compiler_params=_cparams(("arbitrary", "arbitrary")),
        name="hy_filters",
    )(feat, decay, w1p, b1, f1, w2, b2, f2, w3)


def _hy_kspec_kernel(te_ref, to_ref, ab_ref, o_ref, *, inv_len):
    hb = te_ref.shape[0] // 2
    half = ab_ref.shape[2] // 2
    ec = jnp.dot(te_ref[0:hb, :], ab_ref[0, :, 0:half], preferred_element_type=F32)
    oc = jnp.dot(to_ref[0:hb, :], ab_ref[1, :, 0:half], preferred_element_type=F32)
    es = jnp.dot(te_ref[hb:, :], ab_ref[0, :, half:], preferred_element_type=F32)
    os_ = jnp.dot(to_ref[hb:, :], ab_ref[1, :, half:], preferred_element_type=F32)
    o_ref[0, 0:hb, :] = (ec + oc) * inv_len
    o_ref[0, hb:, :] = (es + os_) * inv_len
    o_ref[1, 0:hb, :] = (ec - oc) * inv_len
    o_ref[1, hb:, :] = (os_ - es) * inv_len


def _hy_kspec(fwd, ab, tm):
    _, seq, half = fwd.shape
    return pl.pallas_call(
        functools.partial(_hy_kspec_kernel, inv_len=1.0 / seq),
        grid=(seq // tm,),
        in_specs=[
            pl.BlockSpec((None, tm, half), lambda i: (0, i, 0)),
            pl.BlockSpec((None, tm, half), lambda i: (1, i, 0)),
            pl.BlockSpec((2, half, 4 * MIXER_W), lambda i: (0, 0, 0)),
        ],
        out_specs=pl.BlockSpec((2, tm, 2 * MIXER_W), lambda i: (0, i, 0)),
        out_shape=jax.ShapeDtypeStruct((2, seq, 2 * MIXER_W), F32),
        compiler_params=_cparams(("arbitrary",)),
        name="hy_kspec",
    )(fwd, fwd, ab)


def _hy_fwd_kernel(te_ref, to_ref, u_ref, k_ref, o_ref):
    hb = te_ref.shape[0] // 2
    reps = u_ref.shape[2] // MIXER_W
    ep = jnp.dot(te_ref[...], u_ref[0], preferred_element_type=F32)
    op = jnp.dot(to_ref[...], u_ref[1], preferred_element_type=F32)
    ec, es, oc, os_ = ep[0:hb, :], ep[hb:, :], op[0:hb, :], op[hb:, :]
    tile = lambda v: jnp.concatenate([v] * reps, axis=-1)
    p, q, pm, qm = ec + oc, es + os_, ec - oc, os_ - es
    kr, ki = tile(k_ref[0, 0:hb, :]), tile(k_ref[0, hb:, :])
    krm, kim = tile(k_ref[1, 0:hb, :]), tile(k_ref[1, hb:, :])
    yr, yn = p * kr + q * ki, q * kr - p * ki
    yrm, ynm = pm * krm + qm * kim, qm * krm - pm * kim
    o_ref[0, 0:hb, :] = (yr + yrm).astype(BF16)
    o_ref[0, hb:, :] = (yn - ynm).astype(BF16)
    o_ref[1, 0:hb, :] = (yr - yrm).astype(BF16)
    o_ref[1, hb:, :] = (yn + ynm).astype(BF16)


def _hy_fwd(fwd, u, kspec, order, tm, tn):
    _, seq, half = fwd.shape
    ncol = u.shape[2]
    return pl.pallas_call(
        _hy_fwd_kernel,
        grid=(ncol // tn, seq // tm),
        in_specs=[
            pl.BlockSpec((None, tm, half), lambda j, i: (0, i, 0)),
            pl.BlockSpec((None, tm, half), lambda j, i: (1, i, 0)),
            pl.BlockSpec((2, half, tn), lambda j, i: (0, 0, j)),
            pl.BlockSpec((2, tm, MIXER_W), lambda j, i: (0, i, order)),
        ],
        out_specs=pl.BlockSpec((2, tm, tn), lambda j, i: (0, i, j)),
        out_shape=jax.ShapeDtypeStruct((2, seq, ncol), BF16),
        compiler_params=_cparams(("arbitrary", "arbitrary")),
        name="hy_fwd",
    )(fwd, fwd, u, kspec)


def _hy_inv_kernel(ie_ref, io_ref, y_ref, u_ref, gate_ref, skip_ref, *rest, final):
    reps = u_ref.shape[2] // MIXER_W
    skip = jnp.concatenate([skip_ref[...]] * reps, axis=-1)
    for par, inv_ref in enumerate((ie_ref, io_ref)):
        conv = jnp.dot(inv_ref[...], y_ref[par], preferred_element_type=F32)
        out = gate_ref[par] * (conv + u_ref[par].astype(F32) * skip)
        if final:
            g_ref, o_ref = rest
            for r in range(reps):
                o_ref[r, par] = _rms_gain(out[:, r * MIXER_W:(r + 1) * MIXER_W], g_ref[...]).astype(BF16)
        else:
            (o_ref,) = rest
            o_ref[par] = out.astype(BF16)


def _hy_inv(inv, yspec, u, conv, part, skip, gain, final, tm, tn):
    _, half, seq = inv.shape
    ncol = u.shape[2]
    reps = tn // MIXER_W
    in_specs = [
        pl.BlockSpec((None, tm, seq), lambda j, i: (0, i, 0)),
        pl.BlockSpec((None, tm, seq), lambda j, i: (1, i, 0)),
        pl.BlockSpec((2, seq, tn), lambda j, i: (0, 0, j)),
        pl.BlockSpec((2, tm, tn), lambda j, i: (0, i, j)),
        pl.BlockSpec((None, 2, tm, tn), lambda j, i: (part, 0, i, j)),
        pl.BlockSpec((1, MIXER_W), lambda j, i: (0, 0)),
    ]
    args = [inv, inv, yspec, u, conv, skip]
    if final:
        in_specs.append(pl.BlockSpec((1, MIXER_W), lambda j, i: (0, 0)))
        args.append(gain)
        out_spec = pl.BlockSpec((reps, 2, tm, MIXER_W), lambda j, i: (j, 0, i, 0))
        out_shape = jax.ShapeDtypeStruct((ncol // MIXER_W, 2, half, MIXER_W), BF16)
    else:
        out_spec = pl.BlockSpec((2, tm, tn), lambda j, i: (0, i, j))
        out_shape = jax.ShapeDtypeStruct((2, half, ncol), BF16)
    return pl.pallas_call(
        functools.partial(_hy_inv_kernel, final=final),
        grid=(ncol // tn, half // tm),
        in_specs=in_specs,
        out_specs=out_spec,
        out_shape=out_shape,
        compiler_params=_cparams(("arbitrary", "arbitrary")),
        name="hy_inv",
    )(*args)


def _hy_table_kernel(hc_ref, hs_ref, lc_ref, ls_ref, o_ref, *, interleave):
    n_hi, split = hc_ref.shape[0], lc_ref.shape[0]
    lc, ls = lc_ref[...], ls_ref[...]
    for g in range(n_hi):
        hc, hs = hc_ref[g:g + 1, :], hs_ref[g:g + 1, :]
        cos_t = (hc * lc - hs * ls).astype(BF16)
        sin_t = (hs * lc + hc * ls).astype(BF16)
        rows = slice(g * split, (g + 1) * split)
        if interleave == 0:
            o_ref[rows, :] = cos_t
            o_ref[n_hi * split + g * split:n_hi * split + (g + 1) * split, :] = sin_t
        else:
            w = interleave
            for c in range(cos_t.shape[1] // w):
                o_ref[rows, 2 * c * w:(2 * c + 1) * w] = cos_t[:, c * w:(c + 1) * w]
                o_ref[rows, (2 * c + 1) * w:(2 * c + 2) * w] = sin_t[:, c * w:(c + 1) * w]


def _hy_tables(seq, tm, tm_inv):
    half = seq // 2
    hb = tm // 2
    split = 32
    unit = 2.0 * math.pi / (4 * seq)
    ar = lambda n: jnp.arange(n, dtype=jnp.int32)

    def cos_sin(a, b):
        ang = ((a[..., :, None] * b[..., None, :]) % (4 * seq)).astype(F32) * unit
        return jnp.cos(ang), jnp.sin(ang)

    par = ar(2)[:, None]
    n_all = 2 * ar(half)[None, :] + par
    k_odd = 2 * ar(half) + 1
    f_hc, f_hs = cos_sin(jnp.broadcast_to(2 * split * ar(half // split), (2, half // split)), n_all)
    f_lc, f_ls = cos_sin(jnp.broadcast_to(2 * ar(split) + 1, (2, split)), n_all)
    i_hc, i_hs = cos_sin(2 * split * ar(half // split), k_odd)
    i_lc, i_ls = cos_sin(2 * ar(split)[None, :] + par, jnp.broadcast_to(k_odd, (2, half)))

    n_hi_f = hb // split
    fwd = pl.pallas_call(
        functools.partial(_hy_table_kernel, interleave=0),
        grid=(2, half // hb),
        in_specs=[
            pl.BlockSpec((None, n_hi_f, half), lambda p, i: (p, i, 0)),
            pl.BlockSpec((None, n_hi_f, half), lambda p, i: (p, i, 0)),
            pl.BlockSpec((None, split, half), lambda p, i: (p, 0, 0)),
            pl.BlockSpec((None, split, half), lambda p, i: (p, 0, 0)),
        ],
        out_specs=pl.BlockSpec((None, tm, half), lambda p, i: (p, i, 0)),
        out_shape=jax.ShapeDtypeStruct((2, seq, half), BF16),
        compiler_params=_cparams(("arbitrary", "arbitrary")),
        name="hy_table_fwd",
    )(f_hc, f_hs, f_lc, f_ls)
    n_hi_i = tm_inv // split
    inv = pl.pallas_call(
        functools.partial(_hy_table_kernel, interleave=hb),
        grid=(2, half // tm_inv),
        in_specs=[
            pl.BlockSpec((n_hi_i, half), lambda p, i: (i, 0)),
            pl.BlockSpec((n_hi_i, half), lambda p, i: (i, 0)),
            pl.BlockSpec((None, split, half), lambda p, i: (p, 0, 0)),
            pl.BlockSpec((None, split, half), lambda p, i: (p, 0, 0)),
        ],
        out_specs=pl.BlockSpec((None, tm_inv, seq), lambda p, i: (p, i, 0)),
        out_shape=jax.ShapeDtypeStruct((2, half, seq), BF16),
        compiler_params=_cparams(("arbitrary", "arbitrary")),
        name="hy_table_inv",
    )(i_hc, i_hs, i_lc, i_ls)
    return fwd, inv


def _hy_features(seq):
    t = jnp.linspace(0.0, 1.0, seq, dtype=F32)[:, None]
    bands = (HY_EMB - 1) // 2
    freqs = jnp.linspace(1e-4, bands - 1, bands, dtype=F32)[None, :]
    ang = (2.0 * math.pi / seq) * jnp.arange(seq, dtype=F32)[:, None] * freqs
    feat = jnp.concatenate([t, jnp.cos(ang), -jnp.sin(ang)], -1)
    feat = jnp.pad(feat, ((0, 0), (0, HY_EMB_PAD - HY_EMB)))
    deltas = jnp.abs(jnp.linspace(math.log(HY_TARGET) / HY_SLOW, math.log(HY_TARGET) / HY_FAST,
                                  MIXER_W, dtype=F32))
    decay = jnp.exp(-t * deltas[None, :])
    split = lambda a: jnp.stack([a[0::2], a[1::2]], axis=0)
    return split(feat), split(decay)


def _hy_tiles(seq):
    return min(512, seq), min(256, seq // 2)


def _hyena_mix(z, lp, gain, n_seq, seq, row0, tables):
    tm_f, tm_i = _hy_tiles(seq)
    ncol = n_seq * MIXER_W
    tn = min(1024, ncol)
    fwd, inv = tables
    feat, decay = _hy_features(seq)
    ab = _hy_filters(feat, decay, lp["hy_w1p"], lp["hy_b1"], lp["hy_f1"], lp["hy_w2"], lp["hy_b2"],
                     lp["hy_f2"], lp["hy_w3"])
    kspec = _hy_kspec(fwd, ab, tm_f)
    conv, u = _hy_conv(z, lp["hy_conv_w"], lp["hy_conv_b"], n_seq, seq, row0)
    y0 = _hy_fwd(fwd, u, kspec, 0, tm_f, tn)
    u1 = _hy_inv(inv, y0, u, conv, 0, lp["hy_skip"][0:1], None, False, tm_i, tn)
    y1 = _hy_fwd(fwd, u1, kspec, 1, tm_f, tn)
    out = _hy_inv(inv, y1, u1, conv, 1, lp["hy_skip"][1:2], gain, True, tm_i, tn)
    return jnp.transpose(out, (0, 2, 1, 3)).reshape(n_seq * seq, MIXER_W)


def _gelu(x):
    c = math.sqrt(2.0 / math.pi)
    return 0.5 * x * (1.0 + jnp.tanh(c * (x + 0.044715 * (x * x * x))))


def _gmlp_kernel(u_ref, v_ref, lg_ref, lb_ref, ws_ref, bs_ref, g_ref, o_ref):
    tl = u_ref.shape[0]
    gw = MIXER_W // GMLP_GROUPS
    lane_grp = lax.broadcasted_iota(jnp.int32, (GMLP_CHUNK, MIXER_W), 1) // gw
    for c in range(tl // GMLP_CHUNK):
        rows = slice(c * GMLP_CHUNK, (c + 1) * GMLP_CHUNK)
        v = _ln(_gelu(v_ref[rows, :])) * lg_ref[...] + lb_ref[...]
        vb = v.astype(BF16)
        stacked = jnp.concatenate(
            [jnp.where(lane_grp == g, vb, jnp.zeros_like(vb)) for g in range(GMLP_GROUPS)], axis=0)
        s = jnp.dot(ws_ref[...], stacked, preferred_element_type=F32) + bs_ref[...]
        y = _gelu(u_ref[rows, :]) * s
        o_ref[rows, :] = _rms_gain(y, g_ref[...]).astype(BF16)


def _gmlp_mix(z, ln_g, ln_b, ws_cat, bs_mat, gain, n_rows):
    tl = 512
    full = lambda shape: pl.BlockSpec(shape, lambda i: tuple(0 for _ in shape))
    return pl.pallas_call(
        _gmlp_kernel,
        grid=(n_rows // tl,),
        in_specs=[
            pl.BlockSpec((tl, MIXER_W), lambda i: (i, COL_GU)),
            pl.BlockSpec((tl, MIXER_W), lambda i: (i, COL_GV)),
            full((1, MIXER_W)), full((1, MIXER_W)),
            full((GMLP_CHUNK, GMLP_GROUPS * GMLP_CHUNK)),
            full((GMLP_CHUNK, MIXER_W)),
            full((1, MIXER_W)),
        ],
        out_specs=pl.BlockSpec((tl, MIXER_W), lambda i: (i, 0)),
        out_shape=jax.ShapeDtypeStruct((n_rows, MIXER_W), BF16),
        compiler_params=_cparams(("arbitrary",)),
        name="gmlp_mix",
    )(z, z, ln_g, ln_b, ws_cat, bs_mat, gain)


def _merge_kernel(*refs, alpha, n_lat_tiles, n_tiles):
    has_ctx = n_tiles > n_lat_tiles
    lat_refs, refs = refs[:3], refs[3:]
    if has_ctx:
        ctx_refs, refs = refs[:3], refs[3:]
    (yg_ref, w_ref, x_ref, g1_ref, lg_ref, lb_ref, sh_ref, sc_ref, wrc_ref, rb_ref,
     xo_ref, h_ref, idx_ref, gate_ref, y_a, y_b) = refs
    s = pl.program_id(0)

    @pl.when(s == 0)
    def _():
        y_b[...] = jnp.zeros_like(y_b)

    def step(y_new, y_old):
        parts = [r[...] for r in lat_refs]
        if has_ctx:
            is_ctx = jnp.minimum(s, n_tiles - 1) >= n_lat_tiles
            parts = [jnp.where(is_ctx, c[...], p) for c, p in zip(ctx_refs, parts)]
        ymix = jnp.concatenate(parts + [yg_ref[...]], axis=-1)
        y_new[...] = jnp.dot(ymix, w_ref[...], preferred_element_type=F32)

        x1 = _ln(alpha * x_ref[...] + g1_ref[...] * y_old[...]) * lg_ref[...] + lb_ref[...]
        xo_ref[...] = x1
        h = _ln(x1) * (1.0 + sc_ref[...]) + sh_ref[...]
        h_ref[...] = h
        h_hi = h.astype(BF16)
        h_lo = (h - h_hi.astype(F32)).astype(BF16)
        both = jnp.dot(h_hi, wrc_ref[...], preferred_element_type=F32)
        logits = both[:, :LOGIT_PAD] + (jnp.dot(h_lo, wrc_ref[:, :LOGIT_PAD], preferred_element_type=F32)
                                        + both[:, LOGIT_PAD:])
        idx, gate = _route_rows(logits, rb_ref[...])
        idx_ref[...] = idx
        gate_ref[...] = gate

    even = lax.rem(s, 2) == 0
    pl.when(even)(functools.partial(step, y_a, y_b))
    pl.when(jnp.logical_not(even))(functools.partial(step, y_b, y_a))


def _route_rows(logits, bias):
    neg = -1e30
    lane = lax.broadcasted_iota(jnp.int32, logits.shape, 1)
    lane_f = lane.astype(F32)
    valid = lane < N_EXPERTS
    per = N_EXPERTS // N_EXPERT_GROUPS
    s = jax.nn.sigmoid(logits)
    sel = jnp.where(valid, s + bias, neg)
    sh = [sel] + [pltpu.roll(sel, LOGIT_PAD - j, axis=1) for j in range(1, per)]
    pair = None
    for a in range(per):
        for b in range(a + 1, per):
            t = sh[a] + sh[b]
            pair = t if pair is None else jnp.maximum(pair, t)
    grp = jnp.where(valid & ((lane & (per - 1)) == 0), pair, neg)

    def first_max(v):
        m = jnp.max(v, axis=-1, keepdims=True)
        return jnp.min(jnp.where(v == m, lane_f, float(LOGIT_PAD)), axis=-1, keepdims=True)

    best = first_max(grp).astype(jnp.int32)
    shift = per.bit_length() - 1
    cand = jnp.where(valid & ((lane >> shift) == (best >> shift)), sel, neg)
    i1 = first_max(cand)
    i2 = first_max(jnp.where(lane_f == i1, neg, cand))
    w1 = jnp.sum(jnp.where(lane_f == i1, s, 0.0), axis=-1, keepdims=True)
    w2 = jnp.sum(jnp.where(lane_f == i2, s, 0.0), axis=-1, keepdims=True)
    tot = w1 + w2
    idx = jnp.where(lane == 0, i1, jnp.where(lane == 1, i2, 0.0)).astype(jnp.int32)
    gate = jnp.where(lane == 0, w1 / tot, jnp.where(lane == 1, w2 / tot, 0.0))
    return idx, gate


def _merge(ys_lat, ys_ctx, y_gm, w_out_bf, xa, mod3, ln_g, ln_b, wr_cat, rb_pad, n_rows, n_batch, seq, alpha):
    d = xa.shape[1]
    tm = 512
    nt = n_rows // tm
    n_lat_tiles = ys_lat[0].shape[0] // tm

    def cur(i):
        return jnp.minimum(i, nt - 1)

    def prv(i):
        return jnp.maximum(i - 1, 0)

    def mrow(i):
        return jnp.minimum((prv(i) * tm) // seq, n_batch)

    ymix = pl.BlockSpec((tm, MIXER_W), lambda i: (cur(i), 0))
    ylat = pl.BlockSpec((tm, MIXER_W), lambda i: (jnp.minimum(cur(i), n_lat_tiles - 1), 0))
    mixer_specs, mixer_args = [ylat] * 3, list(ys_lat)
    if ys_ctx is not None:
        n_ctx_tiles = nt - n_lat_tiles
        yctx = pl.BlockSpec((tm, MIXER_W), lambda i: (jnp.clip(cur(i) - n_lat_tiles, 0, n_ctx_tiles - 1), 0))
        mixer_specs, mixer_args = mixer_specs + [yctx] * 3, mixer_args + list(ys_ctx)
    rowvec = pl.BlockSpec((1, d), lambda i: (0, 0))
    return pl.pallas_call(
        functools.partial(_merge_kernel, alpha=alpha, n_lat_tiles=n_lat_tiles, n_tiles=nt),
        grid=(nt + 1,),
        in_specs=mixer_specs + [
            ymix,
            pl.BlockSpec((d, d), lambda i: (0, 0)),
            pl.BlockSpec((tm, d), lambda i: (prv(i), 0)),
            pl.BlockSpec((None, 1, d), lambda i: (mrow(i), 0, 2)),
            rowvec, rowvec,
            pl.BlockSpec((None, 1, d), lambda i: (mrow(i), 0, 3)),
            pl.BlockSpec((None, 1, d), lambda i: (mrow(i), 0, 4)),
            pl.BlockSpec((d, 2 * LOGIT_PAD), lambda i: (0, 0)),
            pl.BlockSpec((1, LOGIT_PAD), lambda i: (0, 0)),
        ],
        out_specs=[
            pl.BlockSpec((tm, d), lambda i: (prv(i), 0)),
            pl.BlockSpec((tm, d), lambda i: (prv(i), 0)),
            pl.BlockSpec((tm, LOGIT_PAD), lambda i: (prv(i), 0)),
            pl.BlockSpec((tm, LOGIT_PAD), lambda i: (prv(i), 0)),
        ],
        out_shape=[
            jax.ShapeDtypeStruct((n_rows, d), F32),
            jax.ShapeDtypeStruct((n_rows, d), F32),
            jax.ShapeDtypeStruct((n_rows, LOGIT_PAD), jnp.int32),
            jax.ShapeDtypeStruct((n_rows, LOGIT_PAD), F32),
        ],
        scratch_shapes=[pltpu.VMEM((tm, d), F32), pltpu.VMEM((tm, d), F32)],
        compiler_params=_cparams(("arbitrary",)),
        name="merge",
    )(*mixer_args, y_gm, w_out_bf, xa, mod3, ln_g, ln_b, mod3, mod3, wr_cat, rb_pad)


def _moe_kernel(be_ref, nu_ref, dest_ref, ws_ref, nxt_ref, pad_ref, h_hbm, wg_hbm, wu_hbm, wd_hbm, o_ref,
                xbuf, sem, tok_ref, stage_g, stage_u, stage_d, wsem, wg_s, wu_s, wd_s, *, layer):
    i = pl.program_id(0)
    n_used = nu_ref[0]
    ring = xbuf.shape[0]
    slot = lax.rem(i, ring)

    @pl.when(i == 0)
    def _():
        def clear(s, carry):
            tok_ref[s] = 0
            return carry

        def place(a, carry):
            tok_ref[dest_ref[a]] = lax.shift_right_logical(a, TOP_K.bit_length() - 1)
            return carry

        for e in range(N_EXPERTS):
            lax.fori_loop(pad_ref[e], pad_ref[N_EXPERTS + e], clear, 0)
        lax.fori_loop(0, dest_ref.shape[0], place, 0, unroll=32)

    def row_copy(blk, r, sl):
        tok = tok_ref[blk * MOE_BLOCK + r]
        return pltpu.make_async_copy(h_hbm.at[pl.ds(tok, 1)], xbuf.at[sl, pl.ds(r, 1)], sem.at[sl])

    def gather_start(blk, sl):
        def body(r, carry):
            row_copy(blk, r, sl).start()
            return carry

        lax.fori_loop(0, MOE_BLOCK, body, 0, unroll=8)

    def gather_wait(sl):
        pltpu.make_async_copy(h_hbm.at[pl.ds(0, MOE_BLOCK)], xbuf.at[sl], sem.at[sl]).wait()

    def weight_copies(e, ws):
        return (pltpu.make_async_copy(wg_hbm.at[layer, e], stage_g.at[ws], wsem.at[ws]),
                pltpu.make_async_copy(wu_hbm.at[layer, e], stage_u.at[ws], wsem.at[ws]),
                pltpu.make_async_copy(wd_hbm.at[layer, e], stage_d.at[ws], wsem.at[ws]))

    @pl.when(i == 0)
    def _():
        for cp in weight_copies(be_ref[0], ws_ref[0]):
            cp.start(priority=1)
        gather_start(0, 0)

        @pl.when(n_used > 1)
        def _():
            gather_start(1, 1)

    active = i < n_used
    new_expert = jnp.logical_or(i == 0, be_ref[i] != be_ref[jnp.maximum(i - 1, 0)])

    @pl.when(jnp.logical_and(active, new_expert))
    def _():
        ws = ws_ref[i]
        for cp in weight_copies(be_ref[i], ws):
            cp.wait()
        nxt = nxt_ref[i]

        @pl.when(nxt >= 0)
        def _():
            for cp in weight_copies(nxt, 1 - ws):
                cp.start(priority=1)

        for src, dst in ((stage_g, wg_s), (stage_u, wu_s), (stage_d, wd_s)):
            def cast(r0, src=src, dst=dst):
                dst[pl.ds(r0, 256), :] = src[ws, pl.ds(r0, 256), :].astype(BF16)

            _row_chunks(dst.shape[0], 256, cast)

    def ffn_block(prefetch_next):
        gather_wait(slot)
        x = xbuf[slot].astype(BF16)
        if prefetch_next:
            nxt = lax.rem(i + 2, ring)
            for r in range(MOE_BLOCK):
                row_copy(i + 2, r, nxt).start()
        g = jnp.dot(x, wg_s[...], preferred_element_type=F32)
        u = jnp.dot(x, wu_s[...], preferred_element_type=F32)
        a = (g * jax.nn.sigmoid(g) * u).astype(BF16)
        o_ref[...] = jnp.dot(a, wd_s[...], preferred_element_type=F32).astype(o_ref.dtype)

    has_next = i + 2 < n_used
    pl.when(jnp.logical_and(active, has_next))(functools.partial(ffn_block, True))
    pl.when(jnp.logical_and(active, jnp.logical_not(has_next)))(functools.partial(ffn_block, False))

    @pl.when(jnp.logical_not(active))
    def _():
        o_ref[...] = jnp.zeros_like(o_ref)


def _moe_experts(h, dest, plan, wg, wu, wd, layer):
    d = h.shape[1]
    n_slots = plan["n_slots"]
    de = wg.shape[3]
    n_blocks = n_slots // MOE_BLOCK
    hbm = pl.BlockSpec(memory_space=pl.ANY)
    grid_spec = pltpu.PrefetchScalarGridSpec(
        num_scalar_prefetch=6,
        grid=(n_blocks,),
        in_specs=[hbm, hbm, hbm, hbm],
        out_specs=pl.BlockSpec((MOE_BLOCK, d), lambda i, *_: (i, 0)),
        scratch_shapes=[
            pltpu.VMEM((3, MOE_BLOCK, d), F32),
            pltpu.SemaphoreType.DMA((3,)),
            pltpu.SMEM((n_slots,), jnp.int32),
            pltpu.VMEM((2, d, de), F32), pltpu.VMEM((2, d, de), F32), pltpu.VMEM((2, de, d), F32),
            pltpu.SemaphoreType.DMA((2,)),
            pltpu.VMEM((d, de), BF16), pltpu.VMEM((d, de), BF16), pltpu.VMEM((de, d), BF16),
        ],
    )
    return pl.pallas_call(
        functools.partial(_moe_kernel, layer=layer),
        grid_spec=grid_spec,
        out_shape=jax.ShapeDtypeStruct((n_slots, d), BF16),
        compiler_params=_cparams(("arbitrary",), MOE_VMEM_LIMIT),
        name="moe_experts",
    )(plan["block_e"], plan["n_used"], dest.reshape(-1), plan["wslot"], plan["next_e"], plan["pad_range"],
      h, wg, wu, wd)


def _assignment_ranks(flat_e):
    a = flat_e.shape[0]
    blk = 128
    nb = a // blk
    onehot = (flat_e[:, None] == jnp.arange(N_EXPERTS)[None, :]).astype(BF16).reshape(nb, blk, N_EXPERTS)
    tri = (jnp.arange(blk)[:, None] >= jnp.arange(blk)[None, :]).astype(BF16)
    intra = jnp.einsum("ij,bjk->bik", tri, onehot, preferred_element_type=F32)
    bsum = intra[:, -1, :]
    before = (jnp.arange(nb)[:, None] > jnp.arange(nb)[None, :]).astype(BF16)
    offs = jnp.dot(before, bsum.astype(BF16), preferred_element_type=F32)
    csum = intra + offs[:, None, :]
    rank = jnp.sum(csum * onehot.astype(F32), axis=-1).reshape(a) - 1.0
    counts = offs[-1] + bsum[-1]
    return rank.astype(jnp.int32), counts.astype(jnp.int32)


def _dispatch_plan(idx):
    t = idx.shape[0]
    a = t * TOP_K
    flat_e = idx.reshape(a)
    rank, counts = _assignment_ranks(flat_e)
    padded = (counts + MOE_BLOCK - 1) // MOE_BLOCK * MOE_BLOCK
    pad_end = jnp.cumsum(padded)
    pad_start = pad_end - padded
    dest = pad_start[flat_e] + rank
    n_blocks = (a + N_EXPERTS * (MOE_BLOCK - 1) + MOE_BLOCK - 1) // MOE_BLOCK
    n_slots = n_blocks * MOE_BLOCK
    blk_start = jnp.arange(n_blocks, dtype=jnp.int32) * MOE_BLOCK
    block_e = jnp.minimum(
        jnp.sum((pad_end[None, :] <= blk_start[:, None]).astype(jnp.int32), axis=1), N_EXPERTS - 1)
    n_used = (pad_end[-1] // MOE_BLOCK).astype(jnp.int32).reshape(1)
    nonempty = counts > 0
    eid = jnp.arange(N_EXPERTS, dtype=jnp.int32)
    run_of_e = jnp.cumsum(nonempty.astype(jnp.int32)) - 1
    later = jnp.logical_and(eid[None, :] > eid[:, None], nonempty[None, :])
    nxt_of_e = jnp.min(jnp.where(later, eid[None, :], N_EXPERTS), axis=1)
    nxt_of_e = jnp.where(nxt_of_e == N_EXPERTS, -1, nxt_of_e)
    plan = {
        "n_slots": n_slots, "block_e": block_e.astype(jnp.int32), "n_used": n_used,
        "wslot": (run_of_e[block_e] % 2).astype(jnp.int32), "next_e": nxt_of_e[block_e].astype(jnp.int32),
        "pad_range": jnp.concatenate([pad_start + counts, pad_end]).astype(jnp.int32),
    }
    return plan, dest.reshape(t, TOP_K)


def _final_kernel(x_ref, y2_ref, gate_ref, g2_ref, lg_ref, lb_ref, o_ref, *, alpha):
    d = x_ref.shape[1]
    f = y2_ref[:, :d].astype(F32) * gate_ref[:, 0:1] + y2_ref[:, d:].astype(F32) * gate_ref[:, 1:2]
    o_ref[...] = _ln(alpha * x_ref[...] + g2_ref[...] * f) * lg_ref[...] + lb_ref[...]


def _final_norm(x1, y2, gate, mod3, ln_g, ln_b, n_batch, seq, alpha):
    n_rows, d = x1.shape
    tm = 512

    def mrow(i):
        return jnp.minimum((i * tm) // seq, n_batch)

    tile = pl.BlockSpec((tm, d), lambda i: (i, 0))
    rowvec = pl.BlockSpec((1, d), lambda i: (0, 0))
    return pl.pallas_call(
        functools.partial(_final_kernel, alpha=alpha),
        grid=(n_rows // tm,),
        in_specs=[tile, pl.BlockSpec((tm, TOP_K * d), lambda i: (i, 0)),
                  pl.BlockSpec((tm, LOGIT_PAD), lambda i: (i, 0)),
                  pl.BlockSpec((None, 1, d), lambda i: (mrow(i), 0, 5)), rowvec, rowvec],
        out_specs=tile,
        out_shape=jax.ShapeDtypeStruct((n_rows, d), F32),
        compiler_params=_cparams(("arbitrary",)),
        name="final_norm",
    )(x1, y2, gate, mod3, ln_g, ln_b)


def _permute_in_cols(w):
    pool, q, k, v, hy, gm = jnp.split(w, (512, 1024, 1152, 1280, 2816), axis=-1)
    return jnp.concatenate([hy, pool, q, gm, k, v], axis=-1)


def kernel(x, c, ctx, c_ctx, w_ada, b_ada, w_in, w_out, mix_norm_g, pool_w, pool_scale, attn_sink,
           hy_conv_w, hy_conv_b, hy_w1, hy_b1, hy_f1, hy_w2, hy_b2, hy_f2, hy_w3, hy_skip,
           gm_ln_g, gm_ln_b, gm_ws, gm_bs, ln1_g, ln1_b, ln2_g, ln2_b, w_router, router_bias,
           w_gate, w_up, w_down):
    n_batch, seq, d = x.shape
    ctx_len = ctx.shape[1]
    depth = w_in.shape[0]
    grid_w = 64
    alpha = (2 * depth) ** 0.25
    n_lat = n_batch * seq
    n_ctx = n_batch * ctx_len

    xa = jnp.concatenate([x.reshape(n_lat, d), ctx.reshape(n_ctx, d)], axis=0)
    cvec = jnp.concatenate([c, c_ctx[None, :], jnp.zeros((8 - n_batch - 1, d), F32)], axis=0)
    mods = _ada_mods(cvec, w_ada, b_ada)
    cos_t, sin_t = _rope_tables(seq, grid_w)
    tabs_lat = _hy_tables(seq, *_hy_tiles(seq))
    tabs_ctx = _hy_tables(ctx_len, *_hy_tiles(ctx_len))
    wr_pad = jnp.pad(w_router, ((0, 0), (0, LOGIT_PAD - N_EXPERTS)))
    wr_hi = wr_pad.astype(BF16)
    wr_lo = (wr_pad - wr_hi.astype(F32)).astype(BF16)
    wr_cat = jnp.concatenate([wr_hi, wr_lo], axis=1)
    rb_pad = jnp.pad(router_bias.astype(F32), (0, LOGIT_PAD - N_EXPERTS)).reshape(1, LOGIT_PAD)
    row = lambda v: v.reshape(1, -1)

    for layer in range(depth):
        last = layer == depth - 1
        mod3 = mods[layer].reshape(8, 1, 6 * d)
        gains = mix_norm_g[layer].reshape(4, 1, MIXER_W)
        lp = {
            "hy_conv_w": hy_conv_w[layer], "hy_conv_b": row(hy_conv_b[layer]),
            "hy_w1p": jnp.pad(hy_w1[layer], ((0, HY_EMB_PAD - HY_EMB), (0, 0))),
            "hy_b1": row(hy_b1[layer]), "hy_f1": row(hy_f1[layer]),
            "hy_w2": hy_w2[layer], "hy_b2": row(hy_b2[layer]), "hy_f2": row(hy_f2[layer]),
            "hy_w3": hy_w3[layer], "hy_skip": hy_skip[layer],
        }
        z = _in_proj(xa, mod3, _permute_in_cols(w_in[layer]).astype(BF16), n_batch, seq)

        pool_bf = pool_w[layer].astype(BF16)
        ps = row(pool_scale[layer])
        ws_cat = jnp.transpose(gm_ws[layer], (1, 0, 2)).reshape(GMLP_CHUNK, GMLP_GROUPS * GMLP_CHUNK)
        bs_mat = jnp.repeat(gm_bs[layer].T, MIXER_W // GMLP_GROUPS, axis=1)
        n_rows = n_lat if last else n_lat + n_ctx

        y_pool = _pool_mix(z, pool_bf, ps, gains[0], n_batch, seq, 0)
        y_attn = _attn_lat(z, attn_sink[layer], cos_t, sin_t, gains[1], n_batch, seq, ctx_len)
        y_hy = _hyena_mix(z, lp, gains[2], n_batch, seq, 0, tabs_lat)
        y_gm = _gmlp_mix(z, row(gm_ln_g[layer]), row(gm_ln_b[layer]), ws_cat.astype(BF16), bs_mat,
                         gains[3], n_rows)
        ys_ctx = None
        if not last:
            ys_ctx = (_pool_mix(z, pool_bf, ps, gains[0], n_batch, ctx_len, n_lat // ctx_len),
                      _attn_ctx(z, attn_sink[layer], gains[1], n_batch, seq, ctx_len),
                      _hyena_mix(z, lp, gains[2], n_batch, ctx_len, n_lat, tabs_ctx))

        x1, h2, idx, gate = _merge((y_pool, y_attn, y_hy), ys_ctx, y_gm, w_out[layer].astype(BF16), xa, mod3,
                                   row(ln1_g[layer]), row(ln1_b[layer]), wr_cat, rb_pad,
                                   n_rows, n_batch, seq, alpha)

        plan, dest = _dispatch_plan(idx[:, :TOP_K])
        ys = _moe_experts(h2, dest, plan, w_gate, w_up, w_down, layer)
        y2 = ys[dest.reshape(-1)].reshape(dest.shape[0], TOP_K * d)
        xa = _final_norm(x1, y2, gate, mod3, row(ln2_g[layer]), row(ln2_b[layer]),
                         n_batch, seq, alpha)
    return xa.reshape(n_batch, seq, d)
```

```python
import functools
import math

import jax
import jax.numpy as jnp
from jax import lax
from jax.experimental import pallas as pl
from jax.experimental.pallas import tpu as pltpu

F32 = jnp.float32
BF16 = jnp.bfloat16
HI = lax.Precision.HIGHEST

LN_EPS = 1e-6
MIXER_W = 512
HEAD_DIM = 64
N_HEADS = 8
N_KV_HEADS = 2
KV_GROUP = 4
KV_W = 128
WINDOW = 128
ROPE_THETA = 10000.0
POOL_WINDOWS = (2, 4, 8, 16)
POOL_GROUP = 128
POOL_HALO = 8
GMLP_CHUNK = 128
GMLP_GROUPS = 8
HY_EMB = 33
HY_EMB_PAD = 128
HY_TARGET = 1e-2
HY_FAST = 0.3
HY_SLOW = 1.5
N_EXPERTS = 32
N_EXPERT_GROUPS = 8
TOP_K = 2
D_EXPERT = 768
MOE_BLOCK = 256
LOGIT_PAD = 128

COL_HY = 0
COL_POOL = 3
COL_Q = 4
COL_GU = 5
COL_GV = 6
COL_K = 28
COL_V = 29
IN_W = 3840

VMEM_LIMIT = 56 * 1024 * 1024
MOE_VMEM_LIMIT = 59 * 1024 * 1024


def _cparams(sem, vmem_limit=VMEM_LIMIT):
    return pltpu.CompilerParams(dimension_semantics=sem, vmem_limit_bytes=vmem_limit)


def _ln(x):
    mu = jnp.mean(x, axis=-1, keepdims=True)
    xc = x - mu
    var = jnp.mean(xc * xc, axis=-1, keepdims=True)
    return xc * lax.rsqrt(var + LN_EPS)


def _rms_gain(y, g):
    return y * lax.rsqrt(jnp.mean(y * y, axis=-1, keepdims=True) + LN_EPS) * g


def _row_chunks(n_rows, chunk, body):
    n = n_rows // chunk
    if n == 1:
        body(0)
        return

    def step(i, carry):
        body(pl.multiple_of(i * chunk, chunk))
        return carry

    lax.fori_loop(0, n, step, 0)


def _ada_kernel(c_ref, w_ref, b_ref, o_ref):
    c = c_ref[...]
    s = c * jax.nn.sigmoid(c)
    o_ref[...] = jnp.dot(s, w_ref[...], precision=HI, preferred_element_type=F32) + b_ref[...]


def _ada_mods(cvec, w_ada, b_ada):
    depth, d, n = w_ada.shape
    tn = 1024
    return pl.pallas_call(
        _ada_kernel,
        grid=(depth, n // tn),
        in_specs=[
            pl.BlockSpec((8, d), lambda l, j: (0, 0)),
            pl.BlockSpec((None, d, tn), lambda l, j: (l, 0, j)),
            pl.BlockSpec((None, 1, tn), lambda l, j: (l, 0, j)),
        ],
        out_specs=pl.BlockSpec((None, 8, tn), lambda l, j: (l, 0, j)),
        out_shape=jax.ShapeDtypeStruct((depth, 8, n), F32),
        compiler_params=_cparams(("arbitrary", "arbitrary")),
        name="ada_mods",
    )(cvec, w_ada, b_ada.reshape(depth, 1, n))


def _inproj_kernel(x_ref, sh_ref, sc_ref, w_ref, o_ref, h_scr):
    @pl.when(pl.program_id(1) == 0)
    def _():
        sh = sh_ref[...]
        sc1 = 1.0 + sc_ref[...]

        def body(r0):
            x = x_ref[pl.ds(r0, 256), :]
            h_scr[pl.ds(r0, 256), :] = (_ln(x) * sc1 + sh).astype(BF16)

        _row_chunks(x_ref.shape[0], 256, body)

    o_ref[...] = jnp.dot(h_scr[...], w_ref[...], preferred_element_type=F32)


def _in_proj(xa, mod3, w_bf, n_batch, seq):
    t, d = xa.shape
    n = w_bf.shape[1]
    tn = 1280
    tm = next(m for m in (1024, 512, 256) if t % m == 0 and seq % m == 0)

    def mrow(i):
        return jnp.minimum((i * tm) // seq, n_batch)

    return pl.pallas_call(
        _inproj_kernel,
        grid=(t // tm, n // tn),
        in_specs=[
            pl.BlockSpec((tm, d), lambda i, j: (i, 0)),
            pl.BlockSpec((None, 1, d), lambda i, j: (mrow(i), 0, 0)),
            pl.BlockSpec((None, 1, d), lambda i, j: (mrow(i), 0, 1)),
            pl.BlockSpec((d, tn), lambda i, j: (0, j)),
        ],
        out_specs=pl.BlockSpec((tm, tn), lambda i, j: (i, j)),
        out_shape=jax.ShapeDtypeStruct((t, n), F32),
        scratch_shapes=[pltpu.VMEM((tm, d), BF16)],
        compiler_params=_cparams(("arbitrary", "arbitrary")),
        name="in_proj",
    )(xa, mod3, mod3, w_bf)


def _pool_kernel(z_ref, w_ref, ps_ref, g_ref, o_ref, buf):
    seq = z_ref.shape[0]
    rc = min(256, seq)
    zeros = jnp.zeros((POOL_HALO, MIXER_W), F32)
    buf[0:POOL_HALO, :] = zeros
    buf[seq + POOL_HALO:seq + 2 * POOL_HALO, :] = zeros

    def fill(r0):
        buf[pl.ds(r0 + POOL_HALO, rc), :] = z_ref[pl.ds(r0, rc), :]

    _row_chunks(seq, rc, fill)

    def body(r0):
        win = buf[pl.ds(r0, rc + 2 * POOL_HALO), :]
        t = r0 + lax.broadcasted_iota(jnp.int32, (rc, 1), 0)
        outs = []
        for gi, w in enumerate(POOL_WINDOWS):
            lanes = slice(gi * POOL_GROUP, (gi + 1) * POOL_GROUP)
            acc = None
            for j in range(-w // 2, w // 2):
                piece = win[POOL_HALO + j:POOL_HALO + j + rc, lanes]
                acc = piece if acc is None else acc + piece
            cnt = (jnp.minimum(t + w // 2, seq) - jnp.maximum(t - w // 2, 0)).astype(F32)
            dlt = acc / cnt - win[POOL_HALO:POOL_HALO + rc, lanes]
            outs.append(jnp.dot(dlt.astype(BF16), w_ref[gi], preferred_element_type=F32))
        y = jnp.concatenate(outs, axis=-1) * ps_ref[...]
        o_ref[pl.ds(r0, rc), :] = _rms_gain(y, g_ref[...]).astype(BF16)

    _row_chunks(seq, rc, body)


def _pool_mix(z, pool_w_bf, pool_scale, gain, n_seq, seq, row_blk0):
    return pl.pallas_call(
        _pool_kernel,
        grid=(n_seq,),
        in_specs=[
            pl.BlockSpec((seq, MIXER_W), lambda s: (row_blk0 + s, COL_POOL)),
            pl.BlockSpec((4, POOL_GROUP, POOL_GROUP), lambda s: (0, 0, 0)),
            pl.BlockSpec((1, MIXER_W), lambda s: (0, 0)),
            pl.BlockSpec((1, MIXER_W), lambda s: (0, 0)),
        ],
        out_specs=pl.BlockSpec((seq, MIXER_W), lambda s: (s, 0)),
        out_shape=jax.ShapeDtypeStruct((n_seq * seq, MIXER_W), BF16),
        scratch_shapes=[pltpu.VMEM((seq + 2 * POOL_HALO, MIXER_W), F32)],
        compiler_params=_cparams(("arbitrary",)),
        name="pool_mix",
    )(z, pool_w_bf, pool_scale, gain)


def _rope(x, cos, sin_signed):
    lane = lax.broadcasted_iota(jnp.int32, x.shape, 1)
    partner = jnp.where((lane & 31) < 16, pltpu.roll(x, 112, axis=1), pltpu.roll(x, 16, axis=1))
    return x * cos + partner * sin_signed


def _attend(qs, sink_col, parts):
    dn = (((1,), (1,)), ((), ()))
    scores = []
    for kk, _, mask in parts:
        s = lax.dot_general(qs, kk, dn, preferred_element_type=F32)
        if mask is not None:
            s = s + mask
        scores.append(s)
    m = sink_col
    for s in scores:
        m = jnp.maximum(m, jnp.max(s, axis=-1, keepdims=True))
    den = jnp.exp(sink_col - m)
    out = None
    for s, (_, vv, _) in zip(scores, parts):
        p = jnp.exp(s - m)
        den = den + jnp.sum(p, axis=-1, keepdims=True)
        o = jnp.dot(p.astype(BF16), vv, preferred_element_type=F32)
        out = o if out is None else out + o
    return out / den


def _heads_attend(q_all, sink_ref, tq, make_parts):
    cols = [None] * N_HEADS
    for kh in range(N_KV_HEADS):
        heads = [kh * KV_GROUP + g for g in range(KV_GROUP)]
        qs = jnp.concatenate([q_all[:, h * HEAD_DIM:(h + 1) * HEAD_DIM] for h in heads], axis=0)
        sink_col = jnp.concatenate([jnp.full((tq, 1), sink_ref[h], F32) for h in heads], axis=0)
        o = _attend(qs.astype(BF16), sink_col, make_parts(kh))
        for g, h in enumerate(heads):
            cols[h] = o[g * tq:(g + 1) * tq, :]
    return jnp.concatenate(cols, axis=-1)


def _attn_lat_kernel(sink_ref, q_ref, k_ref, v_ref, kc_ref, vc_ref, cos_ref, sin_ref, g_ref, o_ref):
    tq = q_ref.shape[0]
    seq = k_ref.shape[0]
    nband = tq + 2 * WINDOW
    i = pl.program_id(1)
    q0 = pl.multiple_of(i * tq, tq)
    k0 = pl.multiple_of(jnp.clip(q0 - WINDOW, 0, seq - nband), WINDOW)

    cq = cos_ref[pl.ds(q0, tq), :]
    sq = sin_ref[pl.ds(q0, tq), :]
    q = q_ref[...]
    q_all = jnp.concatenate(
        [_rope(q[:, c * 128:(c + 1) * 128], cq, sq) for c in range(MIXER_W // 128)], axis=-1)
    q_all = q_all * (HEAD_DIM ** -0.5)

    kb = _rope(k_ref[pl.ds(k0, nband), :], cos_ref[pl.ds(k0, nband), :],
               sin_ref[pl.ds(k0, nband), :]).astype(BF16)
    vb = v_ref[pl.ds(k0, nband), :].astype(BF16)
    kc = kc_ref[...].astype(BF16)
    vc = vc_ref[...].astype(BF16)

    qpos = q0 + lax.broadcasted_iota(jnp.int32, (tq, nband), 0)
    kpos = k0 + lax.broadcasted_iota(jnp.int32, (tq, nband), 1)
    band = jnp.where(jnp.abs(qpos - kpos) <= WINDOW, 0.0, -1e30)
    band = jnp.concatenate([band, jnp.zeros((tq, kc.shape[0]), F32)], axis=1)
    mask = jnp.concatenate([band] * KV_GROUP, axis=0)
    k_all = jnp.concatenate([kb, kc], axis=0)
    v_all = jnp.concatenate([vb, vc], axis=0)

    def make_parts(kh):
        hs = slice(kh * HEAD_DIM, (kh + 1) * HEAD_DIM)
        return [(k_all[:, hs], v_all[:, hs], mask)]

    y = _heads_attend(q_all, sink_ref, tq, make_parts)
    o_ref[...] = _rms_gain(y, g_ref[...]).astype(BF16)


def _attn_ctx_kernel(sink_ref, q_ref, kc_ref, vc_ref, g_ref, o_ref):
    tq = q_ref.shape[0]
    q_all = q_ref[...] * (HEAD_DIM ** -0.5)
    kc = kc_ref[...].astype(BF16)
    vc = vc_ref[...].astype(BF16)

    def make_parts(kh):
        hs = slice(kh * HEAD_DIM, (kh + 1) * HEAD_DIM)
        return [(kc[:, hs], vc[:, hs], None)]

    y = _heads_attend(q_all, sink_ref, tq, make_parts)
    o_ref[...] = _rms_gain(y, g_ref[...]).astype(BF16)


def _attn_lat(z, sink, cos_t, sin_t, gain, n_batch, seq, ctx_len):
    tq = 256
    nq = seq // tq
    ctx_blk0 = (n_batch * seq) // ctx_len
    smem = pl.BlockSpec(memory_space=pltpu.SMEM)
    return pl.pallas_call(
        _attn_lat_kernel,
        grid=(n_batch, nq),
        in_specs=[
            smem,
            pl.BlockSpec((tq, MIXER_W), lambda b, i: (b * nq + i, COL_Q)),
            pl.BlockSpec((seq, KV_W), lambda b, i: (b, COL_K)),
            pl.BlockSpec((seq, KV_W), lambda b, i: (b, COL_V)),
            pl.BlockSpec((ctx_len, KV_W), lambda b, i: (ctx_blk0 + b, COL_K)),
            pl.BlockSpec((ctx_len, KV_W), lambda b, i: (ctx_blk0 + b, COL_V)),
            pl.BlockSpec((seq, 128), lambda b, i: (0, 0)),
            pl.BlockSpec((seq, 128), lambda b, i: (0, 0)),
            pl.BlockSpec((1, MIXER_W), lambda b, i: (0, 0)),
        ],
        out_specs=pl.BlockSpec((tq, MIXER_W), lambda b, i: (b * nq + i, 0)),
        out_shape=jax.ShapeDtypeStruct((n_batch * seq, MIXER_W), BF16),
        compiler_params=_cparams(("arbitrary", "arbitrary")),
        name="attn_lat",
    )(sink, z, z, z, z, z, cos_t, sin_t, gain)


def _attn_ctx(z, sink, gain, n_batch, seq, ctx_len):
    ctx_blk0 = (n_batch * seq) // ctx_len
    smem = pl.BlockSpec(memory_space=pltpu.SMEM)
    return pl.pallas_call(
        _attn_ctx_kernel,
        grid=(n_batch,),
        in_specs=[
            smem,
            pl.BlockSpec((ctx_len, MIXER_W), lambda b: (ctx_blk0 + b, COL_Q)),
            pl.BlockSpec((ctx_len, KV_W), lambda b: (ctx_blk0 + b, COL_K)),
            pl.BlockSpec((ctx_len, KV_W), lambda b: (ctx_blk0 + b, COL_V)),
            pl.BlockSpec((1, MIXER_W), lambda b: (0, 0)),
        ],
        out_specs=pl.BlockSpec((ctx_len, MIXER_W), lambda b: (b, 0)),
        out_shape=jax.ShapeDtypeStruct((n_batch * ctx_len, MIXER_W), BF16),
        compiler_params=_cparams(("arbitrary",)),
        name="attn_ctx",
    )(sink, z, z, z, gain)


def _rope_tables(seq, grid_w):
    nf = HEAD_DIM // 4
    inv = ROPE_THETA ** (-jnp.arange(nf, dtype=F32) / nf)
    t = jnp.arange(seq)
    row = (t // grid_w).astype(F32)[:, None] * inv[None, :]
    col = (t % grid_w).astype(F32)[:, None] * inv[None, :]
    cos_h = jnp.concatenate([jnp.cos(row), jnp.cos(row), jnp.cos(col), jnp.cos(col)], axis=-1)
    sin_h = jnp.concatenate([-jnp.sin(row), jnp.sin(row), -jnp.sin(col), jnp.sin(col)], axis=-1)
    return jnp.tile(cos_h, (1, 2)), jnp.tile(sin_h, (1, 2))


def _hy_conv_kernel(*refs):
    z_refs, zp_refs, zn_refs = refs[0:3], refs[3:6], refs[6:9]
    w_ref, b_ref, g_ref, u_ref = refs[9:]
    i = pl.program_id(1)
    th = z_refs[0].shape[0] // 2
    has_prev = (i > 0).astype(F32)
    has_next = (i < pl.num_programs(1) - 1).astype(F32)
    row = lax.broadcasted_iota(jnp.int32, (th, 128), 0)
    for part in range(3):
        ze = z_refs[part][pl.ds(0, th, stride=2), :]
        zo = z_refs[part][pl.ds(1, th, stride=2), :]
        prev_row = zp_refs[part][POOL_HALO - 1:POOL_HALO, :] * has_prev
        next_row = zn_refs[part][0:1, :] * has_next
        zo_m = jnp.where(row == 0, prev_row, pltpu.roll(zo, 1, axis=0))
        ze_p = jnp.where(row == th - 1, next_row, pltpu.roll(ze, th - 1, axis=0))
        w0, w1, w2, b = w_ref[part, 0:1, :], w_ref[part, 1:2, :], w_ref[part, 2:3, :], b_ref[part]
        ye = b + zo_m * w0 + ze * w1 + zo * w2
        yo = b + ze * w0 + zo * w1 + ze_p * w2
        g_ref[part, 0] = ye
        g_ref[part, 1] = yo
        if part == 2:
            u_ref[0] = ye.astype(BF16)
            u_ref[1] = yo.astype(BF16)


def _hy_conv(z, conv_w, conv_b, n_seq, seq, row0):
    tl = min(2048, seq)
    nt = seq // tl
    blk0 = row0 // tl
    hb0 = row0 // POOL_HALO
    hpt = tl // POOL_HALO
    last_halo = (row0 + n_seq * seq) // POOL_HALO - 1
    cpp = MIXER_W // 128
    ncol = n_seq * MIXER_W

    def main_spec(part):
        return pl.BlockSpec((tl, 128), lambda s, i, c: (blk0 + s * nt + i, part * cpp + c))

    def prev_spec(part):
        return pl.BlockSpec((POOL_HALO, 128),
                            lambda s, i, c: (jnp.maximum(hb0 + (s * nt + i) * hpt - 1, 0), part * cpp + c))

    def next_spec(part):
        return pl.BlockSpec((POOL_HALO, 128),
                            lambda s, i, c: (jnp.minimum(hb0 + (s * nt + i + 1) * hpt, last_halo), part * cpp + c))

    w3 = conv_w.reshape(3, 3, MIXER_W).transpose(1, 0, 2)
    b3 = conv_b.reshape(3, 1, MIXER_W)
    return pl.pallas_call(
        _hy_conv_kernel,
        grid=(n_seq, nt, cpp),
        in_specs=[main_spec(p) for p in range(3)] + [prev_spec(p) for p in range(3)]
        + [next_spec(p) for p in range(3)]
        + [pl.BlockSpec((3, 3, 128), lambda s, i, c: (0, 0, c)),
           pl.BlockSpec((3, 1, 128), lambda s, i, c: (0, 0, c))],
        out_specs=[
            pl.BlockSpec((3, 2, tl // 2, 128), lambda s, i, c: (0, 0, i, s * cpp + c)),
            pl.BlockSpec((2, tl // 2, 128), lambda s, i, c: (0, i, s * cpp + c)),
        ],
        out_shape=[jax.ShapeDtypeStruct((3, 2, seq // 2, ncol), F32),
                   jax.ShapeDtypeStruct((2, seq // 2, ncol), BF16)],
        compiler_params=_cparams(("arbitrary", "arbitrary", "arbitrary")),
        name="hy_conv",
    )(*([z] * 9), w3, b3)


def _hy_filter_kernel(feat_ref, dec_ref, w1_ref, b1_ref, f1_ref, w2_ref, b2_ref, f2_ref, w3_ref, o_ref):
    tl = feat_ref.shape[0]
    h = jnp.dot(feat_ref[...], w1_ref[...], precision=HI, preferred_element_type=F32) + b1_ref[...]
    h = jnp.sin(f1_ref[...] * h)
    h = jnp.dot(h, w2_ref[...], precision=HI, preferred_element_type=F32) + b2_ref[...]
    h = jnp.sin(f2_ref[...] * h)
    filt = jnp.dot(h, w3_ref[...], precision=HI, preferred_element_type=F32)
    dec = dec_ref[...]
    t = 2 * (pl.program_id(1) * tl + lax.broadcasted_iota(jnp.int32, (tl, 1), 0)) + pl.program_id(0)
    not_first = (t > 0).astype(F32)
    for o in range(2):
        hf = filt[:, (2 * o) * MIXER_W:(2 * o + 1) * MIXER_W] * dec
        hb = filt[:, (2 * o + 1) * MIXER_W:(2 * o + 2) * MIXER_W] * dec * not_first
        o_ref[:, o * MIXER_W:(o + 1) * MIXER_W] = (hf + hb).astype(BF16)
        o_ref[:, (2 + o) * MIXER_W:(3 + o) * MIXER_W] = (hb - hf).astype(BF16)


def _hy_filters(feat, decay, w1p, b1, f1, w2, b2, f2, w3):
    half = feat.shape[1]
    tl = min(512, half)
    hid = w2.shape[0]
    full = lambda shape: pl.BlockSpec(shape, lambda p, i: tuple(0 for _ in shape))
    return pl.pallas_call(
        _hy_filter_kernel,
        grid=(2, half // tl),
        in_specs=[
            pl.BlockSpec((None, tl, HY_EMB_PAD), lambda p, i: (p, i, 0)),
            pl.BlockSpec((None, tl, MIXER_W), lambda p, i: (p, i, 0)),
            full((HY_EMB_PAD, hid)), full((1, hid)), full((1, hid)),
            full((hid, hid)), full((1, hid)), full((1, hid)),
            full((hid, 4 * MIXER_W)),
        ],
        out_specs=pl.BlockSpec((None, tl, 4 * MIXER_W), lambda p, i: (p, i, 0)),
        out_shape=jax.ShapeDtypeStruct((2, half, 4 * MIXER_W), BF16),
        compiler_params=_cparams(("arbitrary", "arbitrary")),
        name="hy_filters",
    )(feat, decay, w1p, b1, f1, w2, b2, f2, w3)


def _hy_kspec_kernel(te_ref, to_ref, ab_ref, o_ref, *, inv_len):
    hb = te_ref.shape[0] // 2
    half = ab_ref.shape[2] // 2
    ec = jnp.dot(te_ref[0:hb, :], ab_ref[0, :, 0:half], preferred_element_type=F32)
    oc = jnp.dot(to_ref[0:hb, :], ab_ref[1, :, 0:half], preferred_element_type=F32)
    es = jnp.dot(te_ref[hb:, :], ab_ref[0, :, half:], preferred_element_type=F32)
    os_ = jnp.dot(to_ref[hb:, :], ab_ref[1, :, half:], preferred_element_type=F32)
    o_ref[0, 0:hb, :] = (ec + oc) * inv_len
    o_ref[0, hb:, :] = (es + os_) * inv_len
    o_ref[1, 0:hb, :] = (ec - oc) * inv_len
    o_ref[1, hb:, :] = (os_ - es) * inv_len


def _hy_kspec(fwd, ab, tm):
    _, seq, half = fwd.shape
    return pl.pallas_call(
        functools.partial(_hy_kspec_kernel, inv_len=1.0 / seq),
        grid=(seq // tm,),
        in_specs=[
            pl.BlockSpec((None, tm, half), lambda i: (0, i, 0)),
            pl.BlockSpec((None, tm, half), lambda i: (1, i, 0)),
            pl.BlockSpec((2, half, 4 * MIXER_W), lambda i: (0, 0, 0)),
        ],
        out_specs=pl.BlockSpec((2, tm, 2 * MIXER_W), lambda i: (0, i, 0)),
        out_shape=jax.ShapeDtypeStruct((2, seq, 2 * MIXER_W), F32),
        compiler_params=_cparams(("arbitrary",)),
        name="hy_kspec",
    )(fwd, fwd, ab)


def _hy_fwd_kernel(te_ref, to_ref, u_ref, k_ref, o_ref):
    hb = te_ref.shape[0] // 2
    reps = u_ref.shape[2] // MIXER_W
    ep = jnp.dot(te_ref[...], u_ref[0], preferred_element_type=F32)
    op = jnp.dot(to_ref[...], u_ref[1], preferred_element_type=F32)
    ec, es, oc, os_ = ep[0:hb, :], ep[hb:, :], op[0:hb, :], op[hb:, :]
    tile = lambda v: jnp.concatenate([v] * reps, axis=-1)
    p, q, pm, qm = ec + oc, es + os_, ec - oc, os_ - es
    kr, ki = tile(k_ref[0, 0:hb, :]), tile(k_ref[0, hb:, :])
    krm, kim = tile(k_ref[1, 0:hb, :]), tile(k_ref[1, hb:, :])
    yr, yn = p * kr + q * ki, q * kr - p * ki
    yrm, ynm = pm * krm + qm * kim, qm * krm - pm * kim
    o_ref[0, 0:hb, :] = (yr + yrm).astype(BF16)
    o_ref[0, hb:, :] = (yn - ynm).astype(BF16)
    o_ref[1, 0:hb, :] = (yr - yrm).astype(BF16)
    o_ref[1, hb:, :] = (yn + ynm).astype(BF16)


def _hy_fwd(fwd, u, kspec, order, tm, tn):
    _, seq, half = fwd.shape
    ncol = u.shape[2]
    return pl.pallas_call(
        _hy_fwd_kernel,
        grid=(ncol // tn, seq // tm),
        in_specs=[
            pl.BlockSpec((None, tm, half), lambda j, i: (0, i, 0)),
            pl.BlockSpec((None, tm, half), lambda j, i: (1, i, 0)),
            pl.BlockSpec((2, half, tn), lambda j, i: (0, 0, j)),
            pl.BlockSpec((2, tm, MIXER_W), lambda j, i: (0, i, order)),
        ],
        out_specs=pl.BlockSpec((2, tm, tn), lambda j, i: (0, i, j)),
        out_shape=jax.ShapeDtypeStruct((2, seq, ncol), BF16),
        compiler_params=_cparams(("arbitrary", "arbitrary")),
        name="hy_fwd",
    )(fwd, fwd, u, kspec)


def _hy_inv_kernel(ie_ref, io_ref, y_ref, u_ref, gate_ref, skip_ref, *rest, final):
    reps = u_ref.shape[2] // MIXER_W
    skip = jnp.concatenate([skip_ref[...]] * reps, axis=-1)
    for par, inv_ref in enumerate((ie_ref, io_ref)):
        conv = jnp.dot(inv_ref[...], y_ref[par], preferred_element_type=F32)
        out = gate_ref[par] * (conv + u_ref[par].astype(F32) * skip)
        if final:
            g_ref, o_ref = rest
            for r in range(reps):
                o_ref[r, par] = _rms_gain(out[:, r * MIXER_W:(r + 1) * MIXER_W], g_ref[...]).astype(BF16)
        else:
            (o_ref,) = rest
            o_ref[par] = out.astype(BF16)


def _hy_inv(inv, yspec, u, conv, part, skip, gain, final, tm, tn):
    _, half, seq = inv.shape
    ncol = u.shape[2]
    reps = tn // MIXER_W
    in_specs = [
        pl.BlockSpec((None, tm, seq), lambda j, i: (0, i, 0)),
        pl.BlockSpec((None, tm, seq), lambda j, i: (1, i, 0)),
        pl.BlockSpec((2, seq, tn), lambda j, i: (0, 0, j)),
        pl.BlockSpec((2, tm, tn), lambda j, i: (0, i, j)),
        pl.BlockSpec((None, 2, tm, tn), lambda j, i: (part, 0, i, j)),
        pl.BlockSpec((1, MIXER_W), lambda j, i: (0, 0)),
    ]
    args = [inv, inv, yspec, u, conv, skip]
    if final:
        in_specs.append(pl.BlockSpec((1, MIXER_W), lambda j, i: (0, 0)))
        args.append(gain)
        out_spec = pl.BlockSpec((reps, 2, tm, MIXER_W), lambda j, i: (j, 0, i, 0))
        out_shape = jax.ShapeDtypeStruct((ncol // MIXER_W, 2, half, MIXER_W), BF16)
    else:
        out_spec = pl.BlockSpec((2, tm, tn), lambda j, i: (0, i, j))
        out_shape = jax.ShapeDtypeStruct((2, half, ncol), BF16)
    return pl.pallas_call(
        functools.partial(_hy_inv_kernel, final=final),
        grid=(ncol // tn, half // tm),
        in_specs=in_specs,
        out_specs=out_spec,
        out_shape=out_shape,
        compiler_params=_cparams(("arbitrary", "arbitrary")),
        name="hy_inv",
    )(*args)


def _hy_table_kernel(hc_ref, hs_ref, lc_ref, ls_ref, o_ref, *, interleave):
    n_hi, split = hc_ref.shape[0], lc_ref.shape[0]
    lc, ls = lc_ref[...], ls_ref[...]
    for g in range(n_hi):
        hc, hs = hc_ref[g:g + 1, :], hs_ref[g:g + 1, :]
        cos_t = (hc * lc - hs * ls).astype(BF16)
        sin_t = (hs * lc + hc * ls).astype(BF16)
        rows = slice(g * split, (g + 1) * split)
        if interleave == 0:
            o_ref[rows, :] = cos_t
            o_ref[n_hi * split + g * split:n_hi * split + (g + 1) * split, :] = sin_t
        else:
            w = interleave
            for c in range(cos_t.shape[1] // w):
                o_ref[rows, 2 * c * w:(2 * c + 1) * w] = cos_t[:, c * w:(c + 1) * w]
                o_ref[rows, (2 * c + 1) * w:(2 * c + 2) * w] = sin_t[:, c * w:(c + 1) * w]


def _hy_tables(seq, tm, tm_inv):
    half = seq // 2
    hb = tm // 2
    split = 32
    unit = 2.0 * math.pi / (4 * seq)
    ar = lambda n: jnp.arange(n, dtype=jnp.int32)

    def cos_sin(a, b):
        ang = ((a[..., :, None] * b[..., None, :]) % (4 * seq)).astype(F32) * unit
        return jnp.cos(ang), jnp.sin(ang)

    par = ar(2)[:, None]
    n_all = 2 * ar(half)[None, :] + par
    k_odd = 2 * ar(half) + 1
    f_hc, f_hs = cos_sin(jnp.broadcast_to(2 * split * ar(half // split), (2, half // split)), n_all)
    f_lc, f_ls = cos_sin(jnp.broadcast_to(2 * ar(split) + 1, (2, split)), n_all)
    i_hc, i_hs = cos_sin(2 * split * ar(half // split), k_odd)
    i_lc, i_ls = cos_sin(2 * ar(split)[None, :] + par, jnp.broadcast_to(k_odd, (2, half)))

    n_hi_f = hb // split
    fwd = pl.pallas_call(
        functools.partial(_hy_table_kernel, interleave=0),
        grid=(2, half // hb),
        in_specs=[
            pl.BlockSpec((None, n_hi_f, half), lambda p, i: (p, i, 0)),
            pl.BlockSpec((None, n_hi_f, half), lambda p, i: (p, i, 0)),
            pl.BlockSpec((None, split, half), lambda p, i: (p, 0, 0)),
            pl.BlockSpec((None, split, half), lambda p, i: (p, 0, 0)),
        ],
        out_specs=pl.BlockSpec((None, tm, half), lambda p, i: (p, i, 0)),
        out_shape=jax.ShapeDtypeStruct((2, seq, half), BF16),
        compiler_params=_cparams(("arbitrary", "arbitrary")),
        name="hy_table_fwd",
    )(f_hc, f_hs, f_lc, f_ls)
    n_hi_i = tm_inv // split
    inv = pl.pallas_call(
        functools.partial(_hy_table_kernel, interleave=hb),
        grid=(2, half // tm_inv),
        in_specs=[
            pl.BlockSpec((n_hi_i, half), lambda p, i: (i, 0)),
            pl.BlockSpec((n_hi_i, half), lambda p, i: (i, 0)),
            pl.BlockSpec((None, split, half), lambda p, i: (p, 0, 0)),
            pl.BlockSpec((None, split, half), lambda p, i: (p, 0, 0)),
        ],
        out_specs=pl.BlockSpec((None, tm_inv, seq), lambda p, i: (p, i, 0)),
        out_shape=jax.ShapeDtypeStruct((2, half, seq), BF16),
        compiler_params=_cparams(("arbitrary", "arbitrary")),
        name="hy_table_inv",
    )(i_hc, i_hs, i_lc, i_ls)
    return fwd, inv


def _hy_features(seq):
    t = jnp.linspace(0.0, 1.0, seq, dtype=F32)[:, None]
    bands = (HY_EMB - 1) // 2
    freqs = jnp.linspace(1e-4, bands - 1, bands, dtype=F32)[None, :]
    ang = (2.0 * math.pi / seq) * jnp.arange(seq, dtype=F32)[:, None] * freqs
    feat = jnp.concatenate([t, jnp.cos(ang), -jnp.sin(ang)], -1)
    feat = jnp.pad(feat, ((0, 0), (0, HY_EMB_PAD - HY_EMB)))
    deltas = jnp.abs(jnp.linspace(math.log(HY_TARGET) / HY_SLOW, math.log(HY_TARGET) / HY_FAST,
                                  MIXER_W, dtype=F32))
    decay = jnp.exp(-t * deltas[None, :])
    split = lambda a: jnp.stack([a[0::2], a[1::2]], axis=0)
    return split(feat), split(decay)


def _hy_tiles(seq):
    return min(512, seq), min(256, seq // 2)


def _hyena_mix(z, lp, gain, n_seq, seq, row0, tables):
    tm_f, tm_i = _hy_tiles(seq)
    ncol = n_seq * MIXER_W
    tn = min(1024, ncol)
    fwd, inv = tables
    feat, decay = _hy_features(seq)
    ab = _hy_filters(feat, decay, lp["hy_w1p"], lp["hy_b1"], lp["hy_f1"], lp["hy_w2"], lp["hy_b2"],
                     lp["hy_f2"], lp["hy_w3"])
    kspec = _hy_kspec(fwd, ab, tm_f)
    conv, u = _hy_conv(z, lp["hy_conv_w"], lp["hy_conv_b"], n_seq, seq, row0)
    y0 = _hy_fwd(fwd, u, kspec, 0, tm_f, tn)
    u1 = _hy_inv(inv, y0, u, conv, 0, lp["hy_skip"][0:1], None, False, tm_i, tn)
    y1 = _hy_fwd(fwd, u1, kspec, 1, tm_f, tn)
    out = _hy_inv(inv, y1, u1, conv, 1, lp["hy_skip"][1:2], gain, True, tm_i, tn)
    return jnp.transpose(out, (0, 2, 1, 3)).reshape(n_seq * seq, MIXER_W)


def _gelu(x):
    c = math.sqrt(2.0 / math.pi)
    return 0.5 * x * (1.0 + jnp.tanh(c * (x + 0.044715 * (x * x * x))))


def _gmlp_kernel(u_ref, v_ref, lg_ref, lb_ref, ws_ref, bs_ref, g_ref, o_ref):
    tl = u_ref.shape[0]
    gw = MIXER_W // GMLP_GROUPS
    lane_grp = lax.broadcasted_iota(jnp.int32, (GMLP_CHUNK, MIXER_W), 1) // gw
    for c in range(tl // GMLP_CHUNK):
        rows = slice(c * GMLP_CHUNK, (c + 1) * GMLP_CHUNK)
        v = _ln(_gelu(v_ref[rows, :])) * lg_ref[...] + lb_ref[...]
        vb = v.astype(BF16)
        stacked = jnp.concatenate(
            [jnp.where(lane_grp == g, vb, jnp.zeros_like(vb)) for g in range(GMLP_GROUPS)], axis=0)
        s = jnp.dot(ws_ref[...], stacked, preferred_element_type=F32) + bs_ref[...]
        y = _gelu(u_ref[rows, :]) * s
        o_ref[rows, :] = _rms_gain(y, g_ref[...]).astype(BF16)


def _gmlp_mix(z, ln_g, ln_b, ws_cat, bs_mat, gain, n_rows):
    tl = 512
    full = lambda shape: pl.BlockSpec(shape, lambda i: tuple(0 for _ in shape))
    return pl.pallas_call(
        _gmlp_kernel,
        grid=(n_rows // tl,),
        in_specs=[
            pl.BlockSpec((tl, MIXER_W), lambda i: (i, COL_GU)),
            pl.BlockSpec((tl, MIXER_W), lambda i: (i, COL_GV)),
            full((1, MIXER_W)), full((1, MIXER_W)),
            full((GMLP_CHUNK, GMLP_GROUPS * GMLP_CHUNK)),
            full((GMLP_CHUNK, MIXER_W)),
            full((1, MIXER_W)),
        ],
        out_specs=pl.BlockSpec((tl, MIXER_W), lambda i: (i, 0)),
        out_shape=jax.ShapeDtypeStruct((n_rows, MIXER_W), BF16),
        compiler_params=_cparams(("arbitrary",)),
        name="gmlp_mix",
    )(z, z, ln_g, ln_b, ws_cat, bs_mat, gain)


def _merge_kernel(*refs, alpha, n_lat_tiles, n_tiles):
    has_ctx = n_tiles > n_lat_tiles
    lat_refs, refs = refs[:3], refs[3:]
    if has_ctx:
        ctx_refs, refs = refs[:3], refs[3:]
    (yg_ref, w_ref, x_ref, g1_ref, lg_ref, lb_ref, sh_ref, sc_ref, wrc_ref, rb_ref,
     xo_ref, h_ref, idx_ref, gate_ref, y_a, y_b) = refs
    s = pl.program_id(0)

    @pl.when(s == 0)
    def _():
        y_b[...] = jnp.zeros_like(y_b)

    def step(y_new, y_old):
        parts = [r[...] for r in lat_refs]
        if has_ctx:
            is_ctx = jnp.minimum(s, n_tiles - 1) >= n_lat_tiles
            parts = [jnp.where(is_ctx, c[...], p) for c, p in zip(ctx_refs, parts)]
        ymix = jnp.concatenate(parts + [yg_ref[...]], axis=-1)
        y_new[...] = jnp.dot(ymix, w_ref[...], preferred_element_type=F32)

        x1 = _ln(alpha * x_ref[...] + g1_ref[...] * y_old[...]) * lg_ref[...] + lb_ref[...]
        xo_ref[...] = x1
        h = _ln(x1) * (1.0 + sc_ref[...]) + sh_ref[...]
        h_ref[...] = h
        h_hi = h.astype(BF16)
        h_lo = (h - h_hi.astype(F32)).astype(BF16)
        both = jnp.dot(h_hi, wrc_ref[...], preferred_element_type=F32)
        logits = both[:, :LOGIT_PAD] + (jnp.dot(h_lo, wrc_ref[:, :LOGIT_PAD], preferred_element_type=F32)
                                        + both[:, LOGIT_PAD:])
        idx, gate = _route_rows(logits, rb_ref[...])
        idx_ref[...] = idx
        gate_ref[...] = gate

    even = lax.rem(s, 2) == 0
    pl.when(even)(functools.partial(step, y_a, y_b))
    pl.when(jnp.logical_not(even))(functools.partial(step, y_b, y_a))


def _route_rows(logits, bias):
    neg = -1e30
    lane = lax.broadcasted_iota(jnp.int32, logits.shape, 1)
    lane_f = lane.astype(F32)
    valid = lane < N_EXPERTS
    per = N_EXPERTS // N_EXPERT_GROUPS
    s = jax.nn.sigmoid(logits)
    sel = jnp.where(valid, s + bias, neg)
    sh = [sel] + [pltpu.roll(sel, LOGIT_PAD - j, axis=1) for j in range(1, per)]
    pair = None
    for a in range(per):
        for b in range(a + 1, per):
            t = sh[a] + sh[b]
            pair = t if pair is None else jnp.maximum(pair, t)
    grp = jnp.where(valid & ((lane & (per - 1)) == 0), pair, neg)

    def first_max(v):
        m = jnp.max(v, axis=-1, keepdims=True)
        return jnp.min(jnp.where(v == m, lane_f, float(LOGIT_PAD)), axis=-1, keepdims=True)

    best = first_max(grp).astype(jnp.int32)
    shift = per.bit_length() - 1
    cand = jnp.where(valid & ((lane >> shift) == (best >> shift)), sel, neg)
    i1 = first_max(cand)
    i2 = first_max(jnp.where(lane_f == i1, neg, cand))
    w1 = jnp.sum(jnp.where(lane_f == i1, s, 0.0), axis=-1, keepdims=True)
    w2 = jnp.sum(jnp.where(lane_f == i2, s, 0.0), axis=-1, keepdims=True)
    tot = w1 + w2
    idx = jnp.where(lane == 0, i1, jnp.where(lane == 1, i2, 0.0)).astype(jnp.int32)
    gate = jnp.where(lane == 0, w1 / tot, jnp.where(lane == 1, w2 / tot, 0.0))
    return idx, gate


def _merge(ys_lat, ys_ctx, y_gm, w_out_bf, xa, mod3, ln_g, ln_b, wr_cat, rb_pad, n_rows, n_batch, seq, alpha):
    d = xa.shape[1]
    tm = 512
    nt = n_rows // tm
    n_lat_tiles = ys_lat[0].shape[0] // tm

    def cur(i):
        return jnp.minimum(i, nt - 1)

    def prv(i):
        return jnp.maximum(i - 1, 0)

    def mrow(i):
        return jnp.minimum((prv(i) * tm) // seq, n_batch)

    ymix = pl.BlockSpec((tm, MIXER_W), lambda i: (cur(i), 0))
    ylat = pl.BlockSpec((tm, MIXER_W), lambda i: (jnp.minimum(cur(i), n_lat_tiles - 1), 0))
    mixer_specs, mixer_args = [ylat] * 3, list(ys_lat)
    if ys_ctx is not None:
        n_ctx_tiles = nt - n_lat_tiles
        yctx = pl.BlockSpec((tm, MIXER_W), lambda i: (jnp.clip(cur(i) - n_lat_tiles, 0, n_ctx_tiles - 1), 0))
        mixer_specs, mixer_args = mixer_specs + [yctx] * 3, mixer_args + list(ys_ctx)
    rowvec = pl.BlockSpec((1, d), lambda i: (0, 0))
    return pl.pallas_call(
        functools.partial(_merge_kernel, alpha=alpha, n_lat_tiles=n_lat_tiles, n_tiles=nt),
        grid=(nt + 1,),
        in_specs=mixer_specs + [
            ymix,
            pl.BlockSpec((d, d), lambda i: (0, 0)),
            pl.BlockSpec((tm, d), lambda i: (prv(i), 0)),
            pl.BlockSpec((None, 1, d), lambda i: (mrow(i), 0, 2)),
            rowvec, rowvec,
            pl.BlockSpec((None, 1, d), lambda i: (mrow(i), 0, 3)),
            pl.BlockSpec((None, 1, d), lambda i: (mrow(i), 0, 4)),
            pl.BlockSpec((d, 2 * LOGIT_PAD), lambda i: (0, 0)),
            pl.BlockSpec((1, LOGIT_PAD), lambda i: (0, 0)),
        ],
        out_specs=[
            pl.BlockSpec((tm, d), lambda i: (prv(i), 0)),
            pl.BlockSpec((tm, d), lambda i: (prv(i), 0)),
            pl.BlockSpec((tm, LOGIT_PAD), lambda i: (prv(i), 0)),
            pl.BlockSpec((tm, LOGIT_PAD), lambda i: (prv(i), 0)),
        ],
        out_shape=[
            jax.ShapeDtypeStruct((n_rows, d), F32),
            jax.ShapeDtypeStruct((n_rows, d), F32),
            jax.ShapeDtypeStruct((n_rows, LOGIT_PAD), jnp.int32),
            jax.ShapeDtypeStruct((n_rows, LOGIT_PAD), F32),
        ],
        scratch_shapes=[pltpu.VMEM((tm, d), F32), pltpu.VMEM((tm, d), F32)],
        compiler_params=_cparams(("arbitrary",)),
        name="merge",
    )(*mixer_args, y_gm, w_out_bf, xa, mod3, ln_g, ln_b, mod3, mod3, wr_cat, rb_pad)


def _moe_kernel(be_ref, nu_ref, dest_ref, ws_ref, nxt_ref, pad_ref, h_hbm, wg_hbm, wu_hbm, wd_hbm, o_ref,
                xbuf, sem, tok_ref, stage_g, stage_u, stage_d, wsem, wg_s, wu_s, wd_s, *, layer):
    i = pl.program_id(0)
    n_used = nu_ref[0]
    ring = xbuf.shape[0]
    slot = lax.rem(i, ring)

    @pl.when(i == 0)
    def _():
        def clear(s, carry):
            tok_ref[s] = 0
            return carry

        def place(a, carry):
            tok_ref[dest_ref[a]] = lax.shift_right_logical(a, TOP_K.bit_length() - 1)
            return carry

        for e in range(N_EXPERTS):
            lax.fori_loop(pad_ref[e], pad_ref[N_EXPERTS + e], clear, 0)
        lax.fori_loop(0, dest_ref.shape[0], place, 0, unroll=8)

    def row_copy(blk, r, sl):
        tok = tok_ref[blk * MOE_BLOCK + r]
        return pltpu.make_async_copy(h_hbm.at[pl.ds(tok, 1)], xbuf.at[sl, pl.ds(r, 1)], sem.at[sl])

    def gather_start(blk, sl):
        def body(r, carry):
            row_copy(blk, r, sl).start()
            return carry

        lax.fori_loop(0, MOE_BLOCK, body, 0, unroll=8)

    def gather_wait(sl):
        pltpu.make_async_copy(h_hbm.at[pl.ds(0, MOE_BLOCK)], xbuf.at[sl], sem.at[sl]).wait()

    def weight_copies(e, ws):
        return (pltpu.make_async_copy(wg_hbm.at[layer, e], stage_g.at[ws], wsem.at[ws]),
                pltpu.make_async_copy(wu_hbm.at[layer, e], stage_u.at[ws], wsem.at[ws]),
                pltpu.make_async_copy(wd_hbm.at[layer, e], stage_d.at[ws], wsem.at[ws]))

    @pl.when(i == 0)
    def _():
        for cp in weight_copies(be_ref[0], ws_ref[0]):
            cp.start(priority=1)
        gather_start(0, 0)

        @pl.when(n_used > 1)
        def _():
            gather_start(1, 1)

    active = i < n_used
    new_expert = jnp.logical_or(i == 0, be_ref[i] != be_ref[jnp.maximum(i - 1, 0)])

    @pl.when(jnp.logical_and(active, new_expert))
    def _():
        ws = ws_ref[i]
        for cp in weight_copies(be_ref[i], ws):
            cp.wait()
        nxt = nxt_ref[i]

        @pl.when(nxt >= 0)
        def _():
            for cp in weight_copies(nxt, 1 - ws):
                cp.start(priority=1)

        for src, dst in ((stage_g, wg_s), (stage_u, wu_s), (stage_d, wd_s)):
            def cast(r0, src=src, dst=dst):
                dst[pl.ds(r0, 256), :] = src[ws, pl.ds(r0, 256), :].astype(BF16)

            _row_chunks(dst.shape[0], 256, cast)

    def ffn_block(prefetch_next):
        gather_wait(slot)
        x = xbuf[slot].astype(BF16)
        if prefetch_next:
            nxt = lax.rem(i + 2, ring)
            for r in range(MOE_BLOCK):
                row_copy(i + 2, r, nxt).start()
        g = jnp.dot(x, wg_s[...], preferred_element_type=F32)
        u = jnp.dot(x, wu_s[...], preferred_element_type=F32)
        a = (g * jax.nn.sigmoid(g) * u).astype(BF16)
        o_ref[...] = jnp.dot(a, wd_s[...], preferred_element_type=F32).astype(o_ref.dtype)

    has_next = i + 2 < n_used
    pl.when(jnp.logical_and(active, has_next))(functools.partial(ffn_block, True))
    pl.when(jnp.logical_and(active, jnp.logical_not(has_next)))(functools.partial(ffn_block, False))

    @pl.when(jnp.logical_not(active))
    def _():
        o_ref[...] = jnp.zeros_like(o_ref)


def _moe_experts(h, dest, plan, wg, wu, wd, layer):
    d = h.shape[1]
    n_slots = plan["n_slots"]
    de = wg.shape[3]
    n_blocks = n_slots // MOE_BLOCK
    hbm = pl.BlockSpec(memory_space=pl.ANY)
    grid_spec = pltpu.PrefetchScalarGridSpec(
        num_scalar_prefetch=6,
        grid=(n_blocks,),
        in_specs=[hbm, hbm, hbm, hbm],
        out_specs=pl.BlockSpec((MOE_BLOCK, d), lambda i, *_: (i, 0)),
        scratch_shapes=[
            pltpu.VMEM((3, MOE_BLOCK, d), F32),
            pltpu.SemaphoreType.DMA((3,)),
            pltpu.SMEM((n_slots,), jnp.int32),
            pltpu.VMEM((2, d, de), F32), pltpu.VMEM((2, d, de), F32), pltpu.VMEM((2, de, d), F32),
            pltpu.SemaphoreType.DMA((2,)),
            pltpu.VMEM((d, de), BF16), pltpu.VMEM((d, de), BF16), pltpu.VMEM((de, d), BF16),
        ],
    )
    return pl.pallas_call(
        functools.partial(_moe_kernel, layer=layer),
        grid_spec=grid_spec,
        out_shape=jax.ShapeDtypeStruct((n_slots, d), BF16),
        compiler_params=_cparams(("arbitrary",), MOE_VMEM_LIMIT),
        name="moe_experts",
    )(plan["block_e"], plan["n_used"], dest.reshape(-1), plan["wslot"], plan["next_e"], plan["pad_range"],
      h, wg, wu, wd)


def _assignment_ranks(flat_e):
    a = flat_e.shape[0]
    blk = 128
    nb = a // blk
    onehot = (flat_e[:, None] == jnp.arange(N_EXPERTS)[None, :]).astype(BF16).reshape(nb, blk, N_EXPERTS)
    tri = (jnp.arange(blk)[:, None] >= jnp.arange(blk)[None, :]).astype(BF16)
    intra = jnp.einsum("ij,bjk->bik", tri, onehot, preferred_element_type=F32)
    bsum = intra[:, -1, :]
    before = (jnp.arange(nb)[:, None] > jnp.arange(nb)[None, :]).astype(BF16)
    offs = jnp.dot(before, bsum.astype(BF16), preferred_element_type=F32)
    csum = intra + offs[:, None, :]
    rank = jnp.sum(csum * onehot.astype(F32), axis=-1).reshape(a) - 1.0
    counts = offs[-1] + bsum[-1]
    return rank.astype(jnp.int32), counts.astype(jnp.int32)


def _dispatch_plan(idx):
    t = idx.shape[0]
    a = t * TOP_K
    flat_e = idx.reshape(a)
    rank, counts = _assignment_ranks(flat_e)
    padded = (counts + MOE_BLOCK - 1) // MOE_BLOCK * MOE_BLOCK
    pad_end = jnp.cumsum(padded)
    pad_start = pad_end - padded
    dest = pad_start[flat_e] + rank
    n_blocks = (a + N_EXPERTS * (MOE_BLOCK - 1) + MOE_BLOCK - 1) // MOE_BLOCK
    n_slots = n_blocks * MOE_BLOCK
    blk_start = jnp.arange(n_blocks, dtype=jnp.int32) * MOE_BLOCK
    block_e = jnp.minimum(
        jnp.sum((pad_end[None, :] <= blk_start[:, None]).astype(jnp.int32), axis=1), N_EXPERTS - 1)
    n_used = (pad_end[-1] // MOE_BLOCK).astype(jnp.int32).reshape(1)
    nonempty = counts > 0
    eid = jnp.arange(N_EXPERTS, dtype=jnp.int32)
    run_of_e = jnp.cumsum(nonempty.astype(jnp.int32)) - 1
    later = jnp.logical_and(eid[None, :] > eid[:, None], nonempty[None, :])
    nxt_of_e = jnp.min(jnp.where(later, eid[None, :], N_EXPERTS), axis=1)
    nxt_of_e = jnp.where(nxt_of_e == N_EXPERTS, -1, nxt_of_e)
    plan = {
        "n_slots": n_slots, "block_e": block_e.astype(jnp.int32), "n_used": n_used,
        "wslot": (run_of_e[block_e] % 2).astype(jnp.int32), "next_e": nxt_of_e[block_e].astype(jnp.int32),
        "pad_range": jnp.concatenate([pad_start + counts, pad_end]).astype(jnp.int32),
    }
    return plan, dest.reshape(t, TOP_K)


def _final_kernel(x_ref, ya_ref, yb_ref, gate_ref, g2_ref, lg_ref, lb_ref, o_ref, *, alpha):
    f = ya_ref[...].astype(F32) * gate_ref[:, 0:1] + yb_ref[...].astype(F32) * gate_ref[:, 1:2]
    o_ref[...] = _ln(alpha * x_ref[...] + g2_ref[...] * f) * lg_ref[...] + lb_ref[...]


def _final_norm(x1, ya, yb, gate, mod3, ln_g, ln_b, n_batch, seq, alpha):
    n_rows, d = x1.shape
    tm = 512

    def mrow(i):
        return jnp.minimum((i * tm) // seq, n_batch)

    tile = pl.BlockSpec((tm, d), lambda i: (i, 0))
    rowvec = pl.BlockSpec((1, d), lambda i: (0, 0))
    return pl.pallas_call(
        functools.partial(_final_kernel, alpha=alpha),
        grid=(n_rows // tm,),
        in_specs=[tile, tile, tile, pl.BlockSpec((tm, LOGIT_PAD), lambda i: (i, 0)),
                  pl.BlockSpec((None, 1, d), lambda i: (mrow(i), 0, 5)), rowvec, rowvec],
        out_specs=tile,
        out_shape=jax.ShapeDtypeStruct((n_rows, d), F32),
        compiler_params=_cparams(("arbitrary",)),
        name="final_norm",
    )(x1, ya, yb, gate, mod3, ln_g, ln_b)


def _permute_in_cols(w):
    pool, q, k, v, hy, gm = jnp.split(w, (512, 1024, 1152, 1280, 2816), axis=-1)
    return jnp.concatenate([hy, pool, q, gm, k, v], axis=-1)


def kernel(x, c, ctx, c_ctx, w_ada, b_ada, w_in, w_out, mix_norm_g, pool_w, pool_scale, attn_sink,
           hy_conv_w, hy_conv_b, hy_w1, hy_b1, hy_f1, hy_w2, hy_b2, hy_f2, hy_w3, hy_skip,
           gm_ln_g, gm_ln_b, gm_ws, gm_bs, ln1_g, ln1_b, ln2_g, ln2_b, w_router, router_bias,
           w_gate, w_up, w_down):
    n_batch, seq, d = x.shape
    ctx_len = ctx.shape[1]
    depth = w_in.shape[0]
    grid_w = 64
    alpha = (2 * depth) ** 0.25
    n_lat = n_batch * seq
    n_ctx = n_batch * ctx_len

    xa = jnp.concatenate([x.reshape(n_lat, d), ctx.reshape(n_ctx, d)], axis=0)
    cvec = jnp.concatenate([c, c_ctx[None, :], jnp.zeros((8 - n_batch - 1, d), F32)], axis=0)
    mods = _ada_mods(cvec, w_ada, b_ada)
    cos_t, sin_t = _rope_tables(seq, grid_w)
    tabs_lat = _hy_tables(seq, *_hy_tiles(seq))
    tabs_ctx = _hy_tables(ctx_len, *_hy_tiles(ctx_len))
    wr_pad = jnp.pad(w_router, ((0, 0), (0, LOGIT_PAD - N_EXPERTS)))
    wr_hi = wr_pad.astype(BF16)
    wr_lo = (wr_pad - wr_hi.astype(F32)).astype(BF16)
    wr_cat = jnp.concatenate([wr_hi, wr_lo], axis=1)
    rb_pad = jnp.pad(router_bias.astype(F32), (0, LOGIT_PAD - N_EXPERTS)).reshape(1, LOGIT_PAD)
    row = lambda v: v.reshape(1, -1)

    for layer in range(depth):
        last = layer == depth - 1
        mod3 = mods[layer].reshape(8, 1, 6 * d)
        gains = mix_norm_g[layer].reshape(4, 1, MIXER_W)
        lp = {
            "hy_conv_w": hy_conv_w[layer], "hy_conv_b": row(hy_conv_b[layer]),
            "hy_w1p": jnp.pad(hy_w1[layer], ((0, HY_EMB_PAD - HY_EMB), (0, 0))),
            "hy_b1": row(hy_b1[layer]), "hy_f1": row(hy_f1[layer]),
            "hy_w2": hy_w2[layer], "hy_b2": row(hy_b2[layer]), "hy_f2": row(hy_f2[layer]),
            "hy_w3": hy_w3[layer], "hy_skip": hy_skip[layer],
        }
        z = _in_proj(xa, mod3, _permute_in_cols(w_in[layer]).astype(BF16), n_batch, seq)

        pool_bf = pool_w[layer].astype(BF16)
        ps = row(pool_scale[layer])
        ws_cat = jnp.transpose(gm_ws[layer], (1, 0, 2)).reshape(GMLP_CHUNK, GMLP_GROUPS * GMLP_CHUNK)
        bs_mat = jnp.repeat(gm_bs[layer].T, MIXER_W // GMLP_GROUPS, axis=1)
        n_rows = n_lat if last else n_lat + n_ctx

        y_pool = _pool_mix(z, pool_bf, ps, gains[0], n_batch, seq, 0)
        y_attn = _attn_lat(z, attn_sink[layer], cos_t, sin_t, gains[1], n_batch, seq, ctx_len)
        y_hy = _hyena_mix(z, lp, gains[2], n_batch, seq, 0, tabs_lat)
        y_gm = _gmlp_mix(z, row(gm_ln_g[layer]), row(gm_ln_b[layer]), ws_cat.astype(BF16), bs_mat,
                         gains[3], n_rows)
        ys_ctx = None
        if not last:
            ys_ctx = (_pool_mix(z, pool_bf, ps, gains[0], n_batch, ctx_len, n_lat // ctx_len),
                      _attn_ctx(z, attn_sink[layer], gains[1], n_batch, seq, ctx_len),
                      _hyena_mix(z, lp, gains[2], n_batch, ctx_len, n_lat, tabs_ctx))

        x1, h2, idx, gate = _merge((y_pool, y_attn, y_hy), ys_ctx, y_gm, w_out[layer].astype(BF16), xa, mod3,
                                   row(ln1_g[layer]), row(ln1_b[layer]), wr_cat, rb_pad,
                                   n_rows, n_batch, seq, alpha)

        plan, dest = _dispatch_plan(idx[:, :TOP_K])
        ys = _moe_experts(h2, dest, plan, w_gate, w_up, w_down, layer)
        xa = _final_norm(x1, ys[dest[:, 0]], ys[dest[:, 1]], gate, mod3, row(ln2_g[layer]), row(ln2_b[layer]),
                         n_batch, seq, alpha)
    return xa.reshape(n_batch, seq, d)
```
